```python
import jax, jax.numpy as jnp
from jax import lax
import numpy as np

D_MODEL = 2048
BATCH = 8
SEQ = 2048
DEPTH = 2

GRID_W = 64
CTX_LEN = 256
N_MIXERS = 2
N_RWKV = (DEPTH + 1) // 2
N_CONV = DEPTH // 2
HEAD_SIZE = 64
N_HEADS = D_MODEL // HEAD_SIZE
LORA_DECAY = max(32, int(round(1.8 * D_MODEL ** 0.5 / 32)) * 32)
LORA_A = max(32, int(round(1.8 * D_MODEL ** 0.5 / 32)) * 32)
LORA_GATE = max(32, int(round(0.6 * D_MODEL ** 0.8 / 32)) * 32)
CONV_WIDTH = 3
D_FF = ((8 * D_MODEL // 3 + 255) // 256) * 256
NORM_EPS = 1e-6
GN_EPS = 64e-5

kernel_name = "rwkv7_shortconv_hybrid_dit"


def rmsnorm(h, g):
    hf = h.astype(jnp.float32)
    hn = hf * lax.rsqrt(jnp.mean(hf * hf, axis=-1, keepdims=True) + NORM_EPS)
    return hn.astype(h.dtype) * g


def modulate(h, shift, scale):
    return h * (1 + scale) + shift


def split_heads(t):
    return t.reshape(t.shape[0], t.shape[1], N_HEADS, HEAD_SIZE)


def grid_shift(h):
    Bn, T, D = h.shape
    rows = T // GRID_W
    g = h.reshape(Bn, rows, GRID_W, D)
    q = D // 4
    left = jnp.pad(g[:, :, :-1, :q], ((0, 0), (0, 0), (1, 0), (0, 0)))
    right = jnp.pad(g[:, :, 1:, q:2 * q], ((0, 0), (0, 0), (0, 1), (0, 0)))
    up = jnp.pad(g[:, :-1, :, 2 * q:3 * q], ((0, 0), (1, 0), (0, 0), (0, 0)))
    down = jnp.pad(g[:, 1:, :, 3 * q:], ((0, 0), (0, 1), (0, 0), (0, 0)))
    return jnp.concatenate([left, right, up, down], axis=-1).reshape(Bn, T, D)


def seq_shift(h):
    half = h.shape[-1] // 2
    prev = jnp.pad(h[:, :-1, :half], ((0, 0), (1, 0), (0, 0)))
    nxt = jnp.pad(h[:, 1:, half:], ((0, 0), (0, 1), (0, 0)))
    return jnp.concatenate([prev, nxt], axis=-1)


def wkv_scan(state0, r, decay, k, v, a, b, reverse):
    def step(S, inp):
        r_t, w_t, k_t, v_t, a_t, b_t = inp
        sa = jnp.einsum('bhvk,bhk->bhv', S, a_t)
        S = (S * w_t[:, :, None, :] + sa[..., None] * b_t[:, :, None, :]
             + v_t[..., None] * k_t[:, :, None, :])
        return S, jnp.einsum('bhvk,bhk->bhv', S, r_t)
    xs = tuple(jnp.swapaxes(t.astype(jnp.float32), 0, 1) for t in (r, decay, k, v, a, b))
    state, ys = lax.scan(step, state0, xs, reverse=reverse)
    return state, jnp.swapaxes(ys, 0, 1)


def rwkv_shared(h, h_shift, mix, wr, wk, wv, g1, g2, k_k):
    xx = h_shift - h
    xr, xw, xk, xv, xa, xg = [h + xx * mix[m] for m in range(6)]
    r = split_heads(xr @ wr)
    k = xk @ wk
    v = split_heads(xv @ wv)
    g = jax.nn.sigmoid(xg @ g1) @ g2
    kk = split_heads((k * k_k).astype(jnp.float32))
    kk = kk / jnp.maximum(jnp.sqrt(jnp.sum(kk * kk, axis=-1, keepdims=True)), 1e-12)
    return r, k, v, kk, g, xw, xa


def rwkv_direction(xw, xa, k, kk, w0, w1, w2, a0, a1, a2, k_a):
    log_w = -jax.nn.softplus(-(w0 + jnp.tanh(xw @ w1) @ w2).astype(jnp.float32)) - 0.5
    decay = jnp.exp(-jnp.exp(log_w))
    a = jax.nn.sigmoid(a0 + (xa @ a1) @ a2)
    k_d = k * (1 + (a - 1) * k_a)
    a_h = split_heads(a).astype(jnp.float32)
    return split_heads(decay), split_heads(k_d), -kk, kk * a_h


def rwkv_readout(y, r, ksum, v, g, r_k, ln_w, ln_b, wo, dtype):
    Bn, T = y.shape[0], y.shape[1]
    mu = jnp.mean(y, axis=-1, keepdims=True)
    var = jnp.mean(jnp.square(y - mu), axis=-1, keepdims=True)
    o = ((y - mu) * lax.rsqrt(var + GN_EPS)).reshape(Bn, T, D_MODEL) * ln_w + ln_b
    bonus = jnp.sum(r * ksum * r_k, axis=-1, keepdims=True) * v
    o = o + bonus.reshape(Bn, T, D_MODEL)
    return (o.astype(dtype) * g) @ wo


def rwkv7_mix(hx, hc, need_ctx, mix, wr, wk, wv, wo, w0, w1, w2, a0, a1, a2,
              g1, g2, k_k, k_a, r_k, ln_w, ln_b):
    r_c, k_c, v_c, kk_c, g_c, xw_c, xa_c = rwkv_shared(hc, seq_shift(hc), mix, wr, wk, wv, g1, g2, k_k)
    r_x, k_x, v_x, kk_x, g_x, xw_x, xa_x = rwkv_shared(hx, grid_shift(hx), mix, wr, wk, wv, g1, g2, k_k)
    state0 = jnp.zeros((hx.shape[0], N_HEADS, HEAD_SIZE, HEAD_SIZE), jnp.float32)
    y_c = 0.0
    y_x = 0.0
    ksum_c = 0.0
    ksum_x = 0.0
    for d, rev in enumerate((False, True)):
        dec_c, kd_c, aa_c, bb_c = rwkv_direction(xw_c, xa_c, k_c, kk_c, w0[d], w1[d], w2[d], a0[d], a1[d], a2[d], k_a)
        state_c, yd_c = wkv_scan(state0, r_c, dec_c, kd_c, v_c, aa_c, bb_c, rev)
        dec_x, kd_x, aa_x, bb_x = rwkv_direction(xw_x, xa_x, k_x, kk_x, w0[d], w1[d], w2[d], a0[d], a1[d], a2[d], k_a)
        _, yd_x = wkv_scan(state_c, r_x, dec_x, kd_x, v_x, aa_x, bb_x, rev)
        y_x = y_x + yd_x
        ksum_x = ksum_x + kd_x
        if need_ctx:
            y_c = y_c + yd_c
            ksum_c = ksum_c + kd_c
    out_x = rwkv_readout(y_x, r_x, ksum_x, v_x, g_x, r_k, ln_w, ln_b, wo, hx.dtype)
    out_c = rwkv_readout(y_c, r_c, ksum_c, v_c, g_c, r_k, ln_w, ln_b, wo, hc.dtype) if need_ctx else None
    return out_x, out_c


def short_conv(h, w_in, conv_w, w_out):
    gb, gc, u = jnp.split(h @ w_in, 3, axis=-1)
    z = jnp.pad(gc * u, ((0, 0), (1, 1), (0, 0)))
    conv = z[:, :-2] * conv_w[0] + z[:, 1:-1] * conv_w[1] + z[:, 2:] * conv_w[2]
    return (gb * conv) @ w_out


def swiglu(h, w13, w2):
    a, b = jnp.split(h @ w13, 2, axis=-1)
    return (jax.nn.silu(a) * b) @ w2


def _fwd_setup_inputs(seed: int = 0) -> dict:
    key = jax.random.key(seed)
    ks = iter(jax.random.split(key, 48))

    def nrm(shape, scale):
        return jax.random.normal(next(ks), shape, jnp.float32) * scale

    def unif(shape, lo, hi):
        return jax.random.uniform(next(ks), shape, jnp.float32, lo, hi)

    D, F = D_MODEL, D_FF
    inv = D ** -0.5
    return {
        "x": nrm((BATCH, SEQ, D), 1.0),
        "c": nrm((BATCH, D), 1.0),
        "ctx": nrm((BATCH, CTX_LEN, D), 1.0),
        "c_ctx": nrm((D,), 1.0),
        "norm1_g": 1.0 + nrm((DEPTH, D), 0.02),
        "norm2_g": 1.0 + nrm((DEPTH, D), 0.02),
        "ada_w": nrm((DEPTH, D, 6 * D), 0.5 * inv),
        "ada_b": nrm((DEPTH, 6 * D), 0.02),
        "rw_mix": unif((N_RWKV, 6, D), 0.0, 1.0),
        "rw_wr": nrm((N_RWKV, D, D), inv),
        "rw_wk": nrm((N_RWKV, D, D), inv),
        "rw_wv": nrm((N_RWKV, D, D), inv),
        "rw_wo": nrm((N_RWKV, D, D), inv),
        "rw_w0": unif((N_RWKV, 2, D), -6.5, -1.5),
        "rw_w1": nrm((N_RWKV, 2, D, LORA_DECAY), inv),
        "rw_w2": nrm((N_RWKV, 2, LORA_DECAY, D), 0.5 * LORA_DECAY ** -0.5),
        "rw_a0": nrm((N_RWKV, 2, D), 0.1),
        "rw_a1": nrm((N_RWKV, 2, D, LORA_A), inv),
        "rw_a2": nrm((N_RWKV, 2, LORA_A, D), 0.5 * LORA_A ** -0.5),
        "rw_g1": nrm((N_RWKV, D, LORA_GATE), inv),
        "rw_g2": nrm((N_RWKV, LORA_GATE, D), LORA_GATE ** -0.5),
        "rw_kk": 0.85 + nrm((N_RWKV, D), 0.05),
        "rw_ka": 1.0 + nrm((N_RWKV, D), 0.05),
        "rw_rk": nrm((N_RWKV, N_HEADS, HEAD_SIZE), 0.1),
        "rw_lnw": 1.0 + nrm((N_RWKV, D), 0.02),
        "rw_lnb": nrm((N_RWKV, D), 0.02),
        "sc_win": nrm((N_CONV, D, 3 * D), inv),
        "sc_conv": nrm((N_CONV, CONV_WIDTH, D), CONV_WIDTH ** -0.5),
        "sc_wout": nrm((N_CONV, D, D), inv),
        "ffn_w13": nrm((DEPTH, D, 2 * F), inv),
        "ffn_w2": nrm((DEPTH, F, D), F ** -0.5),
        "final_g": 1.0 + nrm((D,), 0.02),
    }


def _fwd_reference(x, c, ctx, c_ctx, norm1_g, norm2_g, ada_w, ada_b,
              rw_mix, rw_wr, rw_wk, rw_wv, rw_wo, rw_w0, rw_w1, rw_w2,
              rw_a0, rw_a1, rw_a2, rw_g1, rw_g2, rw_kk, rw_ka, rw_rk, rw_lnw, rw_lnb,
              sc_win, sc_conv, sc_wout, ffn_w13, ffn_w2, final_g):
    cond_x = jax.nn.silu(c)
    cond_c = jax.nn.silu(c_ctx)
    for i in range(DEPTH):
        last = i == DEPTH - 1
        is_rwkv = i % N_MIXERS == 0
        j = i // N_MIXERS
        mod_x = (cond_x @ ada_w[i] + ada_b[i])[:, None, :]
        sh1_x, sc1_x, gt1_x, sh2_x, sc2_x, gt2_x = jnp.split(mod_x, 6, axis=-1)
        hx = modulate(rmsnorm(x, norm1_g[i]), sh1_x, sc1_x)
        ctx_used = is_rwkv or not last
        if ctx_used:
            mod_c = (cond_c @ ada_w[i] + ada_b[i])[None, None, :]
            sh1_c, sc1_c, gt1_c, sh2_c, sc2_c, gt2_c = jnp.split(mod_c, 6, axis=-1)
            hc = modulate(rmsnorm(ctx, norm1_g[i]), sh1_c, sc1_c)
        if is_rwkv:
            yx, yc = rwkv7_mix(hx, hc, not last, rw_mix[j], rw_wr[j], rw_wk[j], rw_wv[j], rw_wo[j],
                               rw_w0[j], rw_w1[j], rw_w2[j], rw_a0[j], rw_a1[j], rw_a2[j],
                               rw_g1[j], rw_g2[j], rw_kk[j], rw_ka[j], rw_rk[j], rw_lnw[j], rw_lnb[j])
        else:
            yx = short_conv(hx, sc_win[j], sc_conv[j], sc_wout[j])
            yc = short_conv(hc, sc_win[j], sc_conv[j], sc_wout[j]) if not last else None
        x = x + gt1_x * yx
        x = x + gt2_x * swiglu(modulate(rmsnorm(x, norm2_g[i]), sh2_x, sc2_x), ffn_w13[i], ffn_w2[i])
        if not last:
            ctx = ctx + gt1_c * yc
            ctx = ctx + gt2_c * swiglu(modulate(rmsnorm(ctx, norm2_g[i]), sh2_c, sc2_c), ffn_w13[i], ffn_w2[i])
    return rmsnorm(x, final_g)


import jax as _jax
import jax.numpy as _jnp

TWIN_FORMAT = 'train_step'
FWD_PARAMS = ['x', 'c', 'ctx', 'c_ctx', 'norm1_g', 'norm2_g', 'ada_w', 'ada_b', 'rw_mix', 'rw_wr', 'rw_wk', 'rw_wv', 'rw_wo', 'rw_w0', 'rw_w1', 'rw_w2', 'rw_a0', 'rw_a1', 'rw_a2', 'rw_g1', 'rw_g2', 'rw_kk', 'rw_ka', 'rw_rk', 'rw_lnw', 'rw_lnb', 'sc_win', 'sc_conv', 'sc_wout', 'ffn_w13', 'ffn_w2', 'final_g']
TWIN_WEIGHTS = ['c_ctx', 'norm1_g', 'norm2_g', 'ada_w', 'ada_b', 'rw_mix', 'rw_wr', 'rw_wk', 'rw_wv', 'rw_wo', 'rw_w0', 'rw_w1', 'rw_w2', 'rw_a0', 'rw_a1', 'rw_a2', 'rw_g1', 'rw_g2', 'rw_kk', 'rw_ka', 'rw_rk', 'rw_lnw', 'rw_lnb', 'sc_win', 'sc_conv', 'sc_wout', 'ffn_w13', 'ffn_w2', 'final_g']
TWIN_DIFF_INPUT = 'x'
TWIN_INPUTS = ['x', 'c', 'ctx', 'c_ctx', 'norm1_g', 'norm2_g', 'ada_w', 'ada_b', 'rw_mix', 'rw_wr', 'rw_wk', 'rw_wv', 'rw_wo', 'rw_w0', 'rw_w1', 'rw_w2', 'rw_a0', 'rw_a1', 'rw_a2', 'rw_g1', 'rw_g2', 'rw_kk', 'rw_ka', 'rw_rk', 'rw_lnw', 'rw_lnb', 'sc_win', 'sc_conv', 'sc_wout', 'ffn_w13', 'ffn_w2', 'final_g', 'loss_target', 'm_c_ctx', 'm_norm1_g', 'm_norm2_g', 'm_ada_w', 'm_ada_b', 'm_rw_mix', 'm_rw_wr', 'm_rw_wk', 'm_rw_wv', 'm_rw_wo', 'm_rw_w0', 'm_rw_w1', 'm_rw_w2', 'm_rw_a0', 'm_rw_a1', 'm_rw_a2', 'm_rw_g1', 'm_rw_g2', 'm_rw_kk', 'm_rw_ka', 'm_rw_rk', 'm_rw_lnw', 'm_rw_lnb', 'm_sc_win', 'm_sc_conv', 'm_sc_wout', 'm_ffn_w13', 'm_ffn_w2', 'm_final_g', 'v_c_ctx', 'v_norm1_g', 'v_norm2_g', 'v_ada_w', 'v_ada_b', 'v_rw_mix', 'v_rw_wr', 'v_rw_wk', 'v_rw_wv', 'v_rw_wo', 'v_rw_w0', 'v_rw_w1', 'v_rw_w2', 'v_rw_a0', 'v_rw_a1', 'v_rw_a2', 'v_rw_g1', 'v_rw_g2', 'v_rw_kk', 'v_rw_ka', 'v_rw_rk', 'v_rw_lnw', 'v_rw_lnb', 'v_sc_win', 'v_sc_conv', 'v_sc_wout', 'v_ffn_w13', 'v_ffn_w2', 'v_final_g']
TWIN_OUTPUTS = ['loss', 'grad_x', 'grad_c_ctx', 'grad_norm1_g', 'grad_norm2_g', 'grad_ada_w', 'grad_ada_b', 'grad_rw_mix', 'grad_rw_wr', 'grad_rw_wk', 'grad_rw_wv', 'grad_rw_wo', 'grad_rw_w0', 'grad_rw_w1', 'grad_rw_w2', 'grad_rw_a0', 'grad_rw_a1', 'grad_rw_a2', 'grad_rw_g1', 'grad_rw_g2', 'grad_rw_kk', 'grad_rw_ka', 'grad_rw_rk', 'grad_rw_lnw', 'grad_rw_lnb', 'grad_sc_win', 'grad_sc_conv', 'grad_sc_wout', 'grad_ffn_w13', 'grad_ffn_w2', 'grad_final_g', 'delta_c_ctx', 'delta_norm1_g', 'delta_norm2_g', 'delta_ada_w', 'delta_ada_b', 'delta_rw_mix', 'delta_rw_wr', 'delta_rw_wk', 'delta_rw_wv', 'delta_rw_wo', 'delta_rw_w0', 'delta_rw_w1', 'delta_rw_w2', 'delta_rw_a0', 'delta_rw_a1', 'delta_rw_a2', 'delta_rw_g1', 'delta_rw_g2', 'delta_rw_kk', 'delta_rw_ka', 'delta_rw_rk', 'delta_rw_lnw', 'delta_rw_lnb', 'delta_sc_win', 'delta_sc_conv', 'delta_sc_wout', 'delta_ffn_w13', 'delta_ffn_w2', 'delta_final_g', 'new_m_c_ctx', 'new_m_norm1_g', 'new_m_norm2_g', 'new_m_ada_w', 'new_m_ada_b', 'new_m_rw_mix', 'new_m_rw_wr', 'new_m_rw_wk', 'new_m_rw_wv', 'new_m_rw_wo', 'new_m_rw_w0', 'new_m_rw_w1', 'new_m_rw_w2', 'new_m_rw_a0', 'new_m_rw_a1', 'new_m_rw_a2', 'new_m_rw_g1', 'new_m_rw_g2', 'new_m_rw_kk', 'new_m_rw_ka', 'new_m_rw_rk', 'new_m_rw_lnw', 'new_m_rw_lnb', 'new_m_sc_win', 'new_m_sc_conv', 'new_m_sc_wout', 'new_m_ffn_w13', 'new_m_ffn_w2', 'new_m_final_g', 'new_v_c_ctx', 'new_v_norm1_g', 'new_v_norm2_g', 'new_v_ada_w', 'new_v_ada_b', 'new_v_rw_mix', 'new_v_rw_wr', 'new_v_rw_wk', 'new_v_rw_wv', 'new_v_rw_wo', 'new_v_rw_w0', 'new_v_rw_w1', 'new_v_rw_w2', 'new_v_rw_a0', 'new_v_rw_a1', 'new_v_rw_a2', 'new_v_rw_g1', 'new_v_rw_g2', 'new_v_rw_kk', 'new_v_rw_ka', 'new_v_rw_rk', 'new_v_rw_lnw', 'new_v_rw_lnb', 'new_v_sc_win', 'new_v_sc_conv', 'new_v_sc_wout', 'new_v_ffn_w13', 'new_v_ffn_w2', 'new_v_final_g']
TWIN_LEAF_KINDS = {'loss': 'loss', 'grad_x': 'grad_x', 'grad_c_ctx': 'grad_w', 'grad_norm1_g': 'grad_w', 'grad_norm2_g': 'grad_w', 'grad_ada_w': 'grad_w', 'grad_ada_b': 'grad_w', 'grad_rw_mix': 'grad_w', 'grad_rw_wr': 'grad_w', 'grad_rw_wk': 'grad_w', 'grad_rw_wv': 'grad_w', 'grad_rw_wo': 'grad_w', 'grad_rw_w0': 'grad_w', 'grad_rw_w1': 'grad_w', 'grad_rw_w2': 'grad_w', 'grad_rw_a0': 'grad_w', 'grad_rw_a1': 'grad_w', 'grad_rw_a2': 'grad_w', 'grad_rw_g1': 'grad_w', 'grad_rw_g2': 'grad_w', 'grad_rw_kk': 'grad_w', 'grad_rw_ka': 'grad_w', 'grad_rw_rk': 'grad_w', 'grad_rw_lnw': 'grad_w', 'grad_rw_lnb': 'grad_w', 'grad_sc_win': 'grad_w', 'grad_sc_conv': 'grad_w', 'grad_sc_wout': 'grad_w', 'grad_ffn_w13': 'grad_w', 'grad_ffn_w2': 'grad_w', 'grad_final_g': 'grad_w', 'delta_c_ctx': 'delta_w', 'delta_norm1_g': 'delta_w', 'delta_norm2_g': 'delta_w', 'delta_ada_w': 'delta_w', 'delta_ada_b': 'delta_w', 'delta_rw_mix': 'delta_w', 'delta_rw_wr': 'delta_w', 'delta_rw_wk': 'delta_w', 'delta_rw_wv': 'delta_w', 'delta_rw_wo': 'delta_w', 'delta_rw_w0': 'delta_w', 'delta_rw_w1': 'delta_w', 'delta_rw_w2': 'delta_w', 'delta_rw_a0': 'delta_w', 'delta_rw_a1': 'delta_w', 'delta_rw_a2': 'delta_w', 'delta_rw_g1': 'delta_w', 'delta_rw_g2': 'delta_w', 'delta_rw_kk': 'delta_w', 'delta_rw_ka': 'delta_w', 'delta_rw_rk': 'delta_w', 'delta_rw_lnw': 'delta_w', 'delta_rw_lnb': 'delta_w', 'delta_sc_win': 'delta_w', 'delta_sc_conv': 'delta_w', 'delta_sc_wout': 'delta_w', 'delta_ffn_w13': 'delta_w', 'delta_ffn_w2': 'delta_w', 'delta_final_g': 'delta_w', 'new_m_c_ctx': 'new_m', 'new_m_norm1_g': 'new_m', 'new_m_norm2_g': 'new_m', 'new_m_ada_w': 'new_m', 'new_m_ada_b': 'new_m', 'new_m_rw_mix': 'new_m', 'new_m_rw_wr': 'new_m', 'new_m_rw_wk': 'new_m', 'new_m_rw_wv': 'new_m', 'new_m_rw_wo': 'new_m', 'new_m_rw_w0': 'new_m', 'new_m_rw_w1': 'new_m', 'new_m_rw_w2': 'new_m', 'new_m_rw_a0': 'new_m', 'new_m_rw_a1': 'new_m', 'new_m_rw_a2': 'new_m', 'new_m_rw_g1': 'new_m', 'new_m_rw_g2': 'new_m', 'new_m_rw_kk': 'new_m', 'new_m_rw_ka': 'new_m', 'new_m_rw_rk': 'new_m', 'new_m_rw_lnw': 'new_m', 'new_m_rw_lnb': 'new_m', 'new_m_sc_win': 'new_m', 'new_m_sc_conv': 'new_m', 'new_m_sc_wout': 'new_m', 'new_m_ffn_w13': 'new_m', 'new_m_ffn_w2': 'new_m', 'new_m_final_g': 'new_m', 'new_v_c_ctx': 'new_v', 'new_v_norm1_g': 'new_v', 'new_v_norm2_g': 'new_v', 'new_v_ada_w': 'new_v', 'new_v_ada_b': 'new_v', 'new_v_rw_mix': 'new_v', 'new_v_rw_wr': 'new_v', 'new_v_rw_wk': 'new_v', 'new_v_rw_wv': 'new_v', 'new_v_rw_wo': 'new_v', 'new_v_rw_w0': 'new_v', 'new_v_rw_w1': 'new_v', 'new_v_rw_w2': 'new_v', 'new_v_rw_a0': 'new_v', 'new_v_rw_a1': 'new_v', 'new_v_rw_a2': 'new_v', 'new_v_rw_g1': 'new_v', 'new_v_rw_g2': 'new_v', 'new_v_rw_kk': 'new_v', 'new_v_rw_ka': 'new_v', 'new_v_rw_rk': 'new_v', 'new_v_rw_lnw': 'new_v', 'new_v_rw_lnb': 'new_v', 'new_v_sc_win': 'new_v', 'new_v_sc_conv': 'new_v', 'new_v_sc_wout': 'new_v', 'new_v_ffn_w13': 'new_v', 'new_v_ffn_w2': 'new_v', 'new_v_final_g': 'new_v'}


def _forward(args):
    return _fwd_reference(*[args[k] for k in FWD_PARAMS])


def _output_shape():
    out = _jax.eval_shape(lambda: _forward(_fwd_setup_inputs(0)))
    return out.shape, out.dtype

N_MICROBATCH = 1
ADAM_LR = 0.001
ADAM_B1 = 0.9
ADAM_B2 = 0.999
ADAM_EPS = 1e-08
ADAM_WD = 0.01
ADAM_STEP = 10
PER_EXAMPLE_BATCH_AXIS = {'x': 0, 'c': 0, 'ctx': 0, 'loss_target': 0}
SHARED_INPUTS = []
_WEIGHT_DTYPES = {'c_ctx': _jnp.float32, 'norm1_g': _jnp.float32, 'norm2_g': _jnp.float32, 'ada_w': _jnp.float32, 'ada_b': _jnp.float32, 'rw_mix': _jnp.float32, 'rw_wr': _jnp.float32, 'rw_wk': _jnp.float32, 'rw_wv': _jnp.float32, 'rw_wo': _jnp.float32, 'rw_w0': _jnp.float32, 'rw_w1': _jnp.float32, 'rw_w2': _jnp.float32, 'rw_a0': _jnp.float32, 'rw_a1': _jnp.float32, 'rw_a2': _jnp.float32, 'rw_g1': _jnp.float32, 'rw_g2': _jnp.float32, 'rw_kk': _jnp.float32, 'rw_ka': _jnp.float32, 'rw_rk': _jnp.float32, 'rw_lnw': _jnp.float32, 'rw_lnb': _jnp.float32, 'sc_win': _jnp.float32, 'sc_conv': _jnp.float32, 'sc_wout': _jnp.float32, 'ffn_w13': _jnp.float32, 'ffn_w2': _jnp.float32, 'final_g': _jnp.float32}
MOMENT_SCALE = {'c_ctx': 3.575477e-03, 'norm1_g': 3.097563e-02, 'norm2_g': 1.880152e-02, 'ada_w': 2.417651e-02, 'ada_b': 4.129922e-02, 'rw_mix': 1.647983e-02, 'rw_wr': 1.409302e-02, 'rw_wk': 1.508504e-02, 'rw_wv': 1.377170e-02, 'rw_wo': 1.364764e-02, 'rw_w0': 2.814628e-03, 'rw_w1': 1.367154e-03, 'rw_w2': 6.334486e-04, 'rw_a0': 3.337055e-03, 'rw_a1': 8.463818e-03, 'rw_a2': 3.652007e-03, 'rw_g1': 1.415125e-02, 'rw_g2': 1.344733e-02, 'rw_kk': 2.875442e-02, 'rw_ka': 2.531381e-02, 'rw_rk': 6.786014e-02, 'rw_lnw': 1.154062e-02, 'rw_lnb': 1.160903e-02, 'sc_win': 2.244582e-02, 'sc_conv': 2.293244e-02, 'sc_wout': 2.245223e-02, 'ffn_w13': 8.307957e-03, 'ffn_w2': 1.355246e-02, 'final_g': 8.030362e+00}


def _to_microbatches(a, axis):
    t = _jnp.moveaxis(a, axis, 0)
    t = t.reshape((N_MICROBATCH, t.shape[0] // N_MICROBATCH) + t.shape[1:])
    return _jnp.moveaxis(t, 1, axis + 1)


def setup_inputs(seed: int = 0) -> dict:
    inp = _fwd_setup_inputs(seed)
    key = _jax.random.fold_in(_jax.random.key(seed), 7919)
    shape, _ = _output_shape()
    out = dict(inp)
    out["loss_target"] = _jax.random.normal(_jax.random.fold_in(key, 0), shape, _jnp.float32)
    for i, name in enumerate(TWIN_WEIGHTS):
        w = inp[name].astype(_jnp.float32)
        if MOMENT_SCALE is None:
            s = _jnp.sqrt(_jnp.mean(_jnp.square(w)) + 1e-30)
        else:
            s = MOMENT_SCALE[name]
        km, kv = _jax.random.split(_jax.random.fold_in(key, i + 1))
        out[name] = w
        out["m_" + name] = s * _jax.random.normal(km, w.shape, _jnp.float32)
        out["v_" + name] = (s * s) * _jax.random.uniform(kv, w.shape, _jnp.float32, 0.5, 1.5)
    if N_MICROBATCH > 1:
        for name, axis in PER_EXAMPLE_BATCH_AXIS.items():
            out[name] = _to_microbatches(out[name], axis)
    return {'x': out['x'], 'c': out['c'], 'ctx': out['ctx'], 'c_ctx': out['c_ctx'], 'norm1_g': out['norm1_g'], 'norm2_g': out['norm2_g'], 'ada_w': out['ada_w'], 'ada_b': out['ada_b'], 'rw_mix': out['rw_mix'], 'rw_wr': out['rw_wr'], 'rw_wk': out['rw_wk'], 'rw_wv': out['rw_wv'], 'rw_wo': out['rw_wo'], 'rw_w0': out['rw_w0'], 'rw_w1': out['rw_w1'], 'rw_w2': out['rw_w2'], 'rw_a0': out['rw_a0'], 'rw_a1': out['rw_a1'], 'rw_a2': out['rw_a2'], 'rw_g1': out['rw_g1'], 'rw_g2': out['rw_g2'], 'rw_kk': out['rw_kk'], 'rw_ka': out['rw_ka'], 'rw_rk': out['rw_rk'], 'rw_lnw': out['rw_lnw'], 'rw_lnb': out['rw_lnb'], 'sc_win': out['sc_win'], 'sc_conv': out['sc_conv'], 'sc_wout': out['sc_wout'], 'ffn_w13': out['ffn_w13'], 'ffn_w2': out['ffn_w2'], 'final_g': out['final_g'], 'loss_target': out['loss_target'], 'm_c_ctx': out['m_c_ctx'], 'm_norm1_g': out['m_norm1_g'], 'm_norm2_g': out['m_norm2_g'], 'm_ada_w': out['m_ada_w'], 'm_ada_b': out['m_ada_b'], 'm_rw_mix': out['m_rw_mix'], 'm_rw_wr': out['m_rw_wr'], 'm_rw_wk': out['m_rw_wk'], 'm_rw_wv': out['m_rw_wv'], 'm_rw_wo': out['m_rw_wo'], 'm_rw_w0': out['m_rw_w0'], 'm_rw_w1': out['m_rw_w1'], 'm_rw_w2': out['m_rw_w2'], 'm_rw_a0': out['m_rw_a0'], 'm_rw_a1': out['m_rw_a1'], 'm_rw_a2': out['m_rw_a2'], 'm_rw_g1': out['m_rw_g1'], 'm_rw_g2': out['m_rw_g2'], 'm_rw_kk': out['m_rw_kk'], 'm_rw_ka': out['m_rw_ka'], 'm_rw_rk': out['m_rw_rk'], 'm_rw_lnw': out['m_rw_lnw'], 'm_rw_lnb': out['m_rw_lnb'], 'm_sc_win': out['m_sc_win'], 'm_sc_conv': out['m_sc_conv'], 'm_sc_wout': out['m_sc_wout'], 'm_ffn_w13': out['m_ffn_w13'], 'm_ffn_w2': out['m_ffn_w2'], 'm_final_g': out['m_final_g'], 'v_c_ctx': out['v_c_ctx'], 'v_norm1_g': out['v_norm1_g'], 'v_norm2_g': out['v_norm2_g'], 'v_ada_w': out['v_ada_w'], 'v_ada_b': out['v_ada_b'], 'v_rw_mix': out['v_rw_mix'], 'v_rw_wr': out['v_rw_wr'], 'v_rw_wk': out['v_rw_wk'], 'v_rw_wv': out['v_rw_wv'], 'v_rw_wo': out['v_rw_wo'], 'v_rw_w0': out['v_rw_w0'], 'v_rw_w1': out['v_rw_w1'], 'v_rw_w2': out['v_rw_w2'], 'v_rw_a0': out['v_rw_a0'], 'v_rw_a1': out['v_rw_a1'], 'v_rw_a2': out['v_rw_a2'], 'v_rw_g1': out['v_rw_g1'], 'v_rw_g2': out['v_rw_g2'], 'v_rw_kk': out['v_rw_kk'], 'v_rw_ka': out['v_rw_ka'], 'v_rw_rk': out['v_rw_rk'], 'v_rw_lnw': out['v_rw_lnw'], 'v_rw_lnb': out['v_rw_lnb'], 'v_sc_win': out['v_sc_win'], 'v_sc_conv': out['v_sc_conv'], 'v_sc_wout': out['v_sc_wout'], 'v_ffn_w13': out['v_ffn_w13'], 'v_ffn_w2': out['v_ffn_w2'], 'v_final_g': out['v_final_g']}


def _loss(weights, diff, rest, loss_target):
    with _jax.named_scope("forward"):
        args = {**rest, TWIN_DIFF_INPUT: diff, **{k: w.astype(_WEIGHT_DTYPES[k]) for k, w in weights.items()}}
        y = _forward(args)
    with _jax.named_scope("loss_head"):
        err = _jnp.square(y.astype(_jnp.float32) - loss_target)
        return 0.5 * _jnp.sum(_jnp.mean(err, axis=-1)) if err.ndim else 0.5 * err


def _adamw(w, g, m, v):
    m = ADAM_B1 * m + (1.0 - ADAM_B1) * g
    v = ADAM_B2 * v + (1.0 - ADAM_B2) * _jnp.square(g)
    m_hat = m / (1.0 - ADAM_B1 ** ADAM_STEP)
    v_hat = v / (1.0 - ADAM_B2 ** ADAM_STEP)
    delta = -ADAM_LR * (m_hat / (_jnp.sqrt(v_hat) + ADAM_EPS) + ADAM_WD * w)
    return delta, m, v


def reference(x, c, ctx, c_ctx, norm1_g, norm2_g, ada_w, ada_b, rw_mix, rw_wr, rw_wk, rw_wv, rw_wo, rw_w0, rw_w1, rw_w2, rw_a0, rw_a1, rw_a2, rw_g1, rw_g2, rw_kk, rw_ka, rw_rk, rw_lnw, rw_lnb, sc_win, sc_conv, sc_wout, ffn_w13, ffn_w2, final_g, loss_target, m_c_ctx, m_norm1_g, m_norm2_g, m_ada_w, m_ada_b, m_rw_mix, m_rw_wr, m_rw_wk, m_rw_wv, m_rw_wo, m_rw_w0, m_rw_w1, m_rw_w2, m_rw_a0, m_rw_a1, m_rw_a2, m_rw_g1, m_rw_g2, m_rw_kk, m_rw_ka, m_rw_rk, m_rw_lnw, m_rw_lnb, m_sc_win, m_sc_conv, m_sc_wout, m_ffn_w13, m_ffn_w2, m_final_g, v_c_ctx, v_norm1_g, v_norm2_g, v_ada_w, v_ada_b, v_rw_mix, v_rw_wr, v_rw_wk, v_rw_wv, v_rw_wo, v_rw_w0, v_rw_w1, v_rw_w2, v_rw_a0, v_rw_a1, v_rw_a2, v_rw_g1, v_rw_g2, v_rw_kk, v_rw_ka, v_rw_rk, v_rw_lnw, v_rw_lnb, v_sc_win, v_sc_conv, v_sc_wout, v_ffn_w13, v_ffn_w2, v_final_g):
    given = dict(x=x, c=c, ctx=ctx, c_ctx=c_ctx, norm1_g=norm1_g, norm2_g=norm2_g, ada_w=ada_w, ada_b=ada_b, rw_mix=rw_mix, rw_wr=rw_wr, rw_wk=rw_wk, rw_wv=rw_wv, rw_wo=rw_wo, rw_w0=rw_w0, rw_w1=rw_w1, rw_w2=rw_w2, rw_a0=rw_a0, rw_a1=rw_a1, rw_a2=rw_a2, rw_g1=rw_g1, rw_g2=rw_g2, rw_kk=rw_kk, rw_ka=rw_ka, rw_rk=rw_rk, rw_lnw=rw_lnw, rw_lnb=rw_lnb, sc_win=sc_win, sc_conv=sc_conv, sc_wout=sc_wout, ffn_w13=ffn_w13, ffn_w2=ffn_w2, final_g=final_g, loss_target=loss_target, m_c_ctx=m_c_ctx, m_norm1_g=m_norm1_g, m_norm2_g=m_norm2_g, m_ada_w=m_ada_w, m_ada_b=m_ada_b, m_rw_mix=m_rw_mix, m_rw_wr=m_rw_wr, m_rw_wk=m_rw_wk, m_rw_wv=m_rw_wv, m_rw_wo=m_rw_wo, m_rw_w0=m_rw_w0, m_rw_w1=m_rw_w1, m_rw_w2=m_rw_w2, m_rw_a0=m_rw_a0, m_rw_a1=m_rw_a1, m_rw_a2=m_rw_a2, m_rw_g1=m_rw_g1, m_rw_g2=m_rw_g2, m_rw_kk=m_rw_kk, m_rw_ka=m_rw_ka, m_rw_rk=m_rw_rk, m_rw_lnw=m_rw_lnw, m_rw_lnb=m_rw_lnb, m_sc_win=m_sc_win, m_sc_conv=m_sc_conv, m_sc_wout=m_sc_wout, m_ffn_w13=m_ffn_w13, m_ffn_w2=m_ffn_w2, m_final_g=m_final_g, v_c_ctx=v_c_ctx, v_norm1_g=v_norm1_g, v_norm2_g=v_norm2_g, v_ada_w=v_ada_w, v_ada_b=v_ada_b, v_rw_mix=v_rw_mix, v_rw_wr=v_rw_wr, v_rw_wk=v_rw_wk, v_rw_wv=v_rw_wv, v_rw_wo=v_rw_wo, v_rw_w0=v_rw_w0, v_rw_w1=v_rw_w1, v_rw_w2=v_rw_w2, v_rw_a0=v_rw_a0, v_rw_a1=v_rw_a1, v_rw_a2=v_rw_a2, v_rw_g1=v_rw_g1, v_rw_g2=v_rw_g2, v_rw_kk=v_rw_kk, v_rw_ka=v_rw_ka, v_rw_rk=v_rw_rk, v_rw_lnw=v_rw_lnw, v_rw_lnb=v_rw_lnb, v_sc_win=v_sc_win, v_sc_conv=v_sc_conv, v_sc_wout=v_sc_wout, v_ffn_w13=v_ffn_w13, v_ffn_w2=v_ffn_w2, v_final_g=v_final_g)
    weights = {n: given[n] for n in TWIN_WEIGHTS}
    shared = {n: given[n] for n in SHARED_INPUTS}
    per_example = {n: given[n] for n in ['x', 'c', 'ctx']}
    grad_fn = _jax.value_and_grad(_loss, argnums=(0, 1))

    def one_microbatch(ex, loss_target):
        ex = dict(ex)
        diff = ex.pop(TWIN_DIFF_INPUT)
        return grad_fn(weights, diff, {**shared, **ex}, loss_target)

    if N_MICROBATCH == 1:
        loss, (grad_w, grad_x) = one_microbatch(per_example, given["loss_target"])
    else:
        def body(carry, xs):
            loss_sum, grad_sum = carry
            l_k, (gw_k, gx_k) = one_microbatch(xs[0], xs[1])
            with _jax.named_scope("update"):
                return (loss_sum + l_k, _jax.tree.map(_jnp.add, grad_sum, gw_k)), gx_k

        init = (_jnp.zeros((), _jnp.float32), _jax.tree.map(_jnp.zeros_like, weights))
        (loss, grad_w), grad_x = _jax.lax.scan(body, init, (per_example, given["loss_target"]))
    with _jax.named_scope("update"):
        delta_w, new_m, new_v = {}, {}, {}
        for n in TWIN_WEIGHTS:
            delta_w[n], new_m[n], new_v[n] = _adamw(weights[n], grad_w[n], given["m_" + n], given["v_" + n])
    return (loss, grad_x, *[grad_w[n] for n in TWIN_WEIGHTS], *[delta_w[n] for n in TWIN_WEIGHTS],
            *[new_m[n] for n in TWIN_WEIGHTS], *[new_v[n] for n in TWIN_WEIGHTS])
```

```python
import functools
import math

import numpy as np
import jax
import jax.numpy as jnp
from jax import lax
from jax.experimental import pallas as pl
from jax.experimental.pallas import tpu as pltpu

F32 = jnp.float32
BF16 = jnp.bfloat16

N_DEV = 8
HEAD = 64
LANES = 128
GRID_W = 64
LORA_PAD = 128
NORM_EPS = 1e-6
GN_EPS = 64e-5
ADAM_LR, ADAM_B1, ADAM_B2, ADAM_EPS, ADAM_WD, ADAM_STEP = 0.001, 0.9, 0.999, 1e-08, 0.01, 10
VMEM_LIMIT = 52 * 1024 * 1024
SCAN_CHUNK = 8
HI = lax.Precision.HIGHEST


def _cparams(sem):
    return pltpu.CompilerParams(dimension_semantics=sem, vmem_limit_bytes=VMEM_LIMIT)


def _pick(n, cap, quantum=LANES):
    best = None
    for t in range(quantum, min(n, cap) + 1, quantum):
        if n % t == 0:
            best = t
    return n if best is None else best


def _mm(a, b, *, ta=False, tb=False, out_dtype, name):
    if ta:
        K, M = a.shape
    else:
        M, K = a.shape
    if tb:
        N, Kb = b.shape
    else:
        Kb, N = b.shape
    assert K == Kb, (a.shape, b.shape, ta, tb)
    tm = _pick(M, 1024)
    tn = _pick(N, 1024 if jnp.dtype(out_dtype).itemsize == 2 else 512)
    tk = _pick(K, 512)
    nk = K // tk
    dims = (((0 if ta else 1,), (1 if tb else 0,)), ((), ()))

    def body(a_ref, b_ref, o_ref, acc_ref):
        k = pl.program_id(2)

        @pl.when(k == 0)
        def _():
            acc_ref[...] = jnp.zeros_like(acc_ref)

        acc_ref[...] += lax.dot_general(a_ref[...].astype(BF16), b_ref[...].astype(BF16), dims,
                                        preferred_element_type=F32)

        @pl.when(k == nk - 1)
        def _():
            o_ref[...] = acc_ref[...].astype(o_ref.dtype)

    a_spec = pl.BlockSpec((tk, tm), lambda i, j, k: (k, i)) if ta else pl.BlockSpec((tm, tk), lambda i, j, k: (i, k))
    b_spec = pl.BlockSpec((tn, tk), lambda i, j, k: (j, k)) if tb else pl.BlockSpec((tk, tn), lambda i, j, k: (k, j))
    return pl.pallas_call(
        body, name=name, grid=(M // tm, N // tn, nk),
        in_specs=[a_spec, b_spec],
        out_specs=pl.BlockSpec((tm, tn), lambda i, j, k: (i, j)),
        out_shape=jax.ShapeDtypeStruct((M, N), out_dtype),
        scratch_shapes=[pltpu.VMEM((tm, tn), F32)],
        compiler_params=_cparams(("parallel", "parallel", "arbitrary")),
    )(a, b)


@functools.partial(jax.custom_vjp, nondiff_argnums=(2, 3))
def linear(a, w, out_dtype, name):
    return _mm(a, w, out_dtype=out_dtype, name=name + "_fwd")


def _linear_fwd(a, w, out_dtype, name):
    return _mm(a, w, out_dtype=out_dtype, name=name + "_fwd"), (a, w)


def _linear_bwd(out_dtype, name, res, g):
    a, w = res
    da = _mm(g, w, tb=True, out_dtype=a.dtype, name=name + "_da")
    dw = _mm(a, g, ta=True, out_dtype=w.dtype, name=name + "_dw")
    return da, dw


linear.defvjp(_linear_fwd, _linear_bwd)


def _rw_specs(tiles, col_offs, vecs, consts, tr, tc, nb0):
    tile_specs = [pl.BlockSpec((tr, tc), functools.partial(lambda j, i, off: (i, j + off), off=off))
                  for _, off in zip(tiles, col_offs)]

    def vec_map(S):
        if S == 1:
            return lambda j, i: (0, 0, j)
        return lambda j, i: (jnp.where(i < nb0, 0, 1), 0, j)

    vec_specs = [pl.BlockSpec((None, 1, tc), vec_map(v.shape[0])) for v in vecs]
    const_specs = [pl.BlockSpec(c.shape, lambda j, i: (0, 0)) for c in consts]
    return tile_specs, vec_specs, const_specs


def _rw_forward(name, f, tiles, col_offs, vecs, consts, out_dtypes, tr, tc, nb0, width):
    n = tiles[0].shape[0]
    nt, nv, nc = len(tiles), len(vecs), len(consts)
    tile_specs, vec_specs, const_specs = _rw_specs(tiles, col_offs, vecs, consts, tr, tc, nb0)

    def body(*refs):
        ins = [r[...].astype(F32) for r in refs[:nt]] + [r[...] for r in refs[nt:nt + nv + nc]]
        outs = f(*ins)
        for o_ref, o in zip(refs[nt + nv + nc:], outs):
            o_ref[...] = o.astype(o_ref.dtype)

    return pl.pallas_call(
        body, name=name + "_fwd", grid=(width // tc, n // tr),
        in_specs=tile_specs + vec_specs + const_specs,
        out_specs=[pl.BlockSpec((tr, tc), lambda j, i: (i, j)) for _ in out_dtypes],
        out_shape=[jax.ShapeDtypeStruct((n, width), dt) for dt in out_dtypes],
        compiler_params=_cparams(("parallel", "parallel")),
    )(*tiles, *vecs, *consts)


def _rw_backward(name, f, tiles, col_offs, vecs, consts, douts, tr, tc, nb0, width):
    n = tiles[0].shape[0]
    nt, nv, nc, no = len(tiles), len(vecs), len(consts), len(douts)
    tile_specs, vec_specs, const_specs = _rw_specs(tiles, col_offs, vecs, consts, tr, tc, nb0)

    def body(*refs):
        t_in = [r[...].astype(F32) for r in refs[:nt]]
        v_in = [r[...] for r in refs[nt:nt + nv]]
        c_in = [r[...] for r in refs[nt + nv:nt + nv + nc]]
        d_in = tuple(r[...].astype(F32) for r in refs[nt + nv + nc:nt + nv + nc + no])
        o_refs = refs[nt + nv + nc + no:]
        _, vjp = jax.vjp(lambda *tv: tuple(f(*tv, *c_in)), *t_in, *v_in)
        grads = vjp(d_in)
        for o_ref, g in zip(o_refs[:nt], grads[:nt]):
            o_ref[...] = g.astype(o_ref.dtype)
        i = pl.program_id(1)
        for o_ref, g, v in zip(o_refs[nt:], grads[nt:], vecs):
            first = jnp.logical_or(i == 0, i == nb0) if v.shape[0] == 2 else i == 0

            @pl.when(first)
            def _(o_ref=o_ref, g=g):
                o_ref[...] = g

            @pl.when(jnp.logical_not(first))
            def _(o_ref=o_ref, g=g):
                o_ref[...] += g

    dout_specs = [pl.BlockSpec((tr, tc), lambda j, i: (i, j)) for _ in douts]
    out_specs = [pl.BlockSpec((tr, tc), lambda j, i: (i, j)) for _ in tiles] + list(vec_specs)
    out_shape = ([jax.ShapeDtypeStruct((n, width), t.dtype) for t in tiles]
                 + [jax.ShapeDtypeStruct(v.shape, F32) for v in vecs])
    return pl.pallas_call(
        body, name=name + "_bwd", grid=(width // tc, n // tr),
        in_specs=tile_specs + vec_specs + const_specs + dout_specs,
        out_specs=out_specs, out_shape=out_shape,
        compiler_params=_cparams(("parallel", "arbitrary")),
    )(*tiles, *vecs, *consts, *douts)


def make_rowwise(name, f, out_dtypes, tr, tc, consts=(), nb0=-1):
    consts = tuple(consts)

    @jax.custom_vjp
    def op(tiles, vecs):
        w = tiles[0].shape[1]
        return tuple(_rw_forward(name, f, tiles, (0,) * len(tiles), vecs, consts, out_dtypes, tr, min(tc, w), nb0, w))

    def op_fwd(tiles, vecs):
        return op(tiles, vecs), (tiles, vecs)

    def op_bwd(res, douts):
        tiles, vecs = res
        w = tiles[0].shape[1]
        g = _rw_backward(name, f, tiles, (0,) * len(tiles), vecs, consts, tuple(douts), tr, min(tc, w), nb0, w)
        return tuple(g[:len(tiles)]), tuple(g[len(tiles):])

    op.defvjp(op_fwd, op_bwd)
    return op


def _f_norm_mod(x, g, sh, sc):
    hn = x * lax.rsqrt(jnp.mean(x * x, axis=-1, keepdims=True) + NORM_EPS)
    return ((hn * g) * (1.0 + sc) + sh,)


def _f_res_norm_mod(x, y, gate, g, sh, sc):
    x1 = x + gate * y
    hn = x1 * lax.rsqrt(jnp.mean(x1 * x1, axis=-1, keepdims=True) + NORM_EPS)
    return x1, (hn * g) * (1.0 + sc) + sh


def _head_sum(t, gmat):
    return jnp.dot(t, gmat, precision=HI, preferred_element_type=F32)


def _f_prep(k, lw0, lw1, la0, la1, kkp, kap, w00, w01, a00, a01, gmat):
    t = k * kkp
    kk = t / jnp.maximum(jnp.sqrt(_head_sum(t * t, gmat)), 1e-12)
    outs = [kk]
    decs, kds, sigs = [], [], []
    for lw, la, w0, a0 in ((lw0, la0, w00, a00), (lw1, la1, w01, a01)):
        decs.append(jnp.exp(-jax.nn.sigmoid(w0 + lw) * float(np.exp(-0.5))))
        a = jax.nn.sigmoid(a0 + la)
        sigs.append(a)
        kds.append(k * (1.0 + (a - 1.0) * kap))
    return tuple(outs + decs + kds + sigs)


def _f_readout(y0, y1, r, kd0, kd1, v, g, rk, lnw, lnb, gmat):
    y = y0 + y1
    mu = _head_sum(y, gmat) * (1.0 / HEAD)
    d = y - mu
    var = _head_sum(d * d, gmat) * (1.0 / HEAD)
    o = d * lax.rsqrt(var + GN_EPS) * lnw + lnb
    bonus = _head_sum(r * (kd0 + kd1) * rk, gmat) * v
    return ((o + bonus) * g,)


def _f_swiglu(a, b):
    return (jax.nn.silu(a) * b,)


def swiglu_act(ab, name):
    t, f2 = ab.shape
    fdim = f2 // 2
    tr, tc = _pick(t, 512, 8), _pick(fdim, 512)
    offs = (0, fdim // tc)

    @jax.custom_vjp
    def op(ab_):
        return _rw_forward(name, _f_swiglu, (ab_, ab_), offs, (), (), (BF16,), tr, tc, -1, fdim)[0]

    def op_fwd(ab_):
        return op(ab_), ab_

    def op_bwd(ab_, dact):
        da, db = _rw_backward(name, _f_swiglu, (ab_, ab_), offs, (), (), (dact,), tr, tc, -1, fdim)
        return (jnp.concatenate([da, db], axis=1),)

    op.defvjp(op_fwd, op_bwd)
    return op(ab)


def _row_iota(n, tc):
    return lax.broadcasted_iota(jnp.int32, (n, tc), 0)


def _shift_rows(x, s, keep):
    n = x.shape[0]
    return jnp.where(keep, pltpu.roll(x, s % n, 0), 0.0)


def _unshift_rows(d, s, keep):
    n = d.shape[0]
    return pltpu.roll(jnp.where(keep, d, 0.0), (-s) % n, 0)


def _ctx_shift_spec(L, tc, quarter):
    row = _row_iota(L, tc)
    if quarter < 2:
        return 1, row >= 1
    return -1, row < L - 1


def _grid_shift_spec(T, tc, quarter):
    row = _row_iota(T, tc)
    col = jnp.bitwise_and(row, GRID_W - 1)
    if quarter == 0:
        return 1, col != 0
    if quarter == 1:
        return -1, col != GRID_W - 1
    if quarter == 2:
        return GRID_W, row >= GRID_W
    return -GRID_W, row < T - GRID_W


def _shift_mix_fwd_call(h, mix3, L):
    n, d = h.shape
    T = n - L
    tc = _pick(d // 4, 256)
    nq = (d // 4) // tc

    def body(h_ref, mix_ref, *o_refs):
        q = pl.program_id(0) // nq
        for quarter in range(4):
            @pl.when(q == quarter)
            def _(quarter=quarter):
                for lo, cnt, spec in ((0, L, _ctx_shift_spec), (L, T, _grid_shift_spec)):
                    hh = h_ref[pl.ds(lo, cnt), :]
                    s, keep = spec(cnt, tc, quarter)
                    xx = _shift_rows(hh, s, keep) - hh
                    for m in range(6):
                        o_refs[m][pl.ds(lo, cnt), :] = (hh + xx * mix_ref[m]).astype(BF16)

    return pl.pallas_call(
        body, name="shift_mix_fwd", grid=(d // tc,),
        in_specs=[pl.BlockSpec((n, tc), lambda j: (0, j)), pl.BlockSpec((6, 1, tc), lambda j: (0, 0, j))],
        out_specs=[pl.BlockSpec((n, tc), lambda j: (0, j)) for _ in range(6)],
        out_shape=[jax.ShapeDtypeStruct((n, d), BF16) for _ in range(6)],
        compiler_params=_cparams(("parallel",)),
    )(h, mix3)


def _shift_mix_bwd_call(h, mix3, douts, L):
    n, d = h.shape
    T = n - L
    tc = _pick(d // 4, 256)
    nq = (d // 4) // tc

    def body(h_ref, mix_ref, d0, d1, d2, d3, d4, d5, dh_ref, dmix_ref):
        d_refs = (d0, d1, d2, d3, d4, d5)
        q = pl.program_id(0) // nq
        for quarter in range(4):
            @pl.when(q == quarter)
            def _(quarter=quarter):
                dmix = [jnp.zeros((1, tc), F32) for _ in range(6)]
                for lo, cnt, spec in ((0, L, _ctx_shift_spec), (L, T, _grid_shift_spec)):
                    hh = h_ref[pl.ds(lo, cnt), :]
                    s, keep = spec(cnt, tc, quarter)
                    xx = _shift_rows(hh, s, keep) - hh
                    direct = jnp.zeros((cnt, tc), F32)
                    shifted = jnp.zeros((cnt, tc), F32)
                    for m in range(6):
                        dm = d_refs[m][pl.ds(lo, cnt), :].astype(F32)
                        mx = mix_ref[m]
                        direct = direct + dm * (1.0 - mx)
                        shifted = shifted + dm * mx
                        dmix[m] = dmix[m] + jnp.sum(dm * xx, axis=0, keepdims=True)
                    dh_ref[pl.ds(lo, cnt), :] = direct + _unshift_rows(shifted, s, keep)
                for m in range(6):
                    dmix_ref[m] = dmix[m]

    tile = pl.BlockSpec((n, tc), lambda j: (0, j))
    return pl.pallas_call(
        body, name="shift_mix_bwd", grid=(d // tc,),
        in_specs=[tile, pl.BlockSpec((6, 1, tc), lambda j: (0, 0, j))] + [tile] * 6,
        out_specs=[tile, pl.BlockSpec((6, 1, tc), lambda j: (0, 0, j))],
        out_shape=[jax.ShapeDtypeStruct((n, d), F32), jax.ShapeDtypeStruct((6, 1, d), F32)],
        compiler_params=_cparams(("parallel",)),
    )(h, mix3, *douts)


@functools.partial(jax.custom_vjp, nondiff_argnums=(2,))
def shift_mix(h, mix3, L):
    return tuple(_shift_mix_fwd_call(h, mix3, L))


def _shift_mix_fwd(h, mix3, L):
    return tuple(_shift_mix_fwd_call(h, mix3, L)), (h, mix3)


def _shift_mix_bwd(L, res, douts):
    h, mix3 = res
    dh, dmix = _shift_mix_bwd_call(h, mix3, tuple(douts), L)
    return dh, dmix


shift_mix.defvjp(_shift_mix_fwd, _shift_mix_bwd)


def _conv_specs(T, d, tc):
    nd = d // tc
    ins = [pl.BlockSpec((T, tc), functools.partial(lambda j, off: (0, j + off), off=o * nd)) for o in range(3)]
    return ins, pl.BlockSpec((3, 1, tc), lambda j: (0, 0, j))


def _conv_terms(gc, u, tc):
    T = gc.shape[0]
    row = _row_iota(T, tc)
    z = gc * u
    return z, _shift_rows(z, 1, row >= 1), _shift_rows(z, -1, row < T - 1), row


def _conv_fwd_call(guc, cw3):
    T, d3 = guc.shape
    d = d3 // 3
    tc = _pick(d, 256)
    ins, wspec = _conv_specs(T, d, tc)

    def body(gb_ref, gc_ref, u_ref, w_ref, p_ref):
        z, zp, zn, _ = _conv_terms(gc_ref[...].astype(F32), u_ref[...].astype(F32), tc)
        conv = zp * w_ref[0] + z * w_ref[1] + zn * w_ref[2]
        p_ref[...] = (gb_ref[...].astype(F32) * conv).astype(BF16)

    return pl.pallas_call(
        body, name="conv_fwd", grid=(d // tc,), in_specs=ins + [wspec],
        out_specs=pl.BlockSpec((T, tc), lambda j: (0, j)),
        out_shape=jax.ShapeDtypeStruct((T, d), BF16),
        compiler_params=_cparams(("parallel",)),
    )(guc, guc, guc, cw3)


def _conv_bwd_call(guc, cw3, dp):
    T, d3 = guc.shape
    d = d3 // 3
    tc = _pick(d, 256)
    ins, wspec = _conv_specs(T, d, tc)
    tile = pl.BlockSpec((T, tc), lambda j: (0, j))

    def body(gb_ref, gc_ref, u_ref, w_ref, dp_ref, dgb_ref, dgc_ref, du_ref, dw_ref):
        gc = gc_ref[...].astype(F32)
        u = u_ref[...].astype(F32)
        z, zp, zn, row = _conv_terms(gc, u, tc)
        conv = zp * w_ref[0] + z * w_ref[1] + zn * w_ref[2]
        dpv = dp_ref[...].astype(F32)
        dgb_ref[...] = (dpv * conv).astype(dgb_ref.dtype)
        dconv = dpv * gb_ref[...].astype(F32)
        dz = (_shift_rows(dconv, -1, row < T - 1) * w_ref[0] + dconv * w_ref[1]
              + _shift_rows(dconv, 1, row >= 1) * w_ref[2])
        dgc_ref[...] = (dz * u).astype(dgc_ref.dtype)
        du_ref[...] = (dz * gc).astype(du_ref.dtype)
        dw_ref[0] = jnp.sum(dconv * zp, axis=0, keepdims=True)
        dw_ref[1] = jnp.sum(dconv * z, axis=0, keepdims=True)
        dw_ref[2] = jnp.sum(dconv * zn, axis=0, keepdims=True)

    return pl.pallas_call(
        body, name="conv_bwd", grid=(d // tc,), in_specs=ins + [wspec, tile],
        out_specs=[tile, tile, tile, wspec],
        out_shape=[jax.ShapeDtypeStruct((T, d), guc.dtype)] * 3 + [jax.ShapeDtypeStruct((3, 1, d), F32)],
        compiler_params=_cparams(("parallel",)),
    )(guc, guc, guc, cw3, dp)


@jax.custom_vjp
def gated_conv(guc, cw3):
    return _conv_fwd_call(guc, cw3)


def _gated_conv_fwd(guc, cw3):
    return _conv_fwd_call(guc, cw3), (guc, cw3)


def _gated_conv_bwd(res, dp):
    guc, cw3 = res
    dgb, dgc, du, dw = _conv_bwd_call(guc, cw3, dp)
    return jnp.concatenate([dgb, dgc, du], axis=1), dw


gated_conv.defvjp(_gated_conv_fwd, _gated_conv_bwd)


def _chunk_map(nchunk, nctx_chunk, reverse):
    if not reverse:
        return lambda c: c
    return lambda c: jnp.where(c < nctx_chunk, nctx_chunk - 1 - c, nchunk - 1 - (c - nctx_chunk))


def _spread(row_ref, dst_scr, lo_mask, ni, C):
    for i in range(ni):
        idx = jnp.where(lo_mask, 2 * i, 2 * i + 1).astype(jnp.int32)
        for tt in range(C):
            dst_scr[tt, i] = jnp.take_along_axis(row_ref[tt], idx, axis=1)


def _half_sums(p, lo_mask):
    lo = jnp.sum(jnp.where(lo_mask, p, 0.0), axis=1, keepdims=True)
    hi = jnp.sum(jnp.where(lo_mask, 0.0, p), axis=1, keepdims=True)
    return lo, hi


def _gather_rows(los, his, lane, nh):
    acc = jnp.zeros((nh, LANES), F32)
    for i, (lo, hi) in enumerate(zip(los, his)):
        acc = acc + jnp.where(lane == 2 * i, lo, 0.0) + jnp.where(lane == 2 * i + 1, hi, 0.0)
    return acc


def _wkv_fwd_call(r2, w2, kd2, kk2, as2, v2, nctx, reverse):
    n, nh, _ = r2.shape
    C = SCAN_CHUNK
    ni = HEAD // 2
    nchunk = n // C
    cmap = _chunk_map(nchunk, nctx // C, reverse)

    def body(r_ref, w_ref, kd_ref, kk_ref, as_ref, v_ref, y_ref, sa_ref, sp_ref, s_scr, vc_scr):
        @pl.when(pl.program_id(0) == 0)
        def _():
            s_scr[...] = jnp.zeros_like(s_scr)

        lane = lax.broadcasted_iota(jnp.int32, (nh, LANES), 1)
        lo_mask = lane < HEAD
        _spread(v_ref, vc_scr, lo_mask, ni, C)

        def step(j, carry):
            t = (C - 1 - j) if reverse else j
            kk = kk_ref[t]
            a2 = -kk
            b2 = kk * as_ref[t]
            w = w_ref[t]
            k = kd_ref[t]
            r = r_ref[t]
            los, his = [], []
            for i in range(ni):
                si = s_scr[i]
                sp_ref[t, i] = si
                lo, hi = _half_sums(si * a2, lo_mask)
                los.append(lo)
                his.append(hi)
            ylos, yhis = [], []
            for i in range(ni):
                sa_i = jnp.where(lo_mask, los[i], his[i])
                sn = s_scr[i] * w + sa_i * b2 + vc_scr[t, i] * k
                s_scr[i] = sn
                lo, hi = _half_sums(sn * r, lo_mask)
                ylos.append(lo)
                yhis.append(hi)
            y_ref[t] = _gather_rows(ylos, yhis, lane, nh)
            sa_ref[t] = _gather_rows(los, his, lane, nh)
            return carry

        lax.fori_loop(0, C, step, 0)

    tok = pl.BlockSpec((C, nh, LANES), lambda c: (cmap(c), 0, 0))
    return pl.pallas_call(
        body, name="wkv_fwd_rev" if reverse else "wkv_fwd", grid=(nchunk,),
        in_specs=[tok] * 6,
        out_specs=[tok, tok, pl.BlockSpec((C, ni, nh, LANES), lambda c: (cmap(c), 0, 0, 0))],
        out_shape=[jax.ShapeDtypeStruct((n, nh, LANES), F32), jax.ShapeDtypeStruct((n, nh, LANES), F32),
                   jax.ShapeDtypeStruct((n, ni, nh, LANES), F32)],
        scratch_shapes=[pltpu.VMEM((ni, nh, LANES), F32), pltpu.VMEM((C, ni, nh, LANES), F32)],
        compiler_params=_cparams(("arbitrary",)),
    )(r2, w2, kd2, kk2, as2, v2)


def _wkv_bwd_call(r2, w2, kd2, kk2, as2, v2, sa, sprev, dy, nctx, reverse):
    n, nh, _ = r2.shape
    C = SCAN_CHUNK
    ni = HEAD // 2
    nchunk = n // C
    fmap = _chunk_map(nchunk, nctx // C, reverse)
    cmap = lambda c: fmap(nchunk - 1 - c)

    def body(r_ref, w_ref, kd_ref, kk_ref, as_ref, v_ref, sa_ref, sp_ref, dy_ref,
             dr_ref, dw_ref, dkd_ref, dkk_ref, das_ref, dv_ref, ds_scr, vc_scr, sac_scr, dyc_scr):
        @pl.when(pl.program_id(0) == 0)
        def _():
            ds_scr[...] = jnp.zeros_like(ds_scr)

        lane = lax.broadcasted_iota(jnp.int32, (nh, LANES), 1)
        lo_mask = lane < HEAD
        _spread(v_ref, vc_scr, lo_mask, ni, C)
        _spread(sa_ref, sac_scr, lo_mask, ni, C)
        _spread(dy_ref, dyc_scr, lo_mask, ni, C)

        def step(j, carry):
            t = j if reverse else (C - 1 - j)
            kk = kk_ref[t]
            sig = as_ref[t]
            a2 = -kk
            b2 = kk * sig
            w = w_ref[t]
            k = kd_ref[t]
            r = r_ref[t]
            zero = jnp.zeros((nh, LANES), F32)
            acc_dk, acc_db, acc_dw, acc_g, acc_sady, acc_vdy, acc_da = zero, zero, zero, zero, zero, zero, zero
            dvlo, dvhi, dsalo, dsahi = [], [], [], []
            for i in range(ni):
                dyc = dyc_scr[t, i]
                sp = sp_ref[t, i]
                vc = vc_scr[t, i]
                sac = sac_scr[t, i]
                ds = ds_scr[i] + dyc * r
                ds_scr[i] = ds
                lo, hi = _half_sums(ds * k, lo_mask)
                dvlo.append(lo)
                dvhi.append(hi)
                lo, hi = _half_sums(ds * b2, lo_mask)
                dsalo.append(lo)
                dsahi.append(hi)
                acc_dk = acc_dk + ds * vc
                acc_db = acc_db + ds * sac
                acc_dw = acc_dw + ds * sp
                acc_g = acc_g + sp * dyc
                acc_sady = acc_sady + sac * dyc
                acc_vdy = acc_vdy + vc * dyc
            for i in range(ni):
                dsa_i = jnp.where(lo_mask, dsalo[i], dsahi[i])
                acc_da = acc_da + sp_ref[t, i] * dsa_i
                ds_scr[i] = ds_scr[i] * w + dsa_i * a2
            dr_ref[t] = acc_g * w + b2 * acc_sady + k * acc_vdy
            dw_ref[t] = acc_dw
            dkd_ref[t] = acc_dk
            dkk_ref[t] = acc_db * sig - acc_da
            das_ref[t] = acc_db * kk
            dv_ref[t] = _gather_rows(dvlo, dvhi, lane, nh)
            return carry

        lax.fori_loop(0, C, step, 0)

    tok = pl.BlockSpec((C, nh, LANES), lambda c: (cmap(c), 0, 0))
    big = pltpu.VMEM((C, ni, nh, LANES), F32)
    return pl.pallas_call(
        body, name="wkv_bwd_rev" if reverse else "wkv_bwd", grid=(nchunk,),
        in_specs=[tok] * 7 + [pl.BlockSpec((C, ni, nh, LANES), lambda c: (cmap(c), 0, 0, 0)), tok],
        out_specs=[tok] * 6,
        out_shape=[jax.ShapeDtypeStruct((n, nh, LANES), F32)] * 6,
        scratch_shapes=[pltpu.VMEM((ni, nh, LANES), F32), big, big, big],
        compiler_params=_cparams(("arbitrary",)),
    )(r2, w2, kd2, kk2, as2, v2, sa, sprev, dy)


@functools.partial(jax.custom_vjp, nondiff_argnums=(6, 7))
def wkv_scan(r2, w2, kd2, kk2, as2, v2, nctx, reverse):
    return _wkv_fwd_call(r2, w2, kd2, kk2, as2, v2, nctx, reverse)[0]


def _wkv_scan_fwd(r2, w2, kd2, kk2, as2, v2, nctx, reverse):
    y, sa, sprev = _wkv_fwd_call(r2, w2, kd2, kk2, as2, v2, nctx, reverse)
    return y, (r2, w2, kd2, kk2, as2, v2, sa, sprev)


def _wkv_scan_bwd(nctx, reverse, res, dy):
    r2, w2, kd2, kk2, as2, v2, sa, sprev = res
    dr, dw, dkd, dkk, das, dv = _wkv_bwd_call(r2, w2, kd2, kk2, as2, v2, sa, sprev, dy, nctx, reverse)
    return dr, dw, dkd, dkk, das, dv


wkv_scan.defvjp(_wkv_scan_fwd, _wkv_scan_bwd)


def _tile_heads(t):
    n, d = t.shape
    th = t.reshape(n, d // HEAD, HEAD)
    return jnp.concatenate([th, th], axis=-1)


def loss_head(x3, fo, tgt, gate, g):
    T, d = x3.shape
    tr = _pick(T, 128, 8)

    def body(x_ref, f_ref, t_ref, gate_ref, g_ref, loss_ref, dx_ref, df_ref, dgate_ref, dg_ref):
        tg = t_ref[...]

        def fl(x, fo_, gate_, g_):
            x4 = x + gate_ * fo_
            y = (x4 * lax.rsqrt(jnp.mean(x4 * x4, axis=-1, keepdims=True) + NORM_EPS)) * g_
            return 0.5 * jnp.sum(jnp.mean(jnp.square(y - tg), axis=-1))

        val, vjp = jax.vjp(fl, x_ref[...], f_ref[...], gate_ref[...], g_ref[...])
        dx, dfo, dgate, dg = vjp(jnp.ones((), F32))
        dx_ref[...] = dx
        df_ref[...] = dfo
        i = pl.program_id(0)

        @pl.when(i == 0)
        def _():
            loss_ref[...] = jnp.zeros_like(loss_ref)
            dgate_ref[...] = jnp.zeros_like(dgate_ref)
            dg_ref[...] = jnp.zeros_like(dg_ref)

        loss_ref[...] += jnp.full(loss_ref.shape, val, F32)
        dgate_ref[...] += dgate
        dg_ref[...] += dg

    tile = pl.BlockSpec((tr, d), lambda i: (i, 0))
    vec = pl.BlockSpec((1, d), lambda i: (0, 0))
    return pl.pallas_call(
        body, name="loss_head", grid=(T // tr,),
        in_specs=[tile, tile, tile, vec, vec],
        out_specs=[pl.BlockSpec((8, LANES), lambda i: (0, 0)), tile, tile, vec, vec],
        out_shape=[jax.ShapeDtypeStruct((8, LANES), F32), jax.ShapeDtypeStruct((T, d), F32),
                   jax.ShapeDtypeStruct((T, d), F32), jax.ShapeDtypeStruct((1, d), F32),
                   jax.ShapeDtypeStruct((1, d), F32)],
        compiler_params=_cparams(("arbitrary",)),
    )(x3, fo, tgt, gate, g)


def sum_adam(parts, w, m, v, name, lead=None):
    P, R, Cc = parts.shape
    tc = _pick(Cc, 1024)
    tr = _pick(R, max(8, (256 * 1024) // tc), 8)

    def body(p_ref, w_ref, m_ref, v_ref, g_ref, d_ref, nm_ref, nv_ref):
        g = p_ref[0].astype(F32)
        for s in range(1, P):
            g = g + p_ref[s].astype(F32)
        m_new = ADAM_B1 * m_ref[...] + (1.0 - ADAM_B1) * g
        v_new = ADAM_B2 * v_ref[...] + (1.0 - ADAM_B2) * jnp.square(g)
        m_hat = m_new / (1.0 - ADAM_B1 ** ADAM_STEP)
        v_hat = v_new / (1.0 - ADAM_B2 ** ADAM_STEP)
        g_ref[...] = g
        d_ref[...] = -ADAM_LR * (m_hat / (jnp.sqrt(v_hat) + ADAM_EPS) + ADAM_WD * w_ref[...])
        nm_ref[...] = m_new
        nv_ref[...] = v_new

    if lead is None:
        pspec = pl.BlockSpec((tr, tc), lambda i, j: (i, j))
    else:
        pspec = pl.BlockSpec((None, tr, tc), lambda i, j: (lead, i, j))
    ospec = pl.BlockSpec((tr, tc), lambda i, j: (i, j))
    return pl.pallas_call(
        body, name=name, grid=(R // tr, Cc // tc),
        in_specs=[pl.BlockSpec((P, tr, tc), lambda i, j: (0, i, j)), pspec, pspec, pspec],
        out_specs=[ospec] * 4,
        out_shape=[jax.ShapeDtypeStruct((R, Cc), F32)] * 4,
        compiler_params=_cparams(("parallel", "parallel")),
    )(parts, w, m, v)


def _me():
    return lax.axis_index("x"), lax.axis_index("y"), lax.axis_index("c")


def _peer(p):
    x, y, c = _me()
    px = 1 - x if p & 4 else x
    py = 1 - y if p & 2 else y
    pc = 1 - c if p & 1 else c
    return (px, py, pc), 4 * px + 2 * py + pc


def _exchange(src_of, dst_of, send_sems, recv_sems, local):
    x, y, c = _me()
    me = 4 * x + 2 * y + c
    local.start()
    sends = []
    for p in range(1, N_DEV):
        dev, idx = _peer(p)
        cp = pltpu.make_async_remote_copy(src_ref=src_of(idx), dst_ref=dst_of(me), send_sem=send_sems.at[p],
                                          recv_sem=recv_sems.at[p], device_id=dev,
                                          device_id_type=pl.DeviceIdType.MESH)
        cp.start()
        sends.append(cp)
    for p in range(1, N_DEV):
        dev, idx = _peer(p)
        pltpu.make_async_remote_copy(src_ref=src_of(idx), dst_ref=dst_of(idx), send_sem=send_sems.at[p],
                                     recv_sem=recv_sems.at[p], device_id=dev,
                                     device_id_type=pl.DeviceIdType.MESH).wait_recv()
    for cp in sends:
        cp.wait_send()
    local.wait()


_SEMS = [pltpu.SemaphoreType.DMA((N_DEV,)), pltpu.SemaphoreType.DMA((N_DEV,)), pltpu.SemaphoreType.DMA]
_ANY = pl.BlockSpec(memory_space=pl.ANY)


def all_gather(x, axis, name):
    r, c = x.shape
    shape = (N_DEV * r, c) if axis == 0 else (r, N_DEV * c)

    def body(x_ref, o_ref, send_sems, recv_sems, local_sem):
        def slot(idx):
            if axis == 0:
                return o_ref.at[pl.ds(idx * r, r), :]
            return o_ref.at[:, pl.ds(idx * c, c)]

        x_, y_, c_ = _me()
        me = 4 * x_ + 2 * y_ + c_
        local = pltpu.make_async_copy(x_ref, slot(me), local_sem)
        _exchange(lambda idx: x_ref, slot, send_sems, recv_sems, local)

    return pl.pallas_call(
        body, name=name, in_specs=[_ANY], out_specs=_ANY,
        out_shape=jax.ShapeDtypeStruct(shape, x.dtype), scratch_shapes=_SEMS,
    )(x)


def all_gather_stack(x, name):
    r, c = x.shape

    def body(x_ref, o_ref, send_sems, recv_sems, local_sem):
        x_, y_, c_ = _me()
        me = 4 * x_ + 2 * y_ + c_
        local = pltpu.make_async_copy(x_ref, o_ref.at[me], local_sem)
        _exchange(lambda idx: x_ref, lambda idx: o_ref.at[idx], send_sems, recv_sems, local)

    return pl.pallas_call(
        body, name=name, in_specs=[_ANY], out_specs=_ANY,
        out_shape=jax.ShapeDtypeStruct((N_DEV, r, c), x.dtype), scratch_shapes=_SEMS,
    )(x)


def reduce_scatter_exchange(g, axis, name):
    if axis is None:
        _, r, c = g.shape
    elif axis == 0:
        r, c = g.shape[0] // N_DEV, g.shape[1]
    else:
        r, c = g.shape[0], g.shape[1] // N_DEV

    def body(g_ref, o_ref, send_sems, recv_sems, local_sem):
        def block(idx):
            if axis is None:
                return g_ref.at[idx]
            if axis == 0:
                return g_ref.at[pl.ds(idx * r, r), :]
            return g_ref.at[:, pl.ds(idx * c, c)]

        x_, y_, c_ = _me()
        me = 4 * x_ + 2 * y_ + c_
        local = pltpu.make_async_copy(block(me), o_ref.at[me], local_sem)
        _exchange(block, lambda idx: o_ref.at[idx], send_sems, recv_sems, local)

    return pl.pallas_call(
        body, name=name, in_specs=[_ANY], out_specs=_ANY,
        out_shape=jax.ShapeDtypeStruct((N_DEV, r, c), g.dtype), scratch_shapes=_SEMS,
    )(g)


def _pack(arrs, quantum=8 * LANES):
    flat = jnp.concatenate([a.reshape(-1).astype(F32) for a in arrs])
    pad = (-flat.shape[0]) % quantum
    return jnp.pad(flat, (0, pad)).reshape(-1, LANES)


def _unpack(flat2d, shapes, lead=()):
    flat = flat2d.reshape(lead + (-1,))
    out, off = [], 0
    for s in shapes:
        n = int(np.prod(s))
        out.append(flat[..., off:off + n].reshape(lead + tuple(s)))
        off += n
    return out


def _gather_lastdim(stk):
    return jnp.moveaxis(stk, 0, -2).reshape(stk.shape[1:-1] + (N_DEV * stk.shape[-1],))


def _gather_dim(stk, dim):
    moved = jnp.moveaxis(stk, 0, dim)
    sh = list(stk.shape[1:])
    sh[dim] = sh[dim] * N_DEV
    return moved.reshape(sh)


def _scatter_dim(full, dim):
    sh = list(full.shape)
    sh[dim:dim + 1] = [N_DEV, sh[dim] // N_DEV]
    return jnp.moveaxis(full.reshape(sh), dim, 0)


def _head_group_matrix(tc):
    return np.kron(np.eye(tc // HEAD, dtype=np.float32), np.ones((HEAD, HEAD), np.float32))


def _build_forward(ctx2d, T, D):
    L = ctx2d.shape[0]
    N = L + T
    gm = _head_group_matrix(LANES)
    tr_row = _pick(math.gcd(L, T), 128, 8)
    op_norm = make_rowwise("norm_mod", _f_norm_mod, (F32,), tr_row, D, nb0=L // tr_row)
    op_res = [make_rowwise(f"res_norm_mod{i}", _f_res_norm_mod, (F32, BF16), _pick(T, 128, 8), D) for i in range(3)]
    op_prep = make_rowwise("wkv_prep", _f_prep, (F32,) * 7, _pick(N, 256, 8), LANES, consts=(gm,))
    op_read = make_rowwise("wkv_readout", _f_readout, (BF16,), _pick(T, 256, 8), LANES, consts=(gm,))

    def v3(a):
        return a.reshape(a.shape[0], 1, a.shape[-1])

    def fwd(xin, Ps, Wb):
        modx, modc = Ps["modx"], Ps["modc"]
        cat = jnp.concatenate([ctx2d, xin], axis=0)
        seg = lambda a, b: jnp.stack([a, b])[:, None, :]
        (hcat,) = op_norm((cat,), (Ps["n1"][0][None, None, :], seg(modc[0], modx[0, 0]), seg(modc[1], modx[0, 1])))
        xr, xw, xk, xv, xa, xg = shift_mix(hcat, Ps["mix"][:, None, :], L)
        r = linear(xr, Wb["wr"], F32, "wr")
        k = linear(xk, Wb["wk"], F32, "wk")
        v = linear(xv, Wb["wv"], F32, "wv")
        gl = jax.nn.sigmoid(linear(xg, Wb["g1"], F32, "g1"))
        g = linear(gl.astype(BF16), Wb["g2"], F32, "g2")
        tw = jnp.tanh(linear(xw, Wb["w1"], F32, "w1")).astype(BF16)
        ta = linear(xa, Wb["a1"], F32, "a1").astype(BF16)
        lw = [linear(tw[:, LORA_PAD * d:LORA_PAD * (d + 1)], Wb["w2d"][d], F32, f"w2_{d}") for d in range(2)]
        la = [linear(ta[:, LORA_PAD * d:LORA_PAD * (d + 1)], Wb["a2d"][d], F32, f"a2_{d}") for d in range(2)]
        kk, dec0, dec1, kd0, kd1, as0, as1 = op_prep(
            (k, lw[0], lw[1], la[0], la[1]),
            (v3(Ps["kk"]), v3(Ps["ka"]), Ps["w0"][0][None, None, :], Ps["w0"][1][None, None, :],
             Ps["a0"][0][None, None, :], Ps["a0"][1][None, None, :]))
        r2, kk2, v2 = _tile_heads(r), _tile_heads(kk), _tile_heads(v)
        ys = []
        for d, (dec, kd, sg) in enumerate(((dec0, kd0, as0), (dec1, kd1, as1))):
            y = wkv_scan(r2, _tile_heads(dec), _tile_heads(kd), kk2, _tile_heads(sg), v2, L, d == 1)
            ys.append(y[L:, :, :HEAD].reshape(T, D))
        (o,) = op_read((ys[0], ys[1], r[L:], kd0[L:], kd1[L:], v[L:], g[L:]),
                       (v3(Ps["rk"]), v3(Ps["lnw"]), v3(Ps["lnb"])))
        att = linear(o, Wb["wo"], F32, "wo")
        x1, h2 = op_res[0]((xin, att), (modx[0, 2][None, None, :], Ps["n2"][0][None, None, :],
                                        modx[0, 3][None, None, :], modx[0, 4][None, None, :]))
        act = swiglu_act(linear(h2, Wb["w13_0"], BF16, "w13_0"), "swiglu0")
        f0 = linear(act, Wb["w2_0"], F32, "w2_0")
        x2, h = op_res[1]((x1, f0), (modx[0, 5][None, None, :], Ps["n1"][1][None, None, :],
                                     modx[1, 0][None, None, :], modx[1, 1][None, None, :]))
        guc = linear(h, Wb["win"], BF16, "win")
        p = gated_conv(guc, Ps["conv"][:, None, :])
        cv = linear(p, Wb["wout"], F32, "wout")
        x3, h2b = op_res[2]((x2, cv), (modx[1, 2][None, None, :], Ps["n2"][1][None, None, :],
                                       modx[1, 3][None, None, :], modx[1, 4][None, None, :]))
        act1 = swiglu_act(linear(h2b, Wb["w13_1"], BF16, "w13_1"), "swiglu1")
        f1 = linear(act1, Wb["w2_1"], F32, "w2_1")
        return x3, f1


    return fwd


def kernel(x, c, ctx, c_ctx, norm1_g, norm2_g, ada_w, ada_b, rw_mix, rw_wr, rw_wk, rw_wv, rw_wo, rw_w0, rw_w1, rw_w2, rw_a0, rw_a1, rw_a2, rw_g1, rw_g2, rw_kk, rw_ka, rw_rk, rw_lnw, rw_lnb, sc_win, sc_conv, sc_wout, ffn_w13, ffn_w2, final_g, loss_target, m_c_ctx, m_norm1_g, m_norm2_g, m_ada_w, m_ada_b, m_rw_mix, m_rw_wr, m_rw_wk, m_rw_wv, m_rw_wo, m_rw_w0, m_rw_w1, m_rw_w2, m_rw_a0, m_rw_a1, m_rw_a2, m_rw_g1, m_rw_g2, m_rw_kk, m_rw_ka, m_rw_rk, m_rw_lnw, m_rw_lnb, m_sc_win, m_sc_conv, m_sc_wout, m_ffn_w13, m_ffn_w2, m_final_g, v_c_ctx, v_norm1_g, v_norm2_g, v_ada_w, v_ada_b, v_rw_mix, v_rw_wr, v_rw_wk, v_rw_wv, v_rw_wo, v_rw_w0, v_rw_w1, v_rw_w2, v_rw_a0, v_rw_a1, v_rw_a2, v_rw_g1, v_rw_g2, v_rw_kk, v_rw_ka, v_rw_rk, v_rw_lnw, v_rw_lnb, v_sc_win, v_sc_conv, v_sc_wout, v_ffn_w13, v_ffn_w2, v_final_g):
    W = dict(c_ctx=c_ctx, norm1_g=norm1_g, norm2_g=norm2_g, ada_w=ada_w, ada_b=ada_b, rw_mix=rw_mix, rw_wr=rw_wr,
             rw_wk=rw_wk, rw_wv=rw_wv, rw_wo=rw_wo, rw_w0=rw_w0, rw_w1=rw_w1, rw_w2=rw_w2, rw_a0=rw_a0, rw_a1=rw_a1,
             rw_a2=rw_a2, rw_g1=rw_g1, rw_g2=rw_g2, rw_kk=rw_kk, rw_ka=rw_ka, rw_rk=rw_rk, rw_lnw=rw_lnw,
             rw_lnb=rw_lnb, sc_win=sc_win, sc_conv=sc_conv, sc_wout=sc_wout, ffn_w13=ffn_w13, ffn_w2=ffn_w2,
             final_g=final_g)
    Mo = dict(c_ctx=m_c_ctx, norm1_g=m_norm1_g, norm2_g=m_norm2_g, ada_w=m_ada_w, ada_b=m_ada_b, rw_mix=m_rw_mix,
              rw_wr=m_rw_wr, rw_wk=m_rw_wk, rw_wv=m_rw_wv, rw_wo=m_rw_wo, rw_w0=m_rw_w0, rw_w1=m_rw_w1,
              rw_w2=m_rw_w2, rw_a0=m_rw_a0, rw_a1=m_rw_a1, rw_a2=m_rw_a2, rw_g1=m_rw_g1, rw_g2=m_rw_g2,
              rw_kk=m_rw_kk, rw_ka=m_rw_ka, rw_rk=m_rw_rk, rw_lnw=m_rw_lnw, rw_lnb=m_rw_lnb, sc_win=m_sc_win,
              sc_conv=m_sc_conv, sc_wout=m_sc_wout, ffn_w13=m_ffn_w13, ffn_w2=m_ffn_w2, final_g=m_final_g)
    Vo = dict(c_ctx=v_c_ctx, norm1_g=v_norm1_g, norm2_g=v_norm2_g, ada_w=v_ada_w, ada_b=v_ada_b, rw_mix=v_rw_mix,
              rw_wr=v_rw_wr, rw_wk=v_rw_wk, rw_wv=v_rw_wv, rw_wo=v_rw_wo, rw_w0=v_rw_w0, rw_w1=v_rw_w1,
              rw_w2=v_rw_w2, rw_a0=v_rw_a0, rw_a1=v_rw_a1, rw_a2=v_rw_a2, rw_g1=v_rw_g1, rw_g2=v_rw_g2,
              rw_kk=v_rw_kk, rw_ka=v_rw_ka, rw_rk=v_rw_rk, rw_lnw=v_rw_lnw, rw_lnb=v_rw_lnb, sc_win=v_sc_win,
              sc_conv=v_sc_conv, sc_wout=v_sc_wout, ffn_w13=v_ffn_w13, ffn_w2=v_ffn_w2, final_g=v_final_g)
    names = list(W)

    x2d = x[0]
    ctx2d = ctx[0]
    tgt = loss_target[0]
    T, D = x2d.shape
    L = ctx2d.shape[0]
    N = L + T
    nh = D // HEAD
    mx, my, mc = _me()
    me = 4 * mx + 2 * my + mc
    dloc = D // N_DEV

    lr = rw_w1.shape[-1]
    pad_r = LORA_PAD - lr
    w1p = jnp.pad(rw_w1[0], ((0, 0), (0, 0), (0, pad_r)))
    a1p = jnp.pad(rw_a1[0], ((0, 0), (0, 0), (0, pad_r)))
    w2p = jnp.pad(rw_w2[0], ((0, 0), (0, pad_r), (0, 0)))
    a2p = jnp.pad(rw_a2[0], ((0, 0), (0, pad_r), (0, 0)))
    small_loc = [rw_mix[0], rw_w0[0], rw_a0[0], sc_conv[0], w1p, a1p, w2p, a2p, rw_g1[0], rw_g2[0]]
    small_dim = [1, 1, 1, 1, 1, 1, 2, 2, 0, 1]
    small_shapes = [a.shape for a in small_loc]
    sm_all = all_gather_stack(_pack(small_loc), "ag_small_w")
    sm_parts = _unpack(sm_all, small_shapes, lead=(N_DEV,))
    mix_f, w0_f, a0_f, conv_f, w1_f, a1_f, w2_f, a2_f, g1_f, g2_f = [
        _gather_dim(p, dm) for p, dm in zip(sm_parts, small_dim)]

    c_all = all_gather_stack(jnp.pad(c, ((0, 7), (0, 0))), "ag_c")[:, 0, :]
    cond_pre = jnp.concatenate([c_all, c_ctx[None, :], jnp.zeros((7, D), F32)], axis=0)
    cond_rows = jax.nn.silu(cond_pre)
    ncol = ada_w.shape[-1]
    mod_loc = []
    for i in range(2):
        bi = lax.dynamic_slice(ada_b[i], (me * ncol,), (ncol,))
        mod_loc.append(_mm(cond_rows, ada_w[i], out_dtype=F32, name=f"ada_fwd{i}") + bi[None, :])
    mod_all = all_gather_stack(jnp.concatenate(mod_loc, axis=0), "ag_mod")
    mod_full = _gather_lastdim(mod_all).reshape(2, 16, 6, D)
    mod_x = lax.dynamic_index_in_dim(mod_full, me, axis=1, keepdims=False)
    mod_c = mod_full[0, 8, :2, :]

    def ag_w(wl, axis, name):
        return all_gather(wl.astype(BF16), axis, name)

    Wb = dict(
        wr=ag_w(rw_wr[0], 0, "ag_wr"), wk=ag_w(rw_wk[0], 0, "ag_wk"), wv=ag_w(rw_wv[0], 0, "ag_wv"),
        wo=ag_w(rw_wo[0], 0, "ag_wo"), win=ag_w(sc_win[0], 1, "ag_win"), wout=ag_w(sc_wout[0], 0, "ag_wout"),
        w13_0=ag_w(ffn_w13[0], 1, "ag_w13_0"), w13_1=ag_w(ffn_w13[1], 1, "ag_w13_1"),
        w2_0=ag_w(ffn_w2[0], 0, "ag_w2_0"), w2_1=ag_w(ffn_w2[1], 0, "ag_w2_1"),
        w1=jnp.concatenate([w1_f[0], w1_f[1]], axis=1).astype(BF16),
        a1=jnp.concatenate([a1_f[0], a1_f[1]], axis=1).astype(BF16),
        w2d=w2_f.astype(BF16), a2d=a2_f.astype(BF16),
        g1=g1_f.astype(BF16), g2=g2_f.astype(BF16),
    )
    Ps = dict(n1=norm1_g, n2=norm2_g, modx=mod_x, modc=mod_c, mix=mix_f, w0=w0_f, a0=a0_f, conv=conv_f,
              kk=rw_kk, ka=rw_ka, rk=rw_rk.reshape(1, D), lnw=rw_lnw, lnb=rw_lnb)

    fwd = _build_forward(ctx2d, T, D)
    (x3, f1), vjp_fn = jax.vjp(fwd, x2d, Ps, Wb)
    loss_acc, dx3, df1, dgate, dfinal = loss_head(x3, f1, tgt, mod_x[1, 5][None, :], final_g[None, :])
    dx, dPs, dWb = vjp_fn((dx3, df1))
    loss = lax.psum(loss_acc[0, 0], ("x", "y", "c"))

    dmodx = dPs["modx"].at[1, 5].add(dgate[0])
    dmodc = jnp.concatenate([dPs["modc"], jnp.zeros((4, D), F32)], axis=0)
    drow = jnp.stack([dmodx.reshape(2, 6 * D), jnp.stack([dmodc.reshape(6 * D), jnp.zeros((6 * D,), F32)])], axis=1)
    drow_all = all_gather_stack(drow.reshape(4, 6 * D), "ag_dmod").reshape(N_DEV, 2, 2, 6 * D)
    dctx_tot = drow_all[0, :, 1, :]
    for s in range(1, N_DEV):
        dctx_tot = dctx_tot + drow_all[s, :, 1, :]
    dmod_rows = jnp.concatenate([jnp.moveaxis(drow_all[:, :, 0, :], 0, 1), dctx_tot[:, None, :],
                                 jnp.zeros((2, 7, 6 * D), F32)], axis=1)
    grad_ada_b = dctx_tot
    for s in range(N_DEV):
        grad_ada_b = grad_ada_b + drow_all[s, :, 0, :]
    dmod_mine = lax.dynamic_slice_in_dim(dmod_rows, me * ncol, ncol, axis=2)
    g_ada_w = [_mm(cond_rows, dmod_mine[i], ta=True, out_dtype=F32, name=f"ada_dw{i}") for i in range(2)]
    dcond_part = _mm(dmod_mine[0], ada_w[0], tb=True, out_dtype=F32, name="ada_dcond")[8]

    rep_names = ["c_ctx", "norm1_g", "norm2_g", "rw_kk", "rw_ka", "rw_rk", "rw_lnw", "rw_lnb", "final_g"]
    rep_part = [dcond_part, dPs["n1"], dPs["n2"], dPs["kk"], dPs["ka"], dPs["rk"].reshape(W["rw_rk"].shape),
                dPs["lnw"], dPs["lnb"], dfinal[0]]
    rep_shapes = [W[n_].shape for n_ in rep_names]
    rep_all = all_gather_stack(_pack(rep_part), "ag_rep_grads")
    sg = jax.nn.sigmoid(c_ctx)
    dsilu = sg * (1.0 + c_ctx * (1.0 - sg))
    rep_scale = _pack([dsilu] + [jnp.ones(s, F32) for s in rep_shapes[1:]])
    rep_all = rep_all * rep_scale[None]
    rep_w = _pack([W[n_] for n_ in rep_names])
    rep_m = _pack([Mo[n_] for n_ in rep_names])
    rep_v = _pack([Vo[n_] for n_ in rep_names])
    rep_out = sum_adam(rep_all, rep_w, rep_m, rep_v, "adam_rep")
    results = {}
    for nm_, vals in zip(rep_names, zip(*[_unpack(o, rep_shapes) for o in rep_out])):
        results[nm_] = vals

    results["ada_b"] = tuple(sum_adam(grad_ada_b.reshape(1, 2 * 6, D), ada_b.reshape(12, D), m_ada_b.reshape(12, D),
                                      v_ada_b.reshape(12, D), "adam_ada_b"))
    results["ada_b"] = tuple(o.reshape(ada_b.shape) for o in results["ada_b"])

    outs = [sum_adam(g_ada_w[i][None], ada_w, m_ada_w, v_ada_w, f"adam_ada_w{i}", lead=i) for i in range(2)]
    results["ada_w"] = tuple(jnp.stack([outs[0][q], outs[1][q]]) for q in range(4))

    dw1 = jnp.stack([dWb["w1"][:, :LORA_PAD], dWb["w1"][:, LORA_PAD:]]).astype(F32)
    da1 = jnp.stack([dWb["a1"][:, :LORA_PAD], dWb["a1"][:, LORA_PAD:]]).astype(F32)
    small_g = [dPs["mix"], dPs["w0"], dPs["a0"], dPs["conv"], dw1, da1, dWb["w2d"].astype(F32),
               dWb["a2d"].astype(F32), dWb["g1"].astype(F32), dWb["g2"].astype(F32)]
    blocks = [_scatter_dim(gf, dm) for gf, dm in zip(small_g, small_dim)]
    packed = jnp.concatenate([b.reshape(N_DEV, -1) for b in blocks], axis=1)
    padc = (-packed.shape[1]) % (8 * LANES)
    packed = jnp.pad(packed, ((0, 0), (0, padc))).reshape(N_DEV, -1, LANES)
    sm_recv = reduce_scatter_exchange(packed, None, "rs_small")
    small_names = ["rw_mix", "rw_w0", "rw_a0", "sc_conv", "rw_w1", "rw_a1", "rw_w2", "rw_a2", "rw_g1", "rw_g2"]

    def padded_local(nm_, src):
        a = src[nm_][0]
        if nm_ in ("rw_w1", "rw_a1"):
            return jnp.pad(a, ((0, 0), (0, 0), (0, pad_r)))
        if nm_ in ("rw_w2", "rw_a2"):
            return jnp.pad(a, ((0, 0), (0, pad_r), (0, 0)))
        return a

    sm_out = sum_adam(sm_recv, _pack([padded_local(n_, W) for n_ in small_names]),
                      _pack([padded_local(n_, Mo) for n_ in small_names]),
                      _pack([padded_local(n_, Vo) for n_ in small_names]), "adam_small")
    for nm_, vals in zip(small_names, zip(*[_unpack(o, small_shapes) for o in sm_out])):
        if nm_ in ("rw_w1", "rw_a1"):
            vals = tuple(a[:, :, :lr] for a in vals)
        if nm_ in ("rw_w2", "rw_a2"):
            vals = tuple(a[:, :lr, :] for a in vals)
        results[nm_] = tuple(a[None] for a in vals)

    def rs_adam(gfull, axis, nm_, lead, tag):
        recv = reduce_scatter_exchange(gfull, axis, "rs_" + tag)
        return sum_adam(recv, W[nm_], Mo[nm_], Vo[nm_], "adam_" + tag, lead=lead)

    for nm_, key, axis in (("rw_wr", "wr", 0), ("rw_wk", "wk", 0), ("rw_wv", "wv", 0), ("rw_wo", "wo", 0),
                           ("sc_win", "win", 1), ("sc_wout", "wout", 0)):
        results[nm_] = tuple(a[None] for a in rs_adam(dWb[key], axis, nm_, 0, key))
    for nm_, key, axis in (("ffn_w13", "w13", 1), ("ffn_w2", "w2", 0)):
        outs = [rs_adam(dWb[f"{key}_{i}"], axis, nm_, i, f"{key}_{i}") for i in range(2)]
        results[nm_] = tuple(jnp.stack([outs[0][q], outs[1][q]]) for q in range(4))

    grads = [results[n_][0] for n_ in names]
    deltas = [results[n_][1] for n_ in names]
    new_m = [results[n_][2] for n_ in names]
    new_v = [results[n_][3] for n_ in names]
    return (loss, dx[None], *grads, *deltas, *new_m, *new_v)
```

```python
import functools
import math

import numpy as np
import jax
import jax.numpy as jnp
from jax import lax
from jax.experimental import pallas as pl
from jax.experimental.pallas import tpu as pltpu

F32 = jnp.float32
BF16 = jnp.bfloat16

N_DEV = 8
HEAD = 64
LANES = 128
GRID_W = 64
LORA_PAD = 128
NORM_EPS = 1e-6
GN_EPS = 64e-5
ADAM_LR, ADAM_B1, ADAM_B2, ADAM_EPS, ADAM_WD, ADAM_STEP = 0.001, 0.9, 0.999, 1e-08, 0.01, 10
VMEM_LIMIT = 52 * 1024 * 1024
SCAN_CHUNK = 8
HI = lax.Precision.HIGHEST


def _cparams(sem):
    return pltpu.CompilerParams(dimension_semantics=sem, vmem_limit_bytes=VMEM_LIMIT)


def _pick(n, cap, quantum=LANES):
    best = None
    for t in range(quantum, min(n, cap) + 1, quantum):
        if n % t == 0:
            best = t
    return n if best is None else best


def _mm(a, b, *, ta=False, tb=False, out_dtype, name):
    if ta:
        K, M = a.shape
    else:
        M, K = a.shape
    if tb:
        N, Kb = b.shape
    else:
        Kb, N = b.shape
    assert K == Kb, (a.shape, b.shape, ta, tb)
    tm = _pick(M, 1024)
    tn = _pick(N, 1024 if jnp.dtype(out_dtype).itemsize == 2 else 512)
    tk = _pick(K, 512)
    nk = K // tk
    dims = (((0 if ta else 1,), (1 if tb else 0,)), ((), ()))

    def body(a_ref, b_ref, o_ref, acc_ref):
        k = pl.program_id(2)

        @pl.when(k == 0)
        def _():
            acc_ref[...] = jnp.zeros_like(acc_ref)

        acc_ref[...] += lax.dot_general(a_ref[...].astype(BF16), b_ref[...].astype(BF16), dims,
                                        preferred_element_type=F32)

        @pl.when(k == nk - 1)
        def _():
            o_ref[...] = acc_ref[...].astype(o_ref.dtype)

    a_spec = pl.BlockSpec((tk, tm), lambda i, j, k: (k, i)) if ta else pl.BlockSpec((tm, tk), lambda i, j, k: (i, k))
    b_spec = pl.BlockSpec((tn, tk), lambda i, j, k: (j, k)) if tb else pl.BlockSpec((tk, tn), lambda i, j, k: (k, j))
    return pl.pallas_call(
        body, name=name, grid=(M // tm, N // tn, nk),
        in_specs=[a_spec, b_spec],
        out_specs=pl.BlockSpec((tm, tn), lambda i, j, k: (i, j)),
        out_shape=jax.ShapeDtypeStruct((M, N), out_dtype),
        scratch_shapes=[pltpu.VMEM((tm, tn), F32)],
        compiler_params=_cparams(("parallel", "parallel", "arbitrary")),
    )(a, b)


@functools.partial(jax.custom_vjp, nondiff_argnums=(2, 3))
def linear(a, w, out_dtype, name):
    return _mm(a, w, out_dtype=out_dtype, name=name + "_fwd")


def _linear_fwd(a, w, out_dtype, name):
    return _mm(a, w, out_dtype=out_dtype, name=name + "_fwd"), (a, w)


def _linear_bwd(out_dtype, name, res, g):
    a, w = res
    da = _mm(g, w, tb=True, out_dtype=a.dtype, name=name + "_da")
    dw = _mm(a, g, ta=True, out_dtype=w.dtype, name=name + "_dw")
    return da, dw


linear.defvjp(_linear_fwd, _linear_bwd)


def _rw_specs(tiles, col_offs, vecs, consts, tr, tc, nb0):
    tile_specs = [pl.BlockSpec((tr, tc), functools.partial(lambda j, i, off: (i, j + off), off=off))
                  for _, off in zip(tiles, col_offs)]

    def vec_map(S):
        if S == 1:
            return lambda j, i: (0, 0, j)
        return lambda j, i: (jnp.where(i < nb0, 0, 1), 0, j)

    vec_specs = [pl.BlockSpec((None, 1, tc), vec_map(v.shape[0])) for v in vecs]
    const_specs = [pl.BlockSpec(c.shape, lambda j, i: (0, 0)) for c in consts]
    return tile_specs, vec_specs, const_specs


def _rw_forward(name, f, tiles, col_offs, vecs, consts, out_dtypes, tr, tc, nb0, width):
    n = tiles[0].shape[0]
    nt, nv, nc = len(tiles), len(vecs), len(consts)
    tile_specs, vec_specs, const_specs = _rw_specs(tiles, col_offs, vecs, consts, tr, tc, nb0)

    def body(*refs):
        ins = [r[...].astype(F32) for r in refs[:nt]] + [r[...] for r in refs[nt:nt + nv + nc]]
        outs = f(*ins)
        for o_ref, o in zip(refs[nt + nv + nc:], outs):
            o_ref[...] = o.astype(o_ref.dtype)

    return pl.pallas_call(
        body, name=name + "_fwd", grid=(width // tc, n // tr),
        in_specs=tile_specs + vec_specs + const_specs,
        out_specs=[pl.BlockSpec((tr, tc), lambda j, i: (i, j)) for _ in out_dtypes],
        out_shape=[jax.ShapeDtypeStruct((n, width), dt) for dt in out_dtypes],
        compiler_params=_cparams(("parallel", "parallel")),
    )(*tiles, *vecs, *consts)


def _rw_backward(name, f, tiles, col_offs, vecs, consts, douts, tr, tc, nb0, width):
    n = tiles[0].shape[0]
    nt, nv, nc, no = len(tiles), len(vecs), len(consts), len(douts)
    tile_specs, vec_specs, const_specs = _rw_specs(tiles, col_offs, vecs, consts, tr, tc, nb0)

    def body(*refs):
        t_in = [r[...].astype(F32) for r in refs[:nt]]
        v_in = [r[...] for r in refs[nt:nt + nv]]
        c_in = [r[...] for r in refs[nt + nv:nt + nv + nc]]
        d_in = tuple(r[...].astype(F32) for r in refs[nt + nv + nc:nt + nv + nc + no])
        o_refs = refs[nt + nv + nc + no:]
        _, vjp = jax.vjp(lambda *tv: tuple(f(*tv, *c_in)), *t_in, *v_in)
        grads = vjp(d_in)
        for o_ref, g in zip(o_refs[:nt], grads[:nt]):
            o_ref[...] = g.astype(o_ref.dtype)
        i = pl.program_id(1)
        for o_ref, g, v in zip(o_refs[nt:], grads[nt:], vecs):
            first = jnp.logical_or(i == 0, i == nb0) if v.shape[0] == 2 else i == 0

            @pl.when(first)
            def _(o_ref=o_ref, g=g):
                o_ref[...] = g

            @pl.when(jnp.logical_not(first))
            def _(o_ref=o_ref, g=g):
                o_ref[...] += g

    dout_specs = [pl.BlockSpec((tr, tc), lambda j, i: (i, j)) for _ in douts]
    out_specs = [pl.BlockSpec((tr, tc), lambda j, i: (i, j)) for _ in tiles] + list(vec_specs)
    out_shape = ([jax.ShapeDtypeStruct((n, width), t.dtype) for t in tiles]
                 + [jax.ShapeDtypeStruct(v.shape, F32) for v in vecs])
    return pl.pallas_call(
        body, name=name + "_bwd", grid=(width // tc, n // tr),
        in_specs=tile_specs + vec_specs + const_specs + dout_specs,
        out_specs=out_specs, out_shape=out_shape,
        compiler_params=_cparams(("parallel", "arbitrary")),
    )(*tiles, *vecs, *consts, *douts)


def make_rowwise(name, f, out_dtypes, tr, tc, consts=(), nb0=-1):
    consts = tuple(consts)

    @jax.custom_vjp
    def op(tiles, vecs):
        w = tiles[0].shape[1]
        return tuple(_rw_forward(name, f, tiles, (0,) * len(tiles), vecs, consts, out_dtypes, tr, min(tc, w), nb0, w))

    def op_fwd(tiles, vecs):
        return op(tiles, vecs), (tiles, vecs)

    def op_bwd(res, douts):
        tiles, vecs = res
        w = tiles[0].shape[1]
        g = _rw_backward(name, f, tiles, (0,) * len(tiles), vecs, consts, tuple(douts), tr, min(tc, w), nb0, w)
        return tuple(g[:len(tiles)]), tuple(g[len(tiles):])

    op.defvjp(op_fwd, op_bwd)
    return op


def _f_norm_mod(x, g, sh, sc):
    hn = x * lax.rsqrt(jnp.mean(x * x, axis=-1, keepdims=True) + NORM_EPS)
    return ((hn * g) * (1.0 + sc) + sh,)


def _f_res_norm_mod(x, y, gate, g, sh, sc):
    x1 = x + gate * y
    hn = x1 * lax.rsqrt(jnp.mean(x1 * x1, axis=-1, keepdims=True) + NORM_EPS)
    return x1, (hn * g) * (1.0 + sc) + sh


def _head_sum(t, gmat):
    return jnp.dot(t, gmat, precision=HI, preferred_element_type=F32)


def _f_prep(k, lw0, lw1, la0, la1, kkp, kap, w00, w01, a00, a01, gmat):
    t = k * kkp
    kk = t / jnp.maximum(jnp.sqrt(_head_sum(t * t, gmat)), 1e-12)
    outs = [kk]
    decs, kds, sigs = [], [], []
    for lw, la, w0, a0 in ((lw0, la0, w00, a00), (lw1, la1, w01, a01)):
        decs.append(jnp.exp(-jax.nn.sigmoid(w0 + lw) * float(np.exp(-0.5))))
        a = jax.nn.sigmoid(a0 + la)
        sigs.append(a)
        kds.append(k * (1.0 + (a - 1.0) * kap))
    return tuple(outs + decs + kds + sigs)


def _f_readout(y0, y1, r, kd0, kd1, v, g, rk, lnw, lnb, gmat):
    y = y0 + y1
    mu = _head_sum(y, gmat) * (1.0 / HEAD)
    d = y - mu
    var = _head_sum(d * d, gmat) * (1.0 / HEAD)
    o = d * lax.rsqrt(var + GN_EPS) * lnw + lnb
    bonus = _head_sum(r * (kd0 + kd1) * rk, gmat) * v
    return ((o + bonus) * g,)


def _f_swiglu(a, b):
    return (jax.nn.silu(a) * b,)


def swiglu_act(ab, name):
    t, f2 = ab.shape
    fdim = f2 // 2
    tr, tc = _pick(t, 512, 8), _pick(fdim, 512)
    offs = (0, fdim // tc)

    @jax.custom_vjp
    def op(ab_):
        return _rw_forward(name, _f_swiglu, (ab_, ab_), offs, (), (), (BF16,), tr, tc, -1, fdim)[0]

    def op_fwd(ab_):
        return op(ab_), ab_

    def op_bwd(ab_, dact):
        da, db = _rw_backward(name, _f_swiglu, (ab_, ab_), offs, (), (), (dact,), tr, tc, -1, fdim)
        return (jnp.concatenate([da, db], axis=1),)

    op.defvjp(op_fwd, op_bwd)
    return op(ab)


def _row_iota(n, tc):
    return lax.broadcasted_iota(jnp.int32, (n, tc), 0)


def _shift_rows(x, s, keep):
    n = x.shape[0]
    return jnp.where(keep, pltpu.roll(x, s % n, 0), 0.0)


def _unshift_rows(d, s, keep):
    n = d.shape[0]
    return pltpu.roll(jnp.where(keep, d, 0.0), (-s) % n, 0)


def _ctx_shift_spec(L, tc, quarter):
    row = _row_iota(L, tc)
    if quarter < 2:
        return 1, row >= 1
    return -1, row < L - 1


def _grid_shift_spec(T, tc, quarter):
    row = _row_iota(T, tc)
    col = jnp.bitwise_and(row, GRID_W - 1)
    if quarter == 0:
        return 1, col != 0
    if quarter == 1:
        return -1, col != GRID_W - 1
    if quarter == 2:
        return GRID_W, row >= GRID_W
    return -GRID_W, row < T - GRID_W


def _shift_mix_fwd_call(h, mix3, L):
    n, d = h.shape
    T = n - L
    tc = _pick(d // 4, 256)
    nq = (d // 4) // tc

    def body(h_ref, mix_ref, *o_refs):
        q = pl.program_id(0) // nq
        for quarter in range(4):
            @pl.when(q == quarter)
            def _(quarter=quarter):
                for lo, cnt, spec in ((0, L, _ctx_shift_spec), (L, T, _grid_shift_spec)):
                    hh = h_ref[pl.ds(lo, cnt), :]
                    s, keep = spec(cnt, tc, quarter)
                    xx = _shift_rows(hh, s, keep) - hh
                    for m in range(6):
                        o_refs[m][pl.ds(lo, cnt), :] = (hh + xx * mix_ref[m]).astype(BF16)

    return pl.pallas_call(
        body, name="shift_mix_fwd", grid=(d // tc,),
        in_specs=[pl.BlockSpec((n, tc), lambda j: (0, j)), pl.BlockSpec((6, 1, tc), lambda j: (0, 0, j))],
        out_specs=[pl.BlockSpec((n, tc), lambda j: (0, j)) for _ in range(6)],
        out_shape=[jax.ShapeDtypeStruct((n, d), BF16) for _ in range(6)],
        compiler_params=_cparams(("parallel",)),
    )(h, mix3)


def _shift_mix_bwd_call(h, mix3, douts, L):
    n, d = h.shape
    T = n - L
    tc = _pick(d // 4, 256)
    nq = (d // 4) // tc

    def body(h_ref, mix_ref, d0, d1, d2, d3, d4, d5, dh_ref, dmix_ref):
        d_refs = (d0, d1, d2, d3, d4, d5)
        q = pl.program_id(0) // nq
        for quarter in range(4):
            @pl.when(q == quarter)
            def _(quarter=quarter):
                dmix = [jnp.zeros((1, tc), F32) for _ in range(6)]
                for lo, cnt, spec in ((0, L, _ctx_shift_spec), (L, T, _grid_shift_spec)):
                    hh = h_ref[pl.ds(lo, cnt), :]
                    s, keep = spec(cnt, tc, quarter)
                    xx = _shift_rows(hh, s, keep) - hh
                    direct = jnp.zeros((cnt, tc), F32)
                    shifted = jnp.zeros((cnt, tc), F32)
                    for m in range(6):
                        dm = d_refs[m][pl.ds(lo, cnt), :].astype(F32)
                        mx = mix_ref[m]
                        direct = direct + dm * (1.0 - mx)
                        shifted = shifted + dm * mx
                        dmix[m] = dmix[m] + jnp.sum(dm * xx, axis=0, keepdims=True)
                    dh_ref[pl.ds(lo, cnt), :] = direct + _unshift_rows(shifted, s, keep)
                for m in range(6):
                    dmix_ref[m] = dmix[m]

    tile = pl.BlockSpec((n, tc), lambda j: (0, j))
    return pl.pallas_call(
        body, name="shift_mix_bwd", grid=(d // tc,),
        in_specs=[tile, pl.BlockSpec((6, 1, tc), lambda j: (0, 0, j))] + [tile] * 6,
        out_specs=[tile, pl.BlockSpec((6, 1, tc), lambda j: (0, 0, j))],
        out_shape=[jax.ShapeDtypeStruct((n, d), F32), jax.ShapeDtypeStruct((6, 1, d), F32)],
        compiler_params=_cparams(("parallel",)),
    )(h, mix3, *douts)


@functools.partial(jax.custom_vjp, nondiff_argnums=(2,))
def shift_mix(h, mix3, L):
    return tuple(_shift_mix_fwd_call(h, mix3, L))


def _shift_mix_fwd(h, mix3, L):
    return tuple(_shift_mix_fwd_call(h, mix3, L)), (h, mix3)


def _shift_mix_bwd(L, res, douts):
    h, mix3 = res
    dh, dmix = _shift_mix_bwd_call(h, mix3, tuple(douts), L)
    return dh, dmix


shift_mix.defvjp(_shift_mix_fwd, _shift_mix_bwd)


def _conv_specs(T, d, tc):
    nd = d // tc
    ins = [pl.BlockSpec((T, tc), functools.partial(lambda j, off: (0, j + off), off=o * nd)) for o in range(3)]
    return ins, pl.BlockSpec((3, 1, tc), lambda j: (0, 0, j))


def _conv_terms(gc, u, tc):
    T = gc.shape[0]
    row = _row_iota(T, tc)
    z = gc * u
    return z, _shift_rows(z, 1, row >= 1), _shift_rows(z, -1, row < T - 1), row


def _conv_fwd_call(guc, cw3):
    T, d3 = guc.shape
    d = d3 // 3
    tc = _pick(d, 256)
    ins, wspec = _conv_specs(T, d, tc)

    def body(gb_ref, gc_ref, u_ref, w_ref, p_ref):
        z, zp, zn, _ = _conv_terms(gc_ref[...].astype(F32), u_ref[...].astype(F32), tc)
        conv = zp * w_ref[0] + z * w_ref[1] + zn * w_ref[2]
        p_ref[...] = (gb_ref[...].astype(F32) * conv).astype(BF16)

    return pl.pallas_call(
        body, name="conv_fwd", grid=(d // tc,), in_specs=ins + [wspec],
        out_specs=pl.BlockSpec((T, tc), lambda j: (0, j)),
        out_shape=jax.ShapeDtypeStruct((T, d), BF16),
        compiler_params=_cparams(("parallel",)),
    )(guc, guc, guc, cw3)


def _conv_bwd_call(guc, cw3, dp):
    T, d3 = guc.shape
    d = d3 // 3
    tc = _pick(d, 256)
    ins, wspec = _conv_specs(T, d, tc)
    tile = pl.BlockSpec((T, tc), lambda j: (0, j))

    def body(gb_ref, gc_ref, u_ref, w_ref, dp_ref, dgb_ref, dgc_ref, du_ref, dw_ref):
        gc = gc_ref[...].astype(F32)
        u = u_ref[...].astype(F32)
        z, zp, zn, row = _conv_terms(gc, u, tc)
        conv = zp * w_ref[0] + z * w_ref[1] + zn * w_ref[2]
        dpv = dp_ref[...].astype(F32)
        dgb_ref[...] = (dpv * conv).astype(dgb_ref.dtype)
        dconv = dpv * gb_ref[...].astype(F32)
        dz = (_shift_rows(dconv, -1, row < T - 1) * w_ref[0] + dconv * w_ref[1]
              + _shift_rows(dconv, 1, row >= 1) * w_ref[2])
        dgc_ref[...] = (dz * u).astype(dgc_ref.dtype)
        du_ref[...] = (dz * gc).astype(du_ref.dtype)
        dw_ref[0] = jnp.sum(dconv * zp, axis=0, keepdims=True)
        dw_ref[1] = jnp.sum(dconv * z, axis=0, keepdims=True)
        dw_ref[2] = jnp.sum(dconv * zn, axis=0, keepdims=True)

    return pl.pallas_call(
        body, name="conv_bwd", grid=(d // tc,), in_specs=ins + [wspec, tile],
        out_specs=[tile, tile, tile, wspec],
        out_shape=[jax.ShapeDtypeStruct((T, d), guc.dtype)] * 3 + [jax.ShapeDtypeStruct((3, 1, d), F32)],
        compiler_params=_cparams(("parallel",)),
    )(guc, guc, guc, cw3, dp)


@jax.custom_vjp
def gated_conv(guc, cw3):
    return _conv_fwd_call(guc, cw3)


def _gated_conv_fwd(guc, cw3):
    return _conv_fwd_call(guc, cw3), (guc, cw3)


def _gated_conv_bwd(res, dp):
    guc, cw3 = res
    dgb, dgc, du, dw = _conv_bwd_call(guc, cw3, dp)
    return jnp.concatenate([dgb, dgc, du], axis=1), dw


gated_conv.defvjp(_gated_conv_fwd, _gated_conv_bwd)


def _chunk_map(nchunk, nctx_chunk, reverse):
    if not reverse:
        return lambda c: c
    return lambda c: jnp.where(c < nctx_chunk, nctx_chunk - 1 - c, nchunk - 1 - (c - nctx_chunk))


def _spread(row_ref, dst_scr, lo_mask, ni, C):
    for i in range(ni):
        idx = jnp.where(lo_mask, 2 * i, 2 * i + 1).astype(jnp.int32)
        for tt in range(C):
            dst_scr[tt, i] = jnp.take_along_axis(row_ref[tt], idx, axis=1)


def _half_sums(p, lo_mask):
    lo = jnp.sum(jnp.where(lo_mask, p, 0.0), axis=1, keepdims=True)
    hi = jnp.sum(jnp.where(lo_mask, 0.0, p), axis=1, keepdims=True)
    return lo, hi


def _gather_rows(los, his, lane, nh):
    acc = jnp.zeros((nh, LANES), F32)
    for i, (lo, hi) in enumerate(zip(los, his)):
        acc = acc + jnp.where(lane == 2 * i, lo, 0.0) + jnp.where(lane == 2 * i + 1, hi, 0.0)
    return acc


def _wkv_fwd_call(r2, w2, kd2, kk2, as2, v2, nctx, reverse, xchg_arrs=(), xchg_specs=()):
    n, nh, _ = r2.shape
    C = SCAN_CHUNK
    ni = HEAD // 2
    nchunk = n // C
    cmap = _chunk_map(nchunk, nctx // C, reverse)
    nx = len(xchg_arrs)

    def body(*refs):
        r_ref, w_ref, kd_ref, kk_ref, as_ref, v_ref = refs[:6]
        x_in = refs[6:6 + nx]
        y_ref, sa_ref, sp_ref = refs[6 + nx:9 + nx]
        x_out = refs[9 + nx:9 + 2 * nx]
        s_scr, vc_scr = refs[9 + 2 * nx:11 + 2 * nx]
        if nx:
            _fused_exchanges(list(zip(x_in, x_out)), xchg_specs, *refs[11 + 2 * nx:], first=pl.program_id(0) == 0)

        @pl.when(pl.program_id(0) == 0)
        def _():
            s_scr[...] = jnp.zeros_like(s_scr)

        lane = lax.broadcasted_iota(jnp.int32, (nh, LANES), 1)
        lo_mask = lane < HEAD
        _spread(v_ref, vc_scr, lo_mask, ni, C)

        def step(j, carry):
            t = (C - 1 - j) if reverse else j
            kk = kk_ref[t]
            a2 = -kk
            b2 = kk * as_ref[t]
            w = w_ref[t]
            k = kd_ref[t]
            r = r_ref[t]
            los, his = [], []
            for i in range(ni):
                si = s_scr[i]
                sp_ref[t, i] = si
                lo, hi = _half_sums(si * a2, lo_mask)
                los.append(lo)
                his.append(hi)
            ylos, yhis = [], []
            for i in range(ni):
                sa_i = jnp.where(lo_mask, los[i], his[i])
                sn = s_scr[i] * w + sa_i * b2 + vc_scr[t, i] * k
                s_scr[i] = sn
                lo, hi = _half_sums(sn * r, lo_mask)
                ylos.append(lo)
                yhis.append(hi)
            y_ref[t] = _gather_rows(ylos, yhis, lane, nh)
            sa_ref[t] = _gather_rows(los, his, lane, nh)
            return carry

        lax.fori_loop(0, C, step, 0)
        if nx:
            _fused_exchanges(list(zip(x_in, x_out)), xchg_specs, *refs[11 + 2 * nx:],
                             last=pl.program_id(0) == nchunk - 1)

    tok = pl.BlockSpec((C, nh, LANES), lambda c: (cmap(c), 0, 0))
    return pl.pallas_call(
        body, name="wkv_fwd_rev" if reverse else "wkv_fwd", grid=(nchunk,),
        in_specs=[tok] * 6 + [_ANY] * nx,
        out_specs=[tok, tok, pl.BlockSpec((C, ni, nh, LANES), lambda c: (cmap(c), 0, 0, 0))] + [_ANY] * nx,
        out_shape=[jax.ShapeDtypeStruct((n, nh, LANES), F32), jax.ShapeDtypeStruct((n, nh, LANES), F32),
                   jax.ShapeDtypeStruct((n, ni, nh, LANES), F32)] + _exchange_out_shapes(xchg_arrs, xchg_specs),
        scratch_shapes=[pltpu.VMEM((ni, nh, LANES), F32), pltpu.VMEM((C, ni, nh, LANES), F32)]
        + (_exchange_sems(nx) if nx else []),
        compiler_params=_cparams(("arbitrary",)),
    )(r2, w2, kd2, kk2, as2, v2, *xchg_arrs)


def _wkv_bwd_call(r2, w2, kd2, kk2, as2, v2, sa, sprev, dy, nctx, reverse, xchg_arrs=(), xchg_specs=()):
    n, nh, _ = r2.shape
    C = SCAN_CHUNK
    ni = HEAD // 2
    nchunk = n // C
    fmap = _chunk_map(nchunk, nctx // C, reverse)
    cmap = lambda c: fmap(nchunk - 1 - c)
    nx = len(xchg_arrs)

    def body(*refs):
        r_ref, w_ref, kd_ref, kk_ref, as_ref, v_ref, sa_ref, sp_ref, dy_ref = refs[:9]
        x_in = refs[9:9 + nx]
        dr_ref, dw_ref, dkd_ref, dkk_ref, das_ref, dv_ref = refs[9 + nx:15 + nx]
        x_out = refs[15 + nx:15 + 2 * nx]
        ds_scr, vc_scr, sac_scr, dyc_scr = refs[15 + 2 * nx:19 + 2 * nx]
        if nx:
            _fused_exchanges(list(zip(x_in, x_out)), xchg_specs, *refs[19 + 2 * nx:], first=pl.program_id(0) == 0)

        @pl.when(pl.program_id(0) == 0)
        def _():
            ds_scr[...] = jnp.zeros_like(ds_scr)

        lane = lax.broadcasted_iota(jnp.int32, (nh, LANES), 1)
        lo_mask = lane < HEAD
        _spread(v_ref, vc_scr, lo_mask, ni, C)
        _spread(sa_ref, sac_scr, lo_mask, ni, C)
        _spread(dy_ref, dyc_scr, lo_mask, ni, C)

        def step(j, carry):
            t = j if reverse else (C - 1 - j)
            kk = kk_ref[t]
            sig = as_ref[t]
            a2 = -kk
            b2 = kk * sig
            w = w_ref[t]
            k = kd_ref[t]
            r = r_ref[t]
            zero = jnp.zeros((nh, LANES), F32)
            acc_dk, acc_db, acc_dw, acc_g, acc_sady, acc_vdy, acc_da = zero, zero, zero, zero, zero, zero, zero
            dvlo, dvhi, dsalo, dsahi = [], [], [], []
            for i in range(ni):
                dyc = dyc_scr[t, i]
                sp = sp_ref[t, i]
                vc = vc_scr[t, i]
                sac = sac_scr[t, i]
                ds = ds_scr[i] + dyc * r
                ds_scr[i] = ds
                lo, hi = _half_sums(ds * k, lo_mask)
                dvlo.append(lo)
                dvhi.append(hi)
                lo, hi = _half_sums(ds * b2, lo_mask)
                dsalo.append(lo)
                dsahi.append(hi)
                acc_dk = acc_dk + ds * vc
                acc_db = acc_db + ds * sac
                acc_dw = acc_dw + ds * sp
                acc_g = acc_g + sp * dyc
                acc_sady = acc_sady + sac * dyc
                acc_vdy = acc_vdy + vc * dyc
            for i in range(ni):
                dsa_i = jnp.where(lo_mask, dsalo[i], dsahi[i])
                acc_da = acc_da + sp_ref[t, i] * dsa_i
                ds_scr[i] = ds_scr[i] * w + dsa_i * a2
            dr_ref[t] = acc_g * w + b2 * acc_sady + k * acc_vdy
            dw_ref[t] = acc_dw
            dkd_ref[t] = acc_dk
            dkk_ref[t] = acc_db * sig - acc_da
            das_ref[t] = acc_db * kk
            dv_ref[t] = _gather_rows(dvlo, dvhi, lane, nh)
            return carry

        lax.fori_loop(0, C, step, 0)
        if nx:
            _fused_exchanges(list(zip(x_in, x_out)), xchg_specs, *refs[19 + 2 * nx:],
                             last=pl.program_id(0) == nchunk - 1)

    tok = pl.BlockSpec((C, nh, LANES), lambda c: (cmap(c), 0, 0))
    big = pltpu.VMEM((C, ni, nh, LANES), F32)
    return pl.pallas_call(
        body, name="wkv_bwd_rev" if reverse else "wkv_bwd", grid=(nchunk,),
        in_specs=[tok] * 7 + [pl.BlockSpec((C, ni, nh, LANES), lambda c: (cmap(c), 0, 0, 0)), tok] + [_ANY] * nx,
        out_specs=[tok] * 6 + [_ANY] * nx,
        out_shape=[jax.ShapeDtypeStruct((n, nh, LANES), F32)] * 6 + _exchange_out_shapes(xchg_arrs, xchg_specs),
        scratch_shapes=[pltpu.VMEM((ni, nh, LANES), F32), big, big, big] + (_exchange_sems(nx) if nx else []),
        compiler_params=_cparams(("arbitrary",)),
    )(r2, w2, kd2, kk2, as2, v2, sa, sprev, dy, *xchg_arrs)


def _tile_heads(t):
    n, d = t.shape
    th = t.reshape(n, d // HEAD, HEAD)
    return jnp.concatenate([th, th], axis=-1)


def loss_head(x3, fo, tgt, gate, g):
    T, d = x3.shape
    tr = _pick(T, 128, 8)

    def body(x_ref, f_ref, t_ref, gate_ref, g_ref, loss_ref, dx_ref, df_ref, dgate_ref, dg_ref):
        tg = t_ref[...]

        def fl(x, fo_, gate_, g_):
            x4 = x + gate_ * fo_
            y = (x4 * lax.rsqrt(jnp.mean(x4 * x4, axis=-1, keepdims=True) + NORM_EPS)) * g_
            return 0.5 * jnp.sum(jnp.mean(jnp.square(y - tg), axis=-1))

        val, vjp = jax.vjp(fl, x_ref[...], f_ref[...], gate_ref[...], g_ref[...])
        dx, dfo, dgate, dg = vjp(jnp.ones((), F32))
        dx_ref[...] = dx
        df_ref[...] = dfo
        i = pl.program_id(0)

        @pl.when(i == 0)
        def _():
            loss_ref[...] = jnp.zeros_like(loss_ref)
            dgate_ref[...] = jnp.zeros_like(dgate_ref)
            dg_ref[...] = jnp.zeros_like(dg_ref)

        loss_ref[...] += jnp.full(loss_ref.shape, val, F32)
        dgate_ref[...] += dgate
        dg_ref[...] += dg

    tile = pl.BlockSpec((tr, d), lambda i: (i, 0))
    vec = pl.BlockSpec((1, d), lambda i: (0, 0))
    return pl.pallas_call(
        body, name="loss_head", grid=(T // tr,),
        in_specs=[tile, tile, tile, vec, vec],
        out_specs=[pl.BlockSpec((8, LANES), lambda i: (0, 0)), tile, tile, vec, vec],
        out_shape=[jax.ShapeDtypeStruct((8, LANES), F32), jax.ShapeDtypeStruct((T, d), F32),
                   jax.ShapeDtypeStruct((T, d), F32), jax.ShapeDtypeStruct((1, d), F32),
                   jax.ShapeDtypeStruct((1, d), F32)],
        compiler_params=_cparams(("arbitrary",)),
    )(x3, fo, tgt, gate, g)


def sum_adam(parts, w, m, v, name, lead=None):
    P, R, Cc = parts.shape
    tc = _pick(Cc, 1024)
    tr = _pick(R, max(8, (256 * 1024) // tc), 8)

    def body(p_ref, w_ref, m_ref, v_ref, g_ref, d_ref, nm_ref, nv_ref):
        g = p_ref[0].astype(F32)
        for s in range(1, P):
            g = g + p_ref[s].astype(F32)
        m_new = ADAM_B1 * m_ref[...] + (1.0 - ADAM_B1) * g
        v_new = ADAM_B2 * v_ref[...] + (1.0 - ADAM_B2) * jnp.square(g)
        m_hat = m_new / (1.0 - ADAM_B1 ** ADAM_STEP)
        v_hat = v_new / (1.0 - ADAM_B2 ** ADAM_STEP)
        g_ref[...] = g
        d_ref[...] = -ADAM_LR * (m_hat / (jnp.sqrt(v_hat) + ADAM_EPS) + ADAM_WD * w_ref[...])
        nm_ref[...] = m_new
        nv_ref[...] = v_new

    if lead is None:
        pspec = pl.BlockSpec((tr, tc), lambda i, j: (i, j))
    else:
        pspec = pl.BlockSpec((None, tr, tc), lambda i, j: (lead, i, j))
    ospec = pl.BlockSpec((tr, tc), lambda i, j: (i, j))
    return pl.pallas_call(
        body, name=name, grid=(R // tr, Cc // tc),
        in_specs=[pl.BlockSpec((P, tr, tc), lambda i, j: (0, i, j)), pspec, pspec, pspec],
        out_specs=[ospec] * 4,
        out_shape=[jax.ShapeDtypeStruct((R, Cc), F32)] * 4,
        compiler_params=_cparams(("parallel", "parallel")),
    )(parts, w, m, v)


def _me():
    return lax.axis_index("x"), lax.axis_index("y"), lax.axis_index("c")


def _peer(p):
    x, y, c = _me()
    px = 1 - x if p & 4 else x
    py = 1 - y if p & 2 else y
    pc = 1 - c if p & 1 else c
    return (px, py, pc), 4 * px + 2 * py + pc


def _block_view(ref, axis, idx, r, c):
    if axis is None:
        return ref.at[idx]
    if axis == 0:
        return ref.at[pl.ds(idx * r, r), :]
    return ref.at[:, pl.ds(idx * c, c)]


def _exchange_copies(src_of, dst_of, ssem, rsem, lsem):
    x, y, c = _me()
    me = 4 * x + 2 * y + c
    local = pltpu.make_async_copy(src_of(me), dst_of(me), lsem)
    sends, recvs = [], []
    for p in range(1, N_DEV):
        dev, idx = _peer(p)
        sends.append(pltpu.make_async_remote_copy(src_ref=src_of(idx), dst_ref=dst_of(me), send_sem=ssem(p),
                                                  recv_sem=rsem(p), device_id=dev,
                                                  device_id_type=pl.DeviceIdType.MESH))
        recvs.append(pltpu.make_async_remote_copy(src_ref=src_of(idx), dst_ref=dst_of(idx), send_sem=ssem(p),
                                                  recv_sem=rsem(p), device_id=dev,
                                                  device_id_type=pl.DeviceIdType.MESH))
    return local, sends, recvs


def _exchange_start(*args):
    local, sends, _ = _exchange_copies(*args)
    local.start()
    for cp in sends:
        cp.start()


def _exchange_wait(*args):
    local, sends, recvs = _exchange_copies(*args)
    for cp in recvs:
        cp.wait_recv()
    for cp in sends:
        cp.wait_send()
    local.wait()


def _exchange(src_of, dst_of, send_sems, recv_sems, local_sem):
    args = (src_of, dst_of, lambda p: send_sems.at[p], lambda p: recv_sems.at[p], local_sem)
    _exchange_start(*args)
    _exchange_wait(*args)


def _fused_exchanges(pairs, specs, send_sems, recv_sems, local_sems, first=None, last=None):
    def args(j):
        src, dst = pairs[j]
        kind, axis, r, c = specs[j]
        if kind == "ag":
            src_of = lambda idx: src
            dst_of = lambda idx: _block_view(dst, axis, idx, r, c)
        else:
            src_of = lambda idx: _block_view(src, axis, idx, r, c)
            dst_of = lambda idx: dst.at[idx]
        return (src_of, dst_of, lambda p: send_sems.at[j, p], lambda p: recv_sems.at[j, p], local_sems.at[j])

    if first is not None:
        @pl.when(first)
        def _():
            for j in range(len(pairs)):
                _exchange_start(*args(j))

    if last is not None:
        @pl.when(last)
        def _():
            for j in range(len(pairs)):
                _exchange_wait(*args(j))


def _exchange_out_shapes(arrs, specs):
    out = []
    for a, (kind, axis, r, c) in zip(arrs, specs):
        if kind == "rs":
            out.append(jax.ShapeDtypeStruct((N_DEV, r, c), a.dtype))
        else:
            out.append(jax.ShapeDtypeStruct((N_DEV * r, c) if axis == 0 else (r, N_DEV * c), a.dtype))
    return out


def _exchange_specs(kind, arrs, axes):
    specs = []
    for a, axis in zip(arrs, axes):
        if kind == "ag":
            r, c = a.shape
        elif axis == 0:
            r, c = a.shape[0] // N_DEV, a.shape[1]
        else:
            r, c = a.shape[0], a.shape[1] // N_DEV
        specs.append((kind, axis, r, c))
    return specs


def _exchange_sems(n):
    return [pltpu.SemaphoreType.DMA((n, N_DEV)), pltpu.SemaphoreType.DMA((n, N_DEV)), pltpu.SemaphoreType.DMA((n,))]


_SEMS = [pltpu.SemaphoreType.DMA((N_DEV,)), pltpu.SemaphoreType.DMA((N_DEV,)), pltpu.SemaphoreType.DMA]
_ANY = pl.BlockSpec(memory_space=pl.ANY)


def all_gather(x, axis, name):
    r, c = x.shape
    shape = (N_DEV * r, c) if axis == 0 else (r, N_DEV * c)

    def body(x_ref, o_ref, send_sems, recv_sems, local_sem):
        _exchange(lambda idx: x_ref, lambda idx: _block_view(o_ref, axis, idx, r, c), send_sems, recv_sems, local_sem)

    return pl.pallas_call(
        body, name=name, in_specs=[_ANY], out_specs=_ANY,
        out_shape=jax.ShapeDtypeStruct(shape, x.dtype), scratch_shapes=_SEMS,
    )(x)


def all_gather_stack(x, name):
    r, c = x.shape

    def body(x_ref, o_ref, send_sems, recv_sems, local_sem):
        _exchange(lambda idx: x_ref, lambda idx: o_ref.at[idx], send_sems, recv_sems, local_sem)

    return pl.pallas_call(
        body, name=name, in_specs=[_ANY], out_specs=_ANY,
        out_shape=jax.ShapeDtypeStruct((N_DEV, r, c), x.dtype), scratch_shapes=_SEMS,
    )(x)


def reduce_scatter_exchange(g, axis, name):
    if axis is None:
        _, r, c = g.shape
    elif axis == 0:
        r, c = g.shape[0] // N_DEV, g.shape[1]
    else:
        r, c = g.shape[0], g.shape[1] // N_DEV

    def body(g_ref, o_ref, send_sems, recv_sems, local_sem):
        _exchange(lambda idx: _block_view(g_ref, axis, idx, r, c), lambda idx: o_ref.at[idx],
                  send_sems, recv_sems, local_sem)

    return pl.pallas_call(
        body, name=name, in_specs=[_ANY], out_specs=_ANY,
        out_shape=jax.ShapeDtypeStruct((N_DEV, r, c), g.dtype), scratch_shapes=_SEMS,
    )(g)


def _pack(arrs, quantum=8 * LANES):
    flat = jnp.concatenate([a.reshape(-1).astype(F32) for a in arrs])
    pad = (-flat.shape[0]) % quantum
    return jnp.pad(flat, (0, pad)).reshape(-1, LANES)


def _unpack(flat2d, shapes, lead=()):
    flat = flat2d.reshape(lead + (-1,))
    out, off = [], 0
    for s in shapes:
        n = int(np.prod(s))
        out.append(flat[..., off:off + n].reshape(lead + tuple(s)))
        off += n
    return out


def _gather_lastdim(stk):
    return jnp.moveaxis(stk, 0, -2).reshape(stk.shape[1:-1] + (N_DEV * stk.shape[-1],))


def _gather_dim(stk, dim):
    moved = jnp.moveaxis(stk, 0, dim)
    sh = list(stk.shape[1:])
    sh[dim] = sh[dim] * N_DEV
    return moved.reshape(sh)


def _scatter_dim(full, dim):
    sh = list(full.shape)
    sh[dim:dim + 1] = [N_DEV, sh[dim] // N_DEV]
    return jnp.moveaxis(full.reshape(sh), dim, 0)


def _head_group_matrix(tc):
    return np.kron(np.eye(tc // HEAD, dtype=np.float32), np.ones((HEAD, HEAD), np.float32))


_SCAN_COMM = {0: ("w13_0", "w13_1"), 1: ("win", "w2_0", "w2_1", "wout", "wo")}
_W_AXIS = dict(wr=0, wk=0, wv=0, wo=0, win=1, wout=0, w13_0=1, w13_1=1, w2_0=0, w2_1=0)


def _build_forward(ctx2d, T, D, shards=None, sink=None):
    L = ctx2d.shape[0]
    N = L + T

    def make_scan(d):
        keys = _SCAN_COMM[d] if shards is not None else ()
        axes = tuple(_W_AXIS[k] for k in keys)

        def run_fwd(tok, sh):
            return _wkv_fwd_call(*tok, L, d == 1, tuple(sh), _exchange_specs("ag", sh, axes))

        @jax.custom_vjp
        def op(tok, sh):
            outs = run_fwd(tok, sh)
            return (outs[0],) + tuple(outs[3:])

        def op_fwd(tok, sh):
            outs = run_fwd(tok, sh)
            return (outs[0],) + tuple(outs[3:]), (tok, outs[1], outs[2])

        def op_bwd(res, cts):
            tok, sa, sprev = res
            dg = tuple(cts[1:])
            outs = _wkv_bwd_call(*tok, sa, sprev, cts[0], L, d == 1, dg, _exchange_specs("rs", dg, axes))
            for k, recv in zip(keys, outs[6:]):
                sink[k] = recv
            return tuple(outs[:6]), tuple(jnp.zeros(shards[k].shape, shards[k].dtype) for k in keys)

        op.defvjp(op_fwd, op_bwd)
        return op, keys

    scans = [make_scan(0), make_scan(1)]
    gm = _head_group_matrix(LANES)
    tr_row = _pick(math.gcd(L, T), 128, 8)
    op_norm = make_rowwise("norm_mod", _f_norm_mod, (F32,), tr_row, D, nb0=L // tr_row)
    op_res = [make_rowwise(f"res_norm_mod{i}", _f_res_norm_mod, (F32, BF16), _pick(T, 128, 8), D) for i in range(3)]
    op_prep = make_rowwise("wkv_prep", _f_prep, (F32,) * 7, _pick(N, 256, 8), LANES, consts=(gm,))
    op_read = make_rowwise("wkv_readout", _f_readout, (BF16,), _pick(T, 256, 8), LANES, consts=(gm,))

    def v3(a):
        return a.reshape(a.shape[0], 1, a.shape[-1])

    def fwd(xin, Ps, Wb):
        modx, modc = Ps["modx"], Ps["modc"]
        cat = jnp.concatenate([ctx2d, xin], axis=0)
        seg = lambda a, b: jnp.stack([a, b])[:, None, :]
        (hcat,) = op_norm((cat,), (Ps["n1"][0][None, None, :], seg(modc[0], modx[0, 0]), seg(modc[1], modx[0, 1])))
        xr, xw, xk, xv, xa, xg = shift_mix(hcat, Ps["mix"][:, None, :], L)
        r = linear(xr, Wb["wr"], F32, "wr")
        k = linear(xk, Wb["wk"], F32, "wk")
        v = linear(xv, Wb["wv"], F32, "wv")
        gl = jax.nn.sigmoid(linear(xg, Wb["g1"], F32, "g1"))
        g = linear(gl.astype(BF16), Wb["g2"], F32, "g2")
        tw = jnp.tanh(linear(xw, Wb["w1"], F32, "w1")).astype(BF16)
        ta = linear(xa, Wb["a1"], F32, "a1").astype(BF16)
        lw = [linear(tw[:, LORA_PAD * d:LORA_PAD * (d + 1)], Wb["w2d"][d], F32, f"w2_{d}") for d in range(2)]
        la = [linear(ta[:, LORA_PAD * d:LORA_PAD * (d + 1)], Wb["a2d"][d], F32, f"a2_{d}") for d in range(2)]
        kk, dec0, dec1, kd0, kd1, as0, as1 = op_prep(
            (k, lw[0], lw[1], la[0], la[1]),
            (v3(Ps["kk"]), v3(Ps["ka"]), Ps["w0"][0][None, None, :], Ps["w0"][1][None, None, :],
             Ps["a0"][0][None, None, :], Ps["a0"][1][None, None, :]))
        r2, kk2, v2 = _tile_heads(r), _tile_heads(kk), _tile_heads(v)
        ys = []
        Wb = dict(Wb)
        for d, (dec, kd, sg) in enumerate(((dec0, kd0, as0), (dec1, kd1, as1))):
            op, keys = scans[d]
            outs = op((r2, _tile_heads(dec), _tile_heads(kd), kk2, _tile_heads(sg), v2),
                      tuple(shards[k] for k in keys))
            ys.append(outs[0][L:, :, :HEAD].reshape(T, D))
            Wb.update(zip(keys, outs[1:]))
        (o,) = op_read((ys[0], ys[1], r[L:], kd0[L:], kd1[L:], v[L:], g[L:]),
                       (v3(Ps["rk"]), v3(Ps["lnw"]), v3(Ps["lnb"])))
        att = linear(o, Wb["wo"], F32, "wo")
        x1, h2 = op_res[0]((xin, att), (modx[0, 2][None, None, :], Ps["n2"][0][None, None, :],
                                        modx[0, 3][None, None, :], modx[0, 4][None, None, :]))
        act = swiglu_act(linear(h2, Wb["w13_0"], BF16, "w13_0"), "swiglu0")
        f0 = linear(act, Wb["w2_0"], F32, "w2_0")
        x2, h = op_res[1]((x1, f0), (modx[0, 5][None, None, :], Ps["n1"][1][None, None, :],
                                     modx[1, 0][None, None, :], modx[1, 1][None, None, :]))
        guc = linear(h, Wb["win"], BF16, "win")
        p = gated_conv(guc, Ps["conv"][:, None, :])
        cv = linear(p, Wb["wout"], F32, "wout")
        x3, h2b = op_res[2]((x2, cv), (modx[1, 2][None, None, :], Ps["n2"][1][None, None, :],
                                       modx[1, 3][None, None, :], modx[1, 4][None, None, :]))
        act1 = swiglu_act(linear(h2b, Wb["w13_1"], BF16, "w13_1"), "swiglu1")
        f1 = linear(act1, Wb["w2_1"], F32, "w2_1")
        return x3, f1


    return fwd


def kernel(x, c, ctx, c_ctx, norm1_g, norm2_g, ada_w, ada_b, rw_mix, rw_wr, rw_wk, rw_wv, rw_wo, rw_w0, rw_w1, rw_w2, rw_a0, rw_a1, rw_a2, rw_g1, rw_g2, rw_kk, rw_ka, rw_rk, rw_lnw, rw_lnb, sc_win, sc_conv, sc_wout, ffn_w13, ffn_w2, final_g, loss_target, m_c_ctx, m_norm1_g, m_norm2_g, m_ada_w, m_ada_b, m_rw_mix, m_rw_wr, m_rw_wk, m_rw_wv, m_rw_wo, m_rw_w0, m_rw_w1, m_rw_w2, m_rw_a0, m_rw_a1, m_rw_a2, m_rw_g1, m_rw_g2, m_rw_kk, m_rw_ka, m_rw_rk, m_rw_lnw, m_rw_lnb, m_sc_win, m_sc_conv, m_sc_wout, m_ffn_w13, m_ffn_w2, m_final_g, v_c_ctx, v_norm1_g, v_norm2_g, v_ada_w, v_ada_b, v_rw_mix, v_rw_wr, v_rw_wk, v_rw_wv, v_rw_wo, v_rw_w0, v_rw_w1, v_rw_w2, v_rw_a0, v_rw_a1, v_rw_a2, v_rw_g1, v_rw_g2, v_rw_kk, v_rw_ka, v_rw_rk, v_rw_lnw, v_rw_lnb, v_sc_win, v_sc_conv, v_sc_wout, v_ffn_w13, v_ffn_w2, v_final_g):
    W = dict(c_ctx=c_ctx, norm1_g=norm1_g, norm2_g=norm2_g, ada_w=ada_w, ada_b=ada_b, rw_mix=rw_mix, rw_wr=rw_wr,
             rw_wk=rw_wk, rw_wv=rw_wv, rw_wo=rw_wo, rw_w0=rw_w0, rw_w1=rw_w1, rw_w2=rw_w2, rw_a0=rw_a0, rw_a1=rw_a1,
             rw_a2=rw_a2, rw_g1=rw_g1, rw_g2=rw_g2, rw_kk=rw_kk, rw_ka=rw_ka, rw_rk=rw_rk, rw_lnw=rw_lnw,
             rw_lnb=rw_lnb, sc_win=sc_win, sc_conv=sc_conv, sc_wout=sc_wout, ffn_w13=ffn_w13, ffn_w2=ffn_w2,
             final_g=final_g)
    Mo = dict(c_ctx=m_c_ctx, norm1_g=m_norm1_g, norm2_g=m_norm2_g, ada_w=m_ada_w, ada_b=m_ada_b, rw_mix=m_rw_mix,
              rw_wr=m_rw_wr, rw_wk=m_rw_wk, rw_wv=m_rw_wv, rw_wo=m_rw_wo, rw_w0=m_rw_w0, rw_w1=m_rw_w1,
              rw_w2=m_rw_w2, rw_a0=m_rw_a0, rw_a1=m_rw_a1, rw_a2=m_rw_a2, rw_g1=m_rw_g1, rw_g2=m_rw_g2,
              rw_kk=m_rw_kk, rw_ka=m_rw_ka, rw_rk=m_rw_rk, rw_lnw=m_rw_lnw, rw_lnb=m_rw_lnb, sc_win=m_sc_win,
              sc_conv=m_sc_conv, sc_wout=m_sc_wout, ffn_w13=m_ffn_w13, ffn_w2=m_ffn_w2, final_g=m_final_g)
    Vo = dict(c_ctx=v_c_ctx, norm1_g=v_norm1_g, norm2_g=v_norm2_g, ada_w=v_ada_w, ada_b=v_ada_b, rw_mix=v_rw_mix,
              rw_wr=v_rw_wr, rw_wk=v_rw_wk, rw_wv=v_rw_wv, rw_wo=v_rw_wo, rw_w0=v_rw_w0, rw_w1=v_rw_w1,
              rw_w2=v_rw_w2, rw_a0=v_rw_a0, rw_a1=v_rw_a1, rw_a2=v_rw_a2, rw_g1=v_rw_g1, rw_g2=v_rw_g2,
              rw_kk=v_rw_kk, rw_ka=v_rw_ka, rw_rk=v_rw_rk, rw_lnw=v_rw_lnw, rw_lnb=v_rw_lnb, sc_win=v_sc_win,
              sc_conv=v_sc_conv, sc_wout=v_sc_wout, ffn_w13=v_ffn_w13, ffn_w2=v_ffn_w2, final_g=v_final_g)
    names = list(W)

    x2d = x[0]
    ctx2d = ctx[0]
    tgt = loss_target[0]
    T, D = x2d.shape
    L = ctx2d.shape[0]
    N = L + T
    nh = D // HEAD
    mx, my, mc = _me()
    me = 4 * mx + 2 * my + mc
    dloc = D // N_DEV

    lr = rw_w1.shape[-1]
    pad_r = LORA_PAD - lr
    w1p = jnp.pad(rw_w1[0], ((0, 0), (0, 0), (0, pad_r)))
    a1p = jnp.pad(rw_a1[0], ((0, 0), (0, 0), (0, pad_r)))
    w2p = jnp.pad(rw_w2[0], ((0, 0), (0, pad_r), (0, 0)))
    a2p = jnp.pad(rw_a2[0], ((0, 0), (0, pad_r), (0, 0)))
    small_loc = [rw_mix[0], rw_w0[0], rw_a0[0], sc_conv[0], w1p, a1p, w2p, a2p, rw_g1[0], rw_g2[0]]
    small_dim = [1, 1, 1, 1, 1, 1, 2, 2, 0, 1]
    small_shapes = [a.shape for a in small_loc]
    sm_all = all_gather_stack(_pack(small_loc), "ag_small_w")
    sm_parts = _unpack(sm_all, small_shapes, lead=(N_DEV,))
    mix_f, w0_f, a0_f, conv_f, w1_f, a1_f, w2_f, a2_f, g1_f, g2_f = [
        _gather_dim(p, dm) for p, dm in zip(sm_parts, small_dim)]

    c_all = all_gather_stack(jnp.pad(c, ((0, 7), (0, 0))), "ag_c")[:, 0, :]
    cond_pre = jnp.concatenate([c_all, c_ctx[None, :], jnp.zeros((7, D), F32)], axis=0)
    cond_rows = jax.nn.silu(cond_pre)
    ncol = ada_w.shape[-1]
    mod_loc = []
    for i in range(2):
        bi = lax.dynamic_slice(ada_b[i], (me * ncol,), (ncol,))
        mod_loc.append(_mm(cond_rows, ada_w[i], out_dtype=F32, name=f"ada_fwd{i}") + bi[None, :])
    mod_all = all_gather_stack(jnp.concatenate(mod_loc, axis=0), "ag_mod")
    mod_full = _gather_lastdim(mod_all).reshape(2, 16, 6, D)
    mod_x = lax.dynamic_index_in_dim(mod_full, me, axis=1, keepdims=False)
    mod_c = mod_full[0, 8, :2, :]

    def ag_w(wl, axis, name):
        return all_gather(wl.astype(BF16), axis, name)

    shards = dict(wo=rw_wo[0], win=sc_win[0], wout=sc_wout[0], w13_0=ffn_w13[0], w13_1=ffn_w13[1],
                  w2_0=ffn_w2[0], w2_1=ffn_w2[1])
    shards = {k_: a.astype(BF16) for k_, a in shards.items()}
    sink = {}
    Wb = dict(
        wr=ag_w(rw_wr[0], 0, "ag_wr"), wk=ag_w(rw_wk[0], 0, "ag_wk"), wv=ag_w(rw_wv[0], 0, "ag_wv"),
        w1=jnp.concatenate([w1_f[0], w1_f[1]], axis=1).astype(BF16),
        a1=jnp.concatenate([a1_f[0], a1_f[1]], axis=1).astype(BF16),
        w2d=w2_f.astype(BF16), a2d=a2_f.astype(BF16),
        g1=g1_f.astype(BF16), g2=g2_f.astype(BF16),
    )
    Ps = dict(n1=norm1_g, n2=norm2_g, modx=mod_x, modc=mod_c, mix=mix_f, w0=w0_f, a0=a0_f, conv=conv_f,
              kk=rw_kk, ka=rw_ka, rk=rw_rk.reshape(1, D), lnw=rw_lnw, lnb=rw_lnb)

    fwd = _build_forward(ctx2d, T, D, shards, sink)
    (x3, f1), vjp_fn = jax.vjp(fwd, x2d, Ps, Wb)
    loss_acc, dx3, df1, dgate, dfinal = loss_head(x3, f1, tgt, mod_x[1, 5][None, :], final_g[None, :])
    dx, dPs, dWb = vjp_fn((dx3, df1))
    loss = lax.psum(loss_acc[0, 0], ("x", "y", "c"))

    dmodx = dPs["modx"].at[1, 5].add(dgate[0])
    dmodc = jnp.concatenate([dPs["modc"], jnp.zeros((4, D), F32)], axis=0)
    drow = jnp.stack([dmodx.reshape(2, 6 * D), jnp.stack([dmodc.reshape(6 * D), jnp.zeros((6 * D,), F32)])], axis=1)
    drow_all = all_gather_stack(drow.reshape(4, 6 * D), "ag_dmod").reshape(N_DEV, 2, 2, 6 * D)
    dctx_tot = drow_all[0, :, 1, :]
    for s in range(1, N_DEV):
        dctx_tot = dctx_tot + drow_all[s, :, 1, :]
    dmod_rows = jnp.concatenate([jnp.moveaxis(drow_all[:, :, 0, :], 0, 1), dctx_tot[:, None, :],
                                 jnp.zeros((2, 7, 6 * D), F32)], axis=1)
    grad_ada_b = dctx_tot
    for s in range(N_DEV):
        grad_ada_b = grad_ada_b + drow_all[s, :, 0, :]
    dmod_mine = lax.dynamic_slice_in_dim(dmod_rows, me * ncol, ncol, axis=2)
    g_ada_w = [_mm(cond_rows, dmod_mine[i], ta=True, out_dtype=F32, name=f"ada_dw{i}") for i in range(2)]
    dcond_part = _mm(dmod_mine[0], ada_w[0], tb=True, out_dtype=F32, name="ada_dcond")[8]

    rep_names = ["c_ctx", "norm1_g", "norm2_g", "rw_kk", "rw_ka", "rw_rk", "rw_lnw", "rw_lnb", "final_g"]
    rep_part = [dcond_part, dPs["n1"], dPs["n2"], dPs["kk"], dPs["ka"], dPs["rk"].reshape(W["rw_rk"].shape),
                dPs["lnw"], dPs["lnb"], dfinal[0]]
    rep_shapes = [W[n_].shape for n_ in rep_names]
    rep_all = all_gather_stack(_pack(rep_part), "ag_rep_grads")
    sg = jax.nn.sigmoid(c_ctx)
    dsilu = sg * (1.0 + c_ctx * (1.0 - sg))
    rep_scale = _pack([dsilu] + [jnp.ones(s, F32) for s in rep_shapes[1:]])
    rep_all = rep_all * rep_scale[None]
    rep_w = _pack([W[n_] for n_ in rep_names])
    rep_m = _pack([Mo[n_] for n_ in rep_names])
    rep_v = _pack([Vo[n_] for n_ in rep_names])
    rep_out = sum_adam(rep_all, rep_w, rep_m, rep_v, "adam_rep")
    results = {}
    for nm_, vals in zip(rep_names, zip(*[_unpack(o, rep_shapes) for o in rep_out])):
        results[nm_] = vals

    results["ada_b"] = tuple(sum_adam(grad_ada_b.reshape(1, 2 * 6, D), ada_b.reshape(12, D), m_ada_b.reshape(12, D),
                                      v_ada_b.reshape(12, D), "adam_ada_b"))
    results["ada_b"] = tuple(o.reshape(ada_b.shape) for o in results["ada_b"])

    outs = [sum_adam(g_ada_w[i][None], ada_w, m_ada_w, v_ada_w, f"adam_ada_w{i}", lead=i) for i in range(2)]
    results["ada_w"] = tuple(jnp.stack([outs[0][q], outs[1][q]]) for q in range(4))

    dw1 = jnp.stack([dWb["w1"][:, :LORA_PAD], dWb["w1"][:, LORA_PAD:]]).astype(F32)
    da1 = jnp.stack([dWb["a1"][:, :LORA_PAD], dWb["a1"][:, LORA_PAD:]]).astype(F32)
    small_g = [dPs["mix"], dPs["w0"], dPs["a0"], dPs["conv"], dw1, da1, dWb["w2d"].astype(F32),
               dWb["a2d"].astype(F32), dWb["g1"].astype(F32), dWb["g2"].astype(F32)]
    blocks = [_scatter_dim(gf, dm) for gf, dm in zip(small_g, small_dim)]
    packed = jnp.concatenate([b.reshape(N_DEV, -1) for b in blocks], axis=1)
    padc = (-packed.shape[1]) % (8 * LANES)
    packed = jnp.pad(packed, ((0, 0), (0, padc))).reshape(N_DEV, -1, LANES)
    sm_recv = reduce_scatter_exchange(packed, None, "rs_small")
    small_names = ["rw_mix", "rw_w0", "rw_a0", "sc_conv", "rw_w1", "rw_a1", "rw_w2", "rw_a2", "rw_g1", "rw_g2"]

    def padded_local(nm_, src):
        a = src[nm_][0]
        if nm_ in ("rw_w1", "rw_a1"):
            return jnp.pad(a, ((0, 0), (0, 0), (0, pad_r)))
        if nm_ in ("rw_w2", "rw_a2"):
            return jnp.pad(a, ((0, 0), (0, pad_r), (0, 0)))
        return a

    sm_out = sum_adam(sm_recv, _pack([padded_local(n_, W) for n_ in small_names]),
                      _pack([padded_local(n_, Mo) for n_ in small_names]),
                      _pack([padded_local(n_, Vo) for n_ in small_names]), "adam_small")
    for nm_, vals in zip(small_names, zip(*[_unpack(o, small_shapes) for o in sm_out])):
        if nm_ in ("rw_w1", "rw_a1"):
            vals = tuple(a[:, :, :lr] for a in vals)
        if nm_ in ("rw_w2", "rw_a2"):
            vals = tuple(a[:, :lr, :] for a in vals)
        results[nm_] = tuple(a[None] for a in vals)

    def rs_adam(key, nm_, lead):
        recv = sink[key] if key in sink else reduce_scatter_exchange(dWb[key], _W_AXIS[key], "rs_" + key)
        return sum_adam(recv, W[nm_], Mo[nm_], Vo[nm_], "adam_" + key, lead=lead)

    for nm_, key in (("rw_wr", "wr"), ("rw_wk", "wk"), ("rw_wv", "wv"), ("rw_wo", "wo"), ("sc_win", "win"),
                     ("sc_wout", "wout")):
        results[nm_] = tuple(a[None] for a in rs_adam(key, nm_, 0))
    for nm_, key in (("ffn_w13", "w13"), ("ffn_w2", "w2")):
        outs = [rs_adam(f"{key}_{i}", nm_, i) for i in range(2)]
        results[nm_] = tuple(jnp.stack([outs[0][q], outs[1][q]]) for q in range(4))

    grads = [results[n_][0] for n_ in names]
    deltas = [results[n_][1] for n_ in names]
    new_m = [results[n_][2] for n_ in names]
    new_v = [results[n_][3] for n_ in names]
    return (loss, dx[None], *grads, *deltas, *new_m, *new_v)
```

```python
import functools
import math

import numpy as np
import jax
import jax.numpy as jnp
from jax import lax
from jax.experimental import pallas as pl
from jax.experimental.pallas import tpu as pltpu

F32 = jnp.float32
BF16 = jnp.bfloat16

N_DEV = 8
HEAD = 64
LANES = 128
GRID_W = 64
LORA_PAD = 128
NORM_EPS = 1e-6
GN_EPS = 64e-5
ADAM_LR, ADAM_B1, ADAM_B2, ADAM_EPS, ADAM_WD, ADAM_STEP = 0.001, 0.9, 0.999, 1e-08, 0.01, 10
VMEM_LIMIT = 52 * 1024 * 1024
SCAN_CHUNK = 8
HI = lax.Precision.HIGHEST


def _cparams(sem):
    return pltpu.CompilerParams(dimension_semantics=sem, vmem_limit_bytes=VMEM_LIMIT)


def _pick(n, cap, quantum=LANES):
    best = None
    for t in range(quantum, min(n, cap) + 1, quantum):
        if n % t == 0:
            best = t
    return n if best is None else best


MM_VMEM_BUDGET = 36 * 1024 * 1024


def _divisors(n, cap, quantum=LANES):
    ds = [t for t in range(quantum, min(n, cap) + 1, quantum) if n % t == 0]
    return sorted(ds, reverse=True) or [n]


def _mm_tiles(M, N, K, sa, sb, so):
    tms, tns, tks = _divisors(M, 1024), _divisors(N, 1024), _divisors(K, 2816)
    im = jn = ik = 0

    def est(tm, tn, tk):
        b = 2 * (tm * tk * sa + tk * tn * sb) + 2 * tm * tn * so + tm * tn * 4
        b += tm * tk * 2 if sa == 4 else 0
        b += tk * tn * 2 if sb == 4 else 0
        return b + (tm * tn * 4 if tk < K else 0)

    while est(tms[im], tns[jn], tks[ik]) > MM_VMEM_BUDGET:
        if tms[im] >= tns[jn] and im + 1 < len(tms):
            im += 1
        elif jn + 1 < len(tns):
            jn += 1
        elif im + 1 < len(tms):
            im += 1
        elif ik + 1 < len(tks):
            ik += 1
        else:
            break
    return tms[im], tns[jn], tks[ik]


def _mm(a, b, *, ta=False, tb=False, out_dtype, name):
    if ta:
        K, M = a.shape
    else:
        M, K = a.shape
    if tb:
        N, Kb = b.shape
    else:
        Kb, N = b.shape
    assert K == Kb, (a.shape, b.shape, ta, tb)
    tm, tn, tk = _mm_tiles(M, N, K, a.dtype.itemsize, b.dtype.itemsize, jnp.dtype(out_dtype).itemsize)
    nk = K // tk
    dims = (((0 if ta else 1,), (1 if tb else 0,)), ((), ()))

    def body(a_ref, b_ref, o_ref, *acc):
        part = lax.dot_general(a_ref[...].astype(BF16), b_ref[...].astype(BF16), dims, preferred_element_type=F32)
        if nk == 1:
            o_ref[...] = part.astype(o_ref.dtype)
            return
        acc_ref, = acc
        k = pl.program_id(2)

        @pl.when(k == 0)
        def _():
            acc_ref[...] = part

        @pl.when(k > 0)
        def _():
            acc_ref[...] += part

        @pl.when(k == nk - 1)
        def _():
            o_ref[...] = acc_ref[...].astype(o_ref.dtype)

    a_spec = pl.BlockSpec((tk, tm), lambda i, j, k: (k, i)) if ta else pl.BlockSpec((tm, tk), lambda i, j, k: (i, k))
    b_spec = pl.BlockSpec((tn, tk), lambda i, j, k: (j, k)) if tb else pl.BlockSpec((tk, tn), lambda i, j, k: (k, j))
    return pl.pallas_call(
        body, name=name, grid=(M // tm, N // tn, nk),
        in_specs=[a_spec, b_spec],
        out_specs=pl.BlockSpec((tm, tn), lambda i, j, k: (i, j)),
        out_shape=jax.ShapeDtypeStruct((M, N), out_dtype),
        scratch_shapes=[pltpu.VMEM((tm, tn), F32)] if nk > 1 else [],
        compiler_params=_cparams(("parallel", "parallel", "arbitrary")),
    )(a, b)


@functools.partial(jax.custom_vjp, nondiff_argnums=(2, 3))
def linear(a, w, out_dtype, name):
    return _mm(a, w, out_dtype=out_dtype, name=name + "_fwd")


def _linear_fwd(a, w, out_dtype, name):
    return _mm(a, w, out_dtype=out_dtype, name=name + "_fwd"), (a, w)


def _linear_bwd(out_dtype, name, res, g):
    a, w = res
    da = _mm(g, w, tb=True, out_dtype=a.dtype, name=name + "_da")
    dw = _mm(a, g, ta=True, out_dtype=w.dtype, name=name + "_dw")
    return da, dw


linear.defvjp(_linear_fwd, _linear_bwd)


def _rw_specs(tiles, col_offs, vecs, consts, tr, tc, nb0):
    tile_specs = [pl.BlockSpec((tr, tc), functools.partial(lambda j, i, off: (i, j + off), off=off))
                  for _, off in zip(tiles, col_offs)]

    def vec_map(S):
        if S == 1:
            return lambda j, i: (0, 0, j)
        return lambda j, i: (jnp.where(i < nb0, 0, 1), 0, j)

    vec_specs = [pl.BlockSpec((None, 1, tc), vec_map(v.shape[0])) for v in vecs]
    const_specs = [pl.BlockSpec(c.shape, lambda j, i: (0, 0)) for c in consts]
    return tile_specs, vec_specs, const_specs


def _rw_forward(name, f, tiles, col_offs, vecs, consts, out_dtypes, tr, tc, nb0, width):
    n = tiles[0].shape[0]
    nt, nv, nc = len(tiles), len(vecs), len(consts)
    tile_specs, vec_specs, const_specs = _rw_specs(tiles, col_offs, vecs, consts, tr, tc, nb0)

    def body(*refs):
        ins = [r[...].astype(F32) for r in refs[:nt]] + [r[...] for r in refs[nt:nt + nv + nc]]
        outs = f(*ins)
        for o_ref, o in zip(refs[nt + nv + nc:], outs):
            o_ref[...] = o.astype(o_ref.dtype)

    return pl.pallas_call(
        body, name=name + "_fwd", grid=(width // tc, n // tr),
        in_specs=tile_specs + vec_specs + const_specs,
        out_specs=[pl.BlockSpec((tr, tc), lambda j, i: (i, j)) for _ in out_dtypes],
        out_shape=[jax.ShapeDtypeStruct((n, width), dt) for dt in out_dtypes],
        compiler_params=_cparams(("parallel", "parallel")),
    )(*tiles, *vecs, *consts)


def _rw_backward(name, f, tiles, col_offs, vecs, consts, douts, tr, tc, nb0, width):
    n = tiles[0].shape[0]
    nt, nv, nc, no = len(tiles), len(vecs), len(consts), len(douts)
    tile_specs, vec_specs, const_specs = _rw_specs(tiles, col_offs, vecs, consts, tr, tc, nb0)

    def body(*refs):
        t_in = [r[...].astype(F32) for r in refs[:nt]]
        v_in = [r[...] for r in refs[nt:nt + nv]]
        c_in = [r[...] for r in refs[nt + nv:nt + nv + nc]]
        d_in = tuple(r[...].astype(F32) for r in refs[nt + nv + nc:nt + nv + nc + no])
        o_refs = refs[nt + nv + nc + no:]
        _, vjp = jax.vjp(lambda *tv: tuple(f(*tv, *c_in)), *t_in, *v_in)
        grads = vjp(d_in)
        for o_ref, g in zip(o_refs[:nt], grads[:nt]):
            o_ref[...] = g.astype(o_ref.dtype)
        i = pl.program_id(1)
        for o_ref, g, v in zip(o_refs[nt:], grads[nt:], vecs):
            first = jnp.logical_or(i == 0, i == nb0) if v.shape[0] == 2 else i == 0

            @pl.when(first)
            def _(o_ref=o_ref, g=g):
                o_ref[...] = g

            @pl.when(jnp.logical_not(first))
            def _(o_ref=o_ref, g=g):
                o_ref[...] += g

    dout_specs = [pl.BlockSpec((tr, tc), lambda j, i: (i, j)) for _ in douts]
    out_specs = [pl.BlockSpec((tr, tc), lambda j, i: (i, j)) for _ in tiles] + list(vec_specs)
    out_shape = ([jax.ShapeDtypeStruct((n, width), t.dtype) for t in tiles]
                 + [jax.ShapeDtypeStruct(v.shape, F32) for v in vecs])
    return pl.pallas_call(
        body, name=name + "_bwd", grid=(width // tc, n // tr),
        in_specs=tile_specs + vec_specs + const_specs + dout_specs,
        out_specs=out_specs, out_shape=out_shape,
        compiler_params=_cparams(("parallel", "arbitrary")),
    )(*tiles, *vecs, *consts, *douts)


def make_rowwise(name, f, out_dtypes, tr, tc, consts=(), nb0=-1):
    consts = tuple(consts)

    @jax.custom_vjp
    def op(tiles, vecs):
        w = tiles[0].shape[1]
        return tuple(_rw_forward(name, f, tiles, (0,) * len(tiles), vecs, consts, out_dtypes, tr, min(tc, w), nb0, w))

    def op_fwd(tiles, vecs):
        return op(tiles, vecs), (tiles, vecs)

    def op_bwd(res, douts):
        tiles, vecs = res
        w = tiles[0].shape[1]
        g = _rw_backward(name, f, tiles, (0,) * len(tiles), vecs, consts, tuple(douts), tr, min(tc, w), nb0, w)
        return tuple(g[:len(tiles)]), tuple(g[len(tiles):])

    op.defvjp(op_fwd, op_bwd)
    return op


def _f_norm_mod(x, g, sh, sc):
    hn = x * lax.rsqrt(jnp.mean(x * x, axis=-1, keepdims=True) + NORM_EPS)
    return ((hn * g) * (1.0 + sc) + sh,)


def _f_res_norm_mod(x, y, gate, g, sh, sc):
    x1 = x + gate * y
    hn = x1 * lax.rsqrt(jnp.mean(x1 * x1, axis=-1, keepdims=True) + NORM_EPS)
    return x1, (hn * g) * (1.0 + sc) + sh


def _head_sum_3pass(t, gmat):
    hi = t.astype(BF16)
    r1 = t - hi.astype(F32)
    mid = r1.astype(BF16)
    lo = (r1 - mid.astype(F32)).astype(BF16)
    g = gmat.astype(BF16)
    dot = lambda u: jnp.dot(u, g, preferred_element_type=F32)
    return dot(hi) + dot(mid) + dot(lo)


@jax.custom_vjp
def _head_sum(t, gmat):
    return _head_sum_3pass(t, gmat)


def _head_sum_fwd(t, gmat):
    return _head_sum_3pass(t, gmat), gmat


def _head_sum_bwd(gmat, ct):
    return _head_sum_3pass(ct, gmat), None


_head_sum.defvjp(_head_sum_fwd, _head_sum_bwd)


def _f_prep(k, lw0, lw1, la0, la1, kkp, kap, w00, w01, a00, a01, gmat):
    t = k * kkp
    kk = t / jnp.maximum(jnp.sqrt(_head_sum(t * t, gmat)), 1e-12)
    outs = [kk]
    decs, kds, sigs = [], [], []
    for lw, la, w0, a0 in ((lw0, la0, w00, a00), (lw1, la1, w01, a01)):
        decs.append(jnp.exp(-jax.nn.sigmoid(w0 + lw) * float(np.exp(-0.5))))
        a = jax.nn.sigmoid(a0 + la)
        sigs.append(a)
        kds.append(k * (1.0 + (a - 1.0) * kap))
    return tuple(outs + decs + kds + sigs)


def _f_readout(y0, y1, r, kd0, kd1, v, g, rk, lnw, lnb, gmat):
    y = y0 + y1
    mu = _head_sum(y, gmat) * (1.0 / HEAD)
    d = y - mu
    var = _head_sum(d * d, gmat) * (1.0 / HEAD)
    o = d * lax.rsqrt(var + GN_EPS) * lnw + lnb
    bonus = _head_sum(r * (kd0 + kd1) * rk, gmat) * v
    return ((o + bonus) * g,)


def _f_swiglu(a, b):
    return (jax.nn.silu(a) * b,)


def swiglu_act(ab, name):
    t, f2 = ab.shape
    fdim = f2 // 2
    tr, tc = _pick(t, 512, 8), _pick(fdim, 512)
    offs = (0, fdim // tc)

    @jax.custom_vjp
    def op(ab_):
        return _rw_forward(name, _f_swiglu, (ab_, ab_), offs, (), (), (BF16,), tr, tc, -1, fdim)[0]

    def op_fwd(ab_):
        return op(ab_), ab_

    def op_bwd(ab_, dact):
        da, db = _rw_backward(name, _f_swiglu, (ab_, ab_), offs, (), (), (dact,), tr, tc, -1, fdim)
        return (jnp.concatenate([da, db], axis=1),)

    op.defvjp(op_fwd, op_bwd)
    return op(ab)


def _row_iota(n, tc):
    return lax.broadcasted_iota(jnp.int32, (n, tc), 0)


def _shift_rows(x, s, keep):
    n = x.shape[0]
    return jnp.where(keep, pltpu.roll(x, s % n, 0), 0.0)


def _unshift_rows(d, s, keep):
    n = d.shape[0]
    return pltpu.roll(jnp.where(keep, d, 0.0), (-s) % n, 0)


def _ctx_shift_spec(L, tc, quarter):
    row = _row_iota(L, tc)
    if quarter < 2:
        return 1, row >= 1
    return -1, row < L - 1


def _grid_shift_spec(T, tc, quarter):
    row = _row_iota(T, tc)
    col = jnp.bitwise_and(row, GRID_W - 1)
    if quarter == 0:
        return 1, col != 0
    if quarter == 1:
        return -1, col != GRID_W - 1
    if quarter == 2:
        return GRID_W, row >= GRID_W
    return -GRID_W, row < T - GRID_W


def _shift_mix_fwd_call(h, mix3, L):
    n, d = h.shape
    T = n - L
    tc = _pick(d // 4, 256)
    nq = (d // 4) // tc

    def body(h_ref, mix_ref, *o_refs):
        q = pl.program_id(0) // nq
        for quarter in range(4):
            @pl.when(q == quarter)
            def _(quarter=quarter):
                for lo, cnt, spec in ((0, L, _ctx_shift_spec), (L, T, _grid_shift_spec)):
                    hh = h_ref[pl.ds(lo, cnt), :]
                    s, keep = spec(cnt, tc, quarter)
                    xx = _shift_rows(hh, s, keep) - hh
                    for m in range(6):
                        o_refs[m][pl.ds(lo, cnt), :] = (hh + xx * mix_ref[m]).astype(BF16)

    return pl.pallas_call(
        body, name="shift_mix_fwd", grid=(d // tc,),
        in_specs=[pl.BlockSpec((n, tc), lambda j: (0, j)), pl.BlockSpec((6, 1, tc), lambda j: (0, 0, j))],
        out_specs=[pl.BlockSpec((n, tc), lambda j: (0, j)) for _ in range(6)],
        out_shape=[jax.ShapeDtypeStruct((n, d), BF16) for _ in range(6)],
        compiler_params=_cparams(("parallel",)),
    )(h, mix3)


def _shift_mix_bwd_call(h, mix3, douts, L):
    n, d = h.shape
    T = n - L
    tc = _pick(d // 4, 256)
    nq = (d // 4) // tc

    def body(h_ref, mix_ref, d0, d1, d2, d3, d4, d5, dh_ref, dmix_ref):
        d_refs = (d0, d1, d2, d3, d4, d5)
        q = pl.program_id(0) // nq
        for quarter in range(4):
            @pl.when(q == quarter)
            def _(quarter=quarter):
                dmix = [jnp.zeros((1, tc), F32) for _ in range(6)]
                for lo, cnt, spec in ((0, L, _ctx_shift_spec), (L, T, _grid_shift_spec)):
                    hh = h_ref[pl.ds(lo, cnt), :]
                    s, keep = spec(cnt, tc, quarter)
                    xx = _shift_rows(hh, s, keep) - hh
                    direct = jnp.zeros((cnt, tc), F32)
                    shifted = jnp.zeros((cnt, tc), F32)
                    for m in range(6):
                        dm = d_refs[m][pl.ds(lo, cnt), :].astype(F32)
                        mx = mix_ref[m]
                        direct = direct + dm * (1.0 - mx)
                        shifted = shifted + dm * mx
                        dmix[m] = dmix[m] + jnp.sum(dm * xx, axis=0, keepdims=True)
                    dh_ref[pl.ds(lo, cnt), :] = direct + _unshift_rows(shifted, s, keep)
                for m in range(6):
                    dmix_ref[m] = dmix[m]

    tile = pl.BlockSpec((n, tc), lambda j: (0, j))
    return pl.pallas_call(
        body, name="shift_mix_bwd", grid=(d // tc,),
        in_specs=[tile, pl.BlockSpec((6, 1, tc), lambda j: (0, 0, j))] + [tile] * 6,
        out_specs=[tile, pl.BlockSpec((6, 1, tc), lambda j: (0, 0, j))],
        out_shape=[jax.ShapeDtypeStruct((n, d), F32), jax.ShapeDtypeStruct((6, 1, d), F32)],
        compiler_params=_cparams(("parallel",)),
    )(h, mix3, *douts)


@functools.partial(jax.custom_vjp, nondiff_argnums=(2,))
def shift_mix(h, mix3, L):
    return tuple(_shift_mix_fwd_call(h, mix3, L))


def _shift_mix_fwd(h, mix3, L):
    return tuple(_shift_mix_fwd_call(h, mix3, L)), (h, mix3)


def _shift_mix_bwd(L, res, douts):
    h, mix3 = res
    dh, dmix = _shift_mix_bwd_call(h, mix3, tuple(douts), L)
    return dh, dmix


shift_mix.defvjp(_shift_mix_fwd, _shift_mix_bwd)


def _conv_specs(T, d, tc):
    nd = d // tc
    ins = [pl.BlockSpec((T, tc), functools.partial(lambda j, off: (0, j + off), off=o * nd)) for o in range(3)]
    return ins, pl.BlockSpec((3, 1, tc), lambda j: (0, 0, j))


def _conv_terms(gc, u, tc):
    T = gc.shape[0]
    row = _row_iota(T, tc)
    z = gc * u
    return z, _shift_rows(z, 1, row >= 1), _shift_rows(z, -1, row < T - 1), row


def _conv_fwd_call(guc, cw3):
    T, d3 = guc.shape
    d = d3 // 3
    tc = _pick(d, 256)
    ins, wspec = _conv_specs(T, d, tc)

    def body(gb_ref, gc_ref, u_ref, w_ref, p_ref):
        z, zp, zn, _ = _conv_terms(gc_ref[...].astype(F32), u_ref[...].astype(F32), tc)
        conv = zp * w_ref[0] + z * w_ref[1] + zn * w_ref[2]
        p_ref[...] = (gb_ref[...].astype(F32) * conv).astype(BF16)

    return pl.pallas_call(
        body, name="conv_fwd", grid=(d // tc,), in_specs=ins + [wspec],
        out_specs=pl.BlockSpec((T, tc), lambda j: (0, j)),
        out_shape=jax.ShapeDtypeStruct((T, d), BF16),
        compiler_params=_cparams(("parallel",)),
    )(guc, guc, guc, cw3)


def _conv_bwd_call(guc, cw3, dp):
    T, d3 = guc.shape
    d = d3 // 3
    tc = _pick(d, 256)
    ins, wspec = _conv_specs(T, d, tc)
    tile = pl.BlockSpec((T, tc), lambda j: (0, j))

    def body(gb_ref, gc_ref, u_ref, w_ref, dp_ref, dgb_ref, dgc_ref, du_ref, dw_ref):
        gc = gc_ref[...].astype(F32)
        u = u_ref[...].astype(F32)
        z, zp, zn, row = _conv_terms(gc, u, tc)
        conv = zp * w_ref[0] + z * w_ref[1] + zn * w_ref[2]
        dpv = dp_ref[...].astype(F32)
        dgb_ref[...] = (dpv * conv).astype(dgb_ref.dtype)
        dconv = dpv * gb_ref[...].astype(F32)
        dz = (_shift_rows(dconv, -1, row < T - 1) * w_ref[0] + dconv * w_ref[1]
              + _shift_rows(dconv, 1, row >= 1) * w_ref[2])
        dgc_ref[...] = (dz * u).astype(dgc_ref.dtype)
        du_ref[...] = (dz * gc).astype(du_ref.dtype)
        dw_ref[0] = jnp.sum(dconv * zp, axis=0, keepdims=True)
        dw_ref[1] = jnp.sum(dconv * z, axis=0, keepdims=True)
        dw_ref[2] = jnp.sum(dconv * zn, axis=0, keepdims=True)

    return pl.pallas_call(
        body, name="conv_bwd", grid=(d // tc,), in_specs=ins + [wspec, tile],
        out_specs=[tile, tile, tile, wspec],
        out_shape=[jax.ShapeDtypeStruct((T, d), guc.dtype)] * 3 + [jax.ShapeDtypeStruct((3, 1, d), F32)],
        compiler_params=_cparams(("parallel",)),
    )(guc, guc, guc, cw3, dp)


@jax.custom_vjp
def gated_conv(guc, cw3):
    return _conv_fwd_call(guc, cw3)


def _gated_conv_fwd(guc, cw3):
    return _conv_fwd_call(guc, cw3), (guc, cw3)


def _gated_conv_bwd(res, dp):
    guc, cw3 = res
    dgb, dgc, du, dw = _conv_bwd_call(guc, cw3, dp)
    return jnp.concatenate([dgb, dgc, du], axis=1), dw


gated_conv.defvjp(_gated_conv_fwd, _gated_conv_bwd)


def _chunk_map(nchunk, nctx_chunk, reverse):
    if not reverse:
        return lambda c: c
    return lambda c: jnp.where(c < nctx_chunk, nctx_chunk - 1 - c, nchunk - 1 - (c - nctx_chunk))


def _spread(row_ref, dst_scr, lo_mask, ni, C):
    for i in range(ni):
        idx = jnp.where(lo_mask, 2 * i, 2 * i + 1).astype(jnp.int32)
        for tt in range(C):
            dst_scr[tt, i] = jnp.take_along_axis(row_ref[tt], idx, axis=1)


def _half_sums(p, lo_mask):
    lo = jnp.sum(jnp.where(lo_mask, p, 0.0), axis=1, keepdims=True)
    hi = jnp.sum(jnp.where(lo_mask, 0.0, p), axis=1, keepdims=True)
    return lo, hi


def _gather_rows(los, his, lane, nh):
    acc = jnp.zeros((nh, LANES), F32)
    for i, (lo, hi) in enumerate(zip(los, his)):
        acc = acc + jnp.where(lane == 2 * i, lo, 0.0) + jnp.where(lane == 2 * i + 1, hi, 0.0)
    return acc


def _wkv_fwd_call(r2, w2, kd2, kk2, as2, v2, nctx, reverse, xchg_arrs=(), xchg_specs=()):
    n, nh, _ = r2.shape
    C = SCAN_CHUNK
    ni = HEAD // 2
    nchunk = n // C
    cmap = _chunk_map(nchunk, nctx // C, reverse)
    nx = len(xchg_arrs)

    def body(*refs):
        r_ref, w_ref, kd_ref, kk_ref, as_ref, v_ref = refs[:6]
        x_in = refs[6:6 + nx]
        y_ref, sa_ref, sp_ref = refs[6 + nx:9 + nx]
        x_out = refs[9 + nx:9 + 2 * nx]
        s_scr, vc_scr = refs[9 + 2 * nx:11 + 2 * nx]
        if nx:
            _fused_exchanges(list(zip(x_in, x_out)), xchg_specs, *refs[11 + 2 * nx:], first=pl.program_id(0) == 0)

        @pl.when(pl.program_id(0) == 0)
        def _():
            s_scr[...] = jnp.zeros_like(s_scr)

        lane = lax.broadcasted_iota(jnp.int32, (nh, LANES), 1)
        lo_mask = lane < HEAD
        _spread(v_ref, vc_scr, lo_mask, ni, C)

        def make_step(with_y):
            def step(j, carry):
                t = (C - 1 - j) if reverse else j
                kk = kk_ref[t]
                a2 = -kk
                b2 = kk * as_ref[t]
                w = w_ref[t]
                k = kd_ref[t]
                r = r_ref[t]
                los, his = [], []
                for i in range(ni):
                    si = s_scr[i]
                    sp_ref[t, i] = si
                    lo, hi = _half_sums(si * a2, lo_mask)
                    los.append(lo)
                    his.append(hi)
                ylos, yhis = [], []
                for i in range(ni):
                    sa_i = jnp.where(lo_mask, los[i], his[i])
                    sn = s_scr[i] * w + sa_i * b2 + vc_scr[t, i] * k
                    s_scr[i] = sn
                    if with_y:
                        lo, hi = _half_sums(sn * r, lo_mask)
                        ylos.append(lo)
                        yhis.append(hi)
                y_ref[t] = _gather_rows(ylos, yhis, lane, nh) if with_y else jnp.zeros((nh, LANES), F32)
                sa_ref[t] = _gather_rows(los, his, lane, nh)
                return carry
            return step

        is_ctx = pl.program_id(0) < nctx // C

        @pl.when(is_ctx)
        def _():
            lax.fori_loop(0, C, make_step(False), 0)

        @pl.when(jnp.logical_not(is_ctx))
        def _():
            lax.fori_loop(0, C, make_step(True), 0)

        if nx:
            _fused_exchanges(list(zip(x_in, x_out)), xchg_specs, *refs[11 + 2 * nx:],
                             last=pl.program_id(0) == nchunk - 1)

    tok = pl.BlockSpec((C, nh, LANES), lambda c: (cmap(c), 0, 0))
    return pl.pallas_call(
        body, name="wkv_fwd_rev" if reverse else "wkv_fwd", grid=(nchunk,),
        in_specs=[tok] * 6 + [_ANY] * nx,
        out_specs=[tok, tok, pl.BlockSpec((C, ni, nh, LANES), lambda c: (cmap(c), 0, 0, 0))] + [_ANY] * nx,
        out_shape=[jax.ShapeDtypeStruct((n, nh, LANES), F32), jax.ShapeDtypeStruct((n, nh, LANES), F32),
                   jax.ShapeDtypeStruct((n, ni, nh, LANES), F32)] + _exchange_out_shapes(xchg_arrs, xchg_specs),
        scratch_shapes=[pltpu.VMEM((ni, nh, LANES), F32), pltpu.VMEM((C, ni, nh, LANES), F32)]
        + (_exchange_sems(nx) if nx else []),
        compiler_params=_cparams(("arbitrary",)),
    )(r2, w2, kd2, kk2, as2, v2, *xchg_arrs)


def _wkv_bwd_call(r2, w2, kd2, kk2, as2, v2, sa, sprev, dy, nctx, reverse, xchg_arrs=(), xchg_specs=()):
    n, nh, _ = r2.shape
    C = SCAN_CHUNK
    ni = HEAD // 2
    nchunk = n // C
    fmap = _chunk_map(nchunk, nctx // C, reverse)
    cmap = lambda c: fmap(nchunk - 1 - c)
    nx = len(xchg_arrs)

    def body(*refs):
        r_ref, w_ref, kd_ref, kk_ref, as_ref, v_ref, sa_ref, sp_ref, dy_ref = refs[:9]
        x_in = refs[9:9 + nx]
        dr_ref, dw_ref, dkd_ref, dkk_ref, das_ref, dv_ref = refs[9 + nx:15 + nx]
        x_out = refs[15 + nx:15 + 2 * nx]
        ds_scr, vc_scr, sac_scr, dyc_scr = refs[15 + 2 * nx:19 + 2 * nx]
        if nx:
            _fused_exchanges(list(zip(x_in, x_out)), xchg_specs, *refs[19 + 2 * nx:], first=pl.program_id(0) == 0)

        @pl.when(pl.program_id(0) == 0)
        def _():
            ds_scr[...] = jnp.zeros_like(ds_scr)

        lane = lax.broadcasted_iota(jnp.int32, (nh, LANES), 1)
        lo_mask = lane < HEAD
        _spread(v_ref, vc_scr, lo_mask, ni, C)
        _spread(sa_ref, sac_scr, lo_mask, ni, C)
        is_ctx = pl.program_id(0) >= nchunk - nctx // C

        @pl.when(jnp.logical_not(is_ctx))
        def _():
            _spread(dy_ref, dyc_scr, lo_mask, ni, C)

        def make_step(with_dy):
            def step(j, carry):
                t = j if reverse else (C - 1 - j)
                kk = kk_ref[t]
                sig = as_ref[t]
                a2 = -kk
                b2 = kk * sig
                w = w_ref[t]
                k = kd_ref[t]
                r = r_ref[t]
                zero = jnp.zeros((nh, LANES), F32)
                acc_dk, acc_db, acc_dw, acc_g, acc_sady, acc_vdy, acc_da = zero, zero, zero, zero, zero, zero, zero
                dvlo, dvhi, dsalo, dsahi = [], [], [], []
                for i in range(ni):
                    sp = sp_ref[t, i]
                    vc = vc_scr[t, i]
                    sac = sac_scr[t, i]
                    ds = ds_scr[i]
                    if with_dy:
                        dyc = dyc_scr[t, i]
                        ds = ds + dyc * r
                        ds_scr[i] = ds
                        acc_g = acc_g + sp * dyc
                        acc_sady = acc_sady + sac * dyc
                        acc_vdy = acc_vdy + vc * dyc
                    lo, hi = _half_sums(ds * k, lo_mask)
                    dvlo.append(lo)
                    dvhi.append(hi)
                    lo, hi = _half_sums(ds * b2, lo_mask)
                    dsalo.append(lo)
                    dsahi.append(hi)
                    acc_dk = acc_dk + ds * vc
                    acc_db = acc_db + ds * sac
                    acc_dw = acc_dw + ds * sp
                for i in range(ni):
                    dsa_i = jnp.where(lo_mask, dsalo[i], dsahi[i])
                    acc_da = acc_da + sp_ref[t, i] * dsa_i
                    ds_scr[i] = ds_scr[i] * w + dsa_i * a2
                dr_ref[t] = acc_g * w + b2 * acc_sady + k * acc_vdy
                dw_ref[t] = acc_dw
                dkd_ref[t] = acc_dk
                dkk_ref[t] = acc_db * sig - acc_da
                das_ref[t] = acc_db * kk
                dv_ref[t] = _gather_rows(dvlo, dvhi, lane, nh)
                return carry
            return step

        @pl.when(is_ctx)
        def _():
            lax.fori_loop(0, C, make_step(False), 0)

        @pl.when(jnp.logical_not(is_ctx))
        def _():
            lax.fori_loop(0, C, make_step(True), 0)

        if nx:
            _fused_exchanges(list(zip(x_in, x_out)), xchg_specs, *refs[19 + 2 * nx:],
                             last=pl.program_id(0) == nchunk - 1)

    tok = pl.BlockSpec((C, nh, LANES), lambda c: (cmap(c), 0, 0))
    big = pltpu.VMEM((C, ni, nh, LANES), F32)
    return pl.pallas_call(
        body, name="wkv_bwd_rev" if reverse else "wkv_bwd", grid=(nchunk,),
        in_specs=[tok] * 7 + [pl.BlockSpec((C, ni, nh, LANES), lambda c: (cmap(c), 0, 0, 0)), tok] + [_ANY] * nx,
        out_specs=[tok] * 6 + [_ANY] * nx,
        out_shape=[jax.ShapeDtypeStruct((n, nh, LANES), F32)] * 6 + _exchange_out_shapes(xchg_arrs, xchg_specs),
        scratch_shapes=[pltpu.VMEM((ni, nh, LANES), F32), big, big, big] + (_exchange_sems(nx) if nx else []),
        compiler_params=_cparams(("arbitrary",)),
    )(r2, w2, kd2, kk2, as2, v2, sa, sprev, dy, *xchg_arrs)


def _tile_heads(t):
    n, d = t.shape
    th = t.reshape(n, d // HEAD, HEAD)
    return jnp.concatenate([th, th], axis=-1)


def loss_head(x3, fo, tgt, gate, g):
    T, d = x3.shape
    tr = _pick(T, 128, 8)

    def body(x_ref, f_ref, t_ref, gate_ref, g_ref, loss_ref, dx_ref, df_ref, dgate_ref, dg_ref):
        tg = t_ref[...]

        def fl(x, fo_, gate_, g_):
            x4 = x + gate_ * fo_
            y = (x4 * lax.rsqrt(jnp.mean(x4 * x4, axis=-1, keepdims=True) + NORM_EPS)) * g_
            return 0.5 * jnp.sum(jnp.mean(jnp.square(y - tg), axis=-1))

        val, vjp = jax.vjp(fl, x_ref[...], f_ref[...], gate_ref[...], g_ref[...])
        dx, dfo, dgate, dg = vjp(jnp.ones((), F32))
        dx_ref[...] = dx
        df_ref[...] = dfo
        i = pl.program_id(0)

        @pl.when(i == 0)
        def _():
            loss_ref[...] = jnp.zeros_like(loss_ref)
            dgate_ref[...] = jnp.zeros_like(dgate_ref)
            dg_ref[...] = jnp.zeros_like(dg_ref)

        loss_ref[...] += jnp.full(loss_ref.shape, val, F32)
        dgate_ref[...] += dgate
        dg_ref[...] += dg

    tile = pl.BlockSpec((tr, d), lambda i: (i, 0))
    vec = pl.BlockSpec((1, d), lambda i: (0, 0))
    return pl.pallas_call(
        body, name="loss_head", grid=(T // tr,),
        in_specs=[tile, tile, tile, vec, vec],
        out_specs=[pl.BlockSpec((8, LANES), lambda i: (0, 0)), tile, tile, vec, vec],
        out_shape=[jax.ShapeDtypeStruct((8, LANES), F32), jax.ShapeDtypeStruct((T, d), F32),
                   jax.ShapeDtypeStruct((T, d), F32), jax.ShapeDtypeStruct((1, d), F32),
                   jax.ShapeDtypeStruct((1, d), F32)],
        compiler_params=_cparams(("arbitrary",)),
    )(x3, fo, tgt, gate, g)


def sum_adam(parts, w, m, v, name, lead=None):
    P, R, Cc = parts.shape
    tc = _pick(Cc, 1024)
    tr = _pick(R, max(8, (256 * 1024) // tc), 16 if parts.dtype == BF16 else 8)

    def body(p_ref, w_ref, m_ref, v_ref, g_ref, d_ref, nm_ref, nv_ref):
        g = p_ref[0].astype(F32)
        for s in range(1, P):
            g = g + p_ref[s].astype(F32)
        m_new = ADAM_B1 * m_ref[...] + (1.0 - ADAM_B1) * g
        v_new = ADAM_B2 * v_ref[...] + (1.0 - ADAM_B2) * jnp.square(g)
        m_hat = m_new / (1.0 - ADAM_B1 ** ADAM_STEP)
        v_hat = v_new / (1.0 - ADAM_B2 ** ADAM_STEP)
        g_ref[...] = g
        d_ref[...] = -ADAM_LR * (m_hat / (jnp.sqrt(v_hat) + ADAM_EPS) + ADAM_WD * w_ref[...])
        nm_ref[...] = m_new
        nv_ref[...] = v_new

    if lead is None:
        pspec = pl.BlockSpec((tr, tc), lambda i, j: (i, j))
    else:
        pspec = pl.BlockSpec((None, tr, tc), lambda i, j: (lead, i, j))
    ospec = pl.BlockSpec((tr, tc), lambda i, j: (i, j))
    return pl.pallas_call(
        body, name=name, grid=(R // tr, Cc // tc),
        in_specs=[pl.BlockSpec((P, tr, tc), lambda i, j: (0, i, j)), pspec, pspec, pspec],
        out_specs=[ospec] * 4,
        out_shape=[jax.ShapeDtypeStruct((R, Cc), F32)] * 4,
        compiler_params=_cparams(("parallel", "parallel")),
    )(parts, w, m, v)


def _me():
    return lax.axis_index("x"), lax.axis_index("y"), lax.axis_index("c")


def _peer(p):
    x, y, c = _me()
    px = 1 - x if p & 4 else x
    py = 1 - y if p & 2 else y
    pc = 1 - c if p & 1 else c
    return (px, py, pc), 4 * px + 2 * py + pc


def _block_view(ref, axis, idx, r, c):
    if axis is None:
        return ref.at[idx]
    if axis == 0:
        return ref.at[pl.ds(idx * r, r), :]
    return ref.at[:, pl.ds(idx * c, c)]


def _exchange_copies(src_of, dst_of, ssem, rsem, lsem, with_recvs):
    x, y, c = _me()
    me = 4 * x + 2 * y + c
    local = pltpu.make_async_copy(src_of(me), dst_of(me), lsem)
    sends, recvs = [], []
    for p in range(1, N_DEV):
        dev, idx = _peer(p)
        sends.append(pltpu.make_async_remote_copy(src_ref=src_of(idx), dst_ref=dst_of(me), send_sem=ssem(p),
                                                  recv_sem=rsem(p), device_id=dev,
                                                  device_id_type=pl.DeviceIdType.MESH))
        if with_recvs:
            recvs.append(pltpu.make_async_remote_copy(src_ref=src_of(idx), dst_ref=dst_of(idx), send_sem=ssem(p),
                                                      recv_sem=rsem(p), device_id=dev,
                                                      device_id_type=pl.DeviceIdType.MESH))
    return local, sends, recvs


def _exchange_start(*args):
    local, sends, _ = _exchange_copies(*args, with_recvs=False)
    local.start()
    for cp in sends:
        cp.start()


def _exchange_wait(*args):
    local, sends, recvs = _exchange_copies(*args, with_recvs=True)
    for cp in recvs:
        cp.wait_recv()
    for cp in sends:
        cp.wait_send()
    local.wait()


def _exchange(src_of, dst_of, send_sems, recv_sems, local_sem):
    args = (src_of, dst_of, lambda p: send_sems.at[p], lambda p: recv_sems.at[p], local_sem)
    _exchange_start(*args)
    _exchange_wait(*args)


def _fused_exchanges(pairs, specs, send_sems, recv_sems, local_sems, first=None, last=None):
    def args(j):
        src, dst = pairs[j]
        kind, axis, r, c = specs[j]
        if kind == "ag":
            src_of = lambda idx: src
            dst_of = lambda idx: _block_view(dst, axis, idx, r, c)
        else:
            src_of = lambda idx: _block_view(src, axis, idx, r, c)
            dst_of = lambda idx: dst.at[idx]
        return (src_of, dst_of, lambda p: send_sems.at[j, p], lambda p: recv_sems.at[j, p], local_sems.at[j])

    if first is not None:
        @pl.when(first)
        def _():
            for j in range(len(pairs)):
                _exchange_start(*args(j))

    if last is not None:
        @pl.when(last)
        def _():
            for j in range(len(pairs)):
                _exchange_wait(*args(j))


def _exchange_out_shapes(arrs, specs):
    out = []
    for a, (kind, axis, r, c) in zip(arrs, specs):
        if kind == "rs":
            out.append(jax.ShapeDtypeStruct((N_DEV, r, c), a.dtype))
        else:
            out.append(jax.ShapeDtypeStruct((N_DEV * r, c) if axis == 0 else (r, N_DEV * c), a.dtype))
    return out


def _exchange_specs(kind, arrs, axes):
    specs = []
    for a, axis in zip(arrs, axes):
        if kind == "ag":
            r, c = a.shape
        elif axis == 0:
            r, c = a.shape[0] // N_DEV, a.shape[1]
        else:
            r, c = a.shape[0], a.shape[1] // N_DEV
        specs.append((kind, axis, r, c))
    return specs


def _exchange_sems(n):
    return [pltpu.SemaphoreType.DMA((n, N_DEV)), pltpu.SemaphoreType.DMA((n, N_DEV)), pltpu.SemaphoreType.DMA((n,))]


_SEMS = [pltpu.SemaphoreType.DMA((N_DEV,)), pltpu.SemaphoreType.DMA((N_DEV,)), pltpu.SemaphoreType.DMA]
_ANY = pl.BlockSpec(memory_space=pl.ANY)


def all_gather(x, axis, name):
    r, c = x.shape
    shape = (N_DEV * r, c) if axis == 0 else (r, N_DEV * c)

    def body(x_ref, o_ref, send_sems, recv_sems, local_sem):
        _exchange(lambda idx: x_ref, lambda idx: _block_view(o_ref, axis, idx, r, c), send_sems, recv_sems, local_sem)

    return pl.pallas_call(
        body, name=name, in_specs=[_ANY], out_specs=_ANY,
        out_shape=jax.ShapeDtypeStruct(shape, x.dtype), scratch_shapes=_SEMS,
    )(x)


def all_gather_stack(x, name):
    r, c = x.shape

    def body(x_ref, o_ref, send_sems, recv_sems, local_sem):
        _exchange(lambda idx: x_ref, lambda idx: o_ref.at[idx], send_sems, recv_sems, local_sem)

    return pl.pallas_call(
        body, name=name, in_specs=[_ANY], out_specs=_ANY,
        out_shape=jax.ShapeDtypeStruct((N_DEV, r, c), x.dtype), scratch_shapes=_SEMS,
    )(x)


def reduce_scatter_exchange(g, axis, name):
    if axis is None:
        _, r, c = g.shape
    elif axis == 0:
        r, c = g.shape[0] // N_DEV, g.shape[1]
    else:
        r, c = g.shape[0], g.shape[1] // N_DEV

    def body(g_ref, o_ref, send_sems, recv_sems, local_sem):
        _exchange(lambda idx: _block_view(g_ref, axis, idx, r, c), lambda idx: o_ref.at[idx],
                  send_sems, recv_sems, local_sem)

    return pl.pallas_call(
        body, name=name, in_specs=[_ANY], out_specs=_ANY,
        out_shape=jax.ShapeDtypeStruct((N_DEV, r, c), g.dtype), scratch_shapes=_SEMS,
    )(g)


PACK_QUANTUM = 16 * LANES


def _pack(arrs, dtype=F32, lead=0):
    keep = arrs[0].shape[:lead]
    flat = jnp.concatenate([a.reshape(keep + (-1,)).astype(dtype) for a in arrs], axis=-1)
    pad = (-flat.shape[-1]) % PACK_QUANTUM
    flat = jnp.pad(flat, ((0, 0),) * lead + ((0, pad),))
    return flat.reshape(keep + (-1, LANES))


def _unpack(flat2d, shapes, lead=()):
    flat = flat2d.reshape(lead + (-1,))
    out, off = [], 0
    for s in shapes:
        n = int(np.prod(s))
        out.append(flat[..., off:off + n].reshape(lead + tuple(s)))
        off += n
    return out


def _gather_lastdim(stk):
    return jnp.moveaxis(stk, 0, -2).reshape(stk.shape[1:-1] + (N_DEV * stk.shape[-1],))


def _gather_dim(stk, dim):
    moved = jnp.moveaxis(stk, 0, dim)
    sh = list(stk.shape[1:])
    sh[dim] = sh[dim] * N_DEV
    return moved.reshape(sh)


def _scatter_dim(full, dim):
    sh = list(full.shape)
    sh[dim:dim + 1] = [N_DEV, sh[dim] // N_DEV]
    return jnp.moveaxis(full.reshape(sh), dim, 0)


def _head_group_matrix(tc):
    return np.kron(np.eye(tc // HEAD, dtype=np.float32), np.ones((HEAD, HEAD), np.float32))


_SCAN_COMM = {0: ("w13_0", "w13_1"), 1: ("win", "w2_0", "w2_1", "wout", "wo")}
_W_AXIS = dict(wr=0, wk=0, wv=0, wo=0, win=1, wout=0, w13_0=1, w13_1=1, w2_0=0, w2_1=0)


def _build_forward(ctx2d, T, D, shards=None, sink=None):
    L = ctx2d.shape[0]
    N = L + T

    def make_scan(d):
        keys = _SCAN_COMM[d] if shards is not None else ()
        axes = tuple(_W_AXIS[k] for k in keys)

        def run_fwd(tok, sh):
            return _wkv_fwd_call(*tok, L, d == 1, tuple(sh), _exchange_specs("ag", sh, axes))

        @jax.custom_vjp
        def op(tok, sh):
            outs = run_fwd(tok, sh)
            return (outs[0],) + tuple(outs[3:])

        def op_fwd(tok, sh):
            outs = run_fwd(tok, sh)
            return (outs[0],) + tuple(outs[3:]), (tok, outs[1], outs[2])

        def op_bwd(res, cts):
            tok, sa, sprev = res
            dg = tuple(cts[1:])
            outs = _wkv_bwd_call(*tok, sa, sprev, cts[0], L, d == 1, dg, _exchange_specs("rs", dg, axes))
            for k, recv in zip(keys, outs[6:]):
                sink[k] = recv
            return tuple(outs[:6]), tuple(jnp.zeros(shards[k].shape, shards[k].dtype) for k in keys)

        op.defvjp(op_fwd, op_bwd)
        return op, keys

    scans = [make_scan(0), make_scan(1)]
    gm = _head_group_matrix(LANES)
    tr_row = _pick(math.gcd(L, T), 128, 8)
    op_norm = make_rowwise("norm_mod", _f_norm_mod, (F32,), tr_row, D, nb0=L // tr_row)
    op_res = [make_rowwise(f"res_norm_mod{i}", _f_res_norm_mod, (F32, BF16), _pick(T, 128, 8), D) for i in range(3)]
    op_prep = make_rowwise("wkv_prep", _f_prep, (F32,) * 7, _pick(N, 256, 8), LANES, consts=(gm,))
    op_read = make_rowwise("wkv_readout", _f_readout, (BF16,), _pick(T, 256, 8), LANES, consts=(gm,))

    def v3(a):
        return a.reshape(a.shape[0], 1, a.shape[-1])

    def fwd(xin, Ps, Wb):
        modx, modc = Ps["modx"], Ps["modc"]
        cat = jnp.concatenate([ctx2d, xin], axis=0)
        seg = lambda a, b: jnp.stack([a, b])[:, None, :]
        (hcat,) = op_norm((cat,), (Ps["n1"][0][None, None, :], seg(modc[0], modx[0, 0]), seg(modc[1], modx[0, 1])))
        xr, xw, xk, xv, xa, xg = shift_mix(hcat, Ps["mix"][:, None, :], L)
        r = linear(xr, Wb["wr"], F32, "wr")
        k = linear(xk, Wb["wk"], F32, "wk")
        v = linear(xv, Wb["wv"], F32, "wv")
        gl = jax.nn.sigmoid(linear(xg, Wb["g1"], F32, "g1"))
        g = linear(gl.astype(BF16), Wb["g2"], F32, "g2")
        tw = jnp.tanh(linear(xw, Wb["w1"], F32, "w1")).astype(BF16)
        ta = linear(xa, Wb["a1"], F32, "a1").astype(BF16)
        lw = [linear(tw[:, LORA_PAD * d:LORA_PAD * (d + 1)], Wb["w2d"][d], F32, f"w2_{d}") for d in range(2)]
        la = [linear(ta[:, LORA_PAD * d:LORA_PAD * (d + 1)], Wb["a2d"][d], F32, f"a2_{d}") for d in range(2)]
        kk, dec0, dec1, kd0, kd1, as0, as1 = op_prep(
            (k, lw[0], lw[1], la[0], la[1]),
            (v3(Ps["kk"]), v3(Ps["ka"]), Ps["w0"][0][None, None, :], Ps["w0"][1][None, None, :],
             Ps["a0"][0][None, None, :], Ps["a0"][1][None, None, :]))
        r2, kk2, v2 = _tile_heads(r), _tile_heads(kk), _tile_heads(v)
        ys = []
        Wb = dict(Wb)
        for d, (dec, kd, sg) in enumerate(((dec0, kd0, as0), (dec1, kd1, as1))):
            op, keys = scans[d]
            outs = op((r2, _tile_heads(dec), _tile_heads(kd), kk2, _tile_heads(sg), v2),
                      tuple(shards[k] for k in keys))
            ys.append(outs[0][L:, :, :HEAD].reshape(T, D))
            Wb.update(zip(keys, outs[1:]))
        (o,) = op_read((ys[0], ys[1], r[L:], kd0[L:], kd1[L:], v[L:], g[L:]),
                       (v3(Ps["rk"]), v3(Ps["lnw"]), v3(Ps["lnb"])))
        att = linear(o, Wb["wo"], F32, "wo")
        x1, h2 = op_res[0]((xin, att), (modx[0, 2][None, None, :], Ps["n2"][0][None, None, :],
                                        modx[0, 3][None, None, :], modx[0, 4][None, None, :]))
        act = swiglu_act(linear(h2, Wb["w13_0"], BF16, "w13_0"), "swiglu0")
        f0 = linear(act, Wb["w2_0"], F32, "w2_0")
        x2, h = op_res[1]((x1, f0), (modx[0, 5][None, None, :], Ps["n1"][1][None, None, :],
                                     modx[1, 0][None, None, :], modx[1, 1][None, None, :]))
        guc = linear(h, Wb["win"], BF16, "win")
        p = gated_conv(guc, Ps["conv"][:, None, :])
        cv = linear(p, Wb["wout"], F32, "wout")
        x3, h2b = op_res[2]((x2, cv), (modx[1, 2][None, None, :], Ps["n2"][1][None, None, :],
                                       modx[1, 3][None, None, :], modx[1, 4][None, None, :]))
        act1 = swiglu_act(linear(h2b, Wb["w13_1"], BF16, "w13_1"), "swiglu1")
        f1 = linear(act1, Wb["w2_1"], F32, "w2_1")
        return x3, f1


    return fwd


def kernel(x, c, ctx, c_ctx, norm1_g, norm2_g, ada_w, ada_b, rw_mix, rw_wr, rw_wk, rw_wv, rw_wo, rw_w0, rw_w1, rw_w2, rw_a0, rw_a1, rw_a2, rw_g1, rw_g2, rw_kk, rw_ka, rw_rk, rw_lnw, rw_lnb, sc_win, sc_conv, sc_wout, ffn_w13, ffn_w2, final_g, loss_target, m_c_ctx, m_norm1_g, m_norm2_g, m_ada_w, m_ada_b, m_rw_mix, m_rw_wr, m_rw_wk, m_rw_wv, m_rw_wo, m_rw_w0, m_rw_w1, m_rw_w2, m_rw_a0, m_rw_a1, m_rw_a2, m_rw_g1, m_rw_g2, m_rw_kk, m_rw_ka, m_rw_rk, m_rw_lnw, m_rw_lnb, m_sc_win, m_sc_conv, m_sc_wout, m_ffn_w13, m_ffn_w2, m_final_g, v_c_ctx, v_norm1_g, v_norm2_g, v_ada_w, v_ada_b, v_rw_mix, v_rw_wr, v_rw_wk, v_rw_wv, v_rw_wo, v_rw_w0, v_rw_w1, v_rw_w2, v_rw_a0, v_rw_a1, v_rw_a2, v_rw_g1, v_rw_g2, v_rw_kk, v_rw_ka, v_rw_rk, v_rw_lnw, v_rw_lnb, v_sc_win, v_sc_conv, v_sc_wout, v_ffn_w13, v_ffn_w2, v_final_g):
    W = dict(c_ctx=c_ctx, norm1_g=norm1_g, norm2_g=norm2_g, ada_w=ada_w, ada_b=ada_b, rw_mix=rw_mix, rw_wr=rw_wr,
             rw_wk=rw_wk, rw_wv=rw_wv, rw_wo=rw_wo, rw_w0=rw_w0, rw_w1=rw_w1, rw_w2=rw_w2, rw_a0=rw_a0, rw_a1=rw_a1,
             rw_a2=rw_a2, rw_g1=rw_g1, rw_g2=rw_g2, rw_kk=rw_kk, rw_ka=rw_ka, rw_rk=rw_rk, rw_lnw=rw_lnw,
             rw_lnb=rw_lnb, sc_win=sc_win, sc_conv=sc_conv, sc_wout=sc_wout, ffn_w13=ffn_w13, ffn_w2=ffn_w2,
             final_g=final_g)
    Mo = dict(c_ctx=m_c_ctx, norm1_g=m_norm1_g, norm2_g=m_norm2_g, ada_w=m_ada_w, ada_b=m_ada_b, rw_mix=m_rw_mix,
              rw_wr=m_rw_wr, rw_wk=m_rw_wk, rw_wv=m_rw_wv, rw_wo=m_rw_wo, rw_w0=m_rw_w0, rw_w1=m_rw_w1,
              rw_w2=m_rw_w2, rw_a0=m_rw_a0, rw_a1=m_rw_a1, rw_a2=m_rw_a2, rw_g1=m_rw_g1, rw_g2=m_rw_g2,
              rw_kk=m_rw_kk, rw_ka=m_rw_ka, rw_rk=m_rw_rk, rw_lnw=m_rw_lnw, rw_lnb=m_rw_lnb, sc_win=m_sc_win,
              sc_conv=m_sc_conv, sc_wout=m_sc_wout, ffn_w13=m_ffn_w13, ffn_w2=m_ffn_w2, final_g=m_final_g)
    Vo = dict(c_ctx=v_c_ctx, norm1_g=v_norm1_g, norm2_g=v_norm2_g, ada_w=v_ada_w, ada_b=v_ada_b, rw_mix=v_rw_mix,
              rw_wr=v_rw_wr, rw_wk=v_rw_wk, rw_wv=v_rw_wv, rw_wo=v_rw_wo, rw_w0=v_rw_w0, rw_w1=v_rw_w1,
              rw_w2=v_rw_w2, rw_a0=v_rw_a0, rw_a1=v_rw_a1, rw_a2=v_rw_a2, rw_g1=v_rw_g1, rw_g2=v_rw_g2,
              rw_kk=v_rw_kk, rw_ka=v_rw_ka, rw_rk=v_rw_rk, rw_lnw=v_rw_lnw, rw_lnb=v_rw_lnb, sc_win=v_sc_win,
              sc_conv=v_sc_conv, sc_wout=v_sc_wout, ffn_w13=v_ffn_w13, ffn_w2=v_ffn_w2, final_g=v_final_g)
    names = list(W)

    x2d = x[0]
    ctx2d = ctx[0]
    tgt = loss_target[0]
    T, D = x2d.shape
    L = ctx2d.shape[0]
    N = L + T
    nh = D // HEAD
    mx, my, mc = _me()
    me = 4 * mx + 2 * my + mc
    dloc = D // N_DEV

    lr = rw_w1.shape[-1]
    pad_r = LORA_PAD - lr
    w1p = jnp.pad(rw_w1[0], ((0, 0), (0, 0), (0, pad_r)))
    a1p = jnp.pad(rw_a1[0], ((0, 0), (0, 0), (0, pad_r)))
    w2p = jnp.pad(rw_w2[0], ((0, 0), (0, pad_r), (0, 0)))
    a2p = jnp.pad(rw_a2[0], ((0, 0), (0, pad_r), (0, 0)))
    small_loc = [rw_mix[0], rw_w0[0], rw_a0[0], sc_conv[0], w1p, a1p, w2p, a2p, rw_g1[0], rw_g2[0]]
    small_dim = [1, 1, 1, 1, 1, 1, 2, 2, 0, 1]
    small_shapes = [a.shape for a in small_loc]
    small_groups = ((slice(0, 4), F32, "vec"), (slice(4, 10), BF16, "mat"))
    small_full = []
    for sl, dt, tag in small_groups:
        sm_all = all_gather_stack(_pack(small_loc[sl], dt), "ag_small_" + tag)
        sm_parts = _unpack(sm_all, small_shapes[sl], lead=(N_DEV,))
        small_full += [_gather_dim(p, dm) for p, dm in zip(sm_parts, small_dim[sl])]
    mix_f, w0_f, a0_f, conv_f, w1_f, a1_f, w2_f, a2_f, g1_f, g2_f = small_full

    c_all = all_gather_stack(jnp.pad(c, ((0, 7), (0, 0))), "ag_c")[:, 0, :]
    cond_pre = jnp.concatenate([c_all, c_ctx[None, :], jnp.zeros((7, D), F32)], axis=0)
    cond_rows = jax.nn.silu(cond_pre)
    ncol = ada_w.shape[-1]
    mod_loc = []
    for i in range(2):
        bi = lax.dynamic_slice(ada_b[i], (me * ncol,), (ncol,))
        mod_loc.append(_mm(cond_rows, ada_w[i], out_dtype=F32, name=f"ada_fwd{i}") + bi[None, :])
    mod_all = all_gather_stack(jnp.concatenate(mod_loc, axis=0), "ag_mod")
    mod_full = _gather_lastdim(mod_all).reshape(2, 16, 6, D)
    mod_x = lax.dynamic_index_in_dim(mod_full, me, axis=1, keepdims=False)
    mod_c = mod_full[0, 8, :2, :]

    def ag_w(wl, axis, name):
        return all_gather(wl.astype(BF16), axis, name)

    shards = dict(wo=rw_wo[0], win=sc_win[0], wout=sc_wout[0], w13_0=ffn_w13[0], w13_1=ffn_w13[1],
                  w2_0=ffn_w2[0], w2_1=ffn_w2[1])
    shards = {k_: a.astype(BF16) for k_, a in shards.items()}
    sink = {}
    Wb = dict(
        wr=ag_w(rw_wr[0], 0, "ag_wr"), wk=ag_w(rw_wk[0], 0, "ag_wk"), wv=ag_w(rw_wv[0], 0, "ag_wv"),
        w1=jnp.concatenate([w1_f[0], w1_f[1]], axis=1).astype(BF16),
        a1=jnp.concatenate([a1_f[0], a1_f[1]], axis=1).astype(BF16),
        w2d=w2_f.astype(BF16), a2d=a2_f.astype(BF16),
        g1=g1_f.astype(BF16), g2=g2_f.astype(BF16),
    )
    Ps = dict(n1=norm1_g, n2=norm2_g, modx=mod_x, modc=mod_c, mix=mix_f, w0=w0_f, a0=a0_f, conv=conv_f,
              kk=rw_kk, ka=rw_ka, rk=rw_rk.reshape(1, D), lnw=rw_lnw, lnb=rw_lnb)

    fwd = _build_forward(ctx2d, T, D, shards, sink)
    (x3, f1), vjp_fn = jax.vjp(fwd, x2d, Ps, Wb)
    loss_acc, dx3, df1, dgate, dfinal = loss_head(x3, f1, tgt, mod_x[1, 5][None, :], final_g[None, :])
    dx, dPs, dWb = vjp_fn((dx3, df1))
    loss = lax.psum(loss_acc[0, 0], ("x", "y", "c"))

    dmodx = dPs["modx"].at[1, 5].add(dgate[0])
    dmodc = jnp.concatenate([dPs["modc"], jnp.zeros((4, D), F32)], axis=0)
    drow = jnp.stack([dmodx.reshape(2, 6 * D), jnp.stack([dmodc.reshape(6 * D), jnp.zeros((6 * D,), F32)])], axis=1)
    drow_all = all_gather_stack(drow.reshape(4, 6 * D), "ag_dmod").reshape(N_DEV, 2, 2, 6 * D)
    dctx_tot = drow_all[0, :, 1, :]
    for s in range(1, N_DEV):
        dctx_tot = dctx_tot + drow_all[s, :, 1, :]
    dmod_rows = jnp.concatenate([jnp.moveaxis(drow_all[:, :, 0, :], 0, 1), dctx_tot[:, None, :],
                                 jnp.zeros((2, 7, 6 * D), F32)], axis=1)
    grad_ada_b = dctx_tot
    for s in range(N_DEV):
        grad_ada_b = grad_ada_b + drow_all[s, :, 0, :]
    dmod_mine = lax.dynamic_slice_in_dim(dmod_rows, me * ncol, ncol, axis=2)
    g_ada_w = [_mm(cond_rows, dmod_mine[i], ta=True, out_dtype=F32, name=f"ada_dw{i}") for i in range(2)]
    dcond_part = _mm(dmod_mine[0], ada_w[0], tb=True, out_dtype=F32, name="ada_dcond")[8]

    rep_names = ["c_ctx", "norm1_g", "norm2_g", "rw_kk", "rw_ka", "rw_rk", "rw_lnw", "rw_lnb", "final_g"]
    rep_part = [dcond_part, dPs["n1"], dPs["n2"], dPs["kk"], dPs["ka"], dPs["rk"].reshape(W["rw_rk"].shape),
                dPs["lnw"], dPs["lnb"], dfinal[0]]
    rep_shapes = [W[n_].shape for n_ in rep_names]
    rep_all = all_gather_stack(_pack(rep_part), "ag_rep_grads")
    sg = jax.nn.sigmoid(c_ctx)
    dsilu = sg * (1.0 + c_ctx * (1.0 - sg))
    rep_scale = _pack([dsilu] + [jnp.ones(s, F32) for s in rep_shapes[1:]])
    rep_all = rep_all * rep_scale[None]
    rep_w = _pack([W[n_] for n_ in rep_names])
    rep_m = _pack([Mo[n_] for n_ in rep_names])
    rep_v = _pack([Vo[n_] for n_ in rep_names])
    rep_out = sum_adam(rep_all, rep_w, rep_m, rep_v, "adam_rep")
    results = {}
    for nm_, vals in zip(rep_names, zip(*[_unpack(o, rep_shapes) for o in rep_out])):
        results[nm_] = vals

    results["ada_b"] = tuple(sum_adam(grad_ada_b.reshape(1, 2 * 6, D), ada_b.reshape(12, D), m_ada_b.reshape(12, D),
                                      v_ada_b.reshape(12, D), "adam_ada_b"))
    results["ada_b"] = tuple(o.reshape(ada_b.shape) for o in results["ada_b"])

    outs = [sum_adam(g_ada_w[i][None], ada_w, m_ada_w, v_ada_w, f"adam_ada_w{i}", lead=i) for i in range(2)]
    results["ada_w"] = tuple(jnp.stack([outs[0][q], outs[1][q]]) for q in range(4))

    dw1 = jnp.stack([dWb["w1"][:, :LORA_PAD], dWb["w1"][:, LORA_PAD:]])
    da1 = jnp.stack([dWb["a1"][:, :LORA_PAD], dWb["a1"][:, LORA_PAD:]])
    small_g = [dPs["mix"], dPs["w0"], dPs["a0"], dPs["conv"], dw1, da1, dWb["w2d"], dWb["a2d"], dWb["g1"], dWb["g2"]]
    small_names = ["rw_mix", "rw_w0", "rw_a0", "sc_conv", "rw_w1", "rw_a1", "rw_w2", "rw_a2", "rw_g1", "rw_g2"]

    def padded_local(nm_, src):
        a = src[nm_][0]
        if nm_ in ("rw_w1", "rw_a1"):
            return jnp.pad(a, ((0, 0), (0, 0), (0, pad_r)))
        if nm_ in ("rw_w2", "rw_a2"):
            return jnp.pad(a, ((0, 0), (0, pad_r), (0, 0)))
        return a

    for sl, dt, tag in small_groups:
        blocks = [_scatter_dim(gf, dm) for gf, dm in zip(small_g[sl], small_dim[sl])]
        sm_recv = reduce_scatter_exchange(_pack(blocks, dt, lead=1), None, "rs_small_" + tag)
        sm_out = sum_adam(sm_recv, _pack([padded_local(n_, W) for n_ in small_names[sl]]),
                          _pack([padded_local(n_, Mo) for n_ in small_names[sl]]),
                          _pack([padded_local(n_, Vo) for n_ in small_names[sl]]), "adam_small_" + tag)
        for nm_, vals in zip(small_names[sl], zip(*[_unpack(o, small_shapes[sl]) for o in sm_out])):
            if nm_ in ("rw_w1", "rw_a1"):
                vals = tuple(a[:, :, :lr] for a in vals)
            if nm_ in ("rw_w2", "rw_a2"):
                vals = tuple(a[:, :lr, :] for a in vals)
            results[nm_] = tuple(a[None] for a in vals)

    def rs_adam(key, nm_, lead):
        recv = sink[key] if key in sink else reduce_scatter_exchange(dWb[key], _W_AXIS[key], "rs_" + key)
        return sum_adam(recv, W[nm_], Mo[nm_], Vo[nm_], "adam_" + key, lead=lead)

    for nm_, key in (("rw_wr", "wr"), ("rw_wk", "wk"), ("rw_wv", "wv"), ("rw_wo", "wo"), ("sc_win", "win"),
                     ("sc_wout", "wout")):
        results[nm_] = tuple(a[None] for a in rs_adam(key, nm_, 0))
    for nm_, key in (("ffn_w13", "w13"), ("ffn_w2", "w2")):
        outs = [rs_adam(f"{key}_{i}", nm_, i) for i in range(2)]
        results[nm_] = tuple(jnp.stack([outs[0][q], outs[1][q]]) for q in range(4))

    grads = [results[n_][0] for n_ in names]
    deltas = [results[n_][1] for n_ in names]
    new_m = [results[n_][2] for n_ in names]
    new_v = [results[n_][3] for n_ in names]
    return (loss, dx[None], *grads, *deltas, *new_m, *new_v)
```

```python
import functools
import math

import numpy as np
import jax
import jax.numpy as jnp
from jax import lax
from jax.experimental import pallas as pl
from jax.experimental.pallas import tpu as pltpu

F32 = jnp.float32
BF16 = jnp.bfloat16

N_DEV = 8
HEAD = 64
LANES = 128
GRID_W = 64
LORA_PAD = 128
NORM_EPS = 1e-6
GN_EPS = 64e-5
ADAM_LR, ADAM_B1, ADAM_B2, ADAM_EPS, ADAM_WD, ADAM_STEP = 0.001, 0.9, 0.999, 1e-08, 0.01, 10
VMEM_LIMIT = 52 * 1024 * 1024
SCAN_CHUNK = 8
HI = lax.Precision.HIGHEST


def _cparams(sem):
    return pltpu.CompilerParams(dimension_semantics=sem, vmem_limit_bytes=VMEM_LIMIT)


def _pick(n, cap, quantum=LANES):
    best = None
    for t in range(quantum, min(n, cap) + 1, quantum):
        if n % t == 0:
            best = t
    return n if best is None else best


MM_VMEM_BUDGET = 36 * 1024 * 1024


def _divisors(n, cap, quantum=LANES):
    ds = [t for t in range(quantum, min(n, cap) + 1, quantum) if n % t == 0]
    return sorted(ds, reverse=True) or [n]


def _mm_tiles(M, N, K, sa, sb, so):
    tms, tns, tks = _divisors(M, 1024), _divisors(N, 1024), _divisors(K, 2816)
    im = jn = ik = 0

    def est(tm, tn, tk):
        b = 2 * (tm * tk * sa + tk * tn * sb) + 2 * tm * tn * so + tm * tn * 4
        b += tm * tk * 2 if sa == 4 else 0
        b += tk * tn * 2 if sb == 4 else 0
        return b + (tm * tn * 4 if tk < K else 0)

    while est(tms[im], tns[jn], tks[ik]) > MM_VMEM_BUDGET:
        if tms[im] >= tns[jn] and im + 1 < len(tms):
            im += 1
        elif jn + 1 < len(tns):
            jn += 1
        elif im + 1 < len(tms):
            im += 1
        elif ik + 1 < len(tks):
            ik += 1
        else:
            break
    return tms[im], tns[jn], tks[ik]


def _mm(a, b, *, ta=False, tb=False, out_dtype, name):
    if ta:
        K, M = a.shape
    else:
        M, K = a.shape
    if tb:
        N, Kb = b.shape
    else:
        Kb, N = b.shape
    assert K == Kb, (a.shape, b.shape, ta, tb)
    tm, tn, tk = _mm_tiles(M, N, K, a.dtype.itemsize, b.dtype.itemsize, jnp.dtype(out_dtype).itemsize)
    nk = K // tk
    dims = (((0 if ta else 1,), (1 if tb else 0,)), ((), ()))

    def body(a_ref, b_ref, o_ref, *acc):
        part = lax.dot_general(a_ref[...].astype(BF16), b_ref[...].astype(BF16), dims, preferred_element_type=F32)
        if nk == 1:
            o_ref[...] = part.astype(o_ref.dtype)
            return
        acc_ref, = acc
        k = pl.program_id(2)

        @pl.when(k == 0)
        def _():
            acc_ref[...] = part

        @pl.when(k > 0)
        def _():
            acc_ref[...] += part

        @pl.when(k == nk - 1)
        def _():
            o_ref[...] = acc_ref[...].astype(o_ref.dtype)

    a_spec = pl.BlockSpec((tk, tm), lambda i, j, k: (k, i)) if ta else pl.BlockSpec((tm, tk), lambda i, j, k: (i, k))
    b_spec = pl.BlockSpec((tn, tk), lambda i, j, k: (j, k)) if tb else pl.BlockSpec((tk, tn), lambda i, j, k: (k, j))
    return pl.pallas_call(
        body, name=name, grid=(M // tm, N // tn, nk),
        in_specs=[a_spec, b_spec],
        out_specs=pl.BlockSpec((tm, tn), lambda i, j, k: (i, j)),
        out_shape=jax.ShapeDtypeStruct((M, N), out_dtype),
        scratch_shapes=[pltpu.VMEM((tm, tn), F32)] if nk > 1 else [],
        compiler_params=_cparams(("parallel", "parallel", "arbitrary")),
    )(a, b)


@functools.partial(jax.custom_vjp, nondiff_argnums=(2, 3))
def linear(a, w, out_dtype, name):
    return _mm(a, w, out_dtype=out_dtype, name=name + "_fwd")


def _linear_fwd(a, w, out_dtype, name):
    return _mm(a, w, out_dtype=out_dtype, name=name + "_fwd"), (a, w)


def _linear_bwd(out_dtype, name, res, g):
    a, w = res
    da = _mm(g, w, tb=True, out_dtype=a.dtype, name=name + "_da")
    dw = _mm(a, g, ta=True, out_dtype=w.dtype, name=name + "_dw")
    return da, dw


linear.defvjp(_linear_fwd, _linear_bwd)


def _rw_specs(tiles, col_offs, vecs, consts, tr, tc, nb0):
    tile_specs = [pl.BlockSpec((tr, tc), functools.partial(lambda j, i, off: (i, j + off), off=off))
                  for _, off in zip(tiles, col_offs)]

    def vec_map(S):
        if S == 1:
            return lambda j, i: (0, 0, j)
        return lambda j, i: (jnp.where(i < nb0, 0, 1), 0, j)

    vec_specs = [pl.BlockSpec((None, 1, tc), vec_map(v.shape[0])) for v in vecs]
    const_specs = [pl.BlockSpec(c.shape, lambda j, i: (0, 0)) for c in consts]
    return tile_specs, vec_specs, const_specs


def _rw_forward(name, f, tiles, col_offs, vecs, consts, out_dtypes, tr, tc, nb0, width):
    n = tiles[0].shape[0]
    nt, nv, nc = len(tiles), len(vecs), len(consts)
    tile_specs, vec_specs, const_specs = _rw_specs(tiles, col_offs, vecs, consts, tr, tc, nb0)

    def body(*refs):
        ins = [r[...].astype(F32) for r in refs[:nt]] + [r[...] for r in refs[nt:nt + nv + nc]]
        outs = f(*ins)
        for o_ref, o in zip(refs[nt + nv + nc:], outs):
            o_ref[...] = o.astype(o_ref.dtype)

    return pl.pallas_call(
        body, name=name + "_fwd", grid=(width // tc, n // tr),
        in_specs=tile_specs + vec_specs + const_specs,
        out_specs=[pl.BlockSpec((tr, tc), lambda j, i: (i, j)) for _ in out_dtypes],
        out_shape=[jax.ShapeDtypeStruct((n, width), dt) for dt in out_dtypes],
        compiler_params=_cparams(("parallel", "parallel")),
    )(*tiles, *vecs, *consts)


def _rw_backward(name, f, tiles, col_offs, vecs, consts, douts, tr, tc, nb0, width):
    n = tiles[0].shape[0]
    nt, nv, nc, no = len(tiles), len(vecs), len(consts), len(douts)
    tile_specs, vec_specs, const_specs = _rw_specs(tiles, col_offs, vecs, consts, tr, tc, nb0)

    def body(*refs):
        t_in = [r[...].astype(F32) for r in refs[:nt]]
        v_in = [r[...] for r in refs[nt:nt + nv]]
        c_in = [r[...] for r in refs[nt + nv:nt + nv + nc]]
        d_in = tuple(r[...].astype(F32) for r in refs[nt + nv + nc:nt + nv + nc + no])
        o_refs = refs[nt + nv + nc + no:]
        _, vjp = jax.vjp(lambda *tv: tuple(f(*tv, *c_in)), *t_in, *v_in)
        grads = vjp(d_in)
        for o_ref, g in zip(o_refs[:nt], grads[:nt]):
            o_ref[...] = g.astype(o_ref.dtype)
        i = pl.program_id(1)
        for o_ref, g, v in zip(o_refs[nt:], grads[nt:], vecs):
            first = jnp.logical_or(i == 0, i == nb0) if v.shape[0] == 2 else i == 0

            @pl.when(first)
            def _(o_ref=o_ref, g=g):
                o_ref[...] = g

            @pl.when(jnp.logical_not(first))
            def _(o_ref=o_ref, g=g):
                o_ref[...] += g

    dout_specs = [pl.BlockSpec((tr, tc), lambda j, i: (i, j)) for _ in douts]
    out_specs = [pl.BlockSpec((tr, tc), lambda j, i: (i, j)) for _ in tiles] + list(vec_specs)
    out_shape = ([jax.ShapeDtypeStruct((n, width), t.dtype) for t in tiles]
                 + [jax.ShapeDtypeStruct(v.shape, F32) for v in vecs])
    return pl.pallas_call(
        body, name=name + "_bwd", grid=(width // tc, n // tr),
        in_specs=tile_specs + vec_specs + const_specs + dout_specs,
        out_specs=out_specs, out_shape=out_shape,
        compiler_params=_cparams(("parallel", "arbitrary")),
    )(*tiles, *vecs, *consts, *douts)


def make_rowwise(name, f, out_dtypes, tr, tc, consts=(), nb0=-1):
    consts = tuple(consts)

    @jax.custom_vjp
    def op(tiles, vecs):
        w = tiles[0].shape[1]
        return tuple(_rw_forward(name, f, tiles, (0,) * len(tiles), vecs, consts, out_dtypes, tr, min(tc, w), nb0, w))

    def op_fwd(tiles, vecs):
        return op(tiles, vecs), (tiles, vecs)

    def op_bwd(res, douts):
        tiles, vecs = res
        w = tiles[0].shape[1]
        g = _rw_backward(name, f, tiles, (0,) * len(tiles), vecs, consts, tuple(douts), tr, min(tc, w), nb0, w)
        return tuple(g[:len(tiles)]), tuple(g[len(tiles):])

    op.defvjp(op_fwd, op_bwd)
    return op


def _f_norm_mod(x, g, sh, sc):
    hn = x * lax.rsqrt(jnp.mean(x * x, axis=-1, keepdims=True) + NORM_EPS)
    return ((hn * g) * (1.0 + sc) + sh,)


def _f_res_norm_mod(x, y, gate, g, sh, sc):
    x1 = x + gate * y
    hn = x1 * lax.rsqrt(jnp.mean(x1 * x1, axis=-1, keepdims=True) + NORM_EPS)
    return x1, (hn * g) * (1.0 + sc) + sh


def _head_sum_3pass(t, gmat):
    hi = t.astype(BF16)
    r1 = t - hi.astype(F32)
    mid = r1.astype(BF16)
    lo = (r1 - mid.astype(F32)).astype(BF16)
    g = gmat.astype(BF16)
    dot = lambda u: jnp.dot(u, g, preferred_element_type=F32)
    return dot(hi) + dot(mid) + dot(lo)


@jax.custom_vjp
def _head_sum(t, gmat):
    return _head_sum_3pass(t, gmat)


def _head_sum_fwd(t, gmat):
    return _head_sum_3pass(t, gmat), gmat


def _head_sum_bwd(gmat, ct):
    return _head_sum_3pass(ct, gmat), None


_head_sum.defvjp(_head_sum_fwd, _head_sum_bwd)


def _f_prep(k, lw0, lw1, la0, la1, kkp, kap, w00, w01, a00, a01, gmat):
    t = k * kkp
    kk = t / jnp.maximum(jnp.sqrt(_head_sum(t * t, gmat)), 1e-12)
    outs = [kk]
    decs, kds, sigs = [], [], []
    for lw, la, w0, a0 in ((lw0, la0, w00, a00), (lw1, la1, w01, a01)):
        decs.append(jnp.exp(-jax.nn.sigmoid(w0 + lw) * float(np.exp(-0.5))))
        a = jax.nn.sigmoid(a0 + la)
        sigs.append(a)
        kds.append(k * (1.0 + (a - 1.0) * kap))
    return tuple(outs + decs + kds + sigs)


def _f_readout(y0, y1, r, kd0, kd1, v, g, rk, lnw, lnb, gmat):
    y = y0 + y1
    mu = _head_sum(y, gmat) * (1.0 / HEAD)
    d = y - mu
    var = _head_sum(d * d, gmat) * (1.0 / HEAD)
    o = d * lax.rsqrt(var + GN_EPS) * lnw + lnb
    bonus = _head_sum(r * (kd0 + kd1) * rk, gmat) * v
    return ((o + bonus) * g,)


def _f_swiglu(a, b):
    return (jax.nn.silu(a) * b,)


def swiglu_act(ab, name):
    t, f2 = ab.shape
    fdim = f2 // 2
    tr, tc = _pick(t, 512, 8), _pick(fdim, 512)
    offs = (0, fdim // tc)

    @jax.custom_vjp
    def op(ab_):
        return _rw_forward(name, _f_swiglu, (ab_, ab_), offs, (), (), (BF16,), tr, tc, -1, fdim)[0]

    def op_fwd(ab_):
        return op(ab_), ab_

    def op_bwd(ab_, dact):
        da, db = _rw_backward(name, _f_swiglu, (ab_, ab_), offs, (), (), (dact,), tr, tc, -1, fdim)
        return (jnp.concatenate([da, db], axis=1),)

    op.defvjp(op_fwd, op_bwd)
    return op(ab)


def _row_iota(n, tc):
    return lax.broadcasted_iota(jnp.int32, (n, tc), 0)


def _shift_rows(x, s, keep):
    n = x.shape[0]
    return jnp.where(keep, pltpu.roll(x, s % n, 0), 0.0)


def _unshift_rows(d, s, keep):
    n = d.shape[0]
    return pltpu.roll(jnp.where(keep, d, 0.0), (-s) % n, 0)


def _ctx_shift_spec(L, tc, quarter):
    row = _row_iota(L, tc)
    if quarter < 2:
        return 1, row >= 1
    return -1, row < L - 1


def _grid_shift_spec(T, tc, quarter):
    row = _row_iota(T, tc)
    col = jnp.bitwise_and(row, GRID_W - 1)
    if quarter == 0:
        return 1, col != 0
    if quarter == 1:
        return -1, col != GRID_W - 1
    if quarter == 2:
        return GRID_W, row >= GRID_W
    return -GRID_W, row < T - GRID_W


def _shift_mix_fwd_call(h, mix3, L):
    n, d = h.shape
    T = n - L
    tc = _pick(d // 4, 256)
    nq = (d // 4) // tc

    def body(h_ref, mix_ref, *o_refs):
        q = pl.program_id(0) // nq
        for quarter in range(4):
            @pl.when(q == quarter)
            def _(quarter=quarter):
                for lo, cnt, spec in ((0, L, _ctx_shift_spec), (L, T, _grid_shift_spec)):
                    hh = h_ref[pl.ds(lo, cnt), :]
                    s, keep = spec(cnt, tc, quarter)
                    xx = _shift_rows(hh, s, keep) - hh
                    for m in range(6):
                        o_refs[m][pl.ds(lo, cnt), :] = (hh + xx * mix_ref[m]).astype(BF16)

    return pl.pallas_call(
        body, name="shift_mix_fwd", grid=(d // tc,),
        in_specs=[pl.BlockSpec((n, tc), lambda j: (0, j)), pl.BlockSpec((6, 1, tc), lambda j: (0, 0, j))],
        out_specs=[pl.BlockSpec((n, tc), lambda j: (0, j)) for _ in range(6)],
        out_shape=[jax.ShapeDtypeStruct((n, d), BF16) for _ in range(6)],
        compiler_params=_cparams(("parallel",)),
    )(h, mix3)


def _shift_mix_bwd_call(h, mix3, douts, L):
    n, d = h.shape
    T = n - L
    tc = _pick(d // 4, 256)
    nq = (d // 4) // tc

    def body(h_ref, mix_ref, d0, d1, d2, d3, d4, d5, dh_ref, dmix_ref):
        d_refs = (d0, d1, d2, d3, d4, d5)
        q = pl.program_id(0) // nq
        for quarter in range(4):
            @pl.when(q == quarter)
            def _(quarter=quarter):
                dmix = [jnp.zeros((1, tc), F32) for _ in range(6)]
                for lo, cnt, spec in ((0, L, _ctx_shift_spec), (L, T, _grid_shift_spec)):
                    hh = h_ref[pl.ds(lo, cnt), :]
                    s, keep = spec(cnt, tc, quarter)
                    xx = _shift_rows(hh, s, keep) - hh
                    direct = jnp.zeros((cnt, tc), F32)
                    shifted = jnp.zeros((cnt, tc), F32)
                    for m in range(6):
                        dm = d_refs[m][pl.ds(lo, cnt), :].astype(F32)
                        mx = mix_ref[m]
                        direct = direct + dm * (1.0 - mx)
                        shifted = shifted + dm * mx
                        dmix[m] = dmix[m] + jnp.sum(dm * xx, axis=0, keepdims=True)
                    dh_ref[pl.ds(lo, cnt), :] = direct + _unshift_rows(shifted, s, keep)
                for m in range(6):
                    dmix_ref[m] = dmix[m]

    tile = pl.BlockSpec((n, tc), lambda j: (0, j))
    return pl.pallas_call(
        body, name="shift_mix_bwd", grid=(d // tc,),
        in_specs=[tile, pl.BlockSpec((6, 1, tc), lambda j: (0, 0, j))] + [tile] * 6,
        out_specs=[tile, pl.BlockSpec((6, 1, tc), lambda j: (0, 0, j))],
        out_shape=[jax.ShapeDtypeStruct((n, d), F32), jax.ShapeDtypeStruct((6, 1, d), F32)],
        compiler_params=_cparams(("parallel",)),
    )(h, mix3, *douts)


@functools.partial(jax.custom_vjp, nondiff_argnums=(2,))
def shift_mix(h, mix3, L):
    return tuple(_shift_mix_fwd_call(h, mix3, L))


def _shift_mix_fwd(h, mix3, L):
    return tuple(_shift_mix_fwd_call(h, mix3, L)), (h, mix3)


def _shift_mix_bwd(L, res, douts):
    h, mix3 = res
    dh, dmix = _shift_mix_bwd_call(h, mix3, tuple(douts), L)
    return dh, dmix


shift_mix.defvjp(_shift_mix_fwd, _shift_mix_bwd)


def _conv_specs(T, d, tc):
    nd = d // tc
    ins = [pl.BlockSpec((T, tc), functools.partial(lambda j, off: (0, j + off), off=o * nd)) for o in range(3)]
    return ins, pl.BlockSpec((3, 1, tc), lambda j: (0, 0, j))


def _conv_terms(gc, u, tc):
    T = gc.shape[0]
    row = _row_iota(T, tc)
    z = gc * u
    return z, _shift_rows(z, 1, row >= 1), _shift_rows(z, -1, row < T - 1), row


def _conv_fwd_call(guc, cw3):
    T, d3 = guc.shape
    d = d3 // 3
    tc = _pick(d, 256)
    ins, wspec = _conv_specs(T, d, tc)

    def body(gb_ref, gc_ref, u_ref, w_ref, p_ref):
        z, zp, zn, _ = _conv_terms(gc_ref[...].astype(F32), u_ref[...].astype(F32), tc)
        conv = zp * w_ref[0] + z * w_ref[1] + zn * w_ref[2]
        p_ref[...] = (gb_ref[...].astype(F32) * conv).astype(BF16)

    return pl.pallas_call(
        body, name="conv_fwd", grid=(d // tc,), in_specs=ins + [wspec],
        out_specs=pl.BlockSpec((T, tc), lambda j: (0, j)),
        out_shape=jax.ShapeDtypeStruct((T, d), BF16),
        compiler_params=_cparams(("parallel",)),
    )(guc, guc, guc, cw3)


def _conv_bwd_call(guc, cw3, dp):
    T, d3 = guc.shape
    d = d3 // 3
    tc = _pick(d, 256)
    ins, wspec = _conv_specs(T, d, tc)
    tile = pl.BlockSpec((T, tc), lambda j: (0, j))

    def body(gb_ref, gc_ref, u_ref, w_ref, dp_ref, dgb_ref, dgc_ref, du_ref, dw_ref):
        gc = gc_ref[...].astype(F32)
        u = u_ref[...].astype(F32)
        z, zp, zn, row = _conv_terms(gc, u, tc)
        conv = zp * w_ref[0] + z * w_ref[1] + zn * w_ref[2]
        dpv = dp_ref[...].astype(F32)
        dgb_ref[...] = (dpv * conv).astype(dgb_ref.dtype)
        dconv = dpv * gb_ref[...].astype(F32)
        dz = (_shift_rows(dconv, -1, row < T - 1) * w_ref[0] + dconv * w_ref[1]
              + _shift_rows(dconv, 1, row >= 1) * w_ref[2])
        dgc_ref[...] = (dz * u).astype(dgc_ref.dtype)
        du_ref[...] = (dz * gc).astype(du_ref.dtype)
        dw_ref[0] = jnp.sum(dconv * zp, axis=0, keepdims=True)
        dw_ref[1] = jnp.sum(dconv * z, axis=0, keepdims=True)
        dw_ref[2] = jnp.sum(dconv * zn, axis=0, keepdims=True)

    return pl.pallas_call(
        body, name="conv_bwd", grid=(d // tc,), in_specs=ins + [wspec, tile],
        out_specs=[tile, tile, tile, wspec],
        out_shape=[jax.ShapeDtypeStruct((T, d), guc.dtype)] * 3 + [jax.ShapeDtypeStruct((3, 1, d), F32)],
        compiler_params=_cparams(("parallel",)),
    )(guc, guc, guc, cw3, dp)


@jax.custom_vjp
def gated_conv(guc, cw3):
    return _conv_fwd_call(guc, cw3)


def _gated_conv_fwd(guc, cw3):
    return _conv_fwd_call(guc, cw3), (guc, cw3)


def _gated_conv_bwd(res, dp):
    guc, cw3 = res
    dgb, dgc, du, dw = _conv_bwd_call(guc, cw3, dp)
    return jnp.concatenate([dgb, dgc, du], axis=1), dw


gated_conv.defvjp(_gated_conv_fwd, _gated_conv_bwd)


def _chunk_map(nchunk, nctx_chunk, reverse):
    if not reverse:
        return lambda c: c
    return lambda c: jnp.where(c < nctx_chunk, nctx_chunk - 1 - c, nchunk - 1 - (c - nctx_chunk))


def _spread(row_ref, dst_scr, lo_mask, ni, C):
    for i in range(ni):
        idx = jnp.where(lo_mask, 2 * i, HEAD + 1 + 2 * i).astype(jnp.int32)
        for tt in range(C):
            dst_scr[tt, i] = jnp.take_along_axis(row_ref[tt], idx, axis=1)


def _half_sums(p, lo_mask):
    lo = jnp.sum(jnp.where(lo_mask, p, 0.0), axis=1, keepdims=True)
    hi = jnp.sum(jnp.where(lo_mask, 0.0, p), axis=1, keepdims=True)
    return jnp.where(lo_mask, lo, hi)


def _half_sums_mxu(ps, gmat):
    p = jnp.concatenate(ps, axis=0)
    hi = p.astype(BF16)
    lo = (p - hi.astype(F32)).astype(BF16)
    s = jnp.dot(hi, gmat, preferred_element_type=F32) + jnp.dot(lo, gmat, preferred_element_type=F32)
    nh = ps[0].shape[0]
    return [s[i * nh:(i + 1) * nh] for i in range(len(ps))]


def _split_row(sums, lane, nh):
    acc = jnp.zeros((nh, LANES), F32)
    for i, s in enumerate(sums):
        acc = acc + jnp.where(jnp.logical_or(lane == 2 * i, lane == HEAD + 1 + 2 * i), s, 0.0)
    return acc


def _wkv_fwd_call(r2, w2, kd2, kk2, as2, v2, nctx, reverse, xchg_arrs=(), xchg_specs=()):
    n, nh, _ = r2.shape
    C = SCAN_CHUNK
    ni = HEAD // 2
    nchunk = n // C
    cmap = _chunk_map(nchunk, nctx // C, reverse)
    nx = len(xchg_arrs)

    def body(*refs):
        g_ref, refs = refs[0], refs[1:]
        r_ref, w_ref, kd_ref, kk_ref, as_ref, v_ref = refs[:6]
        x_in = refs[6:6 + nx]
        y_ref, sa_ref, sp_ref = refs[6 + nx:9 + nx]
        x_out = refs[9 + nx:9 + 2 * nx]
        s_scr, vc_scr = refs[9 + 2 * nx:11 + 2 * nx]
        if nx:
            _fused_exchanges(list(zip(x_in, x_out)), xchg_specs, *refs[11 + 2 * nx:], first=pl.program_id(0) == 0)

        @pl.when(pl.program_id(0) == 0)
        def _():
            s_scr[...] = jnp.zeros_like(s_scr)

        lane = lax.broadcasted_iota(jnp.int32, (nh, LANES), 1)
        lo_mask = lane < HEAD
        _spread(v_ref, vc_scr, lo_mask, ni, C)

        def make_step(with_y):
            def step(j, carry):
                t = (C - 1 - j) if reverse else j
                kk = kk_ref[t]
                a2 = -kk
                b2 = kk * as_ref[t]
                w = w_ref[t]
                k = kd_ref[t]
                r = r_ref[t]
                sas = []
                for i in range(ni):
                    si = s_scr[i]
                    sp_ref[t, i] = si
                    sas.append(_half_sums(si * a2, lo_mask))
                qs = []
                for i in range(ni):
                    sn = s_scr[i] * w + sas[i] * b2 + vc_scr[t, i] * k
                    s_scr[i] = sn
                    if with_y:
                        qs.append(sn * r)
                if with_y:
                    y_ref[t] = _split_row(_half_sums_mxu(qs, g_ref[...]), lane, nh)
                else:
                    y_ref[t] = jnp.zeros((nh, LANES), F32)
                sa_ref[t] = _split_row(sas, lane, nh)
                return carry
            return step

        is_ctx = pl.program_id(0) < nctx // C

        @pl.when(is_ctx)
        def _():
            lax.fori_loop(0, C, make_step(False), 0)

        @pl.when(jnp.logical_not(is_ctx))
        def _():
            lax.fori_loop(0, C, make_step(True), 0)

        if nx:
            _fused_exchanges(list(zip(x_in, x_out)), xchg_specs, *refs[11 + 2 * nx:],
                             last=pl.program_id(0) == nchunk - 1)

    tok = pl.BlockSpec((C, nh, LANES), lambda c: (cmap(c), 0, 0))
    return pl.pallas_call(
        body, name="wkv_fwd_rev" if reverse else "wkv_fwd", grid=(nchunk,),
        in_specs=[pl.BlockSpec((LANES, LANES), lambda c: (0, 0))] + [tok] * 6 + [_ANY] * nx,
        out_specs=[tok, tok, pl.BlockSpec((C, ni, nh, LANES), lambda c: (cmap(c), 0, 0, 0))] + [_ANY] * nx,
        out_shape=[jax.ShapeDtypeStruct((n, nh, LANES), F32), jax.ShapeDtypeStruct((n, nh, LANES), F32),
                   jax.ShapeDtypeStruct((n, ni, nh, LANES), F32)] + _exchange_out_shapes(xchg_arrs, xchg_specs),
        scratch_shapes=[pltpu.VMEM((ni, nh, LANES), F32), pltpu.VMEM((C, ni, nh, LANES), F32)]
        + (_exchange_sems(nx) if nx else []),
        compiler_params=_cparams(("arbitrary",)),
    )(_head_group_matrix(LANES).astype(BF16), r2, w2, kd2, kk2, as2, v2, *xchg_arrs)


def _wkv_bwd_call(r2, w2, kd2, kk2, as2, v2, sa, sprev, dy, nctx, reverse, xchg_arrs=(), xchg_specs=()):
    n, nh, _ = r2.shape
    C = SCAN_CHUNK
    ni = HEAD // 2
    nchunk = n // C
    fmap = _chunk_map(nchunk, nctx // C, reverse)
    cmap = lambda c: fmap(nchunk - 1 - c)
    nx = len(xchg_arrs)

    def body(*refs):
        g_ref, refs = refs[0], refs[1:]
        r_ref, w_ref, kd_ref, kk_ref, as_ref, v_ref, sa_ref, sp_ref, dy_ref = refs[:9]
        x_in = refs[9:9 + nx]
        dr_ref, dw_ref, dkd_ref, dkk_ref, das_ref, dv_ref = refs[9 + nx:15 + nx]
        x_out = refs[15 + nx:15 + 2 * nx]
        ds_scr, vc_scr, sac_scr, dyc_scr = refs[15 + 2 * nx:19 + 2 * nx]
        if nx:
            _fused_exchanges(list(zip(x_in, x_out)), xchg_specs, *refs[19 + 2 * nx:], first=pl.program_id(0) == 0)

        @pl.when(pl.program_id(0) == 0)
        def _():
            ds_scr[...] = jnp.zeros_like(ds_scr)

        lane = lax.broadcasted_iota(jnp.int32, (nh, LANES), 1)
        lo_mask = lane < HEAD
        _spread(v_ref, vc_scr, lo_mask, ni, C)
        _spread(sa_ref, sac_scr, lo_mask, ni, C)
        is_ctx = pl.program_id(0) >= nchunk - nctx // C

        @pl.when(jnp.logical_not(is_ctx))
        def _():
            _spread(dy_ref, dyc_scr, lo_mask, ni, C)

        def make_step(with_dy):
            def step(j, carry):
                t = j if reverse else (C - 1 - j)
                kk = kk_ref[t]
                sig = as_ref[t]
                a2 = -kk
                b2 = kk * sig
                w = w_ref[t]
                k = kd_ref[t]
                r = r_ref[t]
                zero = jnp.zeros((nh, LANES), F32)
                acc_dk, acc_db, acc_dw, acc_g, acc_sady, acc_vdy, acc_da = zero, zero, zero, zero, zero, zero, zero
                dvp, dsas = [], []
                for i in range(ni):
                    sp = sp_ref[t, i]
                    vc = vc_scr[t, i]
                    sac = sac_scr[t, i]
                    ds = ds_scr[i]
                    if with_dy:
                        dyc = dyc_scr[t, i]
                        ds = ds + dyc * r
                        ds_scr[i] = ds
                        acc_g = acc_g + sp * dyc
                        acc_sady = acc_sady + sac * dyc
                        acc_vdy = acc_vdy + vc * dyc
                    dvp.append(ds * k)
                    dsas.append(_half_sums(ds * b2, lo_mask))
                    acc_dk = acc_dk + ds * vc
                    acc_db = acc_db + ds * sac
                    acc_dw = acc_dw + ds * sp
                for i in range(ni):
                    acc_da = acc_da + sp_ref[t, i] * dsas[i]
                    ds_scr[i] = ds_scr[i] * w + dsas[i] * a2
                dr_ref[t] = acc_g * w + b2 * acc_sady + k * acc_vdy
                dw_ref[t] = acc_dw
                dkd_ref[t] = acc_dk
                dkk_ref[t] = acc_db * sig - acc_da
                das_ref[t] = acc_db * kk
                dv_ref[t] = _split_row(_half_sums_mxu(dvp, g_ref[...]), lane, nh)
                return carry
            return step

        @pl.when(is_ctx)
        def _():
            lax.fori_loop(0, C, make_step(False), 0)

        @pl.when(jnp.logical_not(is_ctx))
        def _():
            lax.fori_loop(0, C, make_step(True), 0)

        if nx:
            _fused_exchanges(list(zip(x_in, x_out)), xchg_specs, *refs[19 + 2 * nx:],
                             last=pl.program_id(0) == nchunk - 1)

    tok = pl.BlockSpec((C, nh, LANES), lambda c: (cmap(c), 0, 0))
    big = pltpu.VMEM((C, ni, nh, LANES), F32)
    return pl.pallas_call(
        body, name="wkv_bwd_rev" if reverse else "wkv_bwd", grid=(nchunk,),
        in_specs=[pl.BlockSpec((LANES, LANES), lambda c: (0, 0))] + [tok] * 7
        + [pl.BlockSpec((C, ni, nh, LANES), lambda c: (cmap(c), 0, 0, 0)), tok] + [_ANY] * nx,
        out_specs=[tok] * 6 + [_ANY] * nx,
        out_shape=[jax.ShapeDtypeStruct((n, nh, LANES), F32)] * 6 + _exchange_out_shapes(xchg_arrs, xchg_specs),
        scratch_shapes=[pltpu.VMEM((ni, nh, LANES), F32), big, big, big] + (_exchange_sems(nx) if nx else []),
        compiler_params=_cparams(("arbitrary",)),
    )(_head_group_matrix(LANES).astype(BF16), r2, w2, kd2, kk2, as2, v2, sa, sprev, dy, *xchg_arrs)


def _tile_heads(t):
    n, d = t.shape
    th = t.reshape(n, d // HEAD, HEAD)
    return jnp.concatenate([th, th], axis=-1)


def loss_head(x3, fo, tgt, gate, g):
    T, d = x3.shape
    tr = _pick(T, 128, 8)

    def body(x_ref, f_ref, t_ref, gate_ref, g_ref, loss_ref, dx_ref, df_ref, dgate_ref, dg_ref):
        tg = t_ref[...]

        def fl(x, fo_, gate_, g_):
            x4 = x + gate_ * fo_
            y = (x4 * lax.rsqrt(jnp.mean(x4 * x4, axis=-1, keepdims=True) + NORM_EPS)) * g_
            return 0.5 * jnp.sum(jnp.mean(jnp.square(y - tg), axis=-1))

        val, vjp = jax.vjp(fl, x_ref[...], f_ref[...], gate_ref[...], g_ref[...])
        dx, dfo, dgate, dg = vjp(jnp.ones((), F32))
        dx_ref[...] = dx
        df_ref[...] = dfo
        i = pl.program_id(0)

        @pl.when(i == 0)
        def _():
            loss_ref[...] = jnp.zeros_like(loss_ref)
            dgate_ref[...] = jnp.zeros_like(dgate_ref)
            dg_ref[...] = jnp.zeros_like(dg_ref)

        loss_ref[...] += jnp.full(loss_ref.shape, val, F32)
        dgate_ref[...] += dgate
        dg_ref[...] += dg

    tile = pl.BlockSpec((tr, d), lambda i: (i, 0))
    vec = pl.BlockSpec((1, d), lambda i: (0, 0))
    return pl.pallas_call(
        body, name="loss_head", grid=(T // tr,),
        in_specs=[tile, tile, tile, vec, vec],
        out_specs=[pl.BlockSpec((8, LANES), lambda i: (0, 0)), tile, tile, vec, vec],
        out_shape=[jax.ShapeDtypeStruct((8, LANES), F32), jax.ShapeDtypeStruct((T, d), F32),
                   jax.ShapeDtypeStruct((T, d), F32), jax.ShapeDtypeStruct((1, d), F32),
                   jax.ShapeDtypeStruct((1, d), F32)],
        compiler_params=_cparams(("arbitrary",)),
    )(x3, fo, tgt, gate, g)


def sum_adam(parts, w, m, v, name, lead=None):
    P, R, Cc = parts.shape
    tc = _pick(Cc, 1024)
    tr = _pick(R, max(8, (256 * 1024) // tc), 16 if parts.dtype == BF16 else 8)

    def body(p_ref, w_ref, m_ref, v_ref, g_ref, d_ref, nm_ref, nv_ref):
        g = p_ref[0].astype(F32)
        for s in range(1, P):
            g = g + p_ref[s].astype(F32)
        m_new = ADAM_B1 * m_ref[...] + (1.0 - ADAM_B1) * g
        v_new = ADAM_B2 * v_ref[...] + (1.0 - ADAM_B2) * jnp.square(g)
        m_hat = m_new / (1.0 - ADAM_B1 ** ADAM_STEP)
        v_hat = v_new / (1.0 - ADAM_B2 ** ADAM_STEP)
        g_ref[...] = g
        d_ref[...] = -ADAM_LR * (m_hat / (jnp.sqrt(v_hat) + ADAM_EPS) + ADAM_WD * w_ref[...])
        nm_ref[...] = m_new
        nv_ref[...] = v_new

    if lead is None:
        pspec = pl.BlockSpec((tr, tc), lambda i, j: (i, j))
    else:
        pspec = pl.BlockSpec((None, tr, tc), lambda i, j: (lead, i, j))
    ospec = pl.BlockSpec((tr, tc), lambda i, j: (i, j))
    return pl.pallas_call(
        body, name=name, grid=(R // tr, Cc // tc),
        in_specs=[pl.BlockSpec((P, tr, tc), lambda i, j: (0, i, j)), pspec, pspec, pspec],
        out_specs=[ospec] * 4,
        out_shape=[jax.ShapeDtypeStruct((R, Cc), F32)] * 4,
        compiler_params=_cparams(("parallel", "parallel")),
    )(parts, w, m, v)


def _me():
    return lax.axis_index("x"), lax.axis_index("y"), lax.axis_index("c")


def _peer(p):
    x, y, c = _me()
    px = 1 - x if p & 4 else x
    py = 1 - y if p & 2 else y
    pc = 1 - c if p & 1 else c
    return (px, py, pc), 4 * px + 2 * py + pc


def _block_view(ref, axis, idx, r, c):
    if axis is None:
        return ref.at[idx]
    if axis == 0:
        return ref.at[pl.ds(idx * r, r), :]
    return ref.at[:, pl.ds(idx * c, c)]


def _exchange_copies(src_of, dst_of, ssem, rsem, lsem, with_recvs):
    x, y, c = _me()
    me = 4 * x + 2 * y + c
    local = pltpu.make_async_copy(src_of(me), dst_of(me), lsem)
    sends, recvs = [], []
    for p in range(1, N_DEV):
        dev, idx = _peer(p)
        sends.append(pltpu.make_async_remote_copy(src_ref=src_of(idx), dst_ref=dst_of(me), send_sem=ssem(p),
                                                  recv_sem=rsem(p), device_id=dev,
                                                  device_id_type=pl.DeviceIdType.MESH))
        if with_recvs:
            recvs.append(pltpu.make_async_remote_copy(src_ref=src_of(idx), dst_ref=dst_of(idx), send_sem=ssem(p),
                                                      recv_sem=rsem(p), device_id=dev,
                                                      device_id_type=pl.DeviceIdType.MESH))
    return local, sends, recvs


def _exchange_start(*args):
    local, sends, _ = _exchange_copies(*args, with_recvs=False)
    local.start()
    for cp in sends:
        cp.start()


def _exchange_wait(*args):
    local, sends, recvs = _exchange_copies(*args, with_recvs=True)
    for cp in recvs:
        cp.wait_recv()
    for cp in sends:
        cp.wait_send()
    local.wait()


def _exchange(src_of, dst_of, send_sems, recv_sems, local_sem):
    args = (src_of, dst_of, lambda p: send_sems.at[p], lambda p: recv_sems.at[p], local_sem)
    _exchange_start(*args)
    _exchange_wait(*args)


def _fused_exchanges(pairs, specs, send_sems, recv_sems, local_sems, first=None, last=None):
    def args(j):
        src, dst = pairs[j]
        kind, axis, r, c = specs[j]
        if kind == "ag":
            src_of = lambda idx: src
            dst_of = lambda idx: _block_view(dst, axis, idx, r, c)
        else:
            src_of = lambda idx: _block_view(src, axis, idx, r, c)
            dst_of = lambda idx: dst.at[idx]
        return (src_of, dst_of, lambda p: send_sems.at[j, p], lambda p: recv_sems.at[j, p], local_sems.at[j])

    if first is not None:
        @pl.when(first)
        def _():
            for j in range(len(pairs)):
                _exchange_start(*args(j))

    if last is not None:
        @pl.when(last)
        def _():
            for j in range(len(pairs)):
                _exchange_wait(*args(j))


def _exchange_out_shapes(arrs, specs):
    out = []
    for a, (kind, axis, r, c) in zip(arrs, specs):
        if kind == "rs":
            out.append(jax.ShapeDtypeStruct((N_DEV, r, c), a.dtype))
        else:
            out.append(jax.ShapeDtypeStruct((N_DEV * r, c) if axis == 0 else (r, N_DEV * c), a.dtype))
    return out


def _exchange_specs(kind, arrs, axes):
    specs = []
    for a, axis in zip(arrs, axes):
        if kind == "ag":
            r, c = a.shape
        elif axis == 0:
            r, c = a.shape[0] // N_DEV, a.shape[1]
        else:
            r, c = a.shape[0], a.shape[1] // N_DEV
        specs.append((kind, axis, r, c))
    return specs


def _exchange_sems(n):
    return [pltpu.SemaphoreType.DMA((n, N_DEV)), pltpu.SemaphoreType.DMA((n, N_DEV)), pltpu.SemaphoreType.DMA((n,))]


_SEMS = [pltpu.SemaphoreType.DMA((N_DEV,)), pltpu.SemaphoreType.DMA((N_DEV,)), pltpu.SemaphoreType.DMA]
_ANY = pl.BlockSpec(memory_space=pl.ANY)


def all_gather(x, axis, name):
    r, c = x.shape
    shape = (N_DEV * r, c) if axis == 0 else (r, N_DEV * c)

    def body(x_ref, o_ref, send_sems, recv_sems, local_sem):
        _exchange(lambda idx: x_ref, lambda idx: _block_view(o_ref, axis, idx, r, c), send_sems, recv_sems, local_sem)

    return pl.pallas_call(
        body, name=name, in_specs=[_ANY], out_specs=_ANY,
        out_shape=jax.ShapeDtypeStruct(shape, x.dtype), scratch_shapes=_SEMS,
    )(x)


def all_gather_stack(x, name):
    r, c = x.shape

    def body(x_ref, o_ref, send_sems, recv_sems, local_sem):
        _exchange(lambda idx: x_ref, lambda idx: o_ref.at[idx], send_sems, recv_sems, local_sem)

    return pl.pallas_call(
        body, name=name, in_specs=[_ANY], out_specs=_ANY,
        out_shape=jax.ShapeDtypeStruct((N_DEV, r, c), x.dtype), scratch_shapes=_SEMS,
    )(x)


def reduce_scatter_exchange(g, axis, name):
    if axis is None:
        _, r, c = g.shape
    elif axis == 0:
        r, c = g.shape[0] // N_DEV, g.shape[1]
    else:
        r, c = g.shape[0], g.shape[1] // N_DEV

    def body(g_ref, o_ref, send_sems, recv_sems, local_sem):
        _exchange(lambda idx: _block_view(g_ref, axis, idx, r, c), lambda idx: o_ref.at[idx],
                  send_sems, recv_sems, local_sem)

    return pl.pallas_call(
        body, name=name, in_specs=[_ANY], out_specs=_ANY,
        out_shape=jax.ShapeDtypeStruct((N_DEV, r, c), g.dtype), scratch_shapes=_SEMS,
    )(g)


PACK_QUANTUM = 16 * LANES


def _pack(arrs, dtype=F32, lead=0):
    keep = arrs[0].shape[:lead]
    flat = jnp.concatenate([a.reshape(keep + (-1,)).astype(dtype) for a in arrs], axis=-1)
    pad = (-flat.shape[-1]) % PACK_QUANTUM
    flat = jnp.pad(flat, ((0, 0),) * lead + ((0, pad),))
    return flat.reshape(keep + (-1, LANES))


def _unpack(flat2d, shapes, lead=()):
    flat = flat2d.reshape(lead + (-1,))
    out, off = [], 0
    for s in shapes:
        n = int(np.prod(s))
        out.append(flat[..., off:off + n].reshape(lead + tuple(s)))
        off += n
    return out


def _gather_lastdim(stk):
    return jnp.moveaxis(stk, 0, -2).reshape(stk.shape[1:-1] + (N_DEV * stk.shape[-1],))


def _gather_dim(stk, dim):
    moved = jnp.moveaxis(stk, 0, dim)
    sh = list(stk.shape[1:])
    sh[dim] = sh[dim] * N_DEV
    return moved.reshape(sh)


def _scatter_dim(full, dim):
    sh = list(full.shape)
    sh[dim:dim + 1] = [N_DEV, sh[dim] // N_DEV]
    return jnp.moveaxis(full.reshape(sh), dim, 0)


def _head_group_matrix(tc):
    return np.kron(np.eye(tc // HEAD, dtype=np.float32), np.ones((HEAD, HEAD), np.float32))


_SCAN_COMM = {0: ("w13_0", "w13_1"), 1: ("win", "w2_0", "w2_1", "wout", "wo")}
_W_AXIS = dict(wr=0, wk=0, wv=0, wo=0, win=1, wout=0, w13_0=1, w13_1=1, w2_0=0, w2_1=0)


def _build_forward(ctx2d, T, D, shards=None, sink=None):
    L = ctx2d.shape[0]
    N = L + T

    def make_scan(d):
        keys = _SCAN_COMM[d] if shards is not None else ()
        axes = tuple(_W_AXIS[k] for k in keys)

        def run_fwd(tok, sh):
            return _wkv_fwd_call(*tok, L, d == 1, tuple(sh), _exchange_specs("ag", sh, axes))

        @jax.custom_vjp
        def op(tok, sh):
            outs = run_fwd(tok, sh)
            return (outs[0],) + tuple(outs[3:])

        def op_fwd(tok, sh):
            outs = run_fwd(tok, sh)
            return (outs[0],) + tuple(outs[3:]), (tok, outs[1], outs[2])

        def op_bwd(res, cts):
            tok, sa, sprev = res
            dg = tuple(cts[1:])
            outs = _wkv_bwd_call(*tok, sa, sprev, cts[0], L, d == 1, dg, _exchange_specs("rs", dg, axes))
            for k, recv in zip(keys, outs[6:]):
                sink[k] = recv
            return tuple(outs[:6]), tuple(jnp.zeros(shards[k].shape, shards[k].dtype) for k in keys)

        op.defvjp(op_fwd, op_bwd)
        return op, keys

    scans = [make_scan(0), make_scan(1)]
    gm = _head_group_matrix(LANES)
    tr_row = _pick(math.gcd(L, T), 128, 8)
    op_norm = make_rowwise("norm_mod", _f_norm_mod, (F32,), tr_row, D, nb0=L // tr_row)
    op_res = [make_rowwise(f"res_norm_mod{i}", _f_res_norm_mod, (F32, BF16), _pick(T, 128, 8), D) for i in range(3)]
    op_prep = make_rowwise("wkv_prep", _f_prep, (F32,) * 7, _pick(N, 256, 8), LANES, consts=(gm,))
    op_read = make_rowwise("wkv_readout", _f_readout, (BF16,), _pick(T, 256, 8), LANES, consts=(gm,))

    def v3(a):
        return a.reshape(a.shape[0], 1, a.shape[-1])

    def fwd(xin, Ps, Wb):
        modx, modc = Ps["modx"], Ps["modc"]
        cat = jnp.concatenate([ctx2d, xin], axis=0)
        seg = lambda a, b: jnp.stack([a, b])[:, None, :]
        (hcat,) = op_norm((cat,), (Ps["n1"][0][None, None, :], seg(modc[0], modx[0, 0]), seg(modc[1], modx[0, 1])))
        xr, xw, xk, xv, xa, xg = shift_mix(hcat, Ps["mix"][:, None, :], L)
        r = linear(xr, Wb["wr"], F32, "wr")
        k = linear(xk, Wb["wk"], F32, "wk")
        v = linear(xv, Wb["wv"], F32, "wv")
        gl = jax.nn.sigmoid(linear(xg, Wb["g1"], F32, "g1"))
        g = linear(gl.astype(BF16), Wb["g2"], F32, "g2")
        tw = jnp.tanh(linear(xw, Wb["w1"], F32, "w1")).astype(BF16)
        ta = linear(xa, Wb["a1"], F32, "a1").astype(BF16)
        lw = [linear(tw[:, LORA_PAD * d:LORA_PAD * (d + 1)], Wb["w2d"][d], F32, f"w2_{d}") for d in range(2)]
        la = [linear(ta[:, LORA_PAD * d:LORA_PAD * (d + 1)], Wb["a2d"][d], F32, f"a2_{d}") for d in range(2)]
        kk, dec0, dec1, kd0, kd1, as0, as1 = op_prep(
            (k, lw[0], lw[1], la[0], la[1]),
            (v3(Ps["kk"]), v3(Ps["ka"]), Ps["w0"][0][None, None, :], Ps["w0"][1][None, None, :],
             Ps["a0"][0][None, None, :], Ps["a0"][1][None, None, :]))
        r2, kk2, v2 = _tile_heads(r), _tile_heads(kk), _tile_heads(v)
        ys = []
        Wb = dict(Wb)
        for d, (dec, kd, sg) in enumerate(((dec0, kd0, as0), (dec1, kd1, as1))):
            op, keys = scans[d]
            outs = op((r2, _tile_heads(dec), _tile_heads(kd), kk2, _tile_heads(sg), v2),
                      tuple(shards[k] for k in keys))
            yx = outs[0][L:]
            ys.append((yx[:, :, :HEAD] + yx[:, :, HEAD:]).reshape(T, D))
            Wb.update(zip(keys, outs[1:]))
        (o,) = op_read((ys[0], ys[1], r[L:], kd0[L:], kd1[L:], v[L:], g[L:]),
                       (v3(Ps["rk"]), v3(Ps["lnw"]), v3(Ps["lnb"])))
        att = linear(o, Wb["wo"], F32, "wo")
        x1, h2 = op_res[0]((xin, att), (modx[0, 2][None, None, :], Ps["n2"][0][None, None, :],
                                        modx[0, 3][None, None, :], modx[0, 4][None, None, :]))
        act = swiglu_act(linear(h2, Wb["w13_0"], BF16, "w13_0"), "swiglu0")
        f0 = linear(act, Wb["w2_0"], F32, "w2_0")
        x2, h = op_res[1]((x1, f0), (modx[0, 5][None, None, :], Ps["n1"][1][None, None, :],
                                     modx[1, 0][None, None, :], modx[1, 1][None, None, :]))
        guc = linear(h, Wb["win"], BF16, "win")
        p = gated_conv(guc, Ps["conv"][:, None, :])
        cv = linear(p, Wb["wout"], F32, "wout")
        x3, h2b = op_res[2]((x2, cv), (modx[1, 2][None, None, :], Ps["n2"][1][None, None, :],
                                       modx[1, 3][None, None, :], modx[1, 4][None, None, :]))
        act1 = swiglu_act(linear(h2b, Wb["w13_1"], BF16, "w13_1"), "swiglu1")
        f1 = linear(act1, Wb["w2_1"], F32, "w2_1")
        return x3, f1


    return fwd


def kernel(x, c, ctx, c_ctx, norm1_g, norm2_g, ada_w, ada_b, rw_mix, rw_wr, rw_wk, rw_wv, rw_wo, rw_w0, rw_w1, rw_w2, rw_a0, rw_a1, rw_a2, rw_g1, rw_g2, rw_kk, rw_ka, rw_rk, rw_lnw, rw_lnb, sc_win, sc_conv, sc_wout, ffn_w13, ffn_w2, final_g, loss_target, m_c_ctx, m_norm1_g, m_norm2_g, m_ada_w, m_ada_b, m_rw_mix, m_rw_wr, m_rw_wk, m_rw_wv, m_rw_wo, m_rw_w0, m_rw_w1, m_rw_w2, m_rw_a0, m_rw_a1, m_rw_a2, m_rw_g1, m_rw_g2, m_rw_kk, m_rw_ka, m_rw_rk, m_rw_lnw, m_rw_lnb, m_sc_win, m_sc_conv, m_sc_wout, m_ffn_w13, m_ffn_w2, m_final_g, v_c_ctx, v_norm1_g, v_norm2_g, v_ada_w, v_ada_b, v_rw_mix, v_rw_wr, v_rw_wk, v_rw_wv, v_rw_wo, v_rw_w0, v_rw_w1, v_rw_w2, v_rw_a0, v_rw_a1, v_rw_a2, v_rw_g1, v_rw_g2, v_rw_kk, v_rw_ka, v_rw_rk, v_rw_lnw, v_rw_lnb, v_sc_win, v_sc_conv, v_sc_wout, v_ffn_w13, v_ffn_w2, v_final_g):
    W = dict(c_ctx=c_ctx, norm1_g=norm1_g, norm2_g=norm2_g, ada_w=ada_w, ada_b=ada_b, rw_mix=rw_mix, rw_wr=rw_wr,
             rw_wk=rw_wk, rw_wv=rw_wv, rw_wo=rw_wo, rw_w0=rw_w0, rw_w1=rw_w1, rw_w2=rw_w2, rw_a0=rw_a0, rw_a1=rw_a1,
             rw_a2=rw_a2, rw_g1=rw_g1, rw_g2=rw_g2, rw_kk=rw_kk, rw_ka=rw_ka, rw_rk=rw_rk, rw_lnw=rw_lnw,
             rw_lnb=rw_lnb, sc_win=sc_win, sc_conv=sc_conv, sc_wout=sc_wout, ffn_w13=ffn_w13, ffn_w2=ffn_w2,
             final_g=final_g)
    Mo = dict(c_ctx=m_c_ctx, norm1_g=m_norm1_g, norm2_g=m_norm2_g, ada_w=m_ada_w, ada_b=m_ada_b, rw_mix=m_rw_mix,
              rw_wr=m_rw_wr, rw_wk=m_rw_wk, rw_wv=m_rw_wv, rw_wo=m_rw_wo, rw_w0=m_rw_w0, rw_w1=m_rw_w1,
              rw_w2=m_rw_w2, rw_a0=m_rw_a0, rw_a1=m_rw_a1, rw_a2=m_rw_a2, rw_g1=m_rw_g1, rw_g2=m_rw_g2,
              rw_kk=m_rw_kk, rw_ka=m_rw_ka, rw_rk=m_rw_rk, rw_lnw=m_rw_lnw, rw_lnb=m_rw_lnb, sc_win=m_sc_win,
              sc_conv=m_sc_conv, sc_wout=m_sc_wout, ffn_w13=m_ffn_w13, ffn_w2=m_ffn_w2, final_g=m_final_g)
    Vo = dict(c_ctx=v_c_ctx, norm1_g=v_norm1_g, norm2_g=v_norm2_g, ada_w=v_ada_w, ada_b=v_ada_b, rw_mix=v_rw_mix,
              rw_wr=v_rw_wr, rw_wk=v_rw_wk, rw_wv=v_rw_wv, rw_wo=v_rw_wo, rw_w0=v_rw_w0, rw_w1=v_rw_w1,
              rw_w2=v_rw_w2, rw_a0=v_rw_a0, rw_a1=v_rw_a1, rw_a2=v_rw_a2, rw_g1=v_rw_g1, rw_g2=v_rw_g2,
              rw_kk=v_rw_kk, rw_ka=v_rw_ka, rw_rk=v_rw_rk, rw_lnw=v_rw_lnw, rw_lnb=v_rw_lnb, sc_win=v_sc_win,
              sc_conv=v_sc_conv, sc_wout=v_sc_wout, ffn_w13=v_ffn_w13, ffn_w2=v_ffn_w2, final_g=v_final_g)
    names = list(W)

    x2d = x[0]
    ctx2d = ctx[0]
    tgt = loss_target[0]
    T, D = x2d.shape
    L = ctx2d.shape[0]
    N = L + T
    nh = D // HEAD
    mx, my, mc = _me()
    me = 4 * mx + 2 * my + mc
    dloc = D // N_DEV

    lr = rw_w1.shape[-1]
    pad_r = LORA_PAD - lr
    w1p = jnp.pad(rw_w1[0], ((0, 0), (0, 0), (0, pad_r)))
    a1p = jnp.pad(rw_a1[0], ((0, 0), (0, 0), (0, pad_r)))
    w2p = jnp.pad(rw_w2[0], ((0, 0), (0, pad_r), (0, 0)))
    a2p = jnp.pad(rw_a2[0], ((0, 0), (0, pad_r), (0, 0)))
    small_loc = [rw_mix[0], rw_w0[0], rw_a0[0], sc_conv[0], w1p, a1p, w2p, a2p, rw_g1[0], rw_g2[0]]
    small_dim = [1, 1, 1, 1, 1, 1, 2, 2, 0, 1]
    small_shapes = [a.shape for a in small_loc]
    small_groups = ((slice(0, 4), F32, "vec"), (slice(4, 10), BF16, "mat"))
    small_full = []
    for sl, dt, tag in small_groups:
        sm_all = all_gather_stack(_pack(small_loc[sl], dt), "ag_small_" + tag)
        sm_parts = _unpack(sm_all, small_shapes[sl], lead=(N_DEV,))
        small_full += [_gather_dim(p, dm) for p, dm in zip(sm_parts, small_dim[sl])]
    mix_f, w0_f, a0_f, conv_f, w1_f, a1_f, w2_f, a2_f, g1_f, g2_f = small_full

    c_all = all_gather_stack(jnp.pad(c, ((0, 7), (0, 0))), "ag_c")[:, 0, :]
    cond_pre = jnp.concatenate([c_all, c_ctx[None, :], jnp.zeros((7, D), F32)], axis=0)
    cond_rows = jax.nn.silu(cond_pre)
    ncol = ada_w.shape[-1]
    mod_loc = []
    for i in range(2):
        bi = lax.dynamic_slice(ada_b[i], (me * ncol,), (ncol,))
        mod_loc.append(_mm(cond_rows, ada_w[i], out_dtype=F32, name=f"ada_fwd{i}") + bi[None, :])
    mod_all = all_gather_stack(jnp.concatenate(mod_loc, axis=0), "ag_mod")
    mod_full = _gather_lastdim(mod_all).reshape(2, 16, 6, D)
    mod_x = lax.dynamic_index_in_dim(mod_full, me, axis=1, keepdims=False)
    mod_c = mod_full[0, 8, :2, :]

    def ag_w(wl, axis, name):
        return all_gather(wl.astype(BF16), axis, name)

    shards = dict(wo=rw_wo[0], win=sc_win[0], wout=sc_wout[0], w13_0=ffn_w13[0], w13_1=ffn_w13[1],
                  w2_0=ffn_w2[0], w2_1=ffn_w2[1])
    shards = {k_: a.astype(BF16) for k_, a in shards.items()}
    sink = {}
    Wb = dict(
        wr=ag_w(rw_wr[0], 0, "ag_wr"), wk=ag_w(rw_wk[0], 0, "ag_wk"), wv=ag_w(rw_wv[0], 0, "ag_wv"),
        w1=jnp.concatenate([w1_f[0], w1_f[1]], axis=1).astype(BF16),
        a1=jnp.concatenate([a1_f[0], a1_f[1]], axis=1).astype(BF16),
        w2d=w2_f.astype(BF16), a2d=a2_f.astype(BF16),
        g1=g1_f.astype(BF16), g2=g2_f.astype(BF16),
    )
    Ps = dict(n1=norm1_g, n2=norm2_g, modx=mod_x, modc=mod_c, mix=mix_f, w0=w0_f, a0=a0_f, conv=conv_f,
              kk=rw_kk, ka=rw_ka, rk=rw_rk.reshape(1, D), lnw=rw_lnw, lnb=rw_lnb)

    fwd = _build_forward(ctx2d, T, D, shards, sink)
    (x3, f1), vjp_fn = jax.vjp(fwd, x2d, Ps, Wb)
    loss_acc, dx3, df1, dgate, dfinal = loss_head(x3, f1, tgt, mod_x[1, 5][None, :], final_g[None, :])
    dx, dPs, dWb = vjp_fn((dx3, df1))
    loss = lax.psum(loss_acc[0, 0], ("x", "y", "c"))

    dmodx = dPs["modx"].at[1, 5].add(dgate[0])
    dmodc = jnp.concatenate([dPs["modc"], jnp.zeros((4, D), F32)], axis=0)
    drow = jnp.stack([dmodx.reshape(2, 6 * D), jnp.stack([dmodc.reshape(6 * D), jnp.zeros((6 * D,), F32)])], axis=1)
    drow_all = all_gather_stack(drow.reshape(4, 6 * D), "ag_dmod").reshape(N_DEV, 2, 2, 6 * D)
    dctx_tot = drow_all[0, :, 1, :]
    for s in range(1, N_DEV):
        dctx_tot = dctx_tot + drow_all[s, :, 1, :]
    dmod_rows = jnp.concatenate([jnp.moveaxis(drow_all[:, :, 0, :], 0, 1), dctx_tot[:, None, :],
                                 jnp.zeros((2, 7, 6 * D), F32)], axis=1)
    grad_ada_b = dctx_tot
    for s in range(N_DEV):
        grad_ada_b = grad_ada_b + drow_all[s, :, 0, :]
    dmod_mine = lax.dynamic_slice_in_dim(dmod_rows, me * ncol, ncol, axis=2)
    g_ada_w = [_mm(cond_rows, dmod_mine[i], ta=True, out_dtype=F32, name=f"ada_dw{i}") for i in range(2)]
    dcond_part = _mm(dmod_mine[0], ada_w[0], tb=True, out_dtype=F32, name="ada_dcond")[8]

    rep_names = ["c_ctx", "norm1_g", "norm2_g", "rw_kk", "rw_ka", "rw_rk", "rw_lnw", "rw_lnb", "final_g"]
    rep_part = [dcond_part, dPs["n1"], dPs["n2"], dPs["kk"], dPs["ka"], dPs["rk"].reshape(W["rw_rk"].shape),
                dPs["lnw"], dPs["lnb"], dfinal[0]]
    rep_shapes = [W[n_].shape for n_ in rep_names]
    rep_all = all_gather_stack(_pack(rep_part), "ag_rep_grads")
    sg = jax.nn.sigmoid(c_ctx)
    dsilu = sg * (1.0 + c_ctx * (1.0 - sg))
    rep_scale = _pack([dsilu] + [jnp.ones(s, F32) for s in rep_shapes[1:]])
    rep_all = rep_all * rep_scale[None]
    rep_w = _pack([W[n_] for n_ in rep_names])
    rep_m = _pack([Mo[n_] for n_ in rep_names])
    rep_v = _pack([Vo[n_] for n_ in rep_names])
    rep_out = sum_adam(rep_all, rep_w, rep_m, rep_v, "adam_rep")
    results = {}
    for nm_, vals in zip(rep_names, zip(*[_unpack(o, rep_shapes) for o in rep_out])):
        results[nm_] = vals

    results["ada_b"] = tuple(sum_adam(grad_ada_b.reshape(1, 2 * 6, D), ada_b.reshape(12, D), m_ada_b.reshape(12, D),
                                      v_ada_b.reshape(12, D), "adam_ada_b"))
    results["ada_b"] = tuple(o.reshape(ada_b.shape) for o in results["ada_b"])

    outs = [sum_adam(g_ada_w[i][None], ada_w, m_ada_w, v_ada_w, f"adam_ada_w{i}", lead=i) for i in range(2)]
    results["ada_w"] = tuple(jnp.stack([outs[0][q], outs[1][q]]) for q in range(4))

    dw1 = jnp.stack([dWb["w1"][:, :LORA_PAD], dWb["w1"][:, LORA_PAD:]])
    da1 = jnp.stack([dWb["a1"][:, :LORA_PAD], dWb["a1"][:, LORA_PAD:]])
    small_g = [dPs["mix"], dPs["w0"], dPs["a0"], dPs["conv"], dw1, da1, dWb["w2d"], dWb["a2d"], dWb["g1"], dWb["g2"]]
    small_names = ["rw_mix", "rw_w0", "rw_a0", "sc_conv", "rw_w1", "rw_a1", "rw_w2", "rw_a2", "rw_g1", "rw_g2"]

    def padded_local(nm_, src):
        a = src[nm_][0]
        if nm_ in ("rw_w1", "rw_a1"):
            return jnp.pad(a, ((0, 0), (0, 0), (0, pad_r)))
        if nm_ in ("rw_w2", "rw_a2"):
            return jnp.pad(a, ((0, 0), (0, pad_r), (0, 0)))
        return a

    for sl, dt, tag in small_groups:
        blocks = [_scatter_dim(gf, dm) for gf, dm in zip(small_g[sl], small_dim[sl])]
        sm_recv = reduce_scatter_exchange(_pack(blocks, dt, lead=1), None, "rs_small_" + tag)
        sm_out = sum_adam(sm_recv, _pack([padded_local(n_, W) for n_ in small_names[sl]]),
                          _pack([padded_local(n_, Mo) for n_ in small_names[sl]]),
                          _pack([padded_local(n_, Vo) for n_ in small_names[sl]]), "adam_small_" + tag)
        for nm_, vals in zip(small_names[sl], zip(*[_unpack(o, small_shapes[sl]) for o in sm_out])):
            if nm_ in ("rw_w1", "rw_a1"):
                vals = tuple(a[:, :, :lr] for a in vals)
            if nm_ in ("rw_w2", "rw_a2"):
                vals = tuple(a[:, :lr, :] for a in vals)
            results[nm_] = tuple(a[None] for a in vals)

    def rs_adam(key, nm_, lead):
        recv = sink[key] if key in sink else reduce_scatter_exchange(dWb[key], _W_AXIS[key], "rs_" + key)
        return sum_adam(recv, W[nm_], Mo[nm_], Vo[nm_], "adam_" + key, lead=lead)

    for nm_, key in (("rw_wr", "wr"), ("rw_wk", "wk"), ("rw_wv", "wv"), ("rw_wo", "wo"), ("sc_win", "win"),
                     ("sc_wout", "wout")):
        results[nm_] = tuple(a[None] for a in rs_adam(key, nm_, 0))
    for nm_, key in (("ffn_w13", "w13"), ("ffn_w2", "w2")):
        outs = [rs_adam(f"{key}_{i}", nm_, i) for i in range(2)]
        results[nm_] = tuple(jnp.stack([outs[0][q], outs[1][q]]) for q in range(4))

    grads = [results[n_][0] for n_ in names]
    deltas = [results[n_][1] for n_ in names]
    new_m = [results[n_][2] for n_ in names]
    new_v = [results[n_][3] for n_ in names]
    return (loss, dx[None], *grads, *deltas, *new_m, *new_v)
```

```python
import functools
import math

import numpy as np
import jax
import jax.numpy as jnp
from jax import lax
from jax.experimental import pallas as pl
from jax.experimental.pallas import tpu as pltpu

F32 = jnp.float32
BF16 = jnp.bfloat16

N_DEV = 8
HEAD = 64
LANES = 128
GRID_W = 64
LORA_PAD = 128
NORM_EPS = 1e-6
GN_EPS = 64e-5
ADAM_LR, ADAM_B1, ADAM_B2, ADAM_EPS, ADAM_WD, ADAM_STEP = 0.001, 0.9, 0.999, 1e-08, 0.01, 10
VMEM_LIMIT = 52 * 1024 * 1024
SCAN_CHUNK = 8
HI = lax.Precision.HIGHEST


def _cparams(sem):
    return pltpu.CompilerParams(dimension_semantics=sem, vmem_limit_bytes=VMEM_LIMIT)


def _pick(n, cap, quantum=LANES):
    best = None
    for t in range(quantum, min(n, cap) + 1, quantum):
        if n % t == 0:
            best = t
    return n if best is None else best


MM_VMEM_BUDGET = 36 * 1024 * 1024


def _divisors(n, cap, quantum=LANES):
    ds = [t for t in range(quantum, min(n, cap) + 1, quantum) if n % t == 0]
    return sorted(ds, reverse=True) or [n]


def _mm_tiles(M, N, K, sa, sb, so):
    tms, tns, tks = _divisors(M, 1024), _divisors(N, 1024), _divisors(K, 2816)
    im = jn = ik = 0

    def est(tm, tn, tk):
        b = 2 * (tm * tk * sa + tk * tn * sb) + 2 * tm * tn * so + tm * tn * 4
        b += tm * tk * 2 if sa == 4 else 0
        b += tk * tn * 2 if sb == 4 else 0
        return b + (tm * tn * 4 if tk < K else 0)

    while est(tms[im], tns[jn], tks[ik]) > MM_VMEM_BUDGET:
        if tms[im] >= tns[jn] and im + 1 < len(tms):
            im += 1
        elif jn + 1 < len(tns):
            jn += 1
        elif im + 1 < len(tms):
            im += 1
        elif ik + 1 < len(tks):
            ik += 1
        else:
            break
    return tms[im], tns[jn], tks[ik]


def _mm(a, b, *, ta=False, tb=False, out_dtype, name):
    if ta:
        K, M = a.shape
    else:
        M, K = a.shape
    if tb:
        N, Kb = b.shape
    else:
        Kb, N = b.shape
    assert K == Kb, (a.shape, b.shape, ta, tb)
    tm, tn, tk = _mm_tiles(M, N, K, a.dtype.itemsize, b.dtype.itemsize, jnp.dtype(out_dtype).itemsize)
    nk = K // tk
    dims = (((0 if ta else 1,), (1 if tb else 0,)), ((), ()))

    def body(a_ref, b_ref, o_ref, *acc):
        part = lax.dot_general(a_ref[...].astype(BF16), b_ref[...].astype(BF16), dims, preferred_element_type=F32)
        if nk == 1:
            o_ref[...] = part.astype(o_ref.dtype)
            return
        acc_ref, = acc
        k = pl.program_id(2)

        @pl.when(k == 0)
        def _():
            acc_ref[...] = part

        @pl.when(k > 0)
        def _():
            acc_ref[...] += part

        @pl.when(k == nk - 1)
        def _():
            o_ref[...] = acc_ref[...].astype(o_ref.dtype)

    a_spec = pl.BlockSpec((tk, tm), lambda i, j, k: (k, i)) if ta else pl.BlockSpec((tm, tk), lambda i, j, k: (i, k))
    b_spec = pl.BlockSpec((tn, tk), lambda i, j, k: (j, k)) if tb else pl.BlockSpec((tk, tn), lambda i, j, k: (k, j))
    return pl.pallas_call(
        body, name=name, grid=(M // tm, N // tn, nk),
        in_specs=[a_spec, b_spec],
        out_specs=pl.BlockSpec((tm, tn), lambda i, j, k: (i, j)),
        out_shape=jax.ShapeDtypeStruct((M, N), out_dtype),
        scratch_shapes=[pltpu.VMEM((tm, tn), F32)] if nk > 1 else [],
        compiler_params=_cparams(("parallel", "parallel", "arbitrary")),
    )(a, b)


@functools.partial(jax.custom_vjp, nondiff_argnums=(2, 3))
def linear(a, w, out_dtype, name):
    return _mm(a, w, out_dtype=out_dtype, name=name + "_fwd")


def _linear_fwd(a, w, out_dtype, name):
    return _mm(a, w, out_dtype=out_dtype, name=name + "_fwd"), (a, w)


def _linear_bwd(out_dtype, name, res, g):
    a, w = res
    da = _mm(g, w, tb=True, out_dtype=a.dtype, name=name + "_da")
    dw = _mm(a, g, ta=True, out_dtype=w.dtype, name=name + "_dw")
    return da, dw


linear.defvjp(_linear_fwd, _linear_bwd)


def _rw_specs(tiles, col_offs, vecs, consts, tr, tc, nb0):
    tile_specs = [pl.BlockSpec((tr, tc), functools.partial(lambda j, i, off: (i, j + off), off=off))
                  for _, off in zip(tiles, col_offs)]

    def vec_map(S):
        if S == 1:
            return lambda j, i: (0, 0, j)
        return lambda j, i: (jnp.where(i < nb0, 0, 1), 0, j)

    vec_specs = [pl.BlockSpec((None, 1, tc), vec_map(v.shape[0])) for v in vecs]
    const_specs = [pl.BlockSpec(c.shape, lambda j, i: (0, 0)) for c in consts]
    return tile_specs, vec_specs, const_specs


def _rw_forward(name, f, tiles, col_offs, vecs, consts, out_dtypes, tr, tc, nb0, width):
    n = tiles[0].shape[0]
    nt, nv, nc = len(tiles), len(vecs), len(consts)
    tile_specs, vec_specs, const_specs = _rw_specs(tiles, col_offs, vecs, consts, tr, tc, nb0)

    def body(*refs):
        ins = [r[...].astype(F32) for r in refs[:nt]] + [r[...] for r in refs[nt:nt + nv + nc]]
        outs = f(*ins)
        for o_ref, o in zip(refs[nt + nv + nc:], outs):
            o_ref[...] = o.astype(o_ref.dtype)

    return pl.pallas_call(
        body, name=name + "_fwd", grid=(width // tc, n // tr),
        in_specs=tile_specs + vec_specs + const_specs,
        out_specs=[pl.BlockSpec((tr, tc), lambda j, i: (i, j)) for _ in out_dtypes],
        out_shape=[jax.ShapeDtypeStruct((n, width), dt) for dt in out_dtypes],
        compiler_params=_cparams(("parallel", "parallel")),
    )(*tiles, *vecs, *consts)


def _rw_backward(name, f, tiles, col_offs, vecs, consts, douts, tr, tc, nb0, width):
    n = tiles[0].shape[0]
    nt, nv, nc, no = len(tiles), len(vecs), len(consts), len(douts)
    tile_specs, vec_specs, const_specs = _rw_specs(tiles, col_offs, vecs, consts, tr, tc, nb0)

    def body(*refs):
        t_in = [r[...].astype(F32) for r in refs[:nt]]
        v_in = [r[...] for r in refs[nt:nt + nv]]
        c_in = [r[...] for r in refs[nt + nv:nt + nv + nc]]
        d_in = tuple(r[...].astype(F32) for r in refs[nt + nv + nc:nt + nv + nc + no])
        o_refs = refs[nt + nv + nc + no:]
        _, vjp = jax.vjp(lambda *tv: tuple(f(*tv, *c_in)), *t_in, *v_in)
        grads = vjp(d_in)
        for o_ref, g in zip(o_refs[:nt], grads[:nt]):
            o_ref[...] = g.astype(o_ref.dtype)
        i = pl.program_id(1)
        for o_ref, g, v in zip(o_refs[nt:], grads[nt:], vecs):
            first = jnp.logical_or(i == 0, i == nb0) if v.shape[0] == 2 else i == 0

            @pl.when(first)
            def _(o_ref=o_ref, g=g):
                o_ref[...] = g

            @pl.when(jnp.logical_not(first))
            def _(o_ref=o_ref, g=g):
                o_ref[...] += g

    dout_specs = [pl.BlockSpec((tr, tc), lambda j, i: (i, j)) for _ in douts]
    out_specs = [pl.BlockSpec((tr, tc), lambda j, i: (i, j)) for _ in tiles] + list(vec_specs)
    out_shape = ([jax.ShapeDtypeStruct((n, width), t.dtype) for t in tiles]
                 + [jax.ShapeDtypeStruct(v.shape, F32) for v in vecs])
    return pl.pallas_call(
        body, name=name + "_bwd", grid=(width // tc, n // tr),
        in_specs=tile_specs + vec_specs + const_specs + dout_specs,
        out_specs=out_specs, out_shape=out_shape,
        compiler_params=_cparams(("parallel", "arbitrary")),
    )(*tiles, *vecs, *consts, *douts)


def make_rowwise(name, f, out_dtypes, tr, tc, consts=(), nb0=-1):
    consts = tuple(consts)

    @jax.custom_vjp
    def op(tiles, vecs):
        w = tiles[0].shape[1]
        return tuple(_rw_forward(name, f, tiles, (0,) * len(tiles), vecs, consts, out_dtypes, tr, min(tc, w), nb0, w))

    def op_fwd(tiles, vecs):
        return op(tiles, vecs), (tiles, vecs)

    def op_bwd(res, douts):
        tiles, vecs = res
        w = tiles[0].shape[1]
        g = _rw_backward(name, f, tiles, (0,) * len(tiles), vecs, consts, tuple(douts), tr, min(tc, w), nb0, w)
        return tuple(g[:len(tiles)]), tuple(g[len(tiles):])

    op.defvjp(op_fwd, op_bwd)
    return op


def _f_norm_mod(x, g, sh, sc):
    hn = x * lax.rsqrt(jnp.mean(x * x, axis=-1, keepdims=True) + NORM_EPS)
    return ((hn * g) * (1.0 + sc) + sh,)


def _f_res_norm_mod(x, y, gate, g, sh, sc):
    x1 = x + gate * y
    hn = x1 * lax.rsqrt(jnp.mean(x1 * x1, axis=-1, keepdims=True) + NORM_EPS)
    return x1, (hn * g) * (1.0 + sc) + sh


def _head_sum_3pass(t, gmat):
    hi = t.astype(BF16)
    r1 = t - hi.astype(F32)
    mid = r1.astype(BF16)
    lo = (r1 - mid.astype(F32)).astype(BF16)
    g = gmat.astype(BF16)
    dot = lambda u: jnp.dot(u, g, preferred_element_type=F32)
    return dot(hi) + dot(mid) + dot(lo)


@jax.custom_vjp
def _head_sum(t, gmat):
    return _head_sum_3pass(t, gmat)


def _head_sum_fwd(t, gmat):
    return _head_sum_3pass(t, gmat), gmat


def _head_sum_bwd(gmat, ct):
    return _head_sum_3pass(ct, gmat), None


_head_sum.defvjp(_head_sum_fwd, _head_sum_bwd)


def _f_prep(k, lw0, lw1, la0, la1, kkp, kap, w00, w01, a00, a01, gmat):
    t = k * kkp
    kk = t / jnp.maximum(jnp.sqrt(_head_sum(t * t, gmat)), 1e-12)
    outs = [kk]
    decs, kds, sigs = [], [], []
    for lw, la, w0, a0 in ((lw0, la0, w00, a00), (lw1, la1, w01, a01)):
        decs.append(jnp.exp(-jax.nn.sigmoid(w0 + lw) * float(np.exp(-0.5))))
        a = jax.nn.sigmoid(a0 + la)
        sigs.append(a)
        kds.append(k * (1.0 + (a - 1.0) * kap))
    return tuple(outs + decs + kds + sigs)


def _f_readout(y0, y1, r, kd0, kd1, v, g, rk, lnw, lnb, gmat):
    y = y0 + y1
    mu = _head_sum(y, gmat) * (1.0 / HEAD)
    d = y - mu
    var = _head_sum(d * d, gmat) * (1.0 / HEAD)
    o = d * lax.rsqrt(var + GN_EPS) * lnw + lnb
    bonus = _head_sum(r * (kd0 + kd1) * rk, gmat) * v
    return ((o + bonus) * g,)


def _f_swiglu(a, b):
    return (jax.nn.silu(a) * b,)


def swiglu_act(ab, name):
    t, f2 = ab.shape
    fdim = f2 // 2
    tr, tc = _pick(t, 512, 8), _pick(fdim, 512)
    offs = (0, fdim // tc)

    @jax.custom_vjp
    def op(ab_):
        return _rw_forward(name, _f_swiglu, (ab_, ab_), offs, (), (), (BF16,), tr, tc, -1, fdim)[0]

    def op_fwd(ab_):
        return op(ab_), ab_

    def op_bwd(ab_, dact):
        da, db = _rw_backward(name, _f_swiglu, (ab_, ab_), offs, (), (), (dact,), tr, tc, -1, fdim)
        return (jnp.concatenate([da, db], axis=1),)

    op.defvjp(op_fwd, op_bwd)
    return op(ab)


def _row_iota(n, tc):
    return lax.broadcasted_iota(jnp.int32, (n, tc), 0)


def _shift_rows(x, s, keep):
    n = x.shape[0]
    return jnp.where(keep, pltpu.roll(x, s % n, 0), 0.0)


def _unshift_rows(d, s, keep):
    n = d.shape[0]
    return pltpu.roll(jnp.where(keep, d, 0.0), (-s) % n, 0)


def _ctx_shift_spec(L, tc, quarter):
    row = _row_iota(L, tc)
    if quarter < 2:
        return 1, row >= 1
    return -1, row < L - 1


def _grid_shift_spec(T, tc, quarter):
    row = _row_iota(T, tc)
    col = jnp.bitwise_and(row, GRID_W - 1)
    if quarter == 0:
        return 1, col != 0
    if quarter == 1:
        return -1, col != GRID_W - 1
    if quarter == 2:
        return GRID_W, row >= GRID_W
    return -GRID_W, row < T - GRID_W


def _shift_mix_fwd_call(h, mix3, L):
    n, d = h.shape
    T = n - L
    tc = _pick(d // 4, 256)
    nq = (d // 4) // tc

    def body(h_ref, mix_ref, *o_refs):
        q = pl.program_id(0) // nq
        for quarter in range(4):
            @pl.when(q == quarter)
            def _(quarter=quarter):
                for lo, cnt, spec in ((0, L, _ctx_shift_spec), (L, T, _grid_shift_spec)):
                    hh = h_ref[pl.ds(lo, cnt), :]
                    s, keep = spec(cnt, tc, quarter)
                    xx = _shift_rows(hh, s, keep) - hh
                    for m in range(6):
                        o_refs[m][pl.ds(lo, cnt), :] = (hh + xx * mix_ref[m]).astype(BF16)

    return pl.pallas_call(
        body, name="shift_mix_fwd", grid=(d // tc,),
        in_specs=[pl.BlockSpec((n, tc), lambda j: (0, j)), pl.BlockSpec((6, 1, tc), lambda j: (0, 0, j))],
        out_specs=[pl.BlockSpec((n, tc), lambda j: (0, j)) for _ in range(6)],
        out_shape=[jax.ShapeDtypeStruct((n, d), BF16) for _ in range(6)],
        compiler_params=_cparams(("parallel",)),
    )(h, mix3)


def _shift_mix_bwd_call(h, mix3, douts, L):
    n, d = h.shape
    T = n - L
    tc = _pick(d // 4, 256)
    nq = (d // 4) // tc

    def body(h_ref, mix_ref, d0, d1, d2, d3, d4, d5, dh_ref, dmix_ref):
        d_refs = (d0, d1, d2, d3, d4, d5)
        q = pl.program_id(0) // nq
        for quarter in range(4):
            @pl.when(q == quarter)
            def _(quarter=quarter):
                dmix = [jnp.zeros((1, tc), F32) for _ in range(6)]
                for lo, cnt, spec in ((0, L, _ctx_shift_spec), (L, T, _grid_shift_spec)):
                    hh = h_ref[pl.ds(lo, cnt), :]
                    s, keep = spec(cnt, tc, quarter)
                    xx = _shift_rows(hh, s, keep) - hh
                    direct = jnp.zeros((cnt, tc), F32)
                    shifted = jnp.zeros((cnt, tc), F32)
                    for m in range(6):
                        dm = d_refs[m][pl.ds(lo, cnt), :].astype(F32)
                        mx = mix_ref[m]
                        direct = direct + dm * (1.0 - mx)
                        shifted = shifted + dm * mx
                        dmix[m] = dmix[m] + jnp.sum(dm * xx, axis=0, keepdims=True)
                    dh_ref[pl.ds(lo, cnt), :] = direct + _unshift_rows(shifted, s, keep)
                for m in range(6):
                    dmix_ref[m] = dmix[m]

    tile = pl.BlockSpec((n, tc), lambda j: (0, j))
    return pl.pallas_call(
        body, name="shift_mix_bwd", grid=(d // tc,),
        in_specs=[tile, pl.BlockSpec((6, 1, tc), lambda j: (0, 0, j))] + [tile] * 6,
        out_specs=[tile, pl.BlockSpec((6, 1, tc), lambda j: (0, 0, j))],
        out_shape=[jax.ShapeDtypeStruct((n, d), F32), jax.ShapeDtypeStruct((6, 1, d), F32)],
        compiler_params=_cparams(("parallel",)),
    )(h, mix3, *douts)


@functools.partial(jax.custom_vjp, nondiff_argnums=(2,))
def shift_mix(h, mix3, L):
    return tuple(_shift_mix_fwd_call(h, mix3, L))


def _shift_mix_fwd(h, mix3, L):
    return tuple(_shift_mix_fwd_call(h, mix3, L)), (h, mix3)


def _shift_mix_bwd(L, res, douts):
    h, mix3 = res
    dh, dmix = _shift_mix_bwd_call(h, mix3, tuple(douts), L)
    return dh, dmix


shift_mix.defvjp(_shift_mix_fwd, _shift_mix_bwd)


def _conv_specs(T, d, tc):
    nd = d // tc
    ins = [pl.BlockSpec((T, tc), functools.partial(lambda j, off: (0, j + off), off=o * nd)) for o in range(3)]
    return ins, pl.BlockSpec((3, 1, tc), lambda j: (0, 0, j))


def _conv_terms(gc, u, tc):
    T = gc.shape[0]
    row = _row_iota(T, tc)
    z = gc * u
    return z, _shift_rows(z, 1, row >= 1), _shift_rows(z, -1, row < T - 1), row


def _conv_fwd_call(guc, cw3):
    T, d3 = guc.shape
    d = d3 // 3
    tc = _pick(d, 256)
    ins, wspec = _conv_specs(T, d, tc)

    def body(gb_ref, gc_ref, u_ref, w_ref, p_ref):
        z, zp, zn, _ = _conv_terms(gc_ref[...].astype(F32), u_ref[...].astype(F32), tc)
        conv = zp * w_ref[0] + z * w_ref[1] + zn * w_ref[2]
        p_ref[...] = (gb_ref[...].astype(F32) * conv).astype(BF16)

    return pl.pallas_call(
        body, name="conv_fwd", grid=(d // tc,), in_specs=ins + [wspec],
        out_specs=pl.BlockSpec((T, tc), lambda j: (0, j)),
        out_shape=jax.ShapeDtypeStruct((T, d), BF16),
        compiler_params=_cparams(("parallel",)),
    )(guc, guc, guc, cw3)


def _conv_bwd_call(guc, cw3, dp):
    T, d3 = guc.shape
    d = d3 // 3
    tc = _pick(d, 256)
    ins, wspec = _conv_specs(T, d, tc)
    tile = pl.BlockSpec((T, tc), lambda j: (0, j))

    def body(gb_ref, gc_ref, u_ref, w_ref, dp_ref, dgb_ref, dgc_ref, du_ref, dw_ref):
        gc = gc_ref[...].astype(F32)
        u = u_ref[...].astype(F32)
        z, zp, zn, row = _conv_terms(gc, u, tc)
        conv = zp * w_ref[0] + z * w_ref[1] + zn * w_ref[2]
        dpv = dp_ref[...].astype(F32)
        dgb_ref[...] = (dpv * conv).astype(dgb_ref.dtype)
        dconv = dpv * gb_ref[...].astype(F32)
        dz = (_shift_rows(dconv, -1, row < T - 1) * w_ref[0] + dconv * w_ref[1]
              + _shift_rows(dconv, 1, row >= 1) * w_ref[2])
        dgc_ref[...] = (dz * u).astype(dgc_ref.dtype)
        du_ref[...] = (dz * gc).astype(du_ref.dtype)
        dw_ref[0] = jnp.sum(dconv * zp, axis=0, keepdims=True)
        dw_ref[1] = jnp.sum(dconv * z, axis=0, keepdims=True)
        dw_ref[2] = jnp.sum(dconv * zn, axis=0, keepdims=True)

    return pl.pallas_call(
        body, name="conv_bwd", grid=(d // tc,), in_specs=ins + [wspec, tile],
        out_specs=[tile, tile, tile, wspec],
        out_shape=[jax.ShapeDtypeStruct((T, d), guc.dtype)] * 3 + [jax.ShapeDtypeStruct((3, 1, d), F32)],
        compiler_params=_cparams(("parallel",)),
    )(guc, guc, guc, cw3, dp)


@jax.custom_vjp
def gated_conv(guc, cw3):
    return _conv_fwd_call(guc, cw3)


def _gated_conv_fwd(guc, cw3):
    return _conv_fwd_call(guc, cw3), (guc, cw3)


def _gated_conv_bwd(res, dp):
    guc, cw3 = res
    dgb, dgc, du, dw = _conv_bwd_call(guc, cw3, dp)
    return jnp.concatenate([dgb, dgc, du], axis=1), dw


gated_conv.defvjp(_gated_conv_fwd, _gated_conv_bwd)


def _chunk_map(nchunk, nctx_chunk, reverse):
    if not reverse:
        return lambda c: c
    return lambda c: jnp.where(c < nctx_chunk, nctx_chunk - 1 - c, nchunk - 1 - (c - nctx_chunk))


def _spread(row_ref, dst_scr, lo_mask, ni, C):
    for i in range(ni):
        idx = jnp.where(lo_mask, 2 * i, HEAD + 1 + 2 * i).astype(jnp.int32)
        for tt in range(C):
            dst_scr[tt, i] = jnp.take_along_axis(row_ref[tt], idx, axis=1)


def _spread_bf16(row_ref, dst_scr, ni, C):
    packed = [pltpu.bitcast(row_ref[tt].astype(BF16), jnp.int32) for tt in range(C)]
    lane = lax.broadcasted_iota(jnp.int32, packed[0].shape, 1)
    for i in range(ni):
        idx = jnp.where(lane < HEAD, 2 * i, HEAD + 1 + 2 * i).astype(jnp.int32)
        for tt in range(C):
            got = jnp.take_along_axis(packed[tt], idx, axis=1)
            dst_scr[tt, i] = pltpu.bitcast(got, BF16).astype(F32)


def _half_sums(p, lo_mask):
    lo = jnp.sum(jnp.where(lo_mask, p, 0.0), axis=1, keepdims=True)
    hi = jnp.sum(jnp.where(lo_mask, 0.0, p), axis=1, keepdims=True)
    return jnp.where(lo_mask, lo, hi)


def _half_sums_mxu(ps, gmat):
    p = jnp.concatenate(ps, axis=0)
    hi = p.astype(BF16)
    lo = (p - hi.astype(F32)).astype(BF16)
    s = jnp.dot(hi, gmat, preferred_element_type=F32) + jnp.dot(lo, gmat, preferred_element_type=F32)
    nh = ps[0].shape[0]
    return [s[i * nh:(i + 1) * nh] for i in range(len(ps))]


def _split_row(sums, lane, nh):
    acc = jnp.zeros((nh, LANES), F32)
    for i, s in enumerate(sums):
        acc = acc + jnp.where(jnp.logical_or(lane == 2 * i, lane == HEAD + 1 + 2 * i), s, 0.0)
    return acc


def _wkv_fwd_call(r2, w2, kd2, kk2, as2, v2, nctx, reverse, xchg_arrs=(), xchg_specs=()):
    n, nh, _ = r2.shape
    C = SCAN_CHUNK
    ni = HEAD // 2
    nchunk = n // C
    cmap = _chunk_map(nchunk, nctx // C, reverse)
    nx = len(xchg_arrs)

    def body(*refs):
        g_ref, refs = refs[0], refs[1:]
        r_ref, w_ref, kd_ref, kk_ref, as_ref, v_ref = refs[:6]
        x_in = refs[6:6 + nx]
        y_ref, sa_ref, sp_ref = refs[6 + nx:9 + nx]
        x_out = refs[9 + nx:9 + 2 * nx]
        s_scr, vc_scr = refs[9 + 2 * nx:11 + 2 * nx]
        if nx:
            _fused_exchanges(list(zip(x_in, x_out)), xchg_specs, *refs[11 + 2 * nx:], first=pl.program_id(0) == 0)

        @pl.when(pl.program_id(0) == 0)
        def _():
            s_scr[...] = jnp.zeros_like(s_scr)

        lane = lax.broadcasted_iota(jnp.int32, (nh, LANES), 1)
        lo_mask = lane < HEAD
        _spread(v_ref, vc_scr, lo_mask, ni, C)

        def make_step(with_y):
            def step(j, carry):
                t = (C - 1 - j) if reverse else j
                kk = kk_ref[t]
                a2 = -kk
                b2 = kk * as_ref[t]
                w = w_ref[t]
                k = kd_ref[t]
                r = r_ref[t]
                sas = []
                for i in range(ni):
                    si = s_scr[i]
                    sp_ref[t, i] = si
                    sas.append(_half_sums(si * a2, lo_mask))
                qs = []
                for i in range(ni):
                    sn = s_scr[i] * w + sas[i] * b2 + vc_scr[t, i] * k
                    s_scr[i] = sn
                    if with_y:
                        qs.append(sn * r)
                if with_y:
                    y_ref[t] = _split_row(_half_sums_mxu(qs, g_ref[...]), lane, nh)
                else:
                    y_ref[t] = jnp.zeros((nh, LANES), F32)
                sa_ref[t] = _split_row(sas, lane, nh)
                return carry
            return step

        is_ctx = pl.program_id(0) < nctx // C

        @pl.when(is_ctx)
        def _():
            lax.fori_loop(0, C, make_step(False), 0)

        @pl.when(jnp.logical_not(is_ctx))
        def _():
            lax.fori_loop(0, C, make_step(True), 0)

        if nx:
            _fused_exchanges(list(zip(x_in, x_out)), xchg_specs, *refs[11 + 2 * nx:],
                             last=pl.program_id(0) == nchunk - 1)

    tok = pl.BlockSpec((C, nh, LANES), lambda c: (cmap(c), 0, 0))
    return pl.pallas_call(
        body, name="wkv_fwd_rev" if reverse else "wkv_fwd", grid=(nchunk,),
        in_specs=[pl.BlockSpec((LANES, LANES), lambda c: (0, 0))] + [tok] * 6 + [_ANY] * nx,
        out_specs=[tok, tok, pl.BlockSpec((C, ni, nh, LANES), lambda c: (cmap(c), 0, 0, 0))] + [_ANY] * nx,
        out_shape=[jax.ShapeDtypeStruct((n, nh, LANES), F32), jax.ShapeDtypeStruct((n, nh, LANES), F32),
                   jax.ShapeDtypeStruct((n, ni, nh, LANES), F32)] + _exchange_out_shapes(xchg_arrs, xchg_specs),
        scratch_shapes=[pltpu.VMEM((ni, nh, LANES), F32), pltpu.VMEM((C, ni, nh, LANES), F32)]
        + (_exchange_sems(nx) if nx else []),
        compiler_params=_cparams(("arbitrary",)),
    )(_head_group_matrix(LANES).astype(BF16), r2, w2, kd2, kk2, as2, v2, *xchg_arrs)


def _wkv_bwd_call(r2, w2, kd2, kk2, as2, v2, sa, sprev, dy, nctx, reverse, xchg_arrs=(), xchg_specs=()):
    n, nh, _ = r2.shape
    C = SCAN_CHUNK
    ni = HEAD // 2
    nchunk = n // C
    fmap = _chunk_map(nchunk, nctx // C, reverse)
    cmap = lambda c: fmap(nchunk - 1 - c)
    nx = len(xchg_arrs)

    def body(*refs):
        g_ref, refs = refs[0], refs[1:]
        r_ref, w_ref, kd_ref, kk_ref, as_ref, v_ref, sa_ref, sp_ref, dy_ref = refs[:9]
        x_in = refs[9:9 + nx]
        dr_ref, dw_ref, dkd_ref, dkk_ref, das_ref, dv_ref = refs[9 + nx:15 + nx]
        x_out = refs[15 + nx:15 + 2 * nx]
        ds_scr, vc_scr, sac_scr, dyc_scr = refs[15 + 2 * nx:19 + 2 * nx]
        if nx:
            _fused_exchanges(list(zip(x_in, x_out)), xchg_specs, *refs[19 + 2 * nx:], first=pl.program_id(0) == 0)

        @pl.when(pl.program_id(0) == 0)
        def _():
            ds_scr[...] = jnp.zeros_like(ds_scr)

        lane = lax.broadcasted_iota(jnp.int32, (nh, LANES), 1)
        lo_mask = lane < HEAD
        _spread_bf16(v_ref, vc_scr, ni, C)
        _spread_bf16(sa_ref, sac_scr, ni, C)
        is_ctx = pl.program_id(0) >= nchunk - nctx // C

        @pl.when(jnp.logical_not(is_ctx))
        def _():
            _spread(dy_ref, dyc_scr, lo_mask, ni, C)

        def make_step(with_dy):
            def step(j, carry):
                t = j if reverse else (C - 1 - j)
                kk = kk_ref[t]
                sig = as_ref[t]
                a2 = -kk
                b2 = kk * sig
                w = w_ref[t]
                k = kd_ref[t]
                r = r_ref[t]
                zero = jnp.zeros((nh, LANES), F32)
                acc_dk, acc_db, acc_dw, acc_g, acc_sady, acc_vdy, acc_da = zero, zero, zero, zero, zero, zero, zero
                dvp, dsas = [], []
                for i in range(ni):
                    sp = sp_ref[t, i]
                    vc = vc_scr[t, i]
                    sac = sac_scr[t, i]
                    ds = ds_scr[i]
                    if with_dy:
                        dyc = dyc_scr[t, i]
                        ds = ds + dyc * r
                        ds_scr[i] = ds
                        acc_g = acc_g + sp * dyc
                    dvp.append(ds * k)
                    dsas.append(_half_sums(ds * b2, lo_mask))
                    acc_dk = acc_dk + ds * vc
                    acc_db = acc_db + ds * sac
                    acc_dw = acc_dw + ds * sp
                for i in range(ni):
                    acc_da = acc_da + sp_ref[t, i] * dsas[i]
                    ds_scr[i] = ds_scr[i] * w + dsas[i] * a2
                if with_dy:
                    dyr = jnp.where(jnp.bitwise_and(lane, 1) == (lane >= HEAD).astype(jnp.int32), dy_ref[t], 0.0)
                    acc_sady = _half_sums(sa_ref[t] * dyr, lo_mask)
                    acc_vdy = _half_sums(v_ref[t] * dyr, lo_mask)
                dr_ref[t] = acc_g * w + b2 * acc_sady + k * acc_vdy
                dw_ref[t] = acc_dw
                dkd_ref[t] = acc_dk
                dkk_ref[t] = acc_db * sig - acc_da
                das_ref[t] = acc_db * kk
                dv_ref[t] = _split_row(_half_sums_mxu(dvp, g_ref[...]), lane, nh)
                return carry
            return step

        @pl.when(is_ctx)
        def _():
            lax.fori_loop(0, C, make_step(False), 0)

        @pl.when(jnp.logical_not(is_ctx))
        def _():
            lax.fori_loop(0, C, make_step(True), 0)

        if nx:
            _fused_exchanges(list(zip(x_in, x_out)), xchg_specs, *refs[19 + 2 * nx:],
                             last=pl.program_id(0) == nchunk - 1)

    tok = pl.BlockSpec((C, nh, LANES), lambda c: (cmap(c), 0, 0))
    big = pltpu.VMEM((C, ni, nh, LANES), F32)
    return pl.pallas_call(
        body, name="wkv_bwd_rev" if reverse else "wkv_bwd", grid=(nchunk,),
        in_specs=[pl.BlockSpec((LANES, LANES), lambda c: (0, 0))] + [tok] * 7
        + [pl.BlockSpec((C, ni, nh, LANES), lambda c: (cmap(c), 0, 0, 0)), tok] + [_ANY] * nx,
        out_specs=[tok] * 6 + [_ANY] * nx,
        out_shape=[jax.ShapeDtypeStruct((n, nh, LANES), F32)] * 6 + _exchange_out_shapes(xchg_arrs, xchg_specs),
        scratch_shapes=[pltpu.VMEM((ni, nh, LANES), F32), big, big, big] + (_exchange_sems(nx) if nx else []),
        compiler_params=_cparams(("arbitrary",)),
    )(_head_group_matrix(LANES).astype(BF16), r2, w2, kd2, kk2, as2, v2, sa, sprev, dy, *xchg_arrs)


def _tile_heads(t):
    n, d = t.shape
    th = t.reshape(n, d // HEAD, HEAD)
    return jnp.concatenate([th, th], axis=-1)


def loss_head(x3, fo, tgt, gate, g):
    T, d = x3.shape
    tr = _pick(T, 128, 8)

    def body(x_ref, f_ref, t_ref, gate_ref, g_ref, loss_ref, dx_ref, df_ref, dgate_ref, dg_ref):
        tg = t_ref[...]

        def fl(x, fo_, gate_, g_):
            x4 = x + gate_ * fo_
            y = (x4 * lax.rsqrt(jnp.mean(x4 * x4, axis=-1, keepdims=True) + NORM_EPS)) * g_
            return 0.5 * jnp.sum(jnp.mean(jnp.square(y - tg), axis=-1))

        val, vjp = jax.vjp(fl, x_ref[...], f_ref[...], gate_ref[...], g_ref[...])
        dx, dfo, dgate, dg = vjp(jnp.ones((), F32))
        dx_ref[...] = dx
        df_ref[...] = dfo
        i = pl.program_id(0)

        @pl.when(i == 0)
        def _():
            loss_ref[...] = jnp.zeros_like(loss_ref)
            dgate_ref[...] = jnp.zeros_like(dgate_ref)
            dg_ref[...] = jnp.zeros_like(dg_ref)

        loss_ref[...] += jnp.full(loss_ref.shape, val, F32)
        dgate_ref[...] += dgate
        dg_ref[...] += dg

    tile = pl.BlockSpec((tr, d), lambda i: (i, 0))
    vec = pl.BlockSpec((1, d), lambda i: (0, 0))
    return pl.pallas_call(
        body, name="loss_head", grid=(T // tr,),
        in_specs=[tile, tile, tile, vec, vec],
        out_specs=[pl.BlockSpec((8, LANES), lambda i: (0, 0)), tile, tile, vec, vec],
        out_shape=[jax.ShapeDtypeStruct((8, LANES), F32), jax.ShapeDtypeStruct((T, d), F32),
                   jax.ShapeDtypeStruct((T, d), F32), jax.ShapeDtypeStruct((1, d), F32),
                   jax.ShapeDtypeStruct((1, d), F32)],
        compiler_params=_cparams(("arbitrary",)),
    )(x3, fo, tgt, gate, g)


def sum_adam(parts, w, m, v, name, lead=None):
    P, R, Cc = parts.shape
    tc = _pick(Cc, 1024)
    tr = _pick(R, max(8, (256 * 1024) // tc), 16 if parts.dtype == BF16 else 8)

    def body(p_ref, w_ref, m_ref, v_ref, g_ref, d_ref, nm_ref, nv_ref):
        g = p_ref[0].astype(F32)
        for s in range(1, P):
            g = g + p_ref[s].astype(F32)
        m_new = ADAM_B1 * m_ref[...] + (1.0 - ADAM_B1) * g
        v_new = ADAM_B2 * v_ref[...] + (1.0 - ADAM_B2) * jnp.square(g)
        m_hat = m_new / (1.0 - ADAM_B1 ** ADAM_STEP)
        v_hat = v_new / (1.0 - ADAM_B2 ** ADAM_STEP)
        g_ref[...] = g
        d_ref[...] = -ADAM_LR * (m_hat / (jnp.sqrt(v_hat) + ADAM_EPS) + ADAM_WD * w_ref[...])
        nm_ref[...] = m_new
        nv_ref[...] = v_new

    if lead is None:
        pspec = pl.BlockSpec((tr, tc), lambda i, j: (i, j))
    else:
        pspec = pl.BlockSpec((None, tr, tc), lambda i, j: (lead, i, j))
    ospec = pl.BlockSpec((tr, tc), lambda i, j: (i, j))
    return pl.pallas_call(
        body, name=name, grid=(R // tr, Cc // tc),
        in_specs=[pl.BlockSpec((P, tr, tc), lambda i, j: (0, i, j)), pspec, pspec, pspec],
        out_specs=[ospec] * 4,
        out_shape=[jax.ShapeDtypeStruct((R, Cc), F32)] * 4,
        compiler_params=_cparams(("parallel", "parallel")),
    )(parts, w, m, v)


def _me():
    return lax.axis_index("x"), lax.axis_index("y"), lax.axis_index("c")


def _peer(p):
    x, y, c = _me()
    px = 1 - x if p & 4 else x
    py = 1 - y if p & 2 else y
    pc = 1 - c if p & 1 else c
    return (px, py, pc), 4 * px + 2 * py + pc


def _block_view(ref, axis, idx, r, c):
    if axis is None:
        return ref.at[idx]
    if axis == 0:
        return ref.at[pl.ds(idx * r, r), :]
    return ref.at[:, pl.ds(idx * c, c)]


def _exchange_copies(src_of, dst_of, ssem, rsem, lsem, with_recvs):
    x, y, c = _me()
    me = 4 * x + 2 * y + c
    local = pltpu.make_async_copy(src_of(me), dst_of(me), lsem)
    sends, recvs = [], []
    for p in range(1, N_DEV):
        dev, idx = _peer(p)
        sends.append(pltpu.make_async_remote_copy(src_ref=src_of(idx), dst_ref=dst_of(me), send_sem=ssem(p),
                                                  recv_sem=rsem(p), device_id=dev,
                                                  device_id_type=pl.DeviceIdType.MESH))
        if with_recvs:
            recvs.append(pltpu.make_async_remote_copy(src_ref=src_of(idx), dst_ref=dst_of(idx), send_sem=ssem(p),
                                                      recv_sem=rsem(p), device_id=dev,
                                                      device_id_type=pl.DeviceIdType.MESH))
    return local, sends, recvs


def _exchange_start(*args):
    local, sends, _ = _exchange_copies(*args, with_recvs=False)
    local.start()
    for cp in sends:
        cp.start()


def _exchange_wait(*args):
    local, sends, recvs = _exchange_copies(*args, with_recvs=True)
    for cp in recvs:
        cp.wait_recv()
    for cp in sends:
        cp.wait_send()
    local.wait()


def _exchange(src_of, dst_of, send_sems, recv_sems, local_sem):
    args = (src_of, dst_of, lambda p: send_sems.at[p], lambda p: recv_sems.at[p], local_sem)
    _exchange_start(*args)
    _exchange_wait(*args)


def _fused_exchanges(pairs, specs, send_sems, recv_sems, local_sems, first=None, last=None):
    def args(j):
        src, dst = pairs[j]
        kind, axis, r, c = specs[j]
        if kind == "ag":
            src_of = lambda idx: src
            dst_of = lambda idx: _block_view(dst, axis, idx, r, c)
        else:
            src_of = lambda idx: _block_view(src, axis, idx, r, c)
            dst_of = lambda idx: dst.at[idx]
        return (src_of, dst_of, lambda p: send_sems.at[j, p], lambda p: recv_sems.at[j, p], local_sems.at[j])

    if first is not None:
        @pl.when(first)
        def _():
            for j in range(len(pairs)):
                _exchange_start(*args(j))

    if last is not None:
        @pl.when(last)
        def _():
            for j in range(len(pairs)):
                _exchange_wait(*args(j))


def _exchange_out_shapes(arrs, specs):
    out = []
    for a, (kind, axis, r, c) in zip(arrs, specs):
        if kind == "rs":
            out.append(jax.ShapeDtypeStruct((N_DEV, r, c), a.dtype))
        else:
            out.append(jax.ShapeDtypeStruct((N_DEV * r, c) if axis == 0 else (r, N_DEV * c), a.dtype))
    return out


def _exchange_specs(kind, arrs, axes):
    specs = []
    for a, axis in zip(arrs, axes):
        if kind == "ag":
            r, c = a.shape
        elif axis == 0:
            r, c = a.shape[0] // N_DEV, a.shape[1]
        else:
            r, c = a.shape[0], a.shape[1] // N_DEV
        specs.append((kind, axis, r, c))
    return specs


def _exchange_sems(n):
    return [pltpu.SemaphoreType.DMA((n, N_DEV)), pltpu.SemaphoreType.DMA((n, N_DEV)), pltpu.SemaphoreType.DMA((n,))]


_SEMS = [pltpu.SemaphoreType.DMA((N_DEV,)), pltpu.SemaphoreType.DMA((N_DEV,)), pltpu.SemaphoreType.DMA]
_ANY = pl.BlockSpec(memory_space=pl.ANY)


def all_gather(x, axis, name):
    r, c = x.shape
    shape = (N_DEV * r, c) if axis == 0 else (r, N_DEV * c)

    def body(x_ref, o_ref, send_sems, recv_sems, local_sem):
        _exchange(lambda idx: x_ref, lambda idx: _block_view(o_ref, axis, idx, r, c), send_sems, recv_sems, local_sem)

    return pl.pallas_call(
        body, name=name, in_specs=[_ANY], out_specs=_ANY,
        out_shape=jax.ShapeDtypeStruct(shape, x.dtype), scratch_shapes=_SEMS,
    )(x)


def all_gather_stack(x, name):
    r, c = x.shape

    def body(x_ref, o_ref, send_sems, recv_sems, local_sem):
        _exchange(lambda idx: x_ref, lambda idx: o_ref.at[idx], send_sems, recv_sems, local_sem)

    return pl.pallas_call(
        body, name=name, in_specs=[_ANY], out_specs=_ANY,
        out_shape=jax.ShapeDtypeStruct((N_DEV, r, c), x.dtype), scratch_shapes=_SEMS,
    )(x)


def reduce_scatter_exchange(g, axis, name):
    if axis is None:
        _, r, c = g.shape
    elif axis == 0:
        r, c = g.shape[0] // N_DEV, g.shape[1]
    else:
        r, c = g.shape[0], g.shape[1] // N_DEV

    def body(g_ref, o_ref, send_sems, recv_sems, local_sem):
        _exchange(lambda idx: _block_view(g_ref, axis, idx, r, c), lambda idx: o_ref.at[idx],
                  send_sems, recv_sems, local_sem)

    return pl.pallas_call(
        body, name=name, in_specs=[_ANY], out_specs=_ANY,
        out_shape=jax.ShapeDtypeStruct((N_DEV, r, c), g.dtype), scratch_shapes=_SEMS,
    )(g)


PACK_QUANTUM = 16 * LANES


def _pack(arrs, dtype=F32, lead=0):
    keep = arrs[0].shape[:lead]
    flat = jnp.concatenate([a.reshape(keep + (-1,)).astype(dtype) for a in arrs], axis=-1)
    pad = (-flat.shape[-1]) % PACK_QUANTUM
    flat = jnp.pad(flat, ((0, 0),) * lead + ((0, pad),))
    return flat.reshape(keep + (-1, LANES))


def _unpack(flat2d, shapes, lead=()):
    flat = flat2d.reshape(lead + (-1,))
    out, off = [], 0
    for s in shapes:
        n = int(np.prod(s))
        out.append(flat[..., off:off + n].reshape(lead + tuple(s)))
        off += n
    return out


def _gather_lastdim(stk):
    return jnp.moveaxis(stk, 0, -2).reshape(stk.shape[1:-1] + (N_DEV * stk.shape[-1],))


def _gather_dim(stk, dim):
    moved = jnp.moveaxis(stk, 0, dim)
    sh = list(stk.shape[1:])
    sh[dim] = sh[dim] * N_DEV
    return moved.reshape(sh)


def _scatter_dim(full, dim):
    sh = list(full.shape)
    sh[dim:dim + 1] = [N_DEV, sh[dim] // N_DEV]
    return jnp.moveaxis(full.reshape(sh), dim, 0)


def _head_group_matrix(tc):
    return np.kron(np.eye(tc // HEAD, dtype=np.float32), np.ones((HEAD, HEAD), np.float32))


_SCAN_COMM = {0: ("w13_0", "w13_1"), 1: ("win", "w2_0", "w2_1", "wout", "wo")}
_W_AXIS = dict(wr=0, wk=0, wv=0, wo=0, win=1, wout=0, w13_0=1, w13_1=1, w2_0=0, w2_1=0)


def _build_forward(ctx2d, T, D, shards=None, sink=None):
    L = ctx2d.shape[0]
    N = L + T

    def make_scan(d):
        keys = _SCAN_COMM[d] if shards is not None else ()
        axes = tuple(_W_AXIS[k] for k in keys)

        def run_fwd(tok, sh):
            return _wkv_fwd_call(*tok, L, d == 1, tuple(sh), _exchange_specs("ag", sh, axes))

        @jax.custom_vjp
        def op(tok, sh):
            outs = run_fwd(tok, sh)
            return (outs[0],) + tuple(outs[3:])

        def op_fwd(tok, sh):
            outs = run_fwd(tok, sh)
            return (outs[0],) + tuple(outs[3:]), (tok, outs[1], outs[2])

        def op_bwd(res, cts):
            tok, sa, sprev = res
            dg = tuple(cts[1:])
            outs = _wkv_bwd_call(*tok, sa, sprev, cts[0], L, d == 1, dg, _exchange_specs("rs", dg, axes))
            for k, recv in zip(keys, outs[6:]):
                sink[k] = recv
            return tuple(outs[:6]), tuple(jnp.zeros(shards[k].shape, shards[k].dtype) for k in keys)

        op.defvjp(op_fwd, op_bwd)
        return op, keys

    scans = [make_scan(0), make_scan(1)]
    gm = _head_group_matrix(LANES)
    tr_row = _pick(math.gcd(L, T), 128, 8)
    op_norm = make_rowwise("norm_mod", _f_norm_mod, (F32,), tr_row, D, nb0=L // tr_row)
    op_res = [make_rowwise(f"res_norm_mod{i}", _f_res_norm_mod, (F32, BF16), _pick(T, 128, 8), D) for i in range(3)]
    op_prep = make_rowwise("wkv_prep", _f_prep, (F32,) * 7, _pick(N, 256, 8), LANES, consts=(gm,))
    op_read = make_rowwise("wkv_readout", _f_readout, (BF16,), _pick(T, 256, 8), LANES, consts=(gm,))

    def v3(a):
        return a.reshape(a.shape[0], 1, a.shape[-1])

    def fwd(xin, Ps, Wb):
        modx, modc = Ps["modx"], Ps["modc"]
        cat = jnp.concatenate([ctx2d, xin], axis=0)
        seg = lambda a, b: jnp.stack([a, b])[:, None, :]
        (hcat,) = op_norm((cat,), (Ps["n1"][0][None, None, :], seg(modc[0], modx[0, 0]), seg(modc[1], modx[0, 1])))
        xr, xw, xk, xv, xa, xg = shift_mix(hcat, Ps["mix"][:, None, :], L)
        r = linear(xr, Wb["wr"], F32, "wr")
        k = linear(xk, Wb["wk"], F32, "wk")
        v = linear(xv, Wb["wv"], F32, "wv")
        gl = jax.nn.sigmoid(linear(xg, Wb["g1"], F32, "g1"))
        g = linear(gl.astype(BF16), Wb["g2"], F32, "g2")
        tw = jnp.tanh(linear(xw, Wb["w1"], F32, "w1")).astype(BF16)
        ta = linear(xa, Wb["a1"], F32, "a1").astype(BF16)
        lw = [linear(tw[:, LORA_PAD * d:LORA_PAD * (d + 1)], Wb["w2d"][d], F32, f"w2_{d}") for d in range(2)]
        la = [linear(ta[:, LORA_PAD * d:LORA_PAD * (d + 1)], Wb["a2d"][d], F32, f"a2_{d}") for d in range(2)]
        kk, dec0, dec1, kd0, kd1, as0, as1 = op_prep(
            (k, lw[0], lw[1], la[0], la[1]),
            (v3(Ps["kk"]), v3(Ps["ka"]), Ps["w0"][0][None, None, :], Ps["w0"][1][None, None, :],
             Ps["a0"][0][None, None, :], Ps["a0"][1][None, None, :]))
        r2, kk2, v2 = _tile_heads(r), _tile_heads(kk), _tile_heads(v)
        ys = []
        Wb = dict(Wb)
        for d, (dec, kd, sg) in enumerate(((dec0, kd0, as0), (dec1, kd1, as1))):
            op, keys = scans[d]
            outs = op((r2, _tile_heads(dec), _tile_heads(kd), kk2, _tile_heads(sg), v2),
                      tuple(shards[k] for k in keys))
            yx = outs[0][L:]
            ys.append((yx[:, :, :HEAD] + yx[:, :, HEAD:]).reshape(T, D))
            Wb.update(zip(keys, outs[1:]))
        (o,) = op_read((ys[0], ys[1], r[L:], kd0[L:], kd1[L:], v[L:], g[L:]),
                       (v3(Ps["rk"]), v3(Ps["lnw"]), v3(Ps["lnb"])))
        att = linear(o, Wb["wo"], F32, "wo")
        x1, h2 = op_res[0]((xin, att), (modx[0, 2][None, None, :], Ps["n2"][0][None, None, :],
                                        modx[0, 3][None, None, :], modx[0, 4][None, None, :]))
        act = swiglu_act(linear(h2, Wb["w13_0"], BF16, "w13_0"), "swiglu0")
        f0 = linear(act, Wb["w2_0"], F32, "w2_0")
        x2, h = op_res[1]((x1, f0), (modx[0, 5][None, None, :], Ps["n1"][1][None, None, :],
                                     modx[1, 0][None, None, :], modx[1, 1][None, None, :]))
        guc = linear(h, Wb["win"], BF16, "win")
        p = gated_conv(guc, Ps["conv"][:, None, :])
        cv = linear(p, Wb["wout"], F32, "wout")
        x3, h2b = op_res[2]((x2, cv), (modx[1, 2][None, None, :], Ps["n2"][1][None, None, :],
                                       modx[1, 3][None, None, :], modx[1, 4][None, None, :]))
        act1 = swiglu_act(linear(h2b, Wb["w13_1"], BF16, "w13_1"), "swiglu1")
        f1 = linear(act1, Wb["w2_1"], F32, "w2_1")
        return x3, f1


    return fwd


def kernel(x, c, ctx, c_ctx, norm1_g, norm2_g, ada_w, ada_b, rw_mix, rw_wr, rw_wk, rw_wv, rw_wo, rw_w0, rw_w1, rw_w2, rw_a0, rw_a1, rw_a2, rw_g1, rw_g2, rw_kk, rw_ka, rw_rk, rw_lnw, rw_lnb, sc_win, sc_conv, sc_wout, ffn_w13, ffn_w2, final_g, loss_target, m_c_ctx, m_norm1_g, m_norm2_g, m_ada_w, m_ada_b, m_rw_mix, m_rw_wr, m_rw_wk, m_rw_wv, m_rw_wo, m_rw_w0, m_rw_w1, m_rw_w2, m_rw_a0, m_rw_a1, m_rw_a2, m_rw_g1, m_rw_g2, m_rw_kk, m_rw_ka, m_rw_rk, m_rw_lnw, m_rw_lnb, m_sc_win, m_sc_conv, m_sc_wout, m_ffn_w13, m_ffn_w2, m_final_g, v_c_ctx, v_norm1_g, v_norm2_g, v_ada_w, v_ada_b, v_rw_mix, v_rw_wr, v_rw_wk, v_rw_wv, v_rw_wo, v_rw_w0, v_rw_w1, v_rw_w2, v_rw_a0, v_rw_a1, v_rw_a2, v_rw_g1, v_rw_g2, v_rw_kk, v_rw_ka, v_rw_rk, v_rw_lnw, v_rw_lnb, v_sc_win, v_sc_conv, v_sc_wout, v_ffn_w13, v_ffn_w2, v_final_g):
    W = dict(c_ctx=c_ctx, norm1_g=norm1_g, norm2_g=norm2_g, ada_w=ada_w, ada_b=ada_b, rw_mix=rw_mix, rw_wr=rw_wr,
             rw_wk=rw_wk, rw_wv=rw_wv, rw_wo=rw_wo, rw_w0=rw_w0, rw_w1=rw_w1, rw_w2=rw_w2, rw_a0=rw_a0, rw_a1=rw_a1,
             rw_a2=rw_a2, rw_g1=rw_g1, rw_g2=rw_g2, rw_kk=rw_kk, rw_ka=rw_ka, rw_rk=rw_rk, rw_lnw=rw_lnw,
             rw_lnb=rw_lnb, sc_win=sc_win, sc_conv=sc_conv, sc_wout=sc_wout, ffn_w13=ffn_w13, ffn_w2=ffn_w2,
             final_g=final_g)
    Mo = dict(c_ctx=m_c_ctx, norm1_g=m_norm1_g, norm2_g=m_norm2_g, ada_w=m_ada_w, ada_b=m_ada_b, rw_mix=m_rw_mix,
              rw_wr=m_rw_wr, rw_wk=m_rw_wk, rw_wv=m_rw_wv, rw_wo=m_rw_wo, rw_w0=m_rw_w0, rw_w1=m_rw_w1,
              rw_w2=m_rw_w2, rw_a0=m_rw_a0, rw_a1=m_rw_a1, rw_a2=m_rw_a2, rw_g1=m_rw_g1, rw_g2=m_rw_g2,
              rw_kk=m_rw_kk, rw_ka=m_rw_ka, rw_rk=m_rw_rk, rw_lnw=m_rw_lnw, rw_lnb=m_rw_lnb, sc_win=m_sc_win,
              sc_conv=m_sc_conv, sc_wout=m_sc_wout, ffn_w13=m_ffn_w13, ffn_w2=m_ffn_w2, final_g=m_final_g)
    Vo = dict(c_ctx=v_c_ctx, norm1_g=v_norm1_g, norm2_g=v_norm2_g, ada_w=v_ada_w, ada_b=v_ada_b, rw_mix=v_rw_mix,
              rw_wr=v_rw_wr, rw_wk=v_rw_wk, rw_wv=v_rw_wv, rw_wo=v_rw_wo, rw_w0=v_rw_w0, rw_w1=v_rw_w1,
              rw_w2=v_rw_w2, rw_a0=v_rw_a0, rw_a1=v_rw_a1, rw_a2=v_rw_a2, rw_g1=v_rw_g1, rw_g2=v_rw_g2,
              rw_kk=v_rw_kk, rw_ka=v_rw_ka, rw_rk=v_rw_rk, rw_lnw=v_rw_lnw, rw_lnb=v_rw_lnb, sc_win=v_sc_win,
              sc_conv=v_sc_conv, sc_wout=v_sc_wout, ffn_w13=v_ffn_w13, ffn_w2=v_ffn_w2, final_g=v_final_g)
    names = list(W)

    x2d = x[0]
    ctx2d = ctx[0]
    tgt = loss_target[0]
    T, D = x2d.shape
    L = ctx2d.shape[0]
    N = L + T
    nh = D // HEAD
    mx, my, mc = _me()
    me = 4 * mx + 2 * my + mc
    dloc = D // N_DEV

    lr = rw_w1.shape[-1]
    pad_r = LORA_PAD - lr
    w1p = jnp.pad(rw_w1[0], ((0, 0), (0, 0), (0, pad_r)))
    a1p = jnp.pad(rw_a1[0], ((0, 0), (0, 0), (0, pad_r)))
    w2p = jnp.pad(rw_w2[0], ((0, 0), (0, pad_r), (0, 0)))
    a2p = jnp.pad(rw_a2[0], ((0, 0), (0, pad_r), (0, 0)))
    small_loc = [rw_mix[0], rw_w0[0], rw_a0[0], sc_conv[0], w1p, a1p, w2p, a2p, rw_g1[0], rw_g2[0]]
    small_dim = [1, 1, 1, 1, 1, 1, 2, 2, 0, 1]
    small_shapes = [a.shape for a in small_loc]
    small_groups = ((slice(0, 4), F32, "vec"), (slice(4, 10), BF16, "mat"))
    small_full = []
    for sl, dt, tag in small_groups:
        sm_all = all_gather_stack(_pack(small_loc[sl], dt), "ag_small_" + tag)
        sm_parts = _unpack(sm_all, small_shapes[sl], lead=(N_DEV,))
        small_full += [_gather_dim(p, dm) for p, dm in zip(sm_parts, small_dim[sl])]
    mix_f, w0_f, a0_f, conv_f, w1_f, a1_f, w2_f, a2_f, g1_f, g2_f = small_full

    c_all = all_gather_stack(jnp.pad(c, ((0, 7), (0, 0))), "ag_c")[:, 0, :]
    cond_pre = jnp.concatenate([c_all, c_ctx[None, :], jnp.zeros((7, D), F32)], axis=0)
    cond_rows = jax.nn.silu(cond_pre)
    ncol = ada_w.shape[-1]
    mod_loc = []
    for i in range(2):
        bi = lax.dynamic_slice(ada_b[i], (me * ncol,), (ncol,))
        mod_loc.append(_mm(cond_rows, ada_w[i], out_dtype=F32, name=f"ada_fwd{i}") + bi[None, :])
    mod_all = all_gather_stack(jnp.concatenate(mod_loc, axis=0), "ag_mod")
    mod_full = _gather_lastdim(mod_all).reshape(2, 16, 6, D)
    mod_x = lax.dynamic_index_in_dim(mod_full, me, axis=1, keepdims=False)
    mod_c = mod_full[0, 8, :2, :]

    def ag_w(wl, axis, name):
        return all_gather(wl.astype(BF16), axis, name)

    shards = dict(wo=rw_wo[0], win=sc_win[0], wout=sc_wout[0], w13_0=ffn_w13[0], w13_1=ffn_w13[1],
                  w2_0=ffn_w2[0], w2_1=ffn_w2[1])
    shards = {k_: a.astype(BF16) for k_, a in shards.items()}
    sink = {}
    Wb = dict(
        wr=ag_w(rw_wr[0], 0, "ag_wr"), wk=ag_w(rw_wk[0], 0, "ag_wk"), wv=ag_w(rw_wv[0], 0, "ag_wv"),
        w1=jnp.concatenate([w1_f[0], w1_f[1]], axis=1).astype(BF16),
        a1=jnp.concatenate([a1_f[0], a1_f[1]], axis=1).astype(BF16),
        w2d=w2_f.astype(BF16), a2d=a2_f.astype(BF16),
        g1=g1_f.astype(BF16), g2=g2_f.astype(BF16),
    )
    Ps = dict(n1=norm1_g, n2=norm2_g, modx=mod_x, modc=mod_c, mix=mix_f, w0=w0_f, a0=a0_f, conv=conv_f,
              kk=rw_kk, ka=rw_ka, rk=rw_rk.reshape(1, D), lnw=rw_lnw, lnb=rw_lnb)

    fwd = _build_forward(ctx2d, T, D, shards, sink)
    (x3, f1), vjp_fn = jax.vjp(fwd, x2d, Ps, Wb)
    loss_acc, dx3, df1, dgate, dfinal = loss_head(x3, f1, tgt, mod_x[1, 5][None, :], final_g[None, :])
    dx, dPs, dWb = vjp_fn((dx3, df1))
    loss = lax.psum(loss_acc[0, 0], ("x", "y", "c"))

    dmodx = dPs["modx"].at[1, 5].add(dgate[0])
    dmodc = jnp.concatenate([dPs["modc"], jnp.zeros((4, D), F32)], axis=0)
    drow = jnp.stack([dmodx.reshape(2, 6 * D), jnp.stack([dmodc.reshape(6 * D), jnp.zeros((6 * D,), F32)])], axis=1)
    drow_all = all_gather_stack(drow.reshape(4, 6 * D), "ag_dmod").reshape(N_DEV, 2, 2, 6 * D)
    dctx_tot = drow_all[0, :, 1, :]
    for s in range(1, N_DEV):
        dctx_tot = dctx_tot + drow_all[s, :, 1, :]
    dmod_rows = jnp.concatenate([jnp.moveaxis(drow_all[:, :, 0, :], 0, 1), dctx_tot[:, None, :],
                                 jnp.zeros((2, 7, 6 * D), F32)], axis=1)
    grad_ada_b = dctx_tot
    for s in range(N_DEV):
        grad_ada_b = grad_ada_b + drow_all[s, :, 0, :]
    dmod_mine = lax.dynamic_slice_in_dim(dmod_rows, me * ncol, ncol, axis=2)
    g_ada_w = [_mm(cond_rows, dmod_mine[i], ta=True, out_dtype=F32, name=f"ada_dw{i}") for i in range(2)]
    dcond_part = _mm(dmod_mine[0], ada_w[0], tb=True, out_dtype=F32, name="ada_dcond")[8]

    rep_names = ["c_ctx", "norm1_g", "norm2_g", "rw_kk", "rw_ka", "rw_rk", "rw_lnw", "rw_lnb", "final_g"]
    rep_part = [dcond_part, dPs["n1"], dPs["n2"], dPs["kk"], dPs["ka"], dPs["rk"].reshape(W["rw_rk"].shape),
                dPs["lnw"], dPs["lnb"], dfinal[0]]
    rep_shapes = [W[n_].shape for n_ in rep_names]
    rep_all = all_gather_stack(_pack(rep_part), "ag_rep_grads")
    sg = jax.nn.sigmoid(c_ctx)
    dsilu = sg * (1.0 + c_ctx * (1.0 - sg))
    rep_scale = _pack([dsilu] + [jnp.ones(s, F32) for s in rep_shapes[1:]])
    rep_all = rep_all * rep_scale[None]
    rep_w = _pack([W[n_] for n_ in rep_names])
    rep_m = _pack([Mo[n_] for n_ in rep_names])
    rep_v = _pack([Vo[n_] for n_ in rep_names])
    rep_out = sum_adam(rep_all, rep_w, rep_m, rep_v, "adam_rep")
    results = {}
    for nm_, vals in zip(rep_names, zip(*[_unpack(o, rep_shapes) for o in rep_out])):
        results[nm_] = vals

    results["ada_b"] = tuple(sum_adam(grad_ada_b.reshape(1, 2 * 6, D), ada_b.reshape(12, D), m_ada_b.reshape(12, D),
                                      v_ada_b.reshape(12, D), "adam_ada_b"))
    results["ada_b"] = tuple(o.reshape(ada_b.shape) for o in results["ada_b"])

    outs = [sum_adam(g_ada_w[i][None], ada_w, m_ada_w, v_ada_w, f"adam_ada_w{i}", lead=i) for i in range(2)]
    results["ada_w"] = tuple(jnp.stack([outs[0][q], outs[1][q]]) for q in range(4))

    dw1 = jnp.stack([dWb["w1"][:, :LORA_PAD], dWb["w1"][:, LORA_PAD:]])
    da1 = jnp.stack([dWb["a1"][:, :LORA_PAD], dWb["a1"][:, LORA_PAD:]])
    small_g = [dPs["mix"], dPs["w0"], dPs["a0"], dPs["conv"], dw1, da1, dWb["w2d"], dWb["a2d"], dWb["g1"], dWb["g2"]]
    small_names = ["rw_mix", "rw_w0", "rw_a0", "sc_conv", "rw_w1", "rw_a1", "rw_w2", "rw_a2", "rw_g1", "rw_g2"]

    def padded_local(nm_, src):
        a = src[nm_][0]
        if nm_ in ("rw_w1", "rw_a1"):
            return jnp.pad(a, ((0, 0), (0, 0), (0, pad_r)))
        if nm_ in ("rw_w2", "rw_a2"):
            return jnp.pad(a, ((0, 0), (0, pad_r), (0, 0)))
        return a

    for sl, dt, tag in small_groups:
        blocks = [_scatter_dim(gf, dm) for gf, dm in zip(small_g[sl], small_dim[sl])]
        sm_recv = reduce_scatter_exchange(_pack(blocks, dt, lead=1), None, "rs_small_" + tag)
        sm_out = sum_adam(sm_recv, _pack([padded_local(n_, W) for n_ in small_names[sl]]),
                          _pack([padded_local(n_, Mo) for n_ in small_names[sl]]),
                          _pack([padded_local(n_, Vo) for n_ in small_names[sl]]), "adam_small_" + tag)
        for nm_, vals in zip(small_names[sl], zip(*[_unpack(o, small_shapes[sl]) for o in sm_out])):
            if nm_ in ("rw_w1", "rw_a1"):
                vals = tuple(a[:, :, :lr] for a in vals)
            if nm_ in ("rw_w2", "rw_a2"):
                vals = tuple(a[:, :lr, :] for a in vals)
            results[nm_] = tuple(a[None] for a in vals)

    def rs_adam(key, nm_, lead):
        recv = sink[key] if key in sink else reduce_scatter_exchange(dWb[key], _W_AXIS[key], "rs_" + key)
        return sum_adam(recv, W[nm_], Mo[nm_], Vo[nm_], "adam_" + key, lead=lead)

    for nm_, key in (("rw_wr", "wr"), ("rw_wk", "wk"), ("rw_wv", "wv"), ("rw_wo", "wo"), ("sc_win", "win"),
                     ("sc_wout", "wout")):
        results[nm_] = tuple(a[None] for a in rs_adam(key, nm_, 0))
    for nm_, key in (("ffn_w13", "w13"), ("ffn_w2", "w2")):
        outs = [rs_adam(f"{key}_{i}", nm_, i) for i in range(2)]
        results[nm_] = tuple(jnp.stack([outs[0][q], outs[1][q]]) for q in range(4))

    grads = [results[n_][0] for n_ in names]
    deltas = [results[n_][1] for n_ in names]
    new_m = [results[n_][2] for n_ in names]
    new_v = [results[n_][3] for n_ in names]
    return (loss, dx[None], *grads, *deltas, *new_m, *new_v)
```

```python
import functools
import math

import numpy as np
import jax
import jax.numpy as jnp
from jax import lax
from jax.experimental import pallas as pl
from jax.experimental.pallas import tpu as pltpu

F32 = jnp.float32
BF16 = jnp.bfloat16

N_DEV = 8
HEAD = 64
LANES = 128
GRID_W = 64
LORA_PAD = 128
NORM_EPS = 1e-6
GN_EPS = 64e-5
ADAM_LR, ADAM_B1, ADAM_B2, ADAM_EPS, ADAM_WD, ADAM_STEP = 0.001, 0.9, 0.999, 1e-08, 0.01, 10
VMEM_LIMIT = 52 * 1024 * 1024
SCAN_CHUNK = 8
SCAN_UNROLL = 4
HI = lax.Precision.HIGHEST


def _cparams(sem):
    return pltpu.CompilerParams(dimension_semantics=sem, vmem_limit_bytes=VMEM_LIMIT)


def _pick(n, cap, quantum=LANES):
    best = None
    for t in range(quantum, min(n, cap) + 1, quantum):
        if n % t == 0:
            best = t
    return n if best is None else best


MM_VMEM_BUDGET = 36 * 1024 * 1024


def _divisors(n, cap, quantum=LANES):
    ds = [t for t in range(quantum, min(n, cap) + 1, quantum) if n % t == 0]
    return sorted(ds, reverse=True) or [n]


def _mm_tiles(M, N, K, sa, sb, so):
    tms, tns, tks = _divisors(M, 1024), _divisors(N, 1024), _divisors(K, 2816)
    im = jn = ik = 0

    def est(tm, tn, tk):
        b = 2 * (tm * tk * sa + tk * tn * sb) + 2 * tm * tn * so + tm * tn * 4
        b += tm * tk * 2 if sa == 4 else 0
        b += tk * tn * 2 if sb == 4 else 0
        return b + (tm * tn * 4 if tk < K else 0)

    while est(tms[im], tns[jn], tks[ik]) > MM_VMEM_BUDGET:
        if tms[im] >= tns[jn] and im + 1 < len(tms):
            im += 1
        elif jn + 1 < len(tns):
            jn += 1
        elif im + 1 < len(tms):
            im += 1
        elif ik + 1 < len(tks):
            ik += 1
        else:
            break
    return tms[im], tns[jn], tks[ik]


def _mm(a, b, *, ta=False, tb=False, out_dtype, name):
    if ta:
        K, M = a.shape
    else:
        M, K = a.shape
    if tb:
        N, Kb = b.shape
    else:
        Kb, N = b.shape
    assert K == Kb, (a.shape, b.shape, ta, tb)
    tm, tn, tk = _mm_tiles(M, N, K, a.dtype.itemsize, b.dtype.itemsize, jnp.dtype(out_dtype).itemsize)
    nk = K // tk
    dims = (((0 if ta else 1,), (1 if tb else 0,)), ((), ()))

    def body(a_ref, b_ref, o_ref, *acc):
        part = lax.dot_general(a_ref[...].astype(BF16), b_ref[...].astype(BF16), dims, preferred_element_type=F32)
        if nk == 1:
            o_ref[...] = part.astype(o_ref.dtype)
            return
        acc_ref, = acc
        k = pl.program_id(2)

        @pl.when(k == 0)
        def _():
            acc_ref[...] = part

        @pl.when(k > 0)
        def _():
            acc_ref[...] += part

        @pl.when(k == nk - 1)
        def _():
            o_ref[...] = acc_ref[...].astype(o_ref.dtype)

    a_spec = pl.BlockSpec((tk, tm), lambda i, j, k: (k, i)) if ta else pl.BlockSpec((tm, tk), lambda i, j, k: (i, k))
    b_spec = pl.BlockSpec((tn, tk), lambda i, j, k: (j, k)) if tb else pl.BlockSpec((tk, tn), lambda i, j, k: (k, j))
    return pl.pallas_call(
        body, name=name, grid=(M // tm, N // tn, nk),
        in_specs=[a_spec, b_spec],
        out_specs=pl.BlockSpec((tm, tn), lambda i, j, k: (i, j)),
        out_shape=jax.ShapeDtypeStruct((M, N), out_dtype),
        scratch_shapes=[pltpu.VMEM((tm, tn), F32)] if nk > 1 else [],
        compiler_params=_cparams(("parallel", "parallel", "arbitrary")),
    )(a, b)


@functools.partial(jax.custom_vjp, nondiff_argnums=(2, 3))
def linear(a, w, out_dtype, name):
    return _mm(a, w, out_dtype=out_dtype, name=name + "_fwd")


def _linear_fwd(a, w, out_dtype, name):
    return _mm(a, w, out_dtype=out_dtype, name=name + "_fwd"), (a, w)


def _linear_bwd(out_dtype, name, res, g):
    a, w = res
    da = _mm(g, w, tb=True, out_dtype=a.dtype, name=name + "_da")
    dw = _mm(a, g, ta=True, out_dtype=w.dtype, name=name + "_dw")
    return da, dw


linear.defvjp(_linear_fwd, _linear_bwd)


def _rw_specs(tiles, col_offs, vecs, consts, tr, tc, nb0):
    tile_specs = [pl.BlockSpec((tr, tc), functools.partial(lambda j, i, off: (i, j + off), off=off))
                  for _, off in zip(tiles, col_offs)]

    def vec_map(S):
        if S == 1:
            return lambda j, i: (0, 0, j)
        return lambda j, i: (jnp.where(i < nb0, 0, 1), 0, j)

    vec_specs = [pl.BlockSpec((None, 1, tc), vec_map(v.shape[0])) for v in vecs]
    const_specs = [pl.BlockSpec(c.shape, lambda j, i: (0, 0)) for c in consts]
    return tile_specs, vec_specs, const_specs


def _rw_forward(name, f, tiles, col_offs, vecs, consts, out_dtypes, tr, tc, nb0, width):
    n = tiles[0].shape[0]
    nt, nv, nc = len(tiles), len(vecs), len(consts)
    tile_specs, vec_specs, const_specs = _rw_specs(tiles, col_offs, vecs, consts, tr, tc, nb0)

    def body(*refs):
        ins = [r[...].astype(F32) for r in refs[:nt]] + [r[...] for r in refs[nt:nt + nv + nc]]
        outs = f(*ins)
        for o_ref, o in zip(refs[nt + nv + nc:], outs):
            o_ref[...] = o.astype(o_ref.dtype)

    return pl.pallas_call(
        body, name=name + "_fwd", grid=(width // tc, n // tr),
        in_specs=tile_specs + vec_specs + const_specs,
        out_specs=[pl.BlockSpec((tr, tc), lambda j, i: (i, j)) for _ in out_dtypes],
        out_shape=[jax.ShapeDtypeStruct((n, width), dt) for dt in out_dtypes],
        compiler_params=_cparams(("parallel", "parallel")),
    )(*tiles, *vecs, *consts)


def _rw_backward(name, f, tiles, col_offs, vecs, consts, douts, tr, tc, nb0, width):
    n = tiles[0].shape[0]
    nt, nv, nc, no = len(tiles), len(vecs), len(consts), len(douts)
    tile_specs, vec_specs, const_specs = _rw_specs(tiles, col_offs, vecs, consts, tr, tc, nb0)

    def body(*refs):
        t_in = [r[...].astype(F32) for r in refs[:nt]]
        v_in = [r[...] for r in refs[nt:nt + nv]]
        c_in = [r[...] for r in refs[nt + nv:nt + nv + nc]]
        d_in = tuple(r[...].astype(F32) for r in refs[nt + nv + nc:nt + nv + nc + no])
        o_refs = refs[nt + nv + nc + no:]
        _, vjp = jax.vjp(lambda *tv: tuple(f(*tv, *c_in)), *t_in, *v_in)
        grads = vjp(d_in)
        for o_ref, g in zip(o_refs[:nt], grads[:nt]):
            o_ref[...] = g.astype(o_ref.dtype)
        i = pl.program_id(1)
        for o_ref, g, v in zip(o_refs[nt:], grads[nt:], vecs):
            first = jnp.logical_or(i == 0, i == nb0) if v.shape[0] == 2 else i == 0

            @pl.when(first)
            def _(o_ref=o_ref, g=g):
                o_ref[...] = g

            @pl.when(jnp.logical_not(first))
            def _(o_ref=o_ref, g=g):
                o_ref[...] += g

    dout_specs = [pl.BlockSpec((tr, tc), lambda j, i: (i, j)) for _ in douts]
    out_specs = [pl.BlockSpec((tr, tc), lambda j, i: (i, j)) for _ in tiles] + list(vec_specs)
    out_shape = ([jax.ShapeDtypeStruct((n, width), t.dtype) for t in tiles]
                 + [jax.ShapeDtypeStruct(v.shape, F32) for v in vecs])
    return pl.pallas_call(
        body, name=name + "_bwd", grid=(width // tc, n // tr),
        in_specs=tile_specs + vec_specs + const_specs + dout_specs,
        out_specs=out_specs, out_shape=out_shape,
        compiler_params=_cparams(("parallel", "arbitrary")),
    )(*tiles, *vecs, *consts, *douts)


def make_rowwise(name, f, out_dtypes, tr, tc, consts=(), nb0=-1):
    consts = tuple(consts)

    @jax.custom_vjp
    def op(tiles, vecs):
        w = tiles[0].shape[1]
        return tuple(_rw_forward(name, f, tiles, (0,) * len(tiles), vecs, consts, out_dtypes, tr, min(tc, w), nb0, w))

    def op_fwd(tiles, vecs):
        return op(tiles, vecs), (tiles, vecs)

    def op_bwd(res, douts):
        tiles, vecs = res
        w = tiles[0].shape[1]
        g = _rw_backward(name, f, tiles, (0,) * len(tiles), vecs, consts, tuple(douts), tr, min(tc, w), nb0, w)
        return tuple(g[:len(tiles)]), tuple(g[len(tiles):])

    op.defvjp(op_fwd, op_bwd)
    return op


def _f_norm_mod(x, g, sh, sc):
    hn = x * lax.rsqrt(jnp.mean(x * x, axis=-1, keepdims=True) + NORM_EPS)
    return ((hn * g) * (1.0 + sc) + sh,)


def _f_res_norm_mod(x, y, gate, g, sh, sc):
    x1 = x + gate * y
    hn = x1 * lax.rsqrt(jnp.mean(x1 * x1, axis=-1, keepdims=True) + NORM_EPS)
    return x1, (hn * g) * (1.0 + sc) + sh


def _head_sum_3pass(t, gmat):
    hi = t.astype(BF16)
    r1 = t - hi.astype(F32)
    mid = r1.astype(BF16)
    lo = (r1 - mid.astype(F32)).astype(BF16)
    g = gmat.astype(BF16)
    dot = lambda u: jnp.dot(u, g, preferred_element_type=F32)
    return dot(hi) + dot(mid) + dot(lo)


@jax.custom_vjp
def _head_sum(t, gmat):
    return _head_sum_3pass(t, gmat)


def _head_sum_fwd(t, gmat):
    return _head_sum_3pass(t, gmat), gmat


def _head_sum_bwd(gmat, ct):
    return _head_sum_3pass(ct, gmat), None


_head_sum.defvjp(_head_sum_fwd, _head_sum_bwd)


def _f_prep(k, lw0, lw1, la0, la1, kkp, kap, w00, w01, a00, a01, gmat):
    t = k * kkp
    kk = t / jnp.maximum(jnp.sqrt(_head_sum(t * t, gmat)), 1e-12)
    outs = [kk]
    decs, kds, sigs = [], [], []
    for lw, la, w0, a0 in ((lw0, la0, w00, a00), (lw1, la1, w01, a01)):
        decs.append(jnp.exp(-jax.nn.sigmoid(w0 + lw) * float(np.exp(-0.5))))
        a = jax.nn.sigmoid(a0 + la)
        sigs.append(a)
        kds.append(k * (1.0 + (a - 1.0) * kap))
    return tuple(outs + decs + kds + sigs)


def _f_readout(y0, y1, r, kd0, kd1, v, g, rk, lnw, lnb, gmat):
    y = y0 + y1
    mu = _head_sum(y, gmat) * (1.0 / HEAD)
    d = y - mu
    var = _head_sum(d * d, gmat) * (1.0 / HEAD)
    o = d * lax.rsqrt(var + GN_EPS) * lnw + lnb
    bonus = _head_sum(r * (kd0 + kd1) * rk, gmat) * v
    return ((o + bonus) * g,)


def _f_swiglu(a, b):
    return (jax.nn.silu(a) * b,)


def swiglu_act(ab, name):
    t, f2 = ab.shape
    fdim = f2 // 2
    tr, tc = _pick(t, 512, 8), _pick(fdim, 512)
    offs = (0, fdim // tc)

    @jax.custom_vjp
    def op(ab_):
        return _rw_forward(name, _f_swiglu, (ab_, ab_), offs, (), (), (BF16,), tr, tc, -1, fdim)[0]

    def op_fwd(ab_):
        return op(ab_), ab_

    def op_bwd(ab_, dact):
        da, db = _rw_backward(name, _f_swiglu, (ab_, ab_), offs, (), (), (dact,), tr, tc, -1, fdim)
        return (jnp.concatenate([da, db], axis=1),)

    op.defvjp(op_fwd, op_bwd)
    return op(ab)


def _row_iota(n, tc):
    return lax.broadcasted_iota(jnp.int32, (n, tc), 0)


def _shift_rows(x, s, keep):
    n = x.shape[0]
    return jnp.where(keep, pltpu.roll(x, s % n, 0), 0.0)


def _unshift_rows(d, s, keep):
    n = d.shape[0]
    return pltpu.roll(jnp.where(keep, d, 0.0), (-s) % n, 0)


def _ctx_shift_spec(L, tc, quarter):
    row = _row_iota(L, tc)
    if quarter < 2:
        return 1, row >= 1
    return -1, row < L - 1


def _grid_shift_spec(T, tc, quarter):
    row = _row_iota(T, tc)
    col = jnp.bitwise_and(row, GRID_W - 1)
    if quarter == 0:
        return 1, col != 0
    if quarter == 1:
        return -1, col != GRID_W - 1
    if quarter == 2:
        return GRID_W, row >= GRID_W
    return -GRID_W, row < T - GRID_W


def _shift_mix_fwd_call(h, mix3, L):
    n, d = h.shape
    T = n - L
    tc = _pick(d // 4, 256)
    nq = (d // 4) // tc

    def body(h_ref, mix_ref, *o_refs):
        q = pl.program_id(0) // nq
        for quarter in range(4):
            @pl.when(q == quarter)
            def _(quarter=quarter):
                for lo, cnt, spec in ((0, L, _ctx_shift_spec), (L, T, _grid_shift_spec)):
                    hh = h_ref[pl.ds(lo, cnt), :]
                    s, keep = spec(cnt, tc, quarter)
                    xx = _shift_rows(hh, s, keep) - hh
                    for m in range(6):
                        o_refs[m][pl.ds(lo, cnt), :] = (hh + xx * mix_ref[m]).astype(BF16)

    return pl.pallas_call(
        body, name="shift_mix_fwd", grid=(d // tc,),
        in_specs=[pl.BlockSpec((n, tc), lambda j: (0, j)), pl.BlockSpec((6, 1, tc), lambda j: (0, 0, j))],
        out_specs=[pl.BlockSpec((n, tc), lambda j: (0, j)) for _ in range(6)],
        out_shape=[jax.ShapeDtypeStruct((n, d), BF16) for _ in range(6)],
        compiler_params=_cparams(("parallel",)),
    )(h, mix3)


def _shift_mix_bwd_call(h, mix3, douts, L):
    n, d = h.shape
    T = n - L
    tc = _pick(d // 4, 256)
    nq = (d // 4) // tc

    def body(h_ref, mix_ref, d0, d1, d2, d3, d4, d5, dh_ref, dmix_ref):
        d_refs = (d0, d1, d2, d3, d4, d5)
        q = pl.program_id(0) // nq
        for quarter in range(4):
            @pl.when(q == quarter)
            def _(quarter=quarter):
                dmix = [jnp.zeros((1, tc), F32) for _ in range(6)]
                for lo, cnt, spec in ((0, L, _ctx_shift_spec), (L, T, _grid_shift_spec)):
                    hh = h_ref[pl.ds(lo, cnt), :]
                    s, keep = spec(cnt, tc, quarter)
                    xx = _shift_rows(hh, s, keep) - hh
                    direct = jnp.zeros((cnt, tc), F32)
                    shifted = jnp.zeros((cnt, tc), F32)
                    for m in range(6):
                        dm = d_refs[m][pl.ds(lo, cnt), :].astype(F32)
                        mx = mix_ref[m]
                        direct = direct + dm * (1.0 - mx)
                        shifted = shifted + dm * mx
                        dmix[m] = dmix[m] + jnp.sum(dm * xx, axis=0, keepdims=True)
                    dh_ref[pl.ds(lo, cnt), :] = direct + _unshift_rows(shifted, s, keep)
                for m in range(6):
                    dmix_ref[m] = dmix[m]

    tile = pl.BlockSpec((n, tc), lambda j: (0, j))
    return pl.pallas_call(
        body, name="shift_mix_bwd", grid=(d // tc,),
        in_specs=[tile, pl.BlockSpec((6, 1, tc), lambda j: (0, 0, j))] + [tile] * 6,
        out_specs=[tile, pl.BlockSpec((6, 1, tc), lambda j: (0, 0, j))],
        out_shape=[jax.ShapeDtypeStruct((n, d), F32), jax.ShapeDtypeStruct((6, 1, d), F32)],
        compiler_params=_cparams(("parallel",)),
    )(h, mix3, *douts)


@functools.partial(jax.custom_vjp, nondiff_argnums=(2,))
def shift_mix(h, mix3, L):
    return tuple(_shift_mix_fwd_call(h, mix3, L))


def _shift_mix_fwd(h, mix3, L):
    return tuple(_shift_mix_fwd_call(h, mix3, L)), (h, mix3)


def _shift_mix_bwd(L, res, douts):
    h, mix3 = res
    dh, dmix = _shift_mix_bwd_call(h, mix3, tuple(douts), L)
    return dh, dmix


shift_mix.defvjp(_shift_mix_fwd, _shift_mix_bwd)


def _conv_specs(T, d, tc):
    nd = d // tc
    ins = [pl.BlockSpec((T, tc), functools.partial(lambda j, off: (0, j + off), off=o * nd)) for o in range(3)]
    return ins, pl.BlockSpec((3, 1, tc), lambda j: (0, 0, j))


def _conv_terms(gc, u, tc):
    T = gc.shape[0]
    row = _row_iota(T, tc)
    z = gc * u
    return z, _shift_rows(z, 1, row >= 1), _shift_rows(z, -1, row < T - 1), row


def _conv_fwd_call(guc, cw3):
    T, d3 = guc.shape
    d = d3 // 3
    tc = _pick(d, 256)
    ins, wspec = _conv_specs(T, d, tc)

    def body(gb_ref, gc_ref, u_ref, w_ref, p_ref):
        z, zp, zn, _ = _conv_terms(gc_ref[...].astype(F32), u_ref[...].astype(F32), tc)
        conv = zp * w_ref[0] + z * w_ref[1] + zn * w_ref[2]
        p_ref[...] = (gb_ref[...].astype(F32) * conv).astype(BF16)

    return pl.pallas_call(
        body, name="conv_fwd", grid=(d // tc,), in_specs=ins + [wspec],
        out_specs=pl.BlockSpec((T, tc), lambda j: (0, j)),
        out_shape=jax.ShapeDtypeStruct((T, d), BF16),
        compiler_params=_cparams(("parallel",)),
    )(guc, guc, guc, cw3)


def _conv_bwd_call(guc, cw3, dp):
    T, d3 = guc.shape
    d = d3 // 3
    tc = _pick(d, 256)
    ins, wspec = _conv_specs(T, d, tc)
    tile = pl.BlockSpec((T, tc), lambda j: (0, j))

    def body(gb_ref, gc_ref, u_ref, w_ref, dp_ref, dgb_ref, dgc_ref, du_ref, dw_ref):
        gc = gc_ref[...].astype(F32)
        u = u_ref[...].astype(F32)
        z, zp, zn, row = _conv_terms(gc, u, tc)
        conv = zp * w_ref[0] + z * w_ref[1] + zn * w_ref[2]
        dpv = dp_ref[...].astype(F32)
        dgb_ref[...] = (dpv * conv).astype(dgb_ref.dtype)
        dconv = dpv * gb_ref[...].astype(F32)
        dz = (_shift_rows(dconv, -1, row < T - 1) * w_ref[0] + dconv * w_ref[1]
              + _shift_rows(dconv, 1, row >= 1) * w_ref[2])
        dgc_ref[...] = (dz * u).astype(dgc_ref.dtype)
        du_ref[...] = (dz * gc).astype(du_ref.dtype)
        dw_ref[0] = jnp.sum(dconv * zp, axis=0, keepdims=True)
        dw_ref[1] = jnp.sum(dconv * z, axis=0, keepdims=True)
        dw_ref[2] = jnp.sum(dconv * zn, axis=0, keepdims=True)

    return pl.pallas_call(
        body, name="conv_bwd", grid=(d // tc,), in_specs=ins + [wspec, tile],
        out_specs=[tile, tile, tile, wspec],
        out_shape=[jax.ShapeDtypeStruct((T, d), guc.dtype)] * 3 + [jax.ShapeDtypeStruct((3, 1, d), F32)],
        compiler_params=_cparams(("parallel",)),
    )(guc, guc, guc, cw3, dp)


@jax.custom_vjp
def gated_conv(guc, cw3):
    return _conv_fwd_call(guc, cw3)


def _gated_conv_fwd(guc, cw3):
    return _conv_fwd_call(guc, cw3), (guc, cw3)


def _gated_conv_bwd(res, dp):
    guc, cw3 = res
    dgb, dgc, du, dw = _conv_bwd_call(guc, cw3, dp)
    return jnp.concatenate([dgb, dgc, du], axis=1), dw


gated_conv.defvjp(_gated_conv_fwd, _gated_conv_bwd)


def _chunk_map(nchunk, nctx_chunk, reverse):
    if not reverse:
        return lambda c: c
    return lambda c: jnp.where(c < nctx_chunk, nctx_chunk - 1 - c, nchunk - 1 - (c - nctx_chunk))


def _spread_bf16(row_ref, dst_scr, ni, C):
    packed = [pltpu.bitcast(row_ref[tt].astype(BF16), jnp.int32) for tt in range(C)]
    lane = lax.broadcasted_iota(jnp.int32, packed[0].shape, 1)
    for i in range(ni):
        idx = jnp.where(lane < HEAD, 2 * i, HEAD + 1 + 2 * i).astype(jnp.int32)
        for tt in range(C):
            got = jnp.take_along_axis(packed[tt], idx, axis=1)
            dst_scr[tt, i] = pltpu.bitcast(got, BF16).astype(F32)


def _half_sums(p, lo_mask):
    lo = jnp.sum(jnp.where(lo_mask, p, 0.0), axis=1, keepdims=True)
    hi = jnp.sum(jnp.where(lo_mask, 0.0, p), axis=1, keepdims=True)
    return jnp.where(lo_mask, lo, hi)


def _half_sums_mxu(ps, gmat):
    p = jnp.concatenate(ps, axis=0)
    hi = p.astype(BF16)
    lo = (p - hi.astype(F32)).astype(BF16)
    s = jnp.dot(hi, gmat, preferred_element_type=F32) + jnp.dot(lo, gmat, preferred_element_type=F32)
    nh = ps[0].shape[0]
    return [s[i * nh:(i + 1) * nh] for i in range(len(ps))]


def _split_row(sums, lane, nh):
    acc = jnp.zeros((nh, LANES), F32)
    for i, s in enumerate(sums):
        acc = acc + jnp.where(jnp.logical_or(lane == 2 * i, lane == HEAD + 1 + 2 * i), s, 0.0)
    return acc


def _wkv_fwd_call(r2, w2, kd2, kk2, as2, v2, nctx, reverse, xchg_arrs=(), xchg_specs=()):
    n, nh, _ = r2.shape
    C = SCAN_CHUNK
    ni = HEAD // 2
    nchunk = n // C
    cmap = _chunk_map(nchunk, nctx // C, reverse)
    nx = len(xchg_arrs)

    def body(*refs):
        g_ref, refs = refs[0], refs[1:]
        r_ref, w_ref, kd_ref, kk_ref, as_ref, v_ref = refs[:6]
        x_in = refs[6:6 + nx]
        y_ref, sa_ref, sp_ref = refs[6 + nx:9 + nx]
        x_out = refs[9 + nx:9 + 2 * nx]
        s_scr, vc_scr = refs[9 + 2 * nx:11 + 2 * nx]
        if nx:
            _fused_exchanges(list(zip(x_in, x_out)), xchg_specs, *refs[11 + 2 * nx:], first=pl.program_id(0) == 0)

        @pl.when(pl.program_id(0) == 0)
        def _():
            s_scr[...] = jnp.zeros_like(s_scr)

        lane = lax.broadcasted_iota(jnp.int32, (nh, LANES), 1)
        lo_mask = lane < HEAD
        _spread_bf16(v_ref, vc_scr, ni, C)

        def make_step(with_y):
            def step(j, carry):
                t = (C - 1 - j) if reverse else j
                kk = kk_ref[t]
                a2 = -kk
                b2 = kk * as_ref[t]
                w = w_ref[t]
                k = kd_ref[t]
                r = r_ref[t]
                sas = []
                for i in range(ni):
                    si = s_scr[i]
                    sp_ref[t, i] = si
                    sas.append(_half_sums(si * a2, lo_mask))
                qs = []
                for i in range(ni):
                    sn = s_scr[i] * w + sas[i] * b2 + vc_scr[t, i] * k
                    s_scr[i] = sn
                    if with_y:
                        qs.append(sn * r)
                if with_y:
                    y_ref[t] = _split_row(_half_sums_mxu(qs, g_ref[...]), lane, nh)
                else:
                    y_ref[t] = jnp.zeros((nh, LANES), F32)
                sa_ref[t] = _split_row(sas, lane, nh)
                return carry
            return step

        is_ctx = pl.program_id(0) < nctx // C

        @pl.when(is_ctx)
        def _():
            lax.fori_loop(0, C, make_step(False), 0)

        @pl.when(jnp.logical_not(is_ctx))
        def _():
            lax.fori_loop(0, C, make_step(True), 0, unroll=SCAN_UNROLL)

        if nx:
            _fused_exchanges(list(zip(x_in, x_out)), xchg_specs, *refs[11 + 2 * nx:],
                             last=pl.program_id(0) == nchunk - 1)

    tok = pl.BlockSpec((C, nh, LANES), lambda c: (cmap(c), 0, 0))
    return pl.pallas_call(
        body, name="wkv_fwd_rev" if reverse else "wkv_fwd", grid=(nchunk,),
        in_specs=[pl.BlockSpec((LANES, LANES), lambda c: (0, 0))] + [tok] * 6 + [_ANY] * nx,
        out_specs=[tok, tok, pl.BlockSpec((C, ni, nh, LANES), lambda c: (cmap(c), 0, 0, 0))] + [_ANY] * nx,
        out_shape=[jax.ShapeDtypeStruct((n, nh, LANES), F32), jax.ShapeDtypeStruct((n, nh, LANES), F32),
                   jax.ShapeDtypeStruct((n, ni, nh, LANES), F32)] + _exchange_out_shapes(xchg_arrs, xchg_specs),
        scratch_shapes=[pltpu.VMEM((ni, nh, LANES), F32), pltpu.VMEM((C, ni, nh, LANES), F32)]
        + (_exchange_sems(nx) if nx else []),
        compiler_params=_cparams(("arbitrary",)),
    )(_head_group_matrix(LANES).astype(BF16), r2, w2, kd2, kk2, as2, v2, *xchg_arrs)


def _wkv_bwd_call(r2, w2, kd2, kk2, as2, v2, sa, sprev, dy, nctx, reverse, xchg_arrs=(), xchg_specs=()):
    n, nh, _ = r2.shape
    C = SCAN_CHUNK
    ni = HEAD // 2
    nchunk = n // C
    fmap = _chunk_map(nchunk, nctx // C, reverse)
    cmap = lambda c: fmap(nchunk - 1 - c)
    nx = len(xchg_arrs)

    def body(*refs):
        g_ref, refs = refs[0], refs[1:]
        r_ref, w_ref, kd_ref, kk_ref, as_ref, v_ref, sa_ref, sp_ref, dy_ref = refs[:9]
        x_in = refs[9:9 + nx]
        dr_ref, dw_ref, dkd_ref, dkk_ref, das_ref, dv_ref = refs[9 + nx:15 + nx]
        x_out = refs[15 + nx:15 + 2 * nx]
        ds_scr, vc_scr, sac_scr, dyc_scr = refs[15 + 2 * nx:19 + 2 * nx]
        if nx:
            _fused_exchanges(list(zip(x_in, x_out)), xchg_specs, *refs[19 + 2 * nx:], first=pl.program_id(0) == 0)

        @pl.when(pl.program_id(0) == 0)
        def _():
            ds_scr[...] = jnp.zeros_like(ds_scr)

        lane = lax.broadcasted_iota(jnp.int32, (nh, LANES), 1)
        lo_mask = lane < HEAD
        _spread_bf16(v_ref, vc_scr, ni, C)
        _spread_bf16(sa_ref, sac_scr, ni, C)
        is_ctx = pl.program_id(0) >= nchunk - nctx // C

        @pl.when(jnp.logical_not(is_ctx))
        def _():
            _spread_bf16(dy_ref, dyc_scr, ni, C)

        def make_step(with_dy):
            def step(j, carry):
                t = j if reverse else (C - 1 - j)
                kk = kk_ref[t]
                sig = as_ref[t]
                a2 = -kk
                b2 = kk * sig
                w = w_ref[t]
                k = kd_ref[t]
                r = r_ref[t]
                zero = jnp.zeros((nh, LANES), F32)
                acc_dk, acc_db, acc_dw, acc_g, acc_sady, acc_vdy, acc_da = zero, zero, zero, zero, zero, zero, zero
                dvp, dsas = [], []
                for i in range(ni):
                    sp = sp_ref[t, i]
                    vc = vc_scr[t, i]
                    sac = sac_scr[t, i]
                    ds = ds_scr[i]
                    if with_dy:
                        dyc = dyc_scr[t, i]
                        ds = ds + dyc * r
                        ds_scr[i] = ds
                        acc_g = acc_g + sp * dyc
                    dvp.append(ds * k)
                    dsas.append(_half_sums(ds * b2, lo_mask))
                    acc_dk = acc_dk + ds * vc
                    acc_db = acc_db + ds * sac
                    acc_dw = acc_dw + ds * sp
                for i in range(ni):
                    acc_da = acc_da + sp_ref[t, i] * dsas[i]
                    ds_scr[i] = ds_scr[i] * w + dsas[i] * a2
                if with_dy:
                    dyr = jnp.where(jnp.bitwise_and(lane, 1) == (lane >= HEAD).astype(jnp.int32), dy_ref[t], 0.0)
                    acc_sady = _half_sums(sa_ref[t] * dyr, lo_mask)
                    acc_vdy = _half_sums(v_ref[t] * dyr, lo_mask)
                dr_ref[t] = acc_g * w + b2 * acc_sady + k * acc_vdy
                dw_ref[t] = acc_dw
                dkd_ref[t] = acc_dk
                dkk_ref[t] = acc_db * sig - acc_da
                das_ref[t] = acc_db * kk
                dv_ref[t] = _split_row(_half_sums_mxu(dvp, g_ref[...]), lane, nh)
                return carry
            return step

        @pl.when(is_ctx)
        def _():
            lax.fori_loop(0, C, make_step(False), 0)

        @pl.when(jnp.logical_not(is_ctx))
        def _():
            lax.fori_loop(0, C, make_step(True), 0, unroll=SCAN_UNROLL)

        if nx:
            _fused_exchanges(list(zip(x_in, x_out)), xchg_specs, *refs[19 + 2 * nx:],
                             last=pl.program_id(0) == nchunk - 1)

    tok = pl.BlockSpec((C, nh, LANES), lambda c: (cmap(c), 0, 0))
    big = pltpu.VMEM((C, ni, nh, LANES), F32)
    return pl.pallas_call(
        body, name="wkv_bwd_rev" if reverse else "wkv_bwd", grid=(nchunk,),
        in_specs=[pl.BlockSpec((LANES, LANES), lambda c: (0, 0))] + [tok] * 7
        + [pl.BlockSpec((C, ni, nh, LANES), lambda c: (cmap(c), 0, 0, 0)), tok] + [_ANY] * nx,
        out_specs=[tok] * 6 + [_ANY] * nx,
        out_shape=[jax.ShapeDtypeStruct((n, nh, LANES), F32)] * 6 + _exchange_out_shapes(xchg_arrs, xchg_specs),
        scratch_shapes=[pltpu.VMEM((ni, nh, LANES), F32), big, big, big] + (_exchange_sems(nx) if nx else []),
        compiler_params=_cparams(("arbitrary",)),
    )(_head_group_matrix(LANES).astype(BF16), r2, w2, kd2, kk2, as2, v2, sa, sprev, dy, *xchg_arrs)


def _tile_heads(t):
    n, d = t.shape
    th = t.reshape(n, d // HEAD, HEAD)
    return jnp.concatenate([th, th], axis=-1)


def loss_head(x3, fo, tgt, gate, g):
    T, d = x3.shape
    tr = _pick(T, 128, 8)

    def body(x_ref, f_ref, t_ref, gate_ref, g_ref, loss_ref, dx_ref, df_ref, dgate_ref, dg_ref):
        tg = t_ref[...]

        def fl(x, fo_, gate_, g_):
            x4 = x + gate_ * fo_
            y = (x4 * lax.rsqrt(jnp.mean(x4 * x4, axis=-1, keepdims=True) + NORM_EPS)) * g_
            return 0.5 * jnp.sum(jnp.mean(jnp.square(y - tg), axis=-1))

        val, vjp = jax.vjp(fl, x_ref[...], f_ref[...], gate_ref[...], g_ref[...])
        dx, dfo, dgate, dg = vjp(jnp.ones((), F32))
        dx_ref[...] = dx
        df_ref[...] = dfo
        i = pl.program_id(0)

        @pl.when(i == 0)
        def _():
            loss_ref[...] = jnp.zeros_like(loss_ref)
            dgate_ref[...] = jnp.zeros_like(dgate_ref)
            dg_ref[...] = jnp.zeros_like(dg_ref)

        loss_ref[...] += jnp.full(loss_ref.shape, val, F32)
        dgate_ref[...] += dgate
        dg_ref[...] += dg

    tile = pl.BlockSpec((tr, d), lambda i: (i, 0))
    vec = pl.BlockSpec((1, d), lambda i: (0, 0))
    return pl.pallas_call(
        body, name="loss_head", grid=(T // tr,),
        in_specs=[tile, tile, tile, vec, vec],
        out_specs=[pl.BlockSpec((8, LANES), lambda i: (0, 0)), tile, tile, vec, vec],
        out_shape=[jax.ShapeDtypeStruct((8, LANES), F32), jax.ShapeDtypeStruct((T, d), F32),
                   jax.ShapeDtypeStruct((T, d), F32), jax.ShapeDtypeStruct((1, d), F32),
                   jax.ShapeDtypeStruct((1, d), F32)],
        compiler_params=_cparams(("arbitrary",)),
    )(x3, fo, tgt, gate, g)


def sum_adam(parts, w, m, v, name, lead=None):
    P, R, Cc = parts.shape
    tc = _pick(Cc, 1024)
    tr = _pick(R, max(8, (256 * 1024) // tc), 16 if parts.dtype == BF16 else 8)

    def body(p_ref, w_ref, m_ref, v_ref, g_ref, d_ref, nm_ref, nv_ref):
        g = p_ref[0].astype(F32)
        for s in range(1, P):
            g = g + p_ref[s].astype(F32)
        m_new = ADAM_B1 * m_ref[...] + (1.0 - ADAM_B1) * g
        v_new = ADAM_B2 * v_ref[...] + (1.0 - ADAM_B2) * jnp.square(g)
        m_hat = m_new / (1.0 - ADAM_B1 ** ADAM_STEP)
        v_hat = v_new / (1.0 - ADAM_B2 ** ADAM_STEP)
        g_ref[...] = g
        d_ref[...] = -ADAM_LR * (m_hat / (jnp.sqrt(v_hat) + ADAM_EPS) + ADAM_WD * w_ref[...])
        nm_ref[...] = m_new
        nv_ref[...] = v_new

    if lead is None:
        pspec = pl.BlockSpec((tr, tc), lambda i, j: (i, j))
    else:
        pspec = pl.BlockSpec((None, tr, tc), lambda i, j: (lead, i, j))
    ospec = pl.BlockSpec((tr, tc), lambda i, j: (i, j))
    return pl.pallas_call(
        body, name=name, grid=(R // tr, Cc // tc),
        in_specs=[pl.BlockSpec((P, tr, tc), lambda i, j: (0, i, j)), pspec, pspec, pspec],
        out_specs=[ospec] * 4,
        out_shape=[jax.ShapeDtypeStruct((R, Cc), F32)] * 4,
        compiler_params=_cparams(("parallel", "parallel")),
    )(parts, w, m, v)


def _me():
    return lax.axis_index("x"), lax.axis_index("y"), lax.axis_index("c")


def _peer(p):
    x, y, c = _me()
    px = 1 - x if p & 4 else x
    py = 1 - y if p & 2 else y
    pc = 1 - c if p & 1 else c
    return (px, py, pc), 4 * px + 2 * py + pc


def _block_view(ref, axis, idx, r, c):
    if axis is None:
        return ref.at[idx]
    if axis == 0:
        return ref.at[pl.ds(idx * r, r), :]
    return ref.at[:, pl.ds(idx * c, c)]


def _exchange_copies(src_of, dst_of, ssem, rsem, lsem, with_recvs):
    x, y, c = _me()
    me = 4 * x + 2 * y + c
    local = pltpu.make_async_copy(src_of(me), dst_of(me), lsem)
    sends, recvs = [], []
    for p in range(1, N_DEV):
        dev, idx = _peer(p)
        sends.append(pltpu.make_async_remote_copy(src_ref=src_of(idx), dst_ref=dst_of(me), send_sem=ssem(p),
                                                  recv_sem=rsem(p), device_id=dev,
                                                  device_id_type=pl.DeviceIdType.MESH))
        if with_recvs:
            recvs.append(pltpu.make_async_remote_copy(src_ref=src_of(idx), dst_ref=dst_of(idx), send_sem=ssem(p),
                                                      recv_sem=rsem(p), device_id=dev,
                                                      device_id_type=pl.DeviceIdType.MESH))
    return local, sends, recvs


def _exchange_start(*args):
    local, sends, _ = _exchange_copies(*args, with_recvs=False)
    local.start()
    for cp in sends:
        cp.start()


def _exchange_wait(*args):
    local, sends, recvs = _exchange_copies(*args, with_recvs=True)
    for cp in recvs:
        cp.wait_recv()
    for cp in sends:
        cp.wait_send()
    local.wait()


def _exchange(src_of, dst_of, send_sems, recv_sems, local_sem):
    args = (src_of, dst_of, lambda p: send_sems.at[p], lambda p: recv_sems.at[p], local_sem)
    _exchange_start(*args)
    _exchange_wait(*args)


def _fused_exchanges(pairs, specs, send_sems, recv_sems, local_sems, first=None, last=None):
    def args(j):
        src, dst = pairs[j]
        kind, axis, r, c = specs[j]
        if kind == "ag":
            src_of = lambda idx: src
            dst_of = lambda idx: _block_view(dst, axis, idx, r, c)
        else:
            src_of = lambda idx: _block_view(src, axis, idx, r, c)
            dst_of = lambda idx: dst.at[idx]
        return (src_of, dst_of, lambda p: send_sems.at[j, p], lambda p: recv_sems.at[j, p], local_sems.at[j])

    if first is not None:
        @pl.when(first)
        def _():
            for j in range(len(pairs)):
                _exchange_start(*args(j))

    if last is not None:
        @pl.when(last)
        def _():
            for j in range(len(pairs)):
                _exchange_wait(*args(j))


def _exchange_out_shapes(arrs, specs):
    out = []
    for a, (kind, axis, r, c) in zip(arrs, specs):
        if kind == "rs":
            out.append(jax.ShapeDtypeStruct((N_DEV, r, c), a.dtype))
        else:
            out.append(jax.ShapeDtypeStruct((N_DEV * r, c) if axis == 0 else (r, N_DEV * c), a.dtype))
    return out


def _exchange_specs(kind, arrs, axes):
    specs = []
    for a, axis in zip(arrs, axes):
        if kind == "ag":
            r, c = a.shape
        elif axis == 0:
            r, c = a.shape[0] // N_DEV, a.shape[1]
        else:
            r, c = a.shape[0], a.shape[1] // N_DEV
        specs.append((kind, axis, r, c))
    return specs


def _exchange_sems(n):
    return [pltpu.SemaphoreType.DMA((n, N_DEV)), pltpu.SemaphoreType.DMA((n, N_DEV)), pltpu.SemaphoreType.DMA((n,))]


_SEMS = [pltpu.SemaphoreType.DMA((N_DEV,)), pltpu.SemaphoreType.DMA((N_DEV,)), pltpu.SemaphoreType.DMA]
_ANY = pl.BlockSpec(memory_space=pl.ANY)


def all_gather(x, axis, name):
    r, c = x.shape
    shape = (N_DEV * r, c) if axis == 0 else (r, N_DEV * c)

    def body(x_ref, o_ref, send_sems, recv_sems, local_sem):
        _exchange(lambda idx: x_ref, lambda idx: _block_view(o_ref, axis, idx, r, c), send_sems, recv_sems, local_sem)

    return pl.pallas_call(
        body, name=name, in_specs=[_ANY], out_specs=_ANY,
        out_shape=jax.ShapeDtypeStruct(shape, x.dtype), scratch_shapes=_SEMS,
    )(x)


def all_gather_stack(x, name):
    r, c = x.shape

    def body(x_ref, o_ref, send_sems, recv_sems, local_sem):
        _exchange(lambda idx: x_ref, lambda idx: o_ref.at[idx], send_sems, recv_sems, local_sem)

    return pl.pallas_call(
        body, name=name, in_specs=[_ANY], out_specs=_ANY,
        out_shape=jax.ShapeDtypeStruct((N_DEV, r, c), x.dtype), scratch_shapes=_SEMS,
    )(x)


def reduce_scatter_exchange(g, axis, name):
    if axis is None:
        _, r, c = g.shape
    elif axis == 0:
        r, c = g.shape[0] // N_DEV, g.shape[1]
    else:
        r, c = g.shape[0], g.shape[1] // N_DEV

    def body(g_ref, o_ref, send_sems, recv_sems, local_sem):
        _exchange(lambda idx: _block_view(g_ref, axis, idx, r, c), lambda idx: o_ref.at[idx],
                  send_sems, recv_sems, local_sem)

    return pl.pallas_call(
        body, name=name, in_specs=[_ANY], out_specs=_ANY,
        out_shape=jax.ShapeDtypeStruct((N_DEV, r, c), g.dtype), scratch_shapes=_SEMS,
    )(g)


PACK_QUANTUM = 16 * LANES


def _pack(arrs, dtype=F32, lead=0):
    keep = arrs[0].shape[:lead]
    flat = jnp.concatenate([a.reshape(keep + (-1,)).astype(dtype) for a in arrs], axis=-1)
    pad = (-flat.shape[-1]) % PACK_QUANTUM
    flat = jnp.pad(flat, ((0, 0),) * lead + ((0, pad),))
    return flat.reshape(keep + (-1, LANES))


def _unpack(flat2d, shapes, lead=()):
    flat = flat2d.reshape(lead + (-1,))
    out, off = [], 0
    for s in shapes:
        n = int(np.prod(s))
        out.append(flat[..., off:off + n].reshape(lead + tuple(s)))
        off += n
    return out


def _gather_lastdim(stk):
    return jnp.moveaxis(stk, 0, -2).reshape(stk.shape[1:-1] + (N_DEV * stk.shape[-1],))


def _gather_dim(stk, dim):
    moved = jnp.moveaxis(stk, 0, dim)
    sh = list(stk.shape[1:])
    sh[dim] = sh[dim] * N_DEV
    return moved.reshape(sh)


def _scatter_dim(full, dim):
    sh = list(full.shape)
    sh[dim:dim + 1] = [N_DEV, sh[dim] // N_DEV]
    return jnp.moveaxis(full.reshape(sh), dim, 0)


def _head_group_matrix(tc):
    return np.kron(np.eye(tc // HEAD, dtype=np.float32), np.ones((HEAD, HEAD), np.float32))


_SCAN_COMM = {0: ("w13_0", "w13_1"), 1: ("win", "w2_0", "w2_1", "wout", "wo")}
_W_AXIS = dict(wr=0, wk=0, wv=0, wo=0, win=1, wout=0, w13_0=1, w13_1=1, w2_0=0, w2_1=0)


def _build_forward(ctx2d, T, D, shards=None, sink=None):
    L = ctx2d.shape[0]
    N = L + T

    def make_scan(d):
        keys = _SCAN_COMM[d] if shards is not None else ()
        axes = tuple(_W_AXIS[k] for k in keys)

        def run_fwd(tok, sh):
            return _wkv_fwd_call(*tok, L, d == 1, tuple(sh), _exchange_specs("ag", sh, axes))

        @jax.custom_vjp
        def op(tok, sh):
            outs = run_fwd(tok, sh)
            return (outs[0],) + tuple(outs[3:])

        def op_fwd(tok, sh):
            outs = run_fwd(tok, sh)
            return (outs[0],) + tuple(outs[3:]), (tok, outs[1], outs[2])

        def op_bwd(res, cts):
            tok, sa, sprev = res
            dg = tuple(cts[1:])
            outs = _wkv_bwd_call(*tok, sa, sprev, cts[0], L, d == 1, dg, _exchange_specs("rs", dg, axes))
            for k, recv in zip(keys, outs[6:]):
                sink[k] = recv
            return tuple(outs[:6]), tuple(jnp.zeros(shards[k].shape, shards[k].dtype) for k in keys)

        op.defvjp(op_fwd, op_bwd)
        return op, keys

    scans = [make_scan(0), make_scan(1)]
    tc_head = _pick(D, 2 * LANES)
    gm = _head_group_matrix(tc_head)
    tr_row = _pick(math.gcd(L, T), 128, 8)
    op_norm = make_rowwise("norm_mod", _f_norm_mod, (F32,), tr_row, D, nb0=L // tr_row)
    op_res = [make_rowwise(f"res_norm_mod{i}", _f_res_norm_mod, (F32, BF16), _pick(T, 128, 8), D) for i in range(3)]
    op_prep = make_rowwise("wkv_prep", _f_prep, (F32,) * 7, _pick(N, 256, 8), tc_head, consts=(gm,))
    op_read = make_rowwise("wkv_readout", _f_readout, (BF16,), _pick(T, 256, 8), tc_head, consts=(gm,))

    def v3(a):
        return a.reshape(a.shape[0], 1, a.shape[-1])

    def fwd(xin, Ps, Wb):
        modx, modc = Ps["modx"], Ps["modc"]
        cat = jnp.concatenate([ctx2d, xin], axis=0)
        seg = lambda a, b: jnp.stack([a, b])[:, None, :]
        (hcat,) = op_norm((cat,), (Ps["n1"][0][None, None, :], seg(modc[0], modx[0, 0]), seg(modc[1], modx[0, 1])))
        xr, xw, xk, xv, xa, xg = shift_mix(hcat, Ps["mix"][:, None, :], L)
        r = linear(xr, Wb["wr"], F32, "wr")
        k = linear(xk, Wb["wk"], F32, "wk")
        v = linear(xv, Wb["wv"], F32, "wv")
        gl = jax.nn.sigmoid(linear(xg, Wb["g1"], F32, "g1"))
        g = linear(gl.astype(BF16), Wb["g2"], F32, "g2")
        tw = jnp.tanh(linear(xw, Wb["w1"], F32, "w1")).astype(BF16)
        ta = linear(xa, Wb["a1"], F32, "a1").astype(BF16)
        lw = [linear(tw[:, LORA_PAD * d:LORA_PAD * (d + 1)], Wb["w2d"][d], F32, f"w2_{d}") for d in range(2)]
        la = [linear(ta[:, LORA_PAD * d:LORA_PAD * (d + 1)], Wb["a2d"][d], F32, f"a2_{d}") for d in range(2)]
        kk, dec0, dec1, kd0, kd1, as0, as1 = op_prep(
            (k, lw[0], lw[1], la[0], la[1]),
            (v3(Ps["kk"]), v3(Ps["ka"]), Ps["w0"][0][None, None, :], Ps["w0"][1][None, None, :],
             Ps["a0"][0][None, None, :], Ps["a0"][1][None, None, :]))
        r2, kk2, v2 = _tile_heads(r), _tile_heads(kk), _tile_heads(v)
        ys = []
        Wb = dict(Wb)
        for d, (dec, kd, sg) in enumerate(((dec0, kd0, as0), (dec1, kd1, as1))):
            op, keys = scans[d]
            outs = op((r2, _tile_heads(dec), _tile_heads(kd), kk2, _tile_heads(sg), v2),
                      tuple(shards[k] for k in keys))
            yx = outs[0][L:]
            ys.append((yx[:, :, :HEAD] + yx[:, :, HEAD:]).reshape(T, D))
            Wb.update(zip(keys, outs[1:]))
        (o,) = op_read((ys[0], ys[1], r[L:], kd0[L:], kd1[L:], v[L:], g[L:]),
                       (v3(Ps["rk"]), v3(Ps["lnw"]), v3(Ps["lnb"])))
        att = linear(o, Wb["wo"], F32, "wo")
        x1, h2 = op_res[0]((xin, att), (modx[0, 2][None, None, :], Ps["n2"][0][None, None, :],
                                        modx[0, 3][None, None, :], modx[0, 4][None, None, :]))
        act = swiglu_act(linear(h2, Wb["w13_0"], BF16, "w13_0"), "swiglu0")
        f0 = linear(act, Wb["w2_0"], F32, "w2_0")
        x2, h = op_res[1]((x1, f0), (modx[0, 5][None, None, :], Ps["n1"][1][None, None, :],
                                     modx[1, 0][None, None, :], modx[1, 1][None, None, :]))
        guc = linear(h, Wb["win"], BF16, "win")
        p = gated_conv(guc, Ps["conv"][:, None, :])
        cv = linear(p, Wb["wout"], F32, "wout")
        x3, h2b = op_res[2]((x2, cv), (modx[1, 2][None, None, :], Ps["n2"][1][None, None, :],
                                       modx[1, 3][None, None, :], modx[1, 4][None, None, :]))
        act1 = swiglu_act(linear(h2b, Wb["w13_1"], BF16, "w13_1"), "swiglu1")
        f1 = linear(act1, Wb["w2_1"], F32, "w2_1")
        return x3, f1


    return fwd


def kernel(x, c, ctx, c_ctx, norm1_g, norm2_g, ada_w, ada_b, rw_mix, rw_wr, rw_wk, rw_wv, rw_wo, rw_w0, rw_w1, rw_w2, rw_a0, rw_a1, rw_a2, rw_g1, rw_g2, rw_kk, rw_ka, rw_rk, rw_lnw, rw_lnb, sc_win, sc_conv, sc_wout, ffn_w13, ffn_w2, final_g, loss_target, m_c_ctx, m_norm1_g, m_norm2_g, m_ada_w, m_ada_b, m_rw_mix, m_rw_wr, m_rw_wk, m_rw_wv, m_rw_wo, m_rw_w0, m_rw_w1, m_rw_w2, m_rw_a0, m_rw_a1, m_rw_a2, m_rw_g1, m_rw_g2, m_rw_kk, m_rw_ka, m_rw_rk, m_rw_lnw, m_rw_lnb, m_sc_win, m_sc_conv, m_sc_wout, m_ffn_w13, m_ffn_w2, m_final_g, v_c_ctx, v_norm1_g, v_norm2_g, v_ada_w, v_ada_b, v_rw_mix, v_rw_wr, v_rw_wk, v_rw_wv, v_rw_wo, v_rw_w0, v_rw_w1, v_rw_w2, v_rw_a0, v_rw_a1, v_rw_a2, v_rw_g1, v_rw_g2, v_rw_kk, v_rw_ka, v_rw_rk, v_rw_lnw, v_rw_lnb, v_sc_win, v_sc_conv, v_sc_wout, v_ffn_w13, v_ffn_w2, v_final_g):
    W = dict(c_ctx=c_ctx, norm1_g=norm1_g, norm2_g=norm2_g, ada_w=ada_w, ada_b=ada_b, rw_mix=rw_mix, rw_wr=rw_wr,
             rw_wk=rw_wk, rw_wv=rw_wv, rw_wo=rw_wo, rw_w0=rw_w0, rw_w1=rw_w1, rw_w2=rw_w2, rw_a0=rw_a0, rw_a1=rw_a1,
             rw_a2=rw_a2, rw_g1=rw_g1, rw_g2=rw_g2, rw_kk=rw_kk, rw_ka=rw_ka, rw_rk=rw_rk, rw_lnw=rw_lnw,
             rw_lnb=rw_lnb, sc_win=sc_win, sc_conv=sc_conv, sc_wout=sc_wout, ffn_w13=ffn_w13, ffn_w2=ffn_w2,
             final_g=final_g)
    Mo = dict(c_ctx=m_c_ctx, norm1_g=m_norm1_g, norm2_g=m_norm2_g, ada_w=m_ada_w, ada_b=m_ada_b, rw_mix=m_rw_mix,
              rw_wr=m_rw_wr, rw_wk=m_rw_wk, rw_wv=m_rw_wv, rw_wo=m_rw_wo, rw_w0=m_rw_w0, rw_w1=m_rw_w1,
              rw_w2=m_rw_w2, rw_a0=m_rw_a0, rw_a1=m_rw_a1, rw_a2=m_rw_a2, rw_g1=m_rw_g1, rw_g2=m_rw_g2,
              rw_kk=m_rw_kk, rw_ka=m_rw_ka, rw_rk=m_rw_rk, rw_lnw=m_rw_lnw, rw_lnb=m_rw_lnb, sc_win=m_sc_win,
              sc_conv=m_sc_conv, sc_wout=m_sc_wout, ffn_w13=m_ffn_w13, ffn_w2=m_ffn_w2, final_g=m_final_g)
    Vo = dict(c_ctx=v_c_ctx, norm1_g=v_norm1_g, norm2_g=v_norm2_g, ada_w=v_ada_w, ada_b=v_ada_b, rw_mix=v_rw_mix,
              rw_wr=v_rw_wr, rw_wk=v_rw_wk, rw_wv=v_rw_wv, rw_wo=v_rw_wo, rw_w0=v_rw_w0, rw_w1=v_rw_w1,
              rw_w2=v_rw_w2, rw_a0=v_rw_a0, rw_a1=v_rw_a1, rw_a2=v_rw_a2, rw_g1=v_rw_g1, rw_g2=v_rw_g2,
              rw_kk=v_rw_kk, rw_ka=v_rw_ka, rw_rk=v_rw_rk, rw_lnw=v_rw_lnw, rw_lnb=v_rw_lnb, sc_win=v_sc_win,
              sc_conv=v_sc_conv, sc_wout=v_sc_wout, ffn_w13=v_ffn_w13, ffn_w2=v_ffn_w2, final_g=v_final_g)
    names = list(W)

    x2d = x[0]
    ctx2d = ctx[0]
    tgt = loss_target[0]
    T, D = x2d.shape
    L = ctx2d.shape[0]
    N = L + T
    nh = D // HEAD
    mx, my, mc = _me()
    me = 4 * mx + 2 * my + mc
    dloc = D // N_DEV

    lr = rw_w1.shape[-1]
    pad_r = LORA_PAD - lr
    w1p = jnp.pad(rw_w1[0], ((0, 0), (0, 0), (0, pad_r)))
    a1p = jnp.pad(rw_a1[0], ((0, 0), (0, 0), (0, pad_r)))
    w2p = jnp.pad(rw_w2[0], ((0, 0), (0, pad_r), (0, 0)))
    a2p = jnp.pad(rw_a2[0], ((0, 0), (0, pad_r), (0, 0)))
    small_loc = [rw_mix[0], rw_w0[0], rw_a0[0], sc_conv[0], w1p, a1p, w2p, a2p, rw_g1[0], rw_g2[0]]
    small_dim = [1, 1, 1, 1, 1, 1, 2, 2, 0, 1]
    small_shapes = [a.shape for a in small_loc]
    small_groups = ((slice(0, 4), F32, "vec"), (slice(4, 10), BF16, "mat"))
    small_full = []
    for sl, dt, tag in small_groups:
        sm_all = all_gather_stack(_pack(small_loc[sl], dt), "ag_small_" + tag)
        sm_parts = _unpack(sm_all, small_shapes[sl], lead=(N_DEV,))
        small_full += [_gather_dim(p, dm) for p, dm in zip(sm_parts, small_dim[sl])]
    mix_f, w0_f, a0_f, conv_f, w1_f, a1_f, w2_f, a2_f, g1_f, g2_f = small_full

    c_all = all_gather_stack(jnp.pad(c, ((0, 7), (0, 0))), "ag_c")[:, 0, :]
    cond_pre = jnp.concatenate([c_all, c_ctx[None, :], jnp.zeros((7, D), F32)], axis=0)
    cond_rows = jax.nn.silu(cond_pre)
    ncol = ada_w.shape[-1]
    mod_loc = []
    for i in range(2):
        bi = lax.dynamic_slice(ada_b[i], (me * ncol,), (ncol,))
        mod_loc.append(_mm(cond_rows, ada_w[i], out_dtype=F32, name=f"ada_fwd{i}") + bi[None, :])
    mod_all = all_gather_stack(jnp.concatenate(mod_loc, axis=0), "ag_mod")
    mod_full = _gather_lastdim(mod_all).reshape(2, 16, 6, D)
    mod_x = lax.dynamic_index_in_dim(mod_full, me, axis=1, keepdims=False)
    mod_c = mod_full[0, 8, :2, :]

    def ag_w(wl, axis, name):
        return all_gather(wl.astype(BF16), axis, name)

    shards = dict(wo=rw_wo[0], win=sc_win[0], wout=sc_wout[0], w13_0=ffn_w13[0], w13_1=ffn_w13[1],
                  w2_0=ffn_w2[0], w2_1=ffn_w2[1])
    shards = {k_: a.astype(BF16) for k_, a in shards.items()}
    sink = {}
    Wb = dict(
        wr=ag_w(rw_wr[0], 0, "ag_wr"), wk=ag_w(rw_wk[0], 0, "ag_wk"), wv=ag_w(rw_wv[0], 0, "ag_wv"),
        w1=jnp.concatenate([w1_f[0], w1_f[1]], axis=1).astype(BF16),
        a1=jnp.concatenate([a1_f[0], a1_f[1]], axis=1).astype(BF16),
        w2d=w2_f.astype(BF16), a2d=a2_f.astype(BF16),
        g1=g1_f.astype(BF16), g2=g2_f.astype(BF16),
    )
    Ps = dict(n1=norm1_g, n2=norm2_g, modx=mod_x, modc=mod_c, mix=mix_f, w0=w0_f, a0=a0_f, conv=conv_f,
              kk=rw_kk, ka=rw_ka, rk=rw_rk.reshape(1, D), lnw=rw_lnw, lnb=rw_lnb)

    fwd = _build_forward(ctx2d, T, D, shards, sink)
    (x3, f1), vjp_fn = jax.vjp(fwd, x2d, Ps, Wb)
    loss_acc, dx3, df1, dgate, dfinal = loss_head(x3, f1, tgt, mod_x[1, 5][None, :], final_g[None, :])
    dx, dPs, dWb = vjp_fn((dx3, df1))
    loss = lax.psum(loss_acc[0, 0], ("x", "y", "c"))

    dmodx = dPs["modx"].at[1, 5].add(dgate[0])
    dmodc = jnp.concatenate([dPs["modc"], jnp.zeros((4, D), F32)], axis=0)
    drow = jnp.stack([dmodx.reshape(2, 6 * D), jnp.stack([dmodc.reshape(6 * D), jnp.zeros((6 * D,), F32)])], axis=1)
    drow_all = all_gather_stack(drow.reshape(4, 6 * D), "ag_dmod").reshape(N_DEV, 2, 2, 6 * D)
    dctx_tot = drow_all[0, :, 1, :]
    for s in range(1, N_DEV):
        dctx_tot = dctx_tot + drow_all[s, :, 1, :]
    dmod_rows = jnp.concatenate([jnp.moveaxis(drow_all[:, :, 0, :], 0, 1), dctx_tot[:, None, :],
                                 jnp.zeros((2, 7, 6 * D), F32)], axis=1)
    grad_ada_b = dctx_tot
    for s in range(N_DEV):
        grad_ada_b = grad_ada_b + drow_all[s, :, 0, :]
    dmod_mine = lax.dynamic_slice_in_dim(dmod_rows, me * ncol, ncol, axis=2)
    g_ada_w = [_mm(cond_rows, dmod_mine[i], ta=True, out_dtype=F32, name=f"ada_dw{i}") for i in range(2)]
    dcond_part = _mm(dmod_mine[0], ada_w[0], tb=True, out_dtype=F32, name="ada_dcond")[8]

    rep_names = ["c_ctx", "norm1_g", "norm2_g", "rw_kk", "rw_ka", "rw_rk", "rw_lnw", "rw_lnb", "final_g"]
    rep_part = [dcond_part, dPs["n1"], dPs["n2"], dPs["kk"], dPs["ka"], dPs["rk"].reshape(W["rw_rk"].shape),
                dPs["lnw"], dPs["lnb"], dfinal[0]]
    rep_shapes = [W[n_].shape for n_ in rep_names]
    rep_all = all_gather_stack(_pack(rep_part), "ag_rep_grads")
    sg = jax.nn.sigmoid(c_ctx)
    dsilu = sg * (1.0 + c_ctx * (1.0 - sg))
    rep_scale = _pack([dsilu] + [jnp.ones(s, F32) for s in rep_shapes[1:]])
    rep_all = rep_all * rep_scale[None]
    rep_w = _pack([W[n_] for n_ in rep_names])
    rep_m = _pack([Mo[n_] for n_ in rep_names])
    rep_v = _pack([Vo[n_] for n_ in rep_names])
    rep_out = sum_adam(rep_all, rep_w, rep_m, rep_v, "adam_rep")
    results = {}
    for nm_, vals in zip(rep_names, zip(*[_unpack(o, rep_shapes) for o in rep_out])):
        results[nm_] = vals

    results["ada_b"] = tuple(sum_adam(grad_ada_b.reshape(1, 2 * 6, D), ada_b.reshape(12, D), m_ada_b.reshape(12, D),
                                      v_ada_b.reshape(12, D), "adam_ada_b"))
    results["ada_b"] = tuple(o.reshape(ada_b.shape) for o in results["ada_b"])

    outs = [sum_adam(g_ada_w[i][None], ada_w, m_ada_w, v_ada_w, f"adam_ada_w{i}", lead=i) for i in range(2)]
    results["ada_w"] = tuple(jnp.stack([outs[0][q], outs[1][q]]) for q in range(4))

    dw1 = jnp.stack([dWb["w1"][:, :LORA_PAD], dWb["w1"][:, LORA_PAD:]])
    da1 = jnp.stack([dWb["a1"][:, :LORA_PAD], dWb["a1"][:, LORA_PAD:]])
    small_g = [dPs["mix"], dPs["w0"], dPs["a0"], dPs["conv"], dw1, da1, dWb["w2d"], dWb["a2d"], dWb["g1"], dWb["g2"]]
    small_names = ["rw_mix", "rw_w0", "rw_a0", "sc_conv", "rw_w1", "rw_a1", "rw_w2", "rw_a2", "rw_g1", "rw_g2"]

    def padded_local(nm_, src):
        a = src[nm_][0]
        if nm_ in ("rw_w1", "rw_a1"):
            return jnp.pad(a, ((0, 0), (0, 0), (0, pad_r)))
        if nm_ in ("rw_w2", "rw_a2"):
            return jnp.pad(a, ((0, 0), (0, pad_r), (0, 0)))
        return a

    for sl, dt, tag in small_groups:
        blocks = [_scatter_dim(gf, dm) for gf, dm in zip(small_g[sl], small_dim[sl])]
        sm_recv = reduce_scatter_exchange(_pack(blocks, dt, lead=1), None, "rs_small_" + tag)
        sm_out = sum_adam(sm_recv, _pack([padded_local(n_, W) for n_ in small_names[sl]]),
                          _pack([padded_local(n_, Mo) for n_ in small_names[sl]]),
                          _pack([padded_local(n_, Vo) for n_ in small_names[sl]]), "adam_small_" + tag)
        for nm_, vals in zip(small_names[sl], zip(*[_unpack(o, small_shapes[sl]) for o in sm_out])):
            if nm_ in ("rw_w1", "rw_a1"):
                vals = tuple(a[:, :, :lr] for a in vals)
            if nm_ in ("rw_w2", "rw_a2"):
                vals = tuple(a[:, :lr, :] for a in vals)
            results[nm_] = tuple(a[None] for a in vals)

    def rs_adam(key, nm_, lead):
        recv = sink[key] if key in sink else reduce_scatter_exchange(dWb[key], _W_AXIS[key], "rs_" + key)
        return sum_adam(recv, W[nm_], Mo[nm_], Vo[nm_], "adam_" + key, lead=lead)

    for nm_, key in (("rw_wr", "wr"), ("rw_wk", "wk"), ("rw_wv", "wv"), ("rw_wo", "wo"), ("sc_win", "win"),
                     ("sc_wout", "wout")):
        results[nm_] = tuple(a[None] for a in rs_adam(key, nm_, 0))
    for nm_, key in (("ffn_w13", "w13"), ("ffn_w2", "w2")):
        outs = [rs_adam(f"{key}_{i}", nm_, i) for i in range(2)]
        results[nm_] = tuple(jnp.stack([outs[0][q], outs[1][q]]) for q in range(4))

    grads = [results[n_][0] for n_ in names]
    deltas = [results[n_][1] for n_ in names]
    new_m = [results[n_][2] for n_ in names]
    new_v = [results[n_][3] for n_ in names]
    return (loss, dx[None], *grads, *deltas, *new_m, *new_v)
```

```python
import functools
import math

import numpy as np
import jax
import jax.numpy as jnp
from jax import lax
from jax.experimental import pallas as pl
from jax.experimental.pallas import tpu as pltpu

F32 = jnp.float32
BF16 = jnp.bfloat16

N_DEV = 8
HEAD = 64
LANES = 128
GRID_W = 64
LORA_PAD = 128
NORM_EPS = 1e-6
GN_EPS = 64e-5
ADAM_LR, ADAM_B1, ADAM_B2, ADAM_EPS, ADAM_WD, ADAM_STEP = 0.001, 0.9, 0.999, 1e-08, 0.01, 10
VMEM_LIMIT = 52 * 1024 * 1024
SCAN_CHUNK = 8
SCAN_UNROLL = 4
HI = lax.Precision.HIGHEST


def _cparams(sem):
    return pltpu.CompilerParams(dimension_semantics=sem, vmem_limit_bytes=VMEM_LIMIT)


def _pick(n, cap, quantum=LANES):
    best = None
    for t in range(quantum, min(n, cap) + 1, quantum):
        if n % t == 0:
            best = t
    return n if best is None else best


MM_VMEM_BUDGET = 36 * 1024 * 1024


def _divisors(n, cap, quantum=LANES):
    ds = [t for t in range(quantum, min(n, cap) + 1, quantum) if n % t == 0]
    return sorted(ds, reverse=True) or [n]


def _mm_tiles(M, N, K, sa, sb, so):
    tms, tns, tks = _divisors(M, 1024), _divisors(N, 1024), _divisors(K, 2816)
    im = jn = ik = 0

    def est(tm, tn, tk):
        b = 2 * (tm * tk * sa + tk * tn * sb) + 2 * tm * tn * so + tm * tn * 4
        b += tm * tk * 2 if sa == 4 else 0
        b += tk * tn * 2 if sb == 4 else 0
        return b + (tm * tn * 4 if tk < K else 0)

    while est(tms[im], tns[jn], tks[ik]) > MM_VMEM_BUDGET:
        if tms[im] >= tns[jn] and im + 1 < len(tms):
            im += 1
        elif jn + 1 < len(tns):
            jn += 1
        elif im + 1 < len(tms):
            im += 1
        elif ik + 1 < len(tks):
            ik += 1
        else:
            break
    return tms[im], tns[jn], tks[ik]


def _mm(a, b, *, ta=False, tb=False, out_dtype, name):
    if ta:
        K, M = a.shape
    else:
        M, K = a.shape
    if tb:
        N, Kb = b.shape
    else:
        Kb, N = b.shape
    assert K == Kb, (a.shape, b.shape, ta, tb)
    tm, tn, tk = _mm_tiles(M, N, K, a.dtype.itemsize, b.dtype.itemsize, jnp.dtype(out_dtype).itemsize)
    nk = K // tk
    dims = (((0 if ta else 1,), (1 if tb else 0,)), ((), ()))

    def body(a_ref, b_ref, o_ref, *acc):
        part = lax.dot_general(a_ref[...].astype(BF16), b_ref[...].astype(BF16), dims, preferred_element_type=F32)
        if nk == 1:
            o_ref[...] = part.astype(o_ref.dtype)
            return
        acc_ref, = acc
        k = pl.program_id(2)

        @pl.when(k == 0)
        def _():
            acc_ref[...] = part

        @pl.when(k > 0)
        def _():
            acc_ref[...] += part

        @pl.when(k == nk - 1)
        def _():
            o_ref[...] = acc_ref[...].astype(o_ref.dtype)

    a_spec = pl.BlockSpec((tk, tm), lambda i, j, k: (k, i)) if ta else pl.BlockSpec((tm, tk), lambda i, j, k: (i, k))
    b_spec = pl.BlockSpec((tn, tk), lambda i, j, k: (j, k)) if tb else pl.BlockSpec((tk, tn), lambda i, j, k: (k, j))
    return pl.pallas_call(
        body, name=name, grid=(M // tm, N // tn, nk),
        in_specs=[a_spec, b_spec],
        out_specs=pl.BlockSpec((tm, tn), lambda i, j, k: (i, j)),
        out_shape=jax.ShapeDtypeStruct((M, N), out_dtype),
        scratch_shapes=[pltpu.VMEM((tm, tn), F32)] if nk > 1 else [],
        compiler_params=_cparams(("parallel", "parallel", "arbitrary")),
    )(a, b)


@functools.partial(jax.custom_vjp, nondiff_argnums=(2, 3))
def linear(a, w, out_dtype, name):
    return _mm(a, w, out_dtype=out_dtype, name=name + "_fwd")


def _linear_fwd(a, w, out_dtype, name):
    return _mm(a, w, out_dtype=out_dtype, name=name + "_fwd"), (a, w)


def _linear_bwd(out_dtype, name, res, g):
    a, w = res
    da = _mm(g, w, tb=True, out_dtype=a.dtype, name=name + "_da")
    dw = _mm(a, g, ta=True, out_dtype=w.dtype, name=name + "_dw")
    return da, dw


linear.defvjp(_linear_fwd, _linear_bwd)


def _rw_specs(tiles, col_offs, vecs, consts, tr, tc, nb0):
    tile_specs = [pl.BlockSpec((tr, tc), functools.partial(lambda j, i, off: (i, j + off), off=off))
                  for _, off in zip(tiles, col_offs)]

    def vec_map(S):
        if S == 1:
            return lambda j, i: (0, 0, j)
        return lambda j, i: (jnp.where(i < nb0, 0, 1), 0, j)

    vec_specs = [pl.BlockSpec((None, 1, tc), vec_map(v.shape[0])) for v in vecs]
    const_specs = [pl.BlockSpec(c.shape, lambda j, i: (0, 0)) for c in consts]
    return tile_specs, vec_specs, const_specs


def _rw_forward(name, f, tiles, col_offs, vecs, consts, out_dtypes, tr, tc, nb0, width):
    n = tiles[0].shape[0]
    nt, nv, nc = len(tiles), len(vecs), len(consts)
    tile_specs, vec_specs, const_specs = _rw_specs(tiles, col_offs, vecs, consts, tr, tc, nb0)

    def body(*refs):
        ins = [r[...].astype(F32) for r in refs[:nt]] + [r[...] for r in refs[nt:nt + nv + nc]]
        outs = f(*ins)
        for o_ref, o in zip(refs[nt + nv + nc:], outs):
            o_ref[...] = o.astype(o_ref.dtype)

    return pl.pallas_call(
        body, name=name + "_fwd", grid=(width // tc, n // tr),
        in_specs=tile_specs + vec_specs + const_specs,
        out_specs=[pl.BlockSpec((tr, tc), lambda j, i: (i, j)) for _ in out_dtypes],
        out_shape=[jax.ShapeDtypeStruct((n, width), dt) for dt in out_dtypes],
        compiler_params=_cparams(("parallel", "parallel")),
    )(*tiles, *vecs, *consts)


def _rw_backward(name, f, tiles, col_offs, vecs, consts, douts, tr, tc, nb0, width):
    n = tiles[0].shape[0]
    nt, nv, nc, no = len(tiles), len(vecs), len(consts), len(douts)
    tile_specs, vec_specs, const_specs = _rw_specs(tiles, col_offs, vecs, consts, tr, tc, nb0)

    def body(*refs):
        t_in = [r[...].astype(F32) for r in refs[:nt]]
        v_in = [r[...] for r in refs[nt:nt + nv]]
        c_in = [r[...] for r in refs[nt + nv:nt + nv + nc]]
        d_in = tuple(r[...].astype(F32) for r in refs[nt + nv + nc:nt + nv + nc + no])
        o_refs = refs[nt + nv + nc + no:]
        _, vjp = jax.vjp(lambda *tv: tuple(f(*tv, *c_in)), *t_in, *v_in)
        grads = vjp(d_in)
        for o_ref, g in zip(o_refs[:nt], grads[:nt]):
            o_ref[...] = g.astype(o_ref.dtype)
        i = pl.program_id(1)
        for o_ref, g, v in zip(o_refs[nt:], grads[nt:], vecs):
            first = jnp.logical_or(i == 0, i == nb0) if v.shape[0] == 2 else i == 0

            @pl.when(first)
            def _(o_ref=o_ref, g=g):
                o_ref[...] = g

            @pl.when(jnp.logical_not(first))
            def _(o_ref=o_ref, g=g):
                o_ref[...] += g

    dout_specs = [pl.BlockSpec((tr, tc), lambda j, i: (i, j)) for _ in douts]
    out_specs = [pl.BlockSpec((tr, tc), lambda j, i: (i, j)) for _ in tiles] + list(vec_specs)
    out_shape = ([jax.ShapeDtypeStruct((n, width), t.dtype) for t in tiles]
                 + [jax.ShapeDtypeStruct(v.shape, F32) for v in vecs])
    return pl.pallas_call(
        body, name=name + "_bwd", grid=(width // tc, n // tr),
        in_specs=tile_specs + vec_specs + const_specs + dout_specs,
        out_specs=out_specs, out_shape=out_shape,
        compiler_params=_cparams(("parallel", "arbitrary")),
    )(*tiles, *vecs, *consts, *douts)


def make_rowwise(name, f, out_dtypes, tr, tc, consts=(), nb0=-1):
    consts = tuple(consts)

    @jax.custom_vjp
    def op(tiles, vecs):
        w = tiles[0].shape[1]
        return tuple(_rw_forward(name, f, tiles, (0,) * len(tiles), vecs, consts, out_dtypes, tr, min(tc, w), nb0, w))

    def op_fwd(tiles, vecs):
        return op(tiles, vecs), (tiles, vecs)

    def op_bwd(res, douts):
        tiles, vecs = res
        w = tiles[0].shape[1]
        g = _rw_backward(name, f, tiles, (0,) * len(tiles), vecs, consts, tuple(douts), tr, min(tc, w), nb0, w)
        return tuple(g[:len(tiles)]), tuple(g[len(tiles):])

    op.defvjp(op_fwd, op_bwd)
    return op


def _f_norm_mod(x, g, sh, sc):
    hn = x * lax.rsqrt(jnp.mean(x * x, axis=-1, keepdims=True) + NORM_EPS)
    return ((hn * g) * (1.0 + sc) + sh,)


def _f_res_norm_mod(x, y, gate, g, sh, sc):
    x1 = x + gate * y
    hn = x1 * lax.rsqrt(jnp.mean(x1 * x1, axis=-1, keepdims=True) + NORM_EPS)
    return x1, (hn * g) * (1.0 + sc) + sh


def _head_sum_3pass(t, gmat):
    hi = t.astype(BF16)
    r1 = t - hi.astype(F32)
    mid = r1.astype(BF16)
    lo = (r1 - mid.astype(F32)).astype(BF16)
    g = gmat.astype(BF16)
    dot = lambda u: jnp.dot(u, g, preferred_element_type=F32)
    return dot(hi) + dot(mid) + dot(lo)


@jax.custom_vjp
def _head_sum(t, gmat):
    return _head_sum_3pass(t, gmat)


def _head_sum_fwd(t, gmat):
    return _head_sum_3pass(t, gmat), gmat


def _head_sum_bwd(gmat, ct):
    return _head_sum_3pass(ct, gmat), None


_head_sum.defvjp(_head_sum_fwd, _head_sum_bwd)


def _f_prep(k, lw0, lw1, la0, la1, kkp, kap, w00, w01, a00, a01, gmat):
    t = k * kkp
    kk = t / jnp.maximum(jnp.sqrt(_head_sum(t * t, gmat)), 1e-12)
    outs = [kk]
    decs, kds, sigs = [], [], []
    for lw, la, w0, a0 in ((lw0, la0, w00, a00), (lw1, la1, w01, a01)):
        decs.append(jnp.exp(-jax.nn.sigmoid(w0 + lw) * float(np.exp(-0.5))))
        a = jax.nn.sigmoid(a0 + la)
        sigs.append(a)
        kds.append(k * (1.0 + (a - 1.0) * kap))
    return tuple(outs + decs + kds + sigs)


def _f_readout(y0, y1, r, kd0, kd1, v, g, rk, lnw, lnb, gmat):
    y = y0 + y1
    mu = _head_sum(y, gmat) * (1.0 / HEAD)
    d = y - mu
    var = _head_sum(d * d, gmat) * (1.0 / HEAD)
    o = d * lax.rsqrt(var + GN_EPS) * lnw + lnb
    bonus = _head_sum(r * (kd0 + kd1) * rk, gmat) * v
    return ((o + bonus) * g,)


def _f_swiglu(a, b):
    return (jax.nn.silu(a) * b,)


def swiglu_act(ab, name):
    t, f2 = ab.shape
    fdim = f2 // 2
    tr, tc = _pick(t, 512, 8), _pick(fdim, 512)
    offs = (0, fdim // tc)

    @jax.custom_vjp
    def op(ab_):
        return _rw_forward(name, _f_swiglu, (ab_, ab_), offs, (), (), (BF16,), tr, tc, -1, fdim)[0]

    def op_fwd(ab_):
        return op(ab_), ab_

    def op_bwd(ab_, dact):
        da, db = _rw_backward(name, _f_swiglu, (ab_, ab_), offs, (), (), (dact,), tr, tc, -1, fdim)
        return (jnp.concatenate([da, db], axis=1),)

    op.defvjp(op_fwd, op_bwd)
    return op(ab)


def _row_iota(n, tc):
    return lax.broadcasted_iota(jnp.int32, (n, tc), 0)


def _shift_rows(x, s, keep):
    n = x.shape[0]
    return jnp.where(keep, pltpu.roll(x, s % n, 0), 0.0)


def _unshift_rows(d, s, keep):
    n = d.shape[0]
    return pltpu.roll(jnp.where(keep, d, 0.0), (-s) % n, 0)


def _ctx_shift_spec(L, tc, quarter):
    row = _row_iota(L, tc)
    if quarter < 2:
        return 1, row >= 1
    return -1, row < L - 1


def _grid_shift_spec(T, tc, quarter):
    row = _row_iota(T, tc)
    col = jnp.bitwise_and(row, GRID_W - 1)
    if quarter == 0:
        return 1, col != 0
    if quarter == 1:
        return -1, col != GRID_W - 1
    if quarter == 2:
        return GRID_W, row >= GRID_W
    return -GRID_W, row < T - GRID_W


def _shift_mix_fwd_call(h, mix3, L):
    n, d = h.shape
    T = n - L
    tc = _pick(d // 4, 256)
    nq = (d // 4) // tc

    def body(h_ref, mix_ref, *o_refs):
        q = pl.program_id(0) // nq
        for quarter in range(4):
            @pl.when(q == quarter)
            def _(quarter=quarter):
                for lo, cnt, spec in ((0, L, _ctx_shift_spec), (L, T, _grid_shift_spec)):
                    hh = h_ref[pl.ds(lo, cnt), :]
                    s, keep = spec(cnt, tc, quarter)
                    xx = _shift_rows(hh, s, keep) - hh
                    for m in range(6):
                        o_refs[m][pl.ds(lo, cnt), :] = (hh + xx * mix_ref[m]).astype(BF16)

    return pl.pallas_call(
        body, name="shift_mix_fwd", grid=(d // tc,),
        in_specs=[pl.BlockSpec((n, tc), lambda j: (0, j)), pl.BlockSpec((6, 1, tc), lambda j: (0, 0, j))],
        out_specs=[pl.BlockSpec((n, tc), lambda j: (0, j)) for _ in range(6)],
        out_shape=[jax.ShapeDtypeStruct((n, d), BF16) for _ in range(6)],
        compiler_params=_cparams(("parallel",)),
    )(h, mix3)


def _shift_mix_bwd_call(h, mix3, douts, L):
    n, d = h.shape
    T = n - L
    tc = _pick(d // 4, 256)
    nq = (d // 4) // tc

    def body(h_ref, mix_ref, d0, d1, d2, d3, d4, d5, dh_ref, dmix_ref):
        d_refs = (d0, d1, d2, d3, d4, d5)
        q = pl.program_id(0) // nq
        for quarter in range(4):
            @pl.when(q == quarter)
            def _(quarter=quarter):
                dmix = [jnp.zeros((1, tc), F32) for _ in range(6)]
                for lo, cnt, spec in ((0, L, _ctx_shift_spec), (L, T, _grid_shift_spec)):
                    hh = h_ref[pl.ds(lo, cnt), :]
                    s, keep = spec(cnt, tc, quarter)
                    xx = _shift_rows(hh, s, keep) - hh
                    direct = jnp.zeros((cnt, tc), F32)
                    shifted = jnp.zeros((cnt, tc), F32)
                    for m in range(6):
                        dm = d_refs[m][pl.ds(lo, cnt), :].astype(F32)
                        mx = mix_ref[m]
                        direct = direct + dm * (1.0 - mx)
                        shifted = shifted + dm * mx
                        dmix[m] = dmix[m] + jnp.sum(dm * xx, axis=0, keepdims=True)
                    dh_ref[pl.ds(lo, cnt), :] = direct + _unshift_rows(shifted, s, keep)
                for m in range(6):
                    dmix_ref[m] = dmix[m]

    tile = pl.BlockSpec((n, tc), lambda j: (0, j))
    return pl.pallas_call(
        body, name="shift_mix_bwd", grid=(d // tc,),
        in_specs=[tile, pl.BlockSpec((6, 1, tc), lambda j: (0, 0, j))] + [tile] * 6,
        out_specs=[tile, pl.BlockSpec((6, 1, tc), lambda j: (0, 0, j))],
        out_shape=[jax.ShapeDtypeStruct((n, d), F32), jax.ShapeDtypeStruct((6, 1, d), F32)],
        compiler_params=_cparams(("parallel",)),
    )(h, mix3, *douts)


@functools.partial(jax.custom_vjp, nondiff_argnums=(2,))
def shift_mix(h, mix3, L):
    return tuple(_shift_mix_fwd_call(h, mix3, L))


def _shift_mix_fwd(h, mix3, L):
    return tuple(_shift_mix_fwd_call(h, mix3, L)), (h, mix3)


def _shift_mix_bwd(L, res, douts):
    h, mix3 = res
    dh, dmix = _shift_mix_bwd_call(h, mix3, tuple(douts), L)
    return dh, dmix


shift_mix.defvjp(_shift_mix_fwd, _shift_mix_bwd)


def _conv_specs(T, d, tc):
    nd = d // tc
    ins = [pl.BlockSpec((T, tc), functools.partial(lambda j, off: (0, j + off), off=o * nd)) for o in range(3)]
    return ins, pl.BlockSpec((3, 1, tc), lambda j: (0, 0, j))


def _conv_terms(gc, u, tc):
    T = gc.shape[0]
    row = _row_iota(T, tc)
    z = gc * u
    return z, _shift_rows(z, 1, row >= 1), _shift_rows(z, -1, row < T - 1), row


def _conv_fwd_call(guc, cw3):
    T, d3 = guc.shape
    d = d3 // 3
    tc = _pick(d, 256)
    ins, wspec = _conv_specs(T, d, tc)

    def body(gb_ref, gc_ref, u_ref, w_ref, p_ref):
        z, zp, zn, _ = _conv_terms(gc_ref[...].astype(F32), u_ref[...].astype(F32), tc)
        conv = zp * w_ref[0] + z * w_ref[1] + zn * w_ref[2]
        p_ref[...] = (gb_ref[...].astype(F32) * conv).astype(BF16)

    return pl.pallas_call(
        body, name="conv_fwd", grid=(d // tc,), in_specs=ins + [wspec],
        out_specs=pl.BlockSpec((T, tc), lambda j: (0, j)),
        out_shape=jax.ShapeDtypeStruct((T, d), BF16),
        compiler_params=_cparams(("parallel",)),
    )(guc, guc, guc, cw3)


def _conv_bwd_call(guc, cw3, dp):
    T, d3 = guc.shape
    d = d3 // 3
    tc = _pick(d, 256)
    ins, wspec = _conv_specs(T, d, tc)
    tile = pl.BlockSpec((T, tc), lambda j: (0, j))

    def body(gb_ref, gc_ref, u_ref, w_ref, dp_ref, dgb_ref, dgc_ref, du_ref, dw_ref):
        gc = gc_ref[...].astype(F32)
        u = u_ref[...].astype(F32)
        z, zp, zn, row = _conv_terms(gc, u, tc)
        conv = zp * w_ref[0] + z * w_ref[1] + zn * w_ref[2]
        dpv = dp_ref[...].astype(F32)
        dgb_ref[...] = (dpv * conv).astype(dgb_ref.dtype)
        dconv = dpv * gb_ref[...].astype(F32)
        dz = (_shift_rows(dconv, -1, row < T - 1) * w_ref[0] + dconv * w_ref[1]
              + _shift_rows(dconv, 1, row >= 1) * w_ref[2])
        dgc_ref[...] = (dz * u).astype(dgc_ref.dtype)
        du_ref[...] = (dz * gc).astype(du_ref.dtype)
        dw_ref[0] = jnp.sum(dconv * zp, axis=0, keepdims=True)
        dw_ref[1] = jnp.sum(dconv * z, axis=0, keepdims=True)
        dw_ref[2] = jnp.sum(dconv * zn, axis=0, keepdims=True)

    return pl.pallas_call(
        body, name="conv_bwd", grid=(d // tc,), in_specs=ins + [wspec, tile],
        out_specs=[tile, tile, tile, wspec],
        out_shape=[jax.ShapeDtypeStruct((T, d), guc.dtype)] * 3 + [jax.ShapeDtypeStruct((3, 1, d), F32)],
        compiler_params=_cparams(("parallel",)),
    )(guc, guc, guc, cw3, dp)


@jax.custom_vjp
def gated_conv(guc, cw3):
    return _conv_fwd_call(guc, cw3)


def _gated_conv_fwd(guc, cw3):
    return _conv_fwd_call(guc, cw3), (guc, cw3)


def _gated_conv_bwd(res, dp):
    guc, cw3 = res
    dgb, dgc, du, dw = _conv_bwd_call(guc, cw3, dp)
    return jnp.concatenate([dgb, dgc, du], axis=1), dw


gated_conv.defvjp(_gated_conv_fwd, _gated_conv_bwd)


def _chunk_map(nchunk, nctx_chunk, reverse):
    if not reverse:
        return lambda c: c
    return lambda c: jnp.where(c < nctx_chunk, nctx_chunk - 1 - c, nchunk - 1 - (c - nctx_chunk))


def _spread_bf16(row_ref, dst_scr, ni, C):
    packed = [pltpu.bitcast(row_ref[tt].astype(BF16), jnp.int32) for tt in range(C)]
    lane = lax.broadcasted_iota(jnp.int32, packed[0].shape, 1)
    for i in range(ni):
        idx = jnp.where(lane < HEAD, 2 * i, HEAD + 1 + 2 * i).astype(jnp.int32)
        for tt in range(C):
            got = jnp.take_along_axis(packed[tt], idx, axis=1)
            dst_scr[tt, i] = pltpu.bitcast(got, BF16).astype(F32)


def _half_sums(p, lo_mask):
    lo = jnp.sum(jnp.where(lo_mask, p, 0.0), axis=1, keepdims=True)
    hi = jnp.sum(jnp.where(lo_mask, 0.0, p), axis=1, keepdims=True)
    return jnp.where(lo_mask, lo, hi)


def _half_sums_mxu(ps, gmat):
    p = jnp.concatenate(ps, axis=0)
    hi = p.astype(BF16)
    lo = (p - hi.astype(F32)).astype(BF16)
    s = jnp.dot(hi, gmat, preferred_element_type=F32) + jnp.dot(lo, gmat, preferred_element_type=F32)
    nh = ps[0].shape[0]
    return [s[i * nh:(i + 1) * nh] for i in range(len(ps))]


def _split_row(sums, lane, nh):
    acc = jnp.zeros((nh, LANES), F32)
    for i, s in enumerate(sums):
        acc = acc + jnp.where(jnp.logical_or(lane == 2 * i, lane == HEAD + 1 + 2 * i), s, 0.0)
    return acc


def _wkv_fwd_call(r2, w2, kd2, kk2, as2, v2, nctx, reverse, xchg_arrs=(), xchg_specs=()):
    n, nh, _ = r2.shape
    C = SCAN_CHUNK
    ni = HEAD // 2
    nchunk = n // C
    cmap = _chunk_map(nchunk, nctx // C, reverse)
    nx = len(xchg_arrs)

    def body(*refs):
        g_ref, refs = refs[0], refs[1:]
        r_ref, w_ref, kd_ref, kk_ref, as_ref, v_ref = refs[:6]
        x_in = refs[6:6 + nx]
        y_ref, sa_ref, sp_ref = refs[6 + nx:9 + nx]
        x_out = refs[9 + nx:9 + 2 * nx]
        s_scr, vc_scr = refs[9 + 2 * nx:11 + 2 * nx]
        if nx:
            _fused_exchanges(list(zip(x_in, x_out)), xchg_specs, *refs[11 + 2 * nx:], first=pl.program_id(0) == 0)

        @pl.when(pl.program_id(0) == 0)
        def _():
            s_scr[...] = jnp.zeros_like(s_scr)

        lane = lax.broadcasted_iota(jnp.int32, (nh, LANES), 1)
        lo_mask = lane < HEAD
        _spread_bf16(v_ref, vc_scr, ni, C)

        def make_step(with_y):
            def step(j, carry):
                t = (C - 1 - j) if reverse else j
                kk = kk_ref[t]
                a2 = -kk
                b2 = kk * as_ref[t]
                w = w_ref[t]
                k = kd_ref[t]
                r = r_ref[t]
                sas = []
                for i in range(ni):
                    si = s_scr[i]
                    sp_ref[t, i] = si
                    sas.append(_half_sums(si * a2, lo_mask))
                qs = []
                for i in range(ni):
                    sn = s_scr[i] * w + sas[i] * b2 + vc_scr[t, i] * k
                    s_scr[i] = sn
                    if with_y:
                        qs.append(sn * r)
                if with_y:
                    y_ref[t] = _split_row(_half_sums_mxu(qs, g_ref[...]), lane, nh)
                else:
                    y_ref[t] = jnp.zeros((nh, LANES), F32)
                sa_ref[t] = _split_row(sas, lane, nh)
                return carry
            return step

        is_ctx = pl.program_id(0) < nctx // C

        @pl.when(is_ctx)
        def _():
            lax.fori_loop(0, C, make_step(False), 0)

        @pl.when(jnp.logical_not(is_ctx))
        def _():
            lax.fori_loop(0, C, make_step(True), 0, unroll=SCAN_UNROLL)

        if nx:
            _fused_exchanges(list(zip(x_in, x_out)), xchg_specs, *refs[11 + 2 * nx:],
                             last=pl.program_id(0) == nchunk - 1)

    tok = pl.BlockSpec((C, nh, LANES), lambda c: (cmap(c), 0, 0))
    return pl.pallas_call(
        body, name="wkv_fwd_rev" if reverse else "wkv_fwd", grid=(nchunk,),
        in_specs=[pl.BlockSpec((LANES, LANES), lambda c: (0, 0))] + [tok] * 6 + [_ANY] * nx,
        out_specs=[tok, tok, pl.BlockSpec((C, ni, nh, LANES), lambda c: (cmap(c), 0, 0, 0))] + [_ANY] * nx,
        out_shape=[jax.ShapeDtypeStruct((n, nh, LANES), F32), jax.ShapeDtypeStruct((n, nh, LANES), F32),
                   jax.ShapeDtypeStruct((n, ni, nh, LANES), F32)] + _exchange_out_shapes(xchg_arrs, xchg_specs),
        scratch_shapes=[pltpu.VMEM((ni, nh, LANES), F32), pltpu.VMEM((C, ni, nh, LANES), F32)]
        + (_exchange_sems(nx) if nx else []),
        compiler_params=_cparams(("arbitrary",)),
    )(_head_group_matrix(LANES).astype(BF16), r2, w2, kd2, kk2, as2, v2, *xchg_arrs)


def _wkv_bwd_call(r2, w2, kd2, kk2, as2, v2, sa, sprev, dy, nctx, reverse, xchg_arrs=(), xchg_specs=()):
    n, nh, _ = r2.shape
    C = SCAN_CHUNK
    ni = HEAD // 2
    nchunk = n // C
    fmap = _chunk_map(nchunk, nctx // C, reverse)
    cmap = lambda c: fmap(nchunk - 1 - c)
    nx = len(xchg_arrs)

    def body(*refs):
        g_ref, refs = refs[0], refs[1:]
        r_ref, w_ref, kd_ref, kk_ref, as_ref, v_ref, sa_ref, sp_ref, dy_ref = refs[:9]
        x_in = refs[9:9 + nx]
        dr_ref, dw_ref, dkd_ref, dkk_ref, das_ref, dv_ref = refs[9 + nx:15 + nx]
        x_out = refs[15 + nx:15 + 2 * nx]
        ds_scr, vc_scr, sac_scr, dyc_scr = refs[15 + 2 * nx:19 + 2 * nx]
        if nx:
            _fused_exchanges(list(zip(x_in, x_out)), xchg_specs, *refs[19 + 2 * nx:], first=pl.program_id(0) == 0)

        @pl.when(pl.program_id(0) == 0)
        def _():
            ds_scr[...] = jnp.zeros_like(ds_scr)

        lane = lax.broadcasted_iota(jnp.int32, (nh, LANES), 1)
        lo_mask = lane < HEAD
        _spread_bf16(v_ref, vc_scr, ni, C)
        _spread_bf16(sa_ref, sac_scr, ni, C)
        is_ctx = pl.program_id(0) >= nchunk - nctx // C

        @pl.when(jnp.logical_not(is_ctx))
        def _():
            _spread_bf16(dy_ref, dyc_scr, ni, C)

        def make_step(with_dy):
            def step(j, carry):
                t = j if reverse else (C - 1 - j)
                kk = kk_ref[t]
                sig = as_ref[t]
                a2 = -kk
                b2 = kk * sig
                w = w_ref[t]
                k = kd_ref[t]
                r = r_ref[t]
                zero = jnp.zeros((nh, LANES), F32)
                acc_dk, acc_db, acc_dw, acc_g, acc_sady, acc_vdy, acc_da = zero, zero, zero, zero, zero, zero, zero
                dvp, dsas = [], []
                for i in range(ni):
                    sp = sp_ref[t, i]
                    vc = vc_scr[t, i]
                    sac = sac_scr[t, i]
                    ds = ds_scr[i]
                    if with_dy:
                        dyc = dyc_scr[t, i]
                        ds = ds + dyc * r
                        ds_scr[i] = ds
                        acc_g = acc_g + sp * dyc
                    dvp.append(ds * k)
                    dsas.append(_half_sums(ds * b2, lo_mask))
                    acc_dk = acc_dk + ds * vc
                    acc_db = acc_db + ds * sac
                    acc_dw = acc_dw + ds * sp
                for i in range(ni):
                    acc_da = acc_da + sp_ref[t, i] * dsas[i]
                    ds_scr[i] = ds_scr[i] * w + dsas[i] * a2
                if with_dy:
                    dyr = jnp.where(jnp.bitwise_and(lane, 1) == (lane >= HEAD).astype(jnp.int32), dy_ref[t], 0.0)
                    acc_sady = _half_sums(sa_ref[t] * dyr, lo_mask)
                    acc_vdy = _half_sums(v_ref[t] * dyr, lo_mask)
                dr_ref[t] = acc_g * w + b2 * acc_sady + k * acc_vdy
                dw_ref[t] = acc_dw
                dkd_ref[t] = acc_dk
                dkk_ref[t] = acc_db * sig - acc_da
                das_ref[t] = acc_db * kk
                dv_ref[t] = _split_row(_half_sums_mxu(dvp, g_ref[...]), lane, nh)
                return carry
            return step

        @pl.when(is_ctx)
        def _():
            lax.fori_loop(0, C, make_step(False), 0)

        @pl.when(jnp.logical_not(is_ctx))
        def _():
            lax.fori_loop(0, C, make_step(True), 0, unroll=SCAN_UNROLL)

        if nx:
            _fused_exchanges(list(zip(x_in, x_out)), xchg_specs, *refs[19 + 2 * nx:],
                             last=pl.program_id(0) == nchunk - 1)

    tok = pl.BlockSpec((C, nh, LANES), lambda c: (cmap(c), 0, 0))
    big = pltpu.VMEM((C, ni, nh, LANES), F32)
    return pl.pallas_call(
        body, name="wkv_bwd_rev" if reverse else "wkv_bwd", grid=(nchunk,),
        in_specs=[pl.BlockSpec((LANES, LANES), lambda c: (0, 0))] + [tok] * 7
        + [pl.BlockSpec((C, ni, nh, LANES), lambda c: (cmap(c), 0, 0, 0)), tok] + [_ANY] * nx,
        out_specs=[tok] * 6 + [_ANY] * nx,
        out_shape=[jax.ShapeDtypeStruct((n, nh, LANES), F32)] * 6 + _exchange_out_shapes(xchg_arrs, xchg_specs),
        scratch_shapes=[pltpu.VMEM((ni, nh, LANES), F32), big, big, big] + (_exchange_sems(nx) if nx else []),
        compiler_params=_cparams(("arbitrary",)),
    )(_head_group_matrix(LANES).astype(BF16), r2, w2, kd2, kk2, as2, v2, sa, sprev, dy, *xchg_arrs)


def _tile_heads(t):
    n, d = t.shape
    th = t.reshape(n, d // HEAD, HEAD)
    return jnp.concatenate([th, th], axis=-1)


def loss_head(x3, fo, tgt, gate, g):
    T, d = x3.shape
    tr = _pick(T, 128, 8)

    def body(x_ref, f_ref, t_ref, gate_ref, g_ref, loss_ref, dx_ref, df_ref, dgate_ref, dg_ref):
        tg = t_ref[...]

        def fl(x, fo_, gate_, g_):
            x4 = x + gate_ * fo_
            y = (x4 * lax.rsqrt(jnp.mean(x4 * x4, axis=-1, keepdims=True) + NORM_EPS)) * g_
            return 0.5 * jnp.sum(jnp.mean(jnp.square(y - tg), axis=-1))

        val, vjp = jax.vjp(fl, x_ref[...], f_ref[...], gate_ref[...], g_ref[...])
        dx, dfo, dgate, dg = vjp(jnp.ones((), F32))
        dx_ref[...] = dx
        df_ref[...] = dfo
        i = pl.program_id(0)

        @pl.when(i == 0)
        def _():
            loss_ref[...] = jnp.zeros_like(loss_ref)
            dgate_ref[...] = jnp.zeros_like(dgate_ref)
            dg_ref[...] = jnp.zeros_like(dg_ref)

        loss_ref[...] += jnp.full(loss_ref.shape, val, F32)
        dgate_ref[...] += dgate
        dg_ref[...] += dg

    tile = pl.BlockSpec((tr, d), lambda i: (i, 0))
    vec = pl.BlockSpec((1, d), lambda i: (0, 0))
    return pl.pallas_call(
        body, name="loss_head", grid=(T // tr,),
        in_specs=[tile, tile, tile, vec, vec],
        out_specs=[pl.BlockSpec((8, LANES), lambda i: (0, 0)), tile, tile, vec, vec],
        out_shape=[jax.ShapeDtypeStruct((8, LANES), F32), jax.ShapeDtypeStruct((T, d), F32),
                   jax.ShapeDtypeStruct((T, d), F32), jax.ShapeDtypeStruct((1, d), F32),
                   jax.ShapeDtypeStruct((1, d), F32)],
        compiler_params=_cparams(("arbitrary",)),
    )(x3, fo, tgt, gate, g)


def sum_adam(parts, w, m, v, name, lead=None, prev=None):
    P, R, Cc = parts.shape
    tc = _pick(Cc, 1024)
    tr = _pick(R, max(8, (256 * 1024) // tc), 16 if parts.dtype == BF16 else 8)
    prev = tuple(prev) if prev is not None else ()

    def body(p_ref, w_ref, m_ref, v_ref, *rest):
        g_ref, d_ref, nm_ref, nv_ref = rest[len(prev):]
        g = p_ref[0].astype(F32)
        for s in range(1, P):
            g = g + p_ref[s].astype(F32)
        m_new = ADAM_B1 * m_ref[...] + (1.0 - ADAM_B1) * g
        v_new = ADAM_B2 * v_ref[...] + (1.0 - ADAM_B2) * jnp.square(g)
        m_hat = m_new / (1.0 - ADAM_B1 ** ADAM_STEP)
        v_hat = v_new / (1.0 - ADAM_B2 ** ADAM_STEP)
        g_ref[...] = g
        d_ref[...] = -ADAM_LR * (m_hat / (jnp.sqrt(v_hat) + ADAM_EPS) + ADAM_WD * w_ref[...])
        nm_ref[...] = m_new
        nv_ref[...] = v_new

    if lead is None:
        pspec = pl.BlockSpec((tr, tc), lambda i, j: (i, j))
        oshape = (R, Cc)
    else:
        pspec = pl.BlockSpec((None, tr, tc), lambda i, j: (lead, i, j))
        oshape = w.shape
    return pl.pallas_call(
        body, name=name, grid=(R // tr, Cc // tc),
        in_specs=[pl.BlockSpec((P, tr, tc), lambda i, j: (0, i, j)), pspec, pspec, pspec] + [_ANY] * len(prev),
        out_specs=[pspec] * 4,
        out_shape=[jax.ShapeDtypeStruct(oshape, F32)] * 4,
        input_output_aliases={4 + q: q for q in range(len(prev))},
        compiler_params=_cparams(("parallel", "parallel")),
    )(parts, w, m, v, *prev)


def _me():
    return lax.axis_index("x"), lax.axis_index("y"), lax.axis_index("c")


def _peer(p):
    x, y, c = _me()
    px = 1 - x if p & 4 else x
    py = 1 - y if p & 2 else y
    pc = 1 - c if p & 1 else c
    return (px, py, pc), 4 * px + 2 * py + pc


def _block_view(ref, axis, idx, r, c):
    if axis is None:
        return ref.at[idx]
    if axis == 0:
        return ref.at[pl.ds(idx * r, r), :]
    return ref.at[:, pl.ds(idx * c, c)]


def _exchange_copies(src_of, dst_of, ssem, rsem, lsem, with_recvs):
    x, y, c = _me()
    me = 4 * x + 2 * y + c
    local = pltpu.make_async_copy(src_of(me), dst_of(me), lsem)
    sends, recvs = [], []
    for p in range(1, N_DEV):
        dev, idx = _peer(p)
        sends.append(pltpu.make_async_remote_copy(src_ref=src_of(idx), dst_ref=dst_of(me), send_sem=ssem(p),
                                                  recv_sem=rsem(p), device_id=dev,
                                                  device_id_type=pl.DeviceIdType.MESH))
        if with_recvs:
            recvs.append(pltpu.make_async_remote_copy(src_ref=src_of(idx), dst_ref=dst_of(idx), send_sem=ssem(p),
                                                      recv_sem=rsem(p), device_id=dev,
                                                      device_id_type=pl.DeviceIdType.MESH))
    return local, sends, recvs


def _exchange_start(*args):
    local, sends, _ = _exchange_copies(*args, with_recvs=False)
    local.start()
    for cp in sends:
        cp.start()


def _exchange_wait(*args):
    local, sends, recvs = _exchange_copies(*args, with_recvs=True)
    for cp in recvs:
        cp.wait_recv()
    for cp in sends:
        cp.wait_send()
    local.wait()


def _exchange(src_of, dst_of, send_sems, recv_sems, local_sem):
    args = (src_of, dst_of, lambda p: send_sems.at[p], lambda p: recv_sems.at[p], local_sem)
    _exchange_start(*args)
    _exchange_wait(*args)


def _fused_exchanges(pairs, specs, send_sems, recv_sems, local_sems, first=None, last=None):
    def args(j):
        src, dst = pairs[j]
        kind, axis, r, c = specs[j]
        if kind == "ag":
            src_of = lambda idx: src
            dst_of = lambda idx: _block_view(dst, axis, idx, r, c)
        else:
            src_of = lambda idx: _block_view(src, axis, idx, r, c)
            dst_of = lambda idx: dst.at[idx]
        return (src_of, dst_of, lambda p: send_sems.at[j, p], lambda p: recv_sems.at[j, p], local_sems.at[j])

    if first is not None:
        @pl.when(first)
        def _():
            for j in range(len(pairs)):
                _exchange_start(*args(j))

    if last is not None:
        @pl.when(last)
        def _():
            for j in range(len(pairs)):
                _exchange_wait(*args(j))


def _exchange_out_shapes(arrs, specs):
    out = []
    for a, (kind, axis, r, c) in zip(arrs, specs):
        if kind == "rs":
            out.append(jax.ShapeDtypeStruct((N_DEV, r, c), a.dtype))
        else:
            out.append(jax.ShapeDtypeStruct((N_DEV * r, c) if axis == 0 else (r, N_DEV * c), a.dtype))
    return out


def _exchange_specs(kind, arrs, axes):
    specs = []
    for a, axis in zip(arrs, axes):
        if kind == "ag":
            r, c = a.shape
        elif axis == 0:
            r, c = a.shape[0] // N_DEV, a.shape[1]
        else:
            r, c = a.shape[0], a.shape[1] // N_DEV
        specs.append((kind, axis, r, c))
    return specs


def _exchange_sems(n):
    return [pltpu.SemaphoreType.DMA((n, N_DEV)), pltpu.SemaphoreType.DMA((n, N_DEV)), pltpu.SemaphoreType.DMA((n,))]


_SEMS = [pltpu.SemaphoreType.DMA((N_DEV,)), pltpu.SemaphoreType.DMA((N_DEV,)), pltpu.SemaphoreType.DMA]
_ANY = pl.BlockSpec(memory_space=pl.ANY)


def all_gather(x, axis, name):
    r, c = x.shape
    shape = (N_DEV * r, c) if axis == 0 else (r, N_DEV * c)

    def body(x_ref, o_ref, send_sems, recv_sems, local_sem):
        _exchange(lambda idx: x_ref, lambda idx: _block_view(o_ref, axis, idx, r, c), send_sems, recv_sems, local_sem)

    return pl.pallas_call(
        body, name=name, in_specs=[_ANY], out_specs=_ANY,
        out_shape=jax.ShapeDtypeStruct(shape, x.dtype), scratch_shapes=_SEMS,
    )(x)


def all_gather_stack(x, name):
    r, c = x.shape

    def body(x_ref, o_ref, send_sems, recv_sems, local_sem):
        _exchange(lambda idx: x_ref, lambda idx: o_ref.at[idx], send_sems, recv_sems, local_sem)

    return pl.pallas_call(
        body, name=name, in_specs=[_ANY], out_specs=_ANY,
        out_shape=jax.ShapeDtypeStruct((N_DEV, r, c), x.dtype), scratch_shapes=_SEMS,
    )(x)


def reduce_scatter_exchange(g, axis, name):
    if axis is None:
        _, r, c = g.shape
    elif axis == 0:
        r, c = g.shape[0] // N_DEV, g.shape[1]
    else:
        r, c = g.shape[0], g.shape[1] // N_DEV

    def body(g_ref, o_ref, send_sems, recv_sems, local_sem):
        _exchange(lambda idx: _block_view(g_ref, axis, idx, r, c), lambda idx: o_ref.at[idx],
                  send_sems, recv_sems, local_sem)

    return pl.pallas_call(
        body, name=name, in_specs=[_ANY], out_specs=_ANY,
        out_shape=jax.ShapeDtypeStruct((N_DEV, r, c), g.dtype), scratch_shapes=_SEMS,
    )(g)


PACK_QUANTUM = 16 * LANES


def _pack(arrs, dtype=F32, lead=0):
    keep = arrs[0].shape[:lead]
    flat = jnp.concatenate([a.reshape(keep + (-1,)).astype(dtype) for a in arrs], axis=-1)
    pad = (-flat.shape[-1]) % PACK_QUANTUM
    flat = jnp.pad(flat, ((0, 0),) * lead + ((0, pad),))
    return flat.reshape(keep + (-1, LANES))


def _unpack(flat2d, shapes, lead=()):
    flat = flat2d.reshape(lead + (-1,))
    out, off = [], 0
    for s in shapes:
        n = int(np.prod(s))
        out.append(flat[..., off:off + n].reshape(lead + tuple(s)))
        off += n
    return out


def _gather_lastdim(stk):
    return jnp.moveaxis(stk, 0, -2).reshape(stk.shape[1:-1] + (N_DEV * stk.shape[-1],))


def _gather_dim(stk, dim):
    moved = jnp.moveaxis(stk, 0, dim)
    sh = list(stk.shape[1:])
    sh[dim] = sh[dim] * N_DEV
    return moved.reshape(sh)


def _scatter_dim(full, dim):
    sh = list(full.shape)
    sh[dim:dim + 1] = [N_DEV, sh[dim] // N_DEV]
    return jnp.moveaxis(full.reshape(sh), dim, 0)


def _head_group_matrix(tc):
    return np.kron(np.eye(tc // HEAD, dtype=np.float32), np.ones((HEAD, HEAD), np.float32))


_SCAN_COMM = {0: ("w13_0", "w13_1"), 1: ("win", "w2_0", "w2_1", "wout", "wo")}
_W_AXIS = dict(wr=0, wk=0, wv=0, wo=0, win=1, wout=0, w13_0=1, w13_1=1, w2_0=0, w2_1=0)


def _build_forward(ctx2d, T, D, shards=None, sink=None):
    L = ctx2d.shape[0]
    N = L + T

    def make_scan(d):
        keys = _SCAN_COMM[d] if shards is not None else ()
        axes = tuple(_W_AXIS[k] for k in keys)

        def run_fwd(tok, sh):
            return _wkv_fwd_call(*tok, L, d == 1, tuple(sh), _exchange_specs("ag", sh, axes))

        @jax.custom_vjp
        def op(tok, sh):
            outs = run_fwd(tok, sh)
            return (outs[0],) + tuple(outs[3:])

        def op_fwd(tok, sh):
            outs = run_fwd(tok, sh)
            return (outs[0],) + tuple(outs[3:]), (tok, outs[1], outs[2])

        def op_bwd(res, cts):
            tok, sa, sprev = res
            dg = tuple(cts[1:])
            outs = _wkv_bwd_call(*tok, sa, sprev, cts[0], L, d == 1, dg, _exchange_specs("rs", dg, axes))
            for k, recv in zip(keys, outs[6:]):
                sink[k] = recv
            return tuple(outs[:6]), tuple(jnp.zeros(shards[k].shape, shards[k].dtype) for k in keys)

        op.defvjp(op_fwd, op_bwd)
        return op, keys

    scans = [make_scan(0), make_scan(1)]
    tc_head = _pick(D, 2 * LANES)
    gm = _head_group_matrix(tc_head)
    tr_row = _pick(math.gcd(L, T), 128, 8)
    op_norm = make_rowwise("norm_mod", _f_norm_mod, (F32,), tr_row, D, nb0=L // tr_row)
    op_res = [make_rowwise(f"res_norm_mod{i}", _f_res_norm_mod, (F32, BF16), _pick(T, 128, 8), D) for i in range(3)]
    op_prep = make_rowwise("wkv_prep", _f_prep, (F32,) * 7, _pick(N, 256, 8), tc_head, consts=(gm,))
    op_read = make_rowwise("wkv_readout", _f_readout, (BF16,), _pick(N, 256, 8), tc_head, consts=(gm,))

    def v3(a):
        return a.reshape(a.shape[0], 1, a.shape[-1])

    def fwd(xin, Ps, Wb):
        modx, modc = Ps["modx"], Ps["modc"]
        cat = jnp.concatenate([ctx2d, xin], axis=0)
        seg = lambda a, b: jnp.stack([a, b])[:, None, :]
        (hcat,) = op_norm((cat,), (Ps["n1"][0][None, None, :], seg(modc[0], modx[0, 0]), seg(modc[1], modx[0, 1])))
        xr, xw, xk, xv, xa, xg = shift_mix(hcat, Ps["mix"][:, None, :], L)
        r = linear(xr, Wb["wr"], F32, "wr")
        k = linear(xk, Wb["wk"], F32, "wk")
        v = linear(xv, Wb["wv"], F32, "wv")
        gl = jax.nn.sigmoid(linear(xg, Wb["g1"], F32, "g1"))
        g = linear(gl.astype(BF16), Wb["g2"], F32, "g2")
        tw = jnp.tanh(linear(xw, Wb["w1"], F32, "w1")).astype(BF16)
        ta = linear(xa, Wb["a1"], F32, "a1").astype(BF16)
        lw = [linear(tw[:, LORA_PAD * d:LORA_PAD * (d + 1)], Wb["w2d"][d], F32, f"w2_{d}") for d in range(2)]
        la = [linear(ta[:, LORA_PAD * d:LORA_PAD * (d + 1)], Wb["a2d"][d], F32, f"a2_{d}") for d in range(2)]
        kk, dec0, dec1, kd0, kd1, as0, as1 = op_prep(
            (k, lw[0], lw[1], la[0], la[1]),
            (v3(Ps["kk"]), v3(Ps["ka"]), Ps["w0"][0][None, None, :], Ps["w0"][1][None, None, :],
             Ps["a0"][0][None, None, :], Ps["a0"][1][None, None, :]))
        r2, kk2, v2 = _tile_heads(r), _tile_heads(kk), _tile_heads(v)
        ys = []
        Wb = dict(Wb)
        for d, (dec, kd, sg) in enumerate(((dec0, kd0, as0), (dec1, kd1, as1))):
            op, keys = scans[d]
            outs = op((r2, _tile_heads(dec), _tile_heads(kd), kk2, _tile_heads(sg), v2),
                      tuple(shards[k] for k in keys))
            yx = outs[0]
            ys.append((yx[:, :, :HEAD] + yx[:, :, HEAD:]).reshape(N, D))
            Wb.update(zip(keys, outs[1:]))
        (o,) = op_read((ys[0], ys[1], r, kd0, kd1, v, g), (v3(Ps["rk"]), v3(Ps["lnw"]), v3(Ps["lnb"])))
        att = linear(o[L:], Wb["wo"], F32, "wo")
        x1, h2 = op_res[0]((xin, att), (modx[0, 2][None, None, :], Ps["n2"][0][None, None, :],
                                        modx[0, 3][None, None, :], modx[0, 4][None, None, :]))
        act = swiglu_act(linear(h2, Wb["w13_0"], BF16, "w13_0"), "swiglu0")
        f0 = linear(act, Wb["w2_0"], F32, "w2_0")
        x2, h = op_res[1]((x1, f0), (modx[0, 5][None, None, :], Ps["n1"][1][None, None, :],
                                     modx[1, 0][None, None, :], modx[1, 1][None, None, :]))
        guc = linear(h, Wb["win"], BF16, "win")
        p = gated_conv(guc, Ps["conv"][:, None, :])
        cv = linear(p, Wb["wout"], F32, "wout")
        x3, h2b = op_res[2]((x2, cv), (modx[1, 2][None, None, :], Ps["n2"][1][None, None, :],
                                       modx[1, 3][None, None, :], modx[1, 4][None, None, :]))
        act1 = swiglu_act(linear(h2b, Wb["w13_1"], BF16, "w13_1"), "swiglu1")
        f1 = linear(act1, Wb["w2_1"], F32, "w2_1")
        return x3, f1


    return fwd


def kernel(x, c, ctx, c_ctx, norm1_g, norm2_g, ada_w, ada_b, rw_mix, rw_wr, rw_wk, rw_wv, rw_wo, rw_w0, rw_w1, rw_w2, rw_a0, rw_a1, rw_a2, rw_g1, rw_g2, rw_kk, rw_ka, rw_rk, rw_lnw, rw_lnb, sc_win, sc_conv, sc_wout, ffn_w13, ffn_w2, final_g, loss_target, m_c_ctx, m_norm1_g, m_norm2_g, m_ada_w, m_ada_b, m_rw_mix, m_rw_wr, m_rw_wk, m_rw_wv, m_rw_wo, m_rw_w0, m_rw_w1, m_rw_w2, m_rw_a0, m_rw_a1, m_rw_a2, m_rw_g1, m_rw_g2, m_rw_kk, m_rw_ka, m_rw_rk, m_rw_lnw, m_rw_lnb, m_sc_win, m_sc_conv, m_sc_wout, m_ffn_w13, m_ffn_w2, m_final_g, v_c_ctx, v_norm1_g, v_norm2_g, v_ada_w, v_ada_b, v_rw_mix, v_rw_wr, v_rw_wk, v_rw_wv, v_rw_wo, v_rw_w0, v_rw_w1, v_rw_w2, v_rw_a0, v_rw_a1, v_rw_a2, v_rw_g1, v_rw_g2, v_rw_kk, v_rw_ka, v_rw_rk, v_rw_lnw, v_rw_lnb, v_sc_win, v_sc_conv, v_sc_wout, v_ffn_w13, v_ffn_w2, v_final_g):
    W = dict(c_ctx=c_ctx, norm1_g=norm1_g, norm2_g=norm2_g, ada_w=ada_w, ada_b=ada_b, rw_mix=rw_mix, rw_wr=rw_wr,
             rw_wk=rw_wk, rw_wv=rw_wv, rw_wo=rw_wo, rw_w0=rw_w0, rw_w1=rw_w1, rw_w2=rw_w2, rw_a0=rw_a0, rw_a1=rw_a1,
             rw_a2=rw_a2, rw_g1=rw_g1, rw_g2=rw_g2, rw_kk=rw_kk, rw_ka=rw_ka, rw_rk=rw_rk, rw_lnw=rw_lnw,
             rw_lnb=rw_lnb, sc_win=sc_win, sc_conv=sc_conv, sc_wout=sc_wout, ffn_w13=ffn_w13, ffn_w2=ffn_w2,
             final_g=final_g)
    Mo = dict(c_ctx=m_c_ctx, norm1_g=m_norm1_g, norm2_g=m_norm2_g, ada_w=m_ada_w, ada_b=m_ada_b, rw_mix=m_rw_mix,
              rw_wr=m_rw_wr, rw_wk=m_rw_wk, rw_wv=m_rw_wv, rw_wo=m_rw_wo, rw_w0=m_rw_w0, rw_w1=m_rw_w1,
              rw_w2=m_rw_w2, rw_a0=m_rw_a0, rw_a1=m_rw_a1, rw_a2=m_rw_a2, rw_g1=m_rw_g1, rw_g2=m_rw_g2,
              rw_kk=m_rw_kk, rw_ka=m_rw_ka, rw_rk=m_rw_rk, rw_lnw=m_rw_lnw, rw_lnb=m_rw_lnb, sc_win=m_sc_win,
              sc_conv=m_sc_conv, sc_wout=m_sc_wout, ffn_w13=m_ffn_w13, ffn_w2=m_ffn_w2, final_g=m_final_g)
    Vo = dict(c_ctx=v_c_ctx, norm1_g=v_norm1_g, norm2_g=v_norm2_g, ada_w=v_ada_w, ada_b=v_ada_b, rw_mix=v_rw_mix,
              rw_wr=v_rw_wr, rw_wk=v_rw_wk, rw_wv=v_rw_wv, rw_wo=v_rw_wo, rw_w0=v_rw_w0, rw_w1=v_rw_w1,
              rw_w2=v_rw_w2, rw_a0=v_rw_a0, rw_a1=v_rw_a1, rw_a2=v_rw_a2, rw_g1=v_rw_g1, rw_g2=v_rw_g2,
              rw_kk=v_rw_kk, rw_ka=v_rw_ka, rw_rk=v_rw_rk, rw_lnw=v_rw_lnw, rw_lnb=v_rw_lnb, sc_win=v_sc_win,
              sc_conv=v_sc_conv, sc_wout=v_sc_wout, ffn_w13=v_ffn_w13, ffn_w2=v_ffn_w2, final_g=v_final_g)
    names = list(W)

    x2d = x[0]
    ctx2d = ctx[0]
    tgt = loss_target[0]
    T, D = x2d.shape
    L = ctx2d.shape[0]
    N = L + T
    nh = D // HEAD
    mx, my, mc = _me()
    me = 4 * mx + 2 * my + mc
    dloc = D // N_DEV

    lr = rw_w1.shape[-1]
    pad_r = LORA_PAD - lr
    w1p = jnp.pad(rw_w1[0], ((0, 0), (0, 0), (0, pad_r)))
    a1p = jnp.pad(rw_a1[0], ((0, 0), (0, 0), (0, pad_r)))
    w2p = jnp.pad(rw_w2[0], ((0, 0), (0, pad_r), (0, 0)))
    a2p = jnp.pad(rw_a2[0], ((0, 0), (0, pad_r), (0, 0)))
    small_loc = [rw_mix[0], rw_w0[0], rw_a0[0], sc_conv[0], w1p, a1p, w2p, a2p, rw_g1[0], rw_g2[0]]
    small_dim = [1, 1, 1, 1, 1, 1, 2, 2, 0, 1]
    small_shapes = [a.shape for a in small_loc]
    small_groups = ((slice(0, 4), F32, "vec"), (slice(4, 10), BF16, "mat"))
    small_full = []
    for sl, dt, tag in small_groups:
        sm_all = all_gather_stack(_pack(small_loc[sl], dt), "ag_small_" + tag)
        sm_parts = _unpack(sm_all, small_shapes[sl], lead=(N_DEV,))
        small_full += [_gather_dim(p, dm) for p, dm in zip(sm_parts, small_dim[sl])]
    mix_f, w0_f, a0_f, conv_f, w1_f, a1_f, w2_f, a2_f, g1_f, g2_f = small_full

    c_all = all_gather_stack(jnp.pad(c, ((0, 7), (0, 0))), "ag_c")[:, 0, :]
    cond_pre = jnp.concatenate([c_all, c_ctx[None, :], jnp.zeros((7, D), F32)], axis=0)
    cond_rows = jax.nn.silu(cond_pre)
    ncol = ada_w.shape[-1]
    mod_loc = []
    for i in range(2):
        bi = lax.dynamic_slice(ada_b[i], (me * ncol,), (ncol,))
        mod_loc.append(_mm(cond_rows, ada_w[i], out_dtype=F32, name=f"ada_fwd{i}") + bi[None, :])
    mod_all = all_gather_stack(jnp.concatenate(mod_loc, axis=0), "ag_mod")
    mod_full = _gather_lastdim(mod_all).reshape(2, 16, 6, D)
    mod_x = lax.dynamic_index_in_dim(mod_full, me, axis=1, keepdims=False)
    mod_c = mod_full[0, 8, :2, :]

    def ag_w(wl, axis, name):
        return all_gather(wl.astype(BF16), axis, name)

    shards = dict(wo=rw_wo[0], win=sc_win[0], wout=sc_wout[0], w13_0=ffn_w13[0], w13_1=ffn_w13[1],
                  w2_0=ffn_w2[0], w2_1=ffn_w2[1])
    shards = {k_: a.astype(BF16) for k_, a in shards.items()}
    sink = {}
    Wb = dict(
        wr=ag_w(rw_wr[0], 0, "ag_wr"), wk=ag_w(rw_wk[0], 0, "ag_wk"), wv=ag_w(rw_wv[0], 0, "ag_wv"),
        w1=jnp.concatenate([w1_f[0], w1_f[1]], axis=1).astype(BF16),
        a1=jnp.concatenate([a1_f[0], a1_f[1]], axis=1).astype(BF16),
        w2d=w2_f.astype(BF16), a2d=a2_f.astype(BF16),
        g1=g1_f.astype(BF16), g2=g2_f.astype(BF16),
    )
    Ps = dict(n1=norm1_g, n2=norm2_g, modx=mod_x, modc=mod_c, mix=mix_f, w0=w0_f, a0=a0_f, conv=conv_f,
              kk=rw_kk, ka=rw_ka, rk=rw_rk.reshape(1, D), lnw=rw_lnw, lnb=rw_lnb)

    fwd = _build_forward(ctx2d, T, D, shards, sink)
    (x3, f1), vjp_fn = jax.vjp(fwd, x2d, Ps, Wb)
    loss_acc, dx3, df1, dgate, dfinal = loss_head(x3, f1, tgt, mod_x[1, 5][None, :], final_g[None, :])
    dx, dPs, dWb = vjp_fn((dx3, df1))
    loss = lax.psum(loss_acc[0, 0], ("x", "y", "c"))

    dmodx = dPs["modx"].at[1, 5].add(dgate[0])
    dmodc = jnp.concatenate([dPs["modc"], jnp.zeros((4, D), F32)], axis=0)
    drow = jnp.stack([dmodx.reshape(2, 6 * D), jnp.stack([dmodc.reshape(6 * D), jnp.zeros((6 * D,), F32)])], axis=1)
    drow_all = all_gather_stack(drow.reshape(4, 6 * D), "ag_dmod").reshape(N_DEV, 2, 2, 6 * D)
    dctx_tot = drow_all[0, :, 1, :]
    for s in range(1, N_DEV):
        dctx_tot = dctx_tot + drow_all[s, :, 1, :]
    dmod_rows = jnp.concatenate([jnp.moveaxis(drow_all[:, :, 0, :], 0, 1), dctx_tot[:, None, :],
                                 jnp.zeros((2, 7, 6 * D), F32)], axis=1)
    grad_ada_b = dctx_tot
    for s in range(N_DEV):
        grad_ada_b = grad_ada_b + drow_all[s, :, 0, :]
    dmod_mine = lax.dynamic_slice_in_dim(dmod_rows, me * ncol, ncol, axis=2)
    g_ada_w = [_mm(cond_rows, dmod_mine[i], ta=True, out_dtype=F32, name=f"ada_dw{i}") for i in range(2)]
    dcond_part = _mm(dmod_mine[0], ada_w[0], tb=True, out_dtype=F32, name="ada_dcond")[8]

    rep_names = ["c_ctx", "norm1_g", "norm2_g", "rw_kk", "rw_ka", "rw_rk", "rw_lnw", "rw_lnb", "final_g"]
    rep_part = [dcond_part, dPs["n1"], dPs["n2"], dPs["kk"], dPs["ka"], dPs["rk"].reshape(W["rw_rk"].shape),
                dPs["lnw"], dPs["lnb"], dfinal[0]]
    rep_shapes = [W[n_].shape for n_ in rep_names]
    rep_all = all_gather_stack(_pack(rep_part), "ag_rep_grads")
    sg = jax.nn.sigmoid(c_ctx)
    dsilu = sg * (1.0 + c_ctx * (1.0 - sg))
    rep_scale = _pack([dsilu] + [jnp.ones(s, F32) for s in rep_shapes[1:]])
    rep_all = rep_all * rep_scale[None]
    rep_w = _pack([W[n_] for n_ in rep_names])
    rep_m = _pack([Mo[n_] for n_ in rep_names])
    rep_v = _pack([Vo[n_] for n_ in rep_names])
    rep_out = sum_adam(rep_all, rep_w, rep_m, rep_v, "adam_rep")
    results = {}
    for nm_, vals in zip(rep_names, zip(*[_unpack(o, rep_shapes) for o in rep_out])):
        results[nm_] = vals

    results["ada_b"] = tuple(sum_adam(grad_ada_b.reshape(1, 2 * 6, D), ada_b.reshape(12, D), m_ada_b.reshape(12, D),
                                      v_ada_b.reshape(12, D), "adam_ada_b"))
    results["ada_b"] = tuple(o.reshape(ada_b.shape) for o in results["ada_b"])

    outs = sum_adam(g_ada_w[0][None], ada_w, m_ada_w, v_ada_w, "adam_ada_w0", lead=0)
    results["ada_w"] = tuple(sum_adam(g_ada_w[1][None], ada_w, m_ada_w, v_ada_w, "adam_ada_w1", lead=1, prev=outs))

    dw1 = jnp.stack([dWb["w1"][:, :LORA_PAD], dWb["w1"][:, LORA_PAD:]])
    da1 = jnp.stack([dWb["a1"][:, :LORA_PAD], dWb["a1"][:, LORA_PAD:]])
    small_g = [dPs["mix"], dPs["w0"], dPs["a0"], dPs["conv"], dw1, da1, dWb["w2d"], dWb["a2d"], dWb["g1"], dWb["g2"]]
    small_names = ["rw_mix", "rw_w0", "rw_a0", "sc_conv", "rw_w1", "rw_a1", "rw_w2", "rw_a2", "rw_g1", "rw_g2"]

    def padded_local(nm_, src):
        a = src[nm_][0]
        if nm_ in ("rw_w1", "rw_a1"):
            return jnp.pad(a, ((0, 0), (0, 0), (0, pad_r)))
        if nm_ in ("rw_w2", "rw_a2"):
            return jnp.pad(a, ((0, 0), (0, pad_r), (0, 0)))
        return a

    for sl, dt, tag in small_groups:
        blocks = [_scatter_dim(gf, dm) for gf, dm in zip(small_g[sl], small_dim[sl])]
        sm_recv = reduce_scatter_exchange(_pack(blocks, dt, lead=1), None, "rs_small_" + tag)
        sm_out = sum_adam(sm_recv, _pack([padded_local(n_, W) for n_ in small_names[sl]]),
                          _pack([padded_local(n_, Mo) for n_ in small_names[sl]]),
                          _pack([padded_local(n_, Vo) for n_ in small_names[sl]]), "adam_small_" + tag)
        for nm_, vals in zip(small_names[sl], zip(*[_unpack(o, small_shapes[sl]) for o in sm_out])):
            if nm_ in ("rw_w1", "rw_a1"):
                vals = tuple(a[:, :, :lr] for a in vals)
            if nm_ in ("rw_w2", "rw_a2"):
                vals = tuple(a[:, :lr, :] for a in vals)
            results[nm_] = tuple(a[None] for a in vals)

    def rs_adam(key, nm_, lead, prev=None):
        recv = sink[key] if key in sink else reduce_scatter_exchange(dWb[key], _W_AXIS[key], "rs_" + key)
        return sum_adam(recv, W[nm_], Mo[nm_], Vo[nm_], "adam_" + key, lead=lead, prev=prev)

    for nm_, key in (("rw_wr", "wr"), ("rw_wk", "wk"), ("rw_wv", "wv"), ("rw_wo", "wo"), ("sc_win", "win"),
                     ("sc_wout", "wout")):
        results[nm_] = tuple(rs_adam(key, nm_, 0))
    for nm_, key in (("ffn_w13", "w13"), ("ffn_w2", "w2")):
        results[nm_] = tuple(rs_adam(f"{key}_1", nm_, 1, prev=rs_adam(f"{key}_0", nm_, 0)))

    grads = [results[n_][0] for n_ in names]
    deltas = [results[n_][1] for n_ in names]
    new_m = [results[n_][2] for n_ in names]
    new_v = [results[n_][3] for n_ in names]
    return (loss, dx[None], *grads, *deltas, *new_m, *new_v)
```

```python
import functools
import math

import numpy as np
import jax
import jax.numpy as jnp
from jax import lax
from jax.experimental import pallas as pl
from jax.experimental.pallas import tpu as pltpu

F32 = jnp.float32
BF16 = jnp.bfloat16

N_DEV = 8
HEAD = 64
LANES = 128
GRID_W = 64
LORA_PAD = 128
NORM_EPS = 1e-6
GN_EPS = 64e-5
ADAM_LR, ADAM_B1, ADAM_B2, ADAM_EPS, ADAM_WD, ADAM_STEP = 0.001, 0.9, 0.999, 1e-08, 0.01, 10
VMEM_LIMIT = 52 * 1024 * 1024
SCAN_CHUNK = 8
SCAN_UNROLL = 4
HI = lax.Precision.HIGHEST


def _cparams(sem):
    return pltpu.CompilerParams(dimension_semantics=sem, vmem_limit_bytes=VMEM_LIMIT)


def _pick(n, cap, quantum=LANES):
    best = None
    for t in range(quantum, min(n, cap) + 1, quantum):
        if n % t == 0:
            best = t
    return n if best is None else best


MM_VMEM_BUDGET = 36 * 1024 * 1024


def _divisors(n, cap, quantum=LANES):
    ds = [t for t in range(quantum, min(n, cap) + 1, quantum) if n % t == 0]
    return sorted(ds, reverse=True) or [n]


def _mm_tiles(M, N, K, sa, sb, so):
    tms, tns, tks = _divisors(M, 1024), _divisors(N, 1024), _divisors(K, 2816)
    im = jn = ik = 0

    def est(tm, tn, tk):
        b = 2 * (tm * tk * sa + tk * tn * sb) + 2 * tm * tn * so + tm * tn * 4
        b += tm * tk * 2 if sa == 4 else 0
        b += tk * tn * 2 if sb == 4 else 0
        return b + (tm * tn * 4 if tk < K else 0)

    while est(tms[im], tns[jn], tks[ik]) > MM_VMEM_BUDGET:
        if tms[im] >= tns[jn] and im + 1 < len(tms):
            im += 1
        elif jn + 1 < len(tns):
            jn += 1
        elif im + 1 < len(tms):
            im += 1
        elif ik + 1 < len(tks):
            ik += 1
        else:
            break
    return tms[im], tns[jn], tks[ik]


def _mm(a, b, *, ta=False, tb=False, out_dtype, name):
    if ta:
        K, M = a.shape
    else:
        M, K = a.shape
    if tb:
        N, Kb = b.shape
    else:
        Kb, N = b.shape
    assert K == Kb, (a.shape, b.shape, ta, tb)
    tm, tn, tk = _mm_tiles(M, N, K, a.dtype.itemsize, b.dtype.itemsize, jnp.dtype(out_dtype).itemsize)
    nk = K // tk
    dims = (((0 if ta else 1,), (1 if tb else 0,)), ((), ()))

    def body(a_ref, b_ref, o_ref, *acc):
        part = lax.dot_general(a_ref[...].astype(BF16), b_ref[...].astype(BF16), dims, preferred_element_type=F32)
        if nk == 1:
            o_ref[...] = part.astype(o_ref.dtype)
            return
        acc_ref, = acc
        k = pl.program_id(2)

        @pl.when(k == 0)
        def _():
            acc_ref[...] = part

        @pl.when(k > 0)
        def _():
            acc_ref[...] += part

        @pl.when(k == nk - 1)
        def _():
            o_ref[...] = acc_ref[...].astype(o_ref.dtype)

    a_spec = pl.BlockSpec((tk, tm), lambda i, j, k: (k, i)) if ta else pl.BlockSpec((tm, tk), lambda i, j, k: (i, k))
    b_spec = pl.BlockSpec((tn, tk), lambda i, j, k: (j, k)) if tb else pl.BlockSpec((tk, tn), lambda i, j, k: (k, j))
    return pl.pallas_call(
        body, name=name, grid=(M // tm, N // tn, nk),
        in_specs=[a_spec, b_spec],
        out_specs=pl.BlockSpec((tm, tn), lambda i, j, k: (i, j)),
        out_shape=jax.ShapeDtypeStruct((M, N), out_dtype),
        scratch_shapes=[pltpu.VMEM((tm, tn), F32)] if nk > 1 else [],
        compiler_params=_cparams(("parallel", "parallel", "arbitrary")),
    )(a, b)


@functools.partial(jax.custom_vjp, nondiff_argnums=(2, 3))
def linear(a, w, out_dtype, name):
    return _mm(a, w, out_dtype=out_dtype, name=name + "_fwd")


def _linear_fwd(a, w, out_dtype, name):
    return _mm(a, w, out_dtype=out_dtype, name=name + "_fwd"), (a, w)


def _linear_bwd(out_dtype, name, res, g):
    a, w = res
    da = _mm(g, w, tb=True, out_dtype=a.dtype, name=name + "_da")
    dw = _mm(a, g, ta=True, out_dtype=w.dtype, name=name + "_dw")
    return da, dw


linear.defvjp(_linear_fwd, _linear_bwd)


def _rw_specs(tiles, col_offs, vecs, consts, tr, tc, nb0):
    tile_specs = [pl.BlockSpec((tr, tc), functools.partial(lambda j, i, off: (i, j + off), off=off))
                  for _, off in zip(tiles, col_offs)]

    def vec_map(S):
        if S == 1:
            return lambda j, i: (0, 0, j)
        return lambda j, i: (jnp.where(i < nb0, 0, 1), 0, j)

    vec_specs = [pl.BlockSpec((None, 1, tc), vec_map(v.shape[0])) for v in vecs]
    const_specs = [pl.BlockSpec(c.shape, lambda j, i: (0, 0)) for c in consts]
    return tile_specs, vec_specs, const_specs


def _rw_forward(name, f, tiles, col_offs, vecs, consts, out_dtypes, tr, tc, nb0, width):
    n = tiles[0].shape[0]
    nt, nv, nc = len(tiles), len(vecs), len(consts)
    tile_specs, vec_specs, const_specs = _rw_specs(tiles, col_offs, vecs, consts, tr, tc, nb0)

    def body(*refs):
        ins = [r[...].astype(F32) for r in refs[:nt]] + [r[...] for r in refs[nt:nt + nv + nc]]
        outs = f(*ins)
        for o_ref, o in zip(refs[nt + nv + nc:], outs):
            o_ref[...] = o.astype(o_ref.dtype)

    return pl.pallas_call(
        body, name=name + "_fwd", grid=(width // tc, n // tr),
        in_specs=tile_specs + vec_specs + const_specs,
        out_specs=[pl.BlockSpec((tr, tc), lambda j, i: (i, j)) for _ in out_dtypes],
        out_shape=[jax.ShapeDtypeStruct((n, width), dt) for dt in out_dtypes],
        compiler_params=_cparams(("parallel", "parallel")),
    )(*tiles, *vecs, *consts)


def _rw_backward(name, f, tiles, col_offs, vecs, consts, douts, tr, tc, nb0, width):
    n = tiles[0].shape[0]
    nt, nv, nc, no = len(tiles), len(vecs), len(consts), len(douts)
    tile_specs, vec_specs, const_specs = _rw_specs(tiles, col_offs, vecs, consts, tr, tc, nb0)

    def body(*refs):
        t_in = [r[...].astype(F32) for r in refs[:nt]]
        v_in = [r[...] for r in refs[nt:nt + nv]]
        c_in = [r[...] for r in refs[nt + nv:nt + nv + nc]]
        d_in = tuple(r[...].astype(F32) for r in refs[nt + nv + nc:nt + nv + nc + no])
        o_refs = refs[nt + nv + nc + no:]
        _, vjp = jax.vjp(lambda *tv: tuple(f(*tv, *c_in)), *t_in, *v_in)
        grads = vjp(d_in)
        for o_ref, g in zip(o_refs[:nt], grads[:nt]):
            o_ref[...] = g.astype(o_ref.dtype)
        i = pl.program_id(1)
        for o_ref, g, v in zip(o_refs[nt:], grads[nt:], vecs):
            first = jnp.logical_or(i == 0, i == nb0) if v.shape[0] == 2 else i == 0

            @pl.when(first)
            def _(o_ref=o_ref, g=g):
                o_ref[...] = g

            @pl.when(jnp.logical_not(first))
            def _(o_ref=o_ref, g=g):
                o_ref[...] += g

    dout_specs = [pl.BlockSpec((tr, tc), lambda j, i: (i, j)) for _ in douts]
    out_specs = [pl.BlockSpec((tr, tc), lambda j, i: (i, j)) for _ in tiles] + list(vec_specs)
    out_shape = ([jax.ShapeDtypeStruct((n, width), t.dtype) for t in tiles]
                 + [jax.ShapeDtypeStruct(v.shape, F32) for v in vecs])
    return pl.pallas_call(
        body, name=name + "_bwd", grid=(width // tc, n // tr),
        in_specs=tile_specs + vec_specs + const_specs + dout_specs,
        out_specs=out_specs, out_shape=out_shape,
        compiler_params=_cparams(("parallel", "arbitrary")),
    )(*tiles, *vecs, *consts, *douts)


def make_rowwise(name, f, out_dtypes, tr, tc, consts=(), nb0=-1):
    consts = tuple(consts)

    @jax.custom_vjp
    def op(tiles, vecs):
        w = tiles[0].shape[1]
        return tuple(_rw_forward(name, f, tiles, (0,) * len(tiles), vecs, consts, out_dtypes, tr, min(tc, w), nb0, w))

    def op_fwd(tiles, vecs):
        return op(tiles, vecs), (tiles, vecs)

    def op_bwd(res, douts):
        tiles, vecs = res
        w = tiles[0].shape[1]
        g = _rw_backward(name, f, tiles, (0,) * len(tiles), vecs, consts, tuple(douts), tr, min(tc, w), nb0, w)
        return tuple(g[:len(tiles)]), tuple(g[len(tiles):])

    op.defvjp(op_fwd, op_bwd)
    return op


def _f_norm_mod(x, g, sh, sc):
    hn = x * lax.rsqrt(jnp.mean(x * x, axis=-1, keepdims=True) + NORM_EPS)
    return ((hn * g) * (1.0 + sc) + sh,)


def _f_res_norm_mod(x, y, gate, g, sh, sc):
    x1 = x + gate * y
    hn = x1 * lax.rsqrt(jnp.mean(x1 * x1, axis=-1, keepdims=True) + NORM_EPS)
    return x1, (hn * g) * (1.0 + sc) + sh


def _head_sum_3pass(t, gmat):
    hi = t.astype(BF16)
    r1 = t - hi.astype(F32)
    mid = r1.astype(BF16)
    lo = (r1 - mid.astype(F32)).astype(BF16)
    g = gmat.astype(BF16)
    dot = lambda u: jnp.dot(u, g, preferred_element_type=F32)
    return dot(hi) + dot(mid) + dot(lo)


@jax.custom_vjp
def _head_sum(t, gmat):
    return _head_sum_3pass(t, gmat)


def _head_sum_fwd(t, gmat):
    return _head_sum_3pass(t, gmat), gmat


def _head_sum_bwd(gmat, ct):
    return _head_sum_3pass(ct, gmat), None


_head_sum.defvjp(_head_sum_fwd, _head_sum_bwd)


def _f_prep(k, lw0, lw1, la0, la1, kkp, kap, w00, w01, a00, a01, gmat):
    t = k * kkp
    kk = t / jnp.maximum(jnp.sqrt(_head_sum(t * t, gmat)), 1e-12)
    outs = [kk]
    decs, kds, sigs = [], [], []
    for lw, la, w0, a0 in ((lw0, la0, w00, a00), (lw1, la1, w01, a01)):
        decs.append(jnp.exp(-jax.nn.sigmoid(w0 + lw) * float(np.exp(-0.5))))
        a = jax.nn.sigmoid(a0 + la)
        sigs.append(a)
        kds.append(k * (1.0 + (a - 1.0) * kap))
    return tuple(outs + decs + kds + sigs)


def _f_readout(y0, y1, r, kd0, kd1, v, g, rk, lnw, lnb, gmat):
    y = y0 + y1
    mu = _head_sum(y, gmat) * (1.0 / HEAD)
    d = y - mu
    var = _head_sum(d * d, gmat) * (1.0 / HEAD)
    o = d * lax.rsqrt(var + GN_EPS) * lnw + lnb
    bonus = _head_sum(r * (kd0 + kd1) * rk, gmat) * v
    return ((o + bonus) * g,)


def _f_swiglu(a, b):
    return (jax.nn.silu(a) * b,)


def swiglu_act(ab, name):
    t, f2 = ab.shape
    fdim = f2 // 2
    tr, tc = _pick(t, 512, 8), _pick(fdim, 512)
    offs = (0, fdim // tc)

    @jax.custom_vjp
    def op(ab_):
        return _rw_forward(name, _f_swiglu, (ab_, ab_), offs, (), (), (BF16,), tr, tc, -1, fdim)[0]

    def op_fwd(ab_):
        return op(ab_), ab_

    def op_bwd(ab_, dact):
        da, db = _rw_backward(name, _f_swiglu, (ab_, ab_), offs, (), (), (dact,), tr, tc, -1, fdim)
        return (jnp.concatenate([da, db], axis=1),)

    op.defvjp(op_fwd, op_bwd)
    return op(ab)


def _row_iota(n, tc):
    return lax.broadcasted_iota(jnp.int32, (n, tc), 0)


def _shift_rows(x, s, keep):
    n = x.shape[0]
    return jnp.where(keep, pltpu.roll(x, s % n, 0), 0.0)


def _unshift_rows(d, s, keep):
    n = d.shape[0]
    return pltpu.roll(jnp.where(keep, d, 0.0), (-s) % n, 0)


def _ctx_shift_spec(L, tc, quarter):
    row = _row_iota(L, tc)
    if quarter < 2:
        return 1, row >= 1
    return -1, row < L - 1


def _grid_shift_spec(T, tc, quarter):
    row = _row_iota(T, tc)
    col = jnp.bitwise_and(row, GRID_W - 1)
    if quarter == 0:
        return 1, col != 0
    if quarter == 1:
        return -1, col != GRID_W - 1
    if quarter == 2:
        return GRID_W, row >= GRID_W
    return -GRID_W, row < T - GRID_W


def _shift_mix_fwd_call(h, mix3, L):
    n, d = h.shape
    T = n - L
    tc = _pick(d // 4, 256)
    nq = (d // 4) // tc

    def body(h_ref, mix_ref, *o_refs):
        q = pl.program_id(0) // nq
        for quarter in range(4):
            @pl.when(q == quarter)
            def _(quarter=quarter):
                for lo, cnt, spec in ((0, L, _ctx_shift_spec), (L, T, _grid_shift_spec)):
                    hh = h_ref[pl.ds(lo, cnt), :]
                    s, keep = spec(cnt, tc, quarter)
                    xx = _shift_rows(hh, s, keep) - hh
                    for m in range(6):
                        o_refs[m][pl.ds(lo, cnt), :] = (hh + xx * mix_ref[m]).astype(BF16)

    return pl.pallas_call(
        body, name="shift_mix_fwd", grid=(d // tc,),
        in_specs=[pl.BlockSpec((n, tc), lambda j: (0, j)), pl.BlockSpec((6, 1, tc), lambda j: (0, 0, j))],
        out_specs=[pl.BlockSpec((n, tc), lambda j: (0, j)) for _ in range(6)],
        out_shape=[jax.ShapeDtypeStruct((n, d), BF16) for _ in range(6)],
        compiler_params=_cparams(("parallel",)),
    )(h, mix3)


def _shift_mix_bwd_call(h, mix3, douts, L):
    n, d = h.shape
    T = n - L
    tc = _pick(d // 4, 256)
    nq = (d // 4) // tc

    def body(h_ref, mix_ref, d0, d1, d2, d3, d4, d5, dh_ref, dmix_ref):
        d_refs = (d0, d1, d2, d3, d4, d5)
        q = pl.program_id(0) // nq
        for quarter in range(4):
            @pl.when(q == quarter)
            def _(quarter=quarter):
                dmix = [jnp.zeros((1, tc), F32) for _ in range(6)]
                for lo, cnt, spec in ((0, L, _ctx_shift_spec), (L, T, _grid_shift_spec)):
                    hh = h_ref[pl.ds(lo, cnt), :]
                    s, keep = spec(cnt, tc, quarter)
                    xx = _shift_rows(hh, s, keep) - hh
                    direct = jnp.zeros((cnt, tc), F32)
                    shifted = jnp.zeros((cnt, tc), F32)
                    for m in range(6):
                        dm = d_refs[m][pl.ds(lo, cnt), :].astype(F32)
                        mx = mix_ref[m]
                        direct = direct + dm * (1.0 - mx)
                        shifted = shifted + dm * mx
                        dmix[m] = dmix[m] + jnp.sum(dm * xx, axis=0, keepdims=True)
                    dh_ref[pl.ds(lo, cnt), :] = direct + _unshift_rows(shifted, s, keep)
                for m in range(6):
                    dmix_ref[m] = dmix[m]

    tile = pl.BlockSpec((n, tc), lambda j: (0, j))
    return pl.pallas_call(
        body, name="shift_mix_bwd", grid=(d // tc,),
        in_specs=[tile, pl.BlockSpec((6, 1, tc), lambda j: (0, 0, j))] + [tile] * 6,
        out_specs=[tile, pl.BlockSpec((6, 1, tc), lambda j: (0, 0, j))],
        out_shape=[jax.ShapeDtypeStruct((n, d), F32), jax.ShapeDtypeStruct((6, 1, d), F32)],
        compiler_params=_cparams(("parallel",)),
    )(h, mix3, *douts)


@functools.partial(jax.custom_vjp, nondiff_argnums=(2,))
def shift_mix(h, mix3, L):
    return tuple(_shift_mix_fwd_call(h, mix3, L))


def _shift_mix_fwd(h, mix3, L):
    return tuple(_shift_mix_fwd_call(h, mix3, L)), (h, mix3)


def _shift_mix_bwd(L, res, douts):
    h, mix3 = res
    dh, dmix = _shift_mix_bwd_call(h, mix3, tuple(douts), L)
    return dh, dmix


shift_mix.defvjp(_shift_mix_fwd, _shift_mix_bwd)


def _conv_specs(T, d, tc):
    nd = d // tc
    ins = [pl.BlockSpec((T, tc), functools.partial(lambda j, off: (0, j + off), off=o * nd)) for o in range(3)]
    return ins, pl.BlockSpec((3, 1, tc), lambda j: (0, 0, j))


def _conv_terms(gc, u, tc):
    T = gc.shape[0]
    row = _row_iota(T, tc)
    z = gc * u
    return z, _shift_rows(z, 1, row >= 1), _shift_rows(z, -1, row < T - 1), row


def _conv_fwd_call(guc, cw3):
    T, d3 = guc.shape
    d = d3 // 3
    tc = _pick(d, 256)
    ins, wspec = _conv_specs(T, d, tc)

    def body(gb_ref, gc_ref, u_ref, w_ref, p_ref):
        z, zp, zn, _ = _conv_terms(gc_ref[...].astype(F32), u_ref[...].astype(F32), tc)
        conv = zp * w_ref[0] + z * w_ref[1] + zn * w_ref[2]
        p_ref[...] = (gb_ref[...].astype(F32) * conv).astype(BF16)

    return pl.pallas_call(
        body, name="conv_fwd", grid=(d // tc,), in_specs=ins + [wspec],
        out_specs=pl.BlockSpec((T, tc), lambda j: (0, j)),
        out_shape=jax.ShapeDtypeStruct((T, d), BF16),
        compiler_params=_cparams(("parallel",)),
    )(guc, guc, guc, cw3)


def _conv_bwd_call(guc, cw3, dp):
    T, d3 = guc.shape
    d = d3 // 3
    tc = _pick(d, 256)
    ins, wspec = _conv_specs(T, d, tc)
    tile = pl.BlockSpec((T, tc), lambda j: (0, j))

    def body(gb_ref, gc_ref, u_ref, w_ref, dp_ref, dgb_ref, dgc_ref, du_ref, dw_ref):
        gc = gc_ref[...].astype(F32)
        u = u_ref[...].astype(F32)
        z, zp, zn, row = _conv_terms(gc, u, tc)
        conv = zp * w_ref[0] + z * w_ref[1] + zn * w_ref[2]
        dpv = dp_ref[...].astype(F32)
        dgb_ref[...] = (dpv * conv).astype(dgb_ref.dtype)
        dconv = dpv * gb_ref[...].astype(F32)
        dz = (_shift_rows(dconv, -1, row < T - 1) * w_ref[0] + dconv * w_ref[1]
              + _shift_rows(dconv, 1, row >= 1) * w_ref[2])
        dgc_ref[...] = (dz * u).astype(dgc_ref.dtype)
        du_ref[...] = (dz * gc).astype(du_ref.dtype)
        dw_ref[0] = jnp.sum(dconv * zp, axis=0, keepdims=True)
        dw_ref[1] = jnp.sum(dconv * z, axis=0, keepdims=True)
        dw_ref[2] = jnp.sum(dconv * zn, axis=0, keepdims=True)

    return pl.pallas_call(
        body, name="conv_bwd", grid=(d // tc,), in_specs=ins + [wspec, tile],
        out_specs=[tile, tile, tile, wspec],
        out_shape=[jax.ShapeDtypeStruct((T, d), guc.dtype)] * 3 + [jax.ShapeDtypeStruct((3, 1, d), F32)],
        compiler_params=_cparams(("parallel",)),
    )(guc, guc, guc, cw3, dp)


@jax.custom_vjp
def gated_conv(guc, cw3):
    return _conv_fwd_call(guc, cw3)


def _gated_conv_fwd(guc, cw3):
    return _conv_fwd_call(guc, cw3), (guc, cw3)


def _gated_conv_bwd(res, dp):
    guc, cw3 = res
    dgb, dgc, du, dw = _conv_bwd_call(guc, cw3, dp)
    return jnp.concatenate([dgb, dgc, du], axis=1), dw


gated_conv.defvjp(_gated_conv_fwd, _gated_conv_bwd)


def _chunk_map(nchunk, nctx_chunk, reverse):
    if not reverse:
        return lambda c: c
    return lambda c: jnp.where(c < nctx_chunk, nctx_chunk - 1 - c, nchunk - 1 - (c - nctx_chunk))


def _spread_bf16(row_ref, dst_scr, ni, C):
    packed = [pltpu.bitcast(row_ref[tt].astype(BF16), jnp.int32) for tt in range(C)]
    lane = lax.broadcasted_iota(jnp.int32, packed[0].shape, 1)
    for i in range(ni):
        idx = jnp.where(lane < HEAD, 2 * i, HEAD + 1 + 2 * i).astype(jnp.int32)
        for tt in range(C):
            got = jnp.take_along_axis(packed[tt], idx, axis=1)
            dst_scr[tt, i] = pltpu.bitcast(got, BF16).astype(F32)


def _half_sums(p, lo_mask):
    lo = jnp.sum(jnp.where(lo_mask, p, 0.0), axis=1, keepdims=True)
    hi = jnp.sum(jnp.where(lo_mask, 0.0, p), axis=1, keepdims=True)
    return jnp.where(lo_mask, lo, hi)


def _half_sums_mxu(ps, gmat, passes):
    p = jnp.concatenate(ps, axis=0)
    hi = p.astype(BF16)
    s = jnp.dot(hi, gmat, preferred_element_type=F32)
    if passes == 2:
        lo = (p - hi.astype(F32)).astype(BF16)
        s = s + jnp.dot(lo, gmat, preferred_element_type=F32)
    nh = ps[0].shape[0]
    return [s[i * nh:(i + 1) * nh] for i in range(len(ps))]


def _split_row(sums, lane, nh):
    acc = jnp.zeros((nh, LANES), F32)
    for i, s in enumerate(sums):
        acc = acc + jnp.where(jnp.logical_or(lane == 2 * i, lane == HEAD + 1 + 2 * i), s, 0.0)
    return acc


def _wkv_fwd_call(r2, w2, kd2, kk2, as2, v2, nctx, reverse, xchg_arrs=(), xchg_specs=()):
    n, nh, _ = r2.shape
    C = SCAN_CHUNK
    ni = HEAD // 2
    nchunk = n // C
    cmap = _chunk_map(nchunk, nctx // C, reverse)
    nx = len(xchg_arrs)

    def body(*refs):
        g_ref, refs = refs[0], refs[1:]
        r_ref, w_ref, kd_ref, kk_ref, as_ref, v_ref = refs[:6]
        x_in = refs[6:6 + nx]
        y_ref, sa_ref, sp_ref = refs[6 + nx:9 + nx]
        x_out = refs[9 + nx:9 + 2 * nx]
        s_scr, vc_scr = refs[9 + 2 * nx:11 + 2 * nx]
        if nx:
            _fused_exchanges(list(zip(x_in, x_out)), xchg_specs, *refs[11 + 2 * nx:], first=pl.program_id(0) == 0)

        @pl.when(pl.program_id(0) == 0)
        def _():
            s_scr[...] = jnp.zeros_like(s_scr)

        lane = lax.broadcasted_iota(jnp.int32, (nh, LANES), 1)
        lo_mask = lane < HEAD
        _spread_bf16(v_ref, vc_scr, ni, C)

        def make_step(with_y):
            def step(j, carry):
                t = (C - 1 - j) if reverse else j
                kk = kk_ref[t]
                a2 = -kk
                b2 = kk * as_ref[t]
                w = w_ref[t]
                k = kd_ref[t]
                r = r_ref[t]
                sas = []
                for i in range(ni):
                    si = s_scr[i]
                    sp_ref[t, i] = si
                    sas.append(_half_sums(si * a2, lo_mask))
                qs = []
                for i in range(ni):
                    sn = s_scr[i] * w + sas[i] * b2 + vc_scr[t, i] * k
                    s_scr[i] = sn
                    if with_y:
                        qs.append(sn * r)
                if with_y:
                    y_ref[t] = _split_row(_half_sums_mxu(qs, g_ref[...], 2), lane, nh)
                else:
                    y_ref[t] = jnp.zeros((nh, LANES), F32)
                sa_ref[t] = _split_row(sas, lane, nh)
                return carry
            return step

        is_ctx = pl.program_id(0) < nctx // C

        @pl.when(is_ctx)
        def _():
            lax.fori_loop(0, C, make_step(False), 0)

        @pl.when(jnp.logical_not(is_ctx))
        def _():
            lax.fori_loop(0, C, make_step(True), 0, unroll=SCAN_UNROLL)

        if nx:
            _fused_exchanges(list(zip(x_in, x_out)), xchg_specs, *refs[11 + 2 * nx:],
                             last=pl.program_id(0) == nchunk - 1)

    tok = pl.BlockSpec((C, nh, LANES), lambda c: (cmap(c), 0, 0))
    return pl.pallas_call(
        body, name="wkv_fwd_rev" if reverse else "wkv_fwd", grid=(nchunk,),
        in_specs=[pl.BlockSpec((LANES, LANES), lambda c: (0, 0))] + [tok] * 6 + [_ANY] * nx,
        out_specs=[tok, tok, pl.BlockSpec((C, ni, nh, LANES), lambda c: (cmap(c), 0, 0, 0))] + [_ANY] * nx,
        out_shape=[jax.ShapeDtypeStruct((n, nh, LANES), F32), jax.ShapeDtypeStruct((n, nh, LANES), F32),
                   jax.ShapeDtypeStruct((n, ni, nh, LANES), F32)] + _exchange_out_shapes(xchg_arrs, xchg_specs),
        scratch_shapes=[pltpu.VMEM((ni, nh, LANES), F32), pltpu.VMEM((C, ni, nh, LANES), F32)]
        + (_exchange_sems(nx) if nx else []),
        compiler_params=_cparams(("arbitrary",)),
    )(_head_group_matrix(LANES).astype(BF16), r2, w2, kd2, kk2, as2, v2, *xchg_arrs)


def _wkv_bwd_call(r2, w2, kd2, kk2, as2, v2, sa, sprev, dy, nctx, reverse, xchg_arrs=(), xchg_specs=()):
    n, nh, _ = r2.shape
    C = SCAN_CHUNK
    ni = HEAD // 2
    nchunk = n // C
    fmap = _chunk_map(nchunk, nctx // C, reverse)
    cmap = lambda c: fmap(nchunk - 1 - c)
    nx = len(xchg_arrs)

    def body(*refs):
        g_ref, refs = refs[0], refs[1:]
        r_ref, w_ref, kd_ref, kk_ref, as_ref, v_ref, sa_ref, sp_ref, dy_ref = refs[:9]
        x_in = refs[9:9 + nx]
        dr_ref, dw_ref, dkd_ref, dkk_ref, das_ref, dv_ref = refs[9 + nx:15 + nx]
        x_out = refs[15 + nx:15 + 2 * nx]
        ds_scr, vc_scr, sac_scr, dyc_scr = refs[15 + 2 * nx:19 + 2 * nx]
        if nx:
            _fused_exchanges(list(zip(x_in, x_out)), xchg_specs, *refs[19 + 2 * nx:], first=pl.program_id(0) == 0)

        @pl.when(pl.program_id(0) == 0)
        def _():
            ds_scr[...] = jnp.zeros_like(ds_scr)

        lane = lax.broadcasted_iota(jnp.int32, (nh, LANES), 1)
        lo_mask = lane < HEAD
        _spread_bf16(v_ref, vc_scr, ni, C)
        _spread_bf16(sa_ref, sac_scr, ni, C)
        is_ctx = pl.program_id(0) >= nchunk - nctx // C

        @pl.when(jnp.logical_not(is_ctx))
        def _():
            _spread_bf16(dy_ref, dyc_scr, ni, C)

        def make_step(with_dy):
            def step(j, carry):
                t = j if reverse else (C - 1 - j)
                kk = kk_ref[t]
                sig = as_ref[t]
                a2 = -kk
                b2 = kk * sig
                w = w_ref[t]
                k = kd_ref[t]
                r = r_ref[t]
                zero = jnp.zeros((nh, LANES), F32)
                acc_dk, acc_db, acc_dw, acc_g, acc_sady, acc_vdy, acc_da = zero, zero, zero, zero, zero, zero, zero
                dvp, dsas = [], []
                for i in range(ni):
                    sp = sp_ref[t, i]
                    vc = vc_scr[t, i]
                    sac = sac_scr[t, i]
                    ds = ds_scr[i]
                    if with_dy:
                        dyc = dyc_scr[t, i]
                        ds = ds + dyc * r
                        ds_scr[i] = ds
                        acc_g = acc_g + sp * dyc
                    dvp.append(ds * k)
                    dsas.append(_half_sums(ds * b2, lo_mask))
                    acc_dk = acc_dk + ds * vc
                    acc_db = acc_db + ds * sac
                    acc_dw = acc_dw + ds * sp
                for i in range(ni):
                    acc_da = acc_da + sp_ref[t, i] * dsas[i]
                    ds_scr[i] = ds_scr[i] * w + dsas[i] * a2
                if with_dy:
                    dyr = jnp.where(jnp.bitwise_and(lane, 1) == (lane >= HEAD).astype(jnp.int32), dy_ref[t], 0.0)
                    acc_sady = _half_sums(sa_ref[t] * dyr, lo_mask)
                    acc_vdy = _half_sums(v_ref[t] * dyr, lo_mask)
                dr_ref[t] = acc_g * w + b2 * acc_sady + k * acc_vdy
                dw_ref[t] = acc_dw
                dkd_ref[t] = acc_dk
                dkk_ref[t] = acc_db * sig - acc_da
                das_ref[t] = acc_db * kk
                dv_ref[t] = _split_row(_half_sums_mxu(dvp, g_ref[...], 1), lane, nh)
                return carry
            return step

        @pl.when(is_ctx)
        def _():
            lax.fori_loop(0, C, make_step(False), 0)

        @pl.when(jnp.logical_not(is_ctx))
        def _():
            lax.fori_loop(0, C, make_step(True), 0, unroll=SCAN_UNROLL)

        if nx:
            _fused_exchanges(list(zip(x_in, x_out)), xchg_specs, *refs[19 + 2 * nx:],
                             last=pl.program_id(0) == nchunk - 1)

    tok = pl.BlockSpec((C, nh, LANES), lambda c: (cmap(c), 0, 0))
    big = pltpu.VMEM((C, ni, nh, LANES), F32)
    return pl.pallas_call(
        body, name="wkv_bwd_rev" if reverse else "wkv_bwd", grid=(nchunk,),
        in_specs=[pl.BlockSpec((LANES, LANES), lambda c: (0, 0))] + [tok] * 7
        + [pl.BlockSpec((C, ni, nh, LANES), lambda c: (cmap(c), 0, 0, 0)), tok] + [_ANY] * nx,
        out_specs=[tok] * 6 + [_ANY] * nx,
        out_shape=[jax.ShapeDtypeStruct((n, nh, LANES), F32)] * 6 + _exchange_out_shapes(xchg_arrs, xchg_specs),
        scratch_shapes=[pltpu.VMEM((ni, nh, LANES), F32), big, big, big] + (_exchange_sems(nx) if nx else []),
        compiler_params=_cparams(("arbitrary",)),
    )(_head_group_matrix(LANES).astype(BF16), r2, w2, kd2, kk2, as2, v2, sa, sprev, dy, *xchg_arrs)


def _tile_heads(t):
    n, d = t.shape
    th = t.reshape(n, d // HEAD, HEAD)
    return jnp.concatenate([th, th], axis=-1)


def loss_head(x3, fo, tgt, gate, g):
    T, d = x3.shape
    tr = _pick(T, 128, 8)

    def body(x_ref, f_ref, t_ref, gate_ref, g_ref, loss_ref, dx_ref, df_ref, dgate_ref, dg_ref):
        tg = t_ref[...]

        def fl(x, fo_, gate_, g_):
            x4 = x + gate_ * fo_
            y = (x4 * lax.rsqrt(jnp.mean(x4 * x4, axis=-1, keepdims=True) + NORM_EPS)) * g_
            return 0.5 * jnp.sum(jnp.mean(jnp.square(y - tg), axis=-1))

        val, vjp = jax.vjp(fl, x_ref[...], f_ref[...], gate_ref[...], g_ref[...])
        dx, dfo, dgate, dg = vjp(jnp.ones((), F32))
        dx_ref[...] = dx
        df_ref[...] = dfo
        i = pl.program_id(0)

        @pl.when(i == 0)
        def _():
            loss_ref[...] = jnp.zeros_like(loss_ref)
            dgate_ref[...] = jnp.zeros_like(dgate_ref)
            dg_ref[...] = jnp.zeros_like(dg_ref)

        loss_ref[...] += jnp.full(loss_ref.shape, val, F32)
        dgate_ref[...] += dgate
        dg_ref[...] += dg

    tile = pl.BlockSpec((tr, d), lambda i: (i, 0))
    vec = pl.BlockSpec((1, d), lambda i: (0, 0))
    return pl.pallas_call(
        body, name="loss_head", grid=(T // tr,),
        in_specs=[tile, tile, tile, vec, vec],
        out_specs=[pl.BlockSpec((8, LANES), lambda i: (0, 0)), tile, tile, vec, vec],
        out_shape=[jax.ShapeDtypeStruct((8, LANES), F32), jax.ShapeDtypeStruct((T, d), F32),
                   jax.ShapeDtypeStruct((T, d), F32), jax.ShapeDtypeStruct((1, d), F32),
                   jax.ShapeDtypeStruct((1, d), F32)],
        compiler_params=_cparams(("arbitrary",)),
    )(x3, fo, tgt, gate, g)


def sum_adam(parts, w, m, v, name, lead=None, prev=None):
    P, R, Cc = parts.shape
    tc = _pick(Cc, 1024)
    tr = _pick(R, max(8, (256 * 1024) // tc), 16 if parts.dtype == BF16 else 8)
    prev = tuple(prev) if prev is not None else ()

    def body(p_ref, w_ref, m_ref, v_ref, *rest):
        g_ref, d_ref, nm_ref, nv_ref = rest[len(prev):]
        g = p_ref[0].astype(F32)
        for s in range(1, P):
            g = g + p_ref[s].astype(F32)
        m_new = ADAM_B1 * m_ref[...] + (1.0 - ADAM_B1) * g
        v_new = ADAM_B2 * v_ref[...] + (1.0 - ADAM_B2) * jnp.square(g)
        m_hat = m_new / (1.0 - ADAM_B1 ** ADAM_STEP)
        v_hat = v_new / (1.0 - ADAM_B2 ** ADAM_STEP)
        g_ref[...] = g
        d_ref[...] = -ADAM_LR * (m_hat / (jnp.sqrt(v_hat) + ADAM_EPS) + ADAM_WD * w_ref[...])
        nm_ref[...] = m_new
        nv_ref[...] = v_new

    if lead is None:
        pspec = pl.BlockSpec((tr, tc), lambda i, j: (i, j))
        oshape = (R, Cc)
    else:
        pspec = pl.BlockSpec((None, tr, tc), lambda i, j: (lead, i, j))
        oshape = w.shape
    return pl.pallas_call(
        body, name=name, grid=(R // tr, Cc // tc),
        in_specs=[pl.BlockSpec((P, tr, tc), lambda i, j: (0, i, j)), pspec, pspec, pspec] + [_ANY] * len(prev),
        out_specs=[pspec] * 4,
        out_shape=[jax.ShapeDtypeStruct(oshape, F32)] * 4,
        input_output_aliases={4 + q: q for q in range(len(prev))},
        compiler_params=_cparams(("parallel", "parallel")),
    )(parts, w, m, v, *prev)


def _me():
    return lax.axis_index("x"), lax.axis_index("y"), lax.axis_index("c")


def _peer(p):
    x, y, c = _me()
    px = 1 - x if p & 4 else x
    py = 1 - y if p & 2 else y
    pc = 1 - c if p & 1 else c
    return (px, py, pc), 4 * px + 2 * py + pc


def _block_view(ref, axis, idx, r, c):
    if axis is None:
        return ref.at[idx]
    if axis == 0:
        return ref.at[pl.ds(idx * r, r), :]
    return ref.at[:, pl.ds(idx * c, c)]


def _exchange_copies(src_of, dst_of, ssem, rsem, lsem, with_recvs):
    x, y, c = _me()
    me = 4 * x + 2 * y + c
    local = pltpu.make_async_copy(src_of(me), dst_of(me), lsem)
    sends, recvs = [], []
    for p in range(1, N_DEV):
        dev, idx = _peer(p)
        sends.append(pltpu.make_async_remote_copy(src_ref=src_of(idx), dst_ref=dst_of(me), send_sem=ssem(p),
                                                  recv_sem=rsem(p), device_id=dev,
                                                  device_id_type=pl.DeviceIdType.MESH))
        if with_recvs:
            recvs.append(pltpu.make_async_remote_copy(src_ref=src_of(idx), dst_ref=dst_of(idx), send_sem=ssem(p),
                                                      recv_sem=rsem(p), device_id=dev,
                                                      device_id_type=pl.DeviceIdType.MESH))
    return local, sends, recvs


def _exchange_start(*args):
    local, sends, _ = _exchange_copies(*args, with_recvs=False)
    local.start()
    for cp in sends:
        cp.start()


def _exchange_wait(*args):
    local, sends, recvs = _exchange_copies(*args, with_recvs=True)
    for cp in recvs:
        cp.wait_recv()
    for cp in sends:
        cp.wait_send()
    local.wait()


def _exchange(src_of, dst_of, send_sems, recv_sems, local_sem):
    args = (src_of, dst_of, lambda p: send_sems.at[p], lambda p: recv_sems.at[p], local_sem)
    _exchange_start(*args)
    _exchange_wait(*args)


def _fused_exchanges(pairs, specs, send_sems, recv_sems, local_sems, first=None, last=None):
    def args(j):
        src, dst = pairs[j]
        kind, axis, r, c = specs[j]
        if kind == "ag":
            src_of = lambda idx: src
            dst_of = lambda idx: _block_view(dst, axis, idx, r, c)
        else:
            src_of = lambda idx: _block_view(src, axis, idx, r, c)
            dst_of = lambda idx: dst.at[idx]
        return (src_of, dst_of, lambda p: send_sems.at[j, p], lambda p: recv_sems.at[j, p], local_sems.at[j])

    if first is not None:
        @pl.when(first)
        def _():
            for j in range(len(pairs)):
                _exchange_start(*args(j))

    if last is not None:
        @pl.when(last)
        def _():
            for j in range(len(pairs)):
                _exchange_wait(*args(j))


def _exchange_out_shapes(arrs, specs):
    out = []
    for a, (kind, axis, r, c) in zip(arrs, specs):
        if kind == "rs":
            out.append(jax.ShapeDtypeStruct((N_DEV, r, c), a.dtype))
        else:
            out.append(jax.ShapeDtypeStruct((N_DEV * r, c) if axis == 0 else (r, N_DEV * c), a.dtype))
    return out


def _exchange_specs(kind, arrs, axes):
    specs = []
    for a, axis in zip(arrs, axes):
        if kind == "ag":
            r, c = a.shape
        elif axis == 0:
            r, c = a.shape[0] // N_DEV, a.shape[1]
        else:
            r, c = a.shape[0], a.shape[1] // N_DEV
        specs.append((kind, axis, r, c))
    return specs


def _exchange_sems(n):
    return [pltpu.SemaphoreType.DMA((n, N_DEV)), pltpu.SemaphoreType.DMA((n, N_DEV)), pltpu.SemaphoreType.DMA((n,))]


_SEMS = [pltpu.SemaphoreType.DMA((N_DEV,)), pltpu.SemaphoreType.DMA((N_DEV,)), pltpu.SemaphoreType.DMA]
_ANY = pl.BlockSpec(memory_space=pl.ANY)


def all_gather(x, axis, name):
    r, c = x.shape
    shape = (N_DEV * r, c) if axis == 0 else (r, N_DEV * c)

    def body(x_ref, o_ref, send_sems, recv_sems, local_sem):
        _exchange(lambda idx: x_ref, lambda idx: _block_view(o_ref, axis, idx, r, c), send_sems, recv_sems, local_sem)

    return pl.pallas_call(
        body, name=name, in_specs=[_ANY], out_specs=_ANY,
        out_shape=jax.ShapeDtypeStruct(shape, x.dtype), scratch_shapes=_SEMS,
    )(x)


def all_gather_stack(x, name):
    r, c = x.shape

    def body(x_ref, o_ref, send_sems, recv_sems, local_sem):
        _exchange(lambda idx: x_ref, lambda idx: o_ref.at[idx], send_sems, recv_sems, local_sem)

    return pl.pallas_call(
        body, name=name, in_specs=[_ANY], out_specs=_ANY,
        out_shape=jax.ShapeDtypeStruct((N_DEV, r, c), x.dtype), scratch_shapes=_SEMS,
    )(x)


def reduce_scatter_exchange(g, axis, name):
    if axis is None:
        _, r, c = g.shape
    elif axis == 0:
        r, c = g.shape[0] // N_DEV, g.shape[1]
    else:
        r, c = g.shape[0], g.shape[1] // N_DEV

    def body(g_ref, o_ref, send_sems, recv_sems, local_sem):
        _exchange(lambda idx: _block_view(g_ref, axis, idx, r, c), lambda idx: o_ref.at[idx],
                  send_sems, recv_sems, local_sem)

    return pl.pallas_call(
        body, name=name, in_specs=[_ANY], out_specs=_ANY,
        out_shape=jax.ShapeDtypeStruct((N_DEV, r, c), g.dtype), scratch_shapes=_SEMS,
    )(g)


PACK_QUANTUM = 16 * LANES


def _pack(arrs, dtype=F32, lead=0):
    keep = arrs[0].shape[:lead]
    flat = jnp.concatenate([a.reshape(keep + (-1,)).astype(dtype) for a in arrs], axis=-1)
    pad = (-flat.shape[-1]) % PACK_QUANTUM
    flat = jnp.pad(flat, ((0, 0),) * lead + ((0, pad),))
    return flat.reshape(keep + (-1, LANES))


def _unpack(flat2d, shapes, lead=()):
    flat = flat2d.reshape(lead + (-1,))
    out, off = [], 0
    for s in shapes:
        n = int(np.prod(s))
        out.append(flat[..., off:off + n].reshape(lead + tuple(s)))
        off += n
    return out


def _gather_lastdim(stk):
    return jnp.moveaxis(stk, 0, -2).reshape(stk.shape[1:-1] + (N_DEV * stk.shape[-1],))


def _gather_dim(stk, dim):
    moved = jnp.moveaxis(stk, 0, dim)
    sh = list(stk.shape[1:])
    sh[dim] = sh[dim] * N_DEV
    return moved.reshape(sh)


def _scatter_dim(full, dim):
    sh = list(full.shape)
    sh[dim:dim + 1] = [N_DEV, sh[dim] // N_DEV]
    return jnp.moveaxis(full.reshape(sh), dim, 0)


def _head_group_matrix(tc):
    return np.kron(np.eye(tc // HEAD, dtype=np.float32), np.ones((HEAD, HEAD), np.float32))


_SCAN_COMM = {0: ("w13_0", "w13_1"), 1: ("win", "w2_0", "w2_1", "wout", "wo")}
_W_AXIS = dict(wr=0, wk=0, wv=0, wo=0, win=1, wout=0, w13_0=1, w13_1=1, w2_0=0, w2_1=0)


def _build_forward(ctx2d, T, D, shards=None, sink=None):
    L = ctx2d.shape[0]
    N = L + T

    def make_scan(d):
        keys = _SCAN_COMM[d] if shards is not None else ()
        axes = tuple(_W_AXIS[k] for k in keys)

        def run_fwd(tok, sh):
            return _wkv_fwd_call(*tok, L, d == 1, tuple(sh), _exchange_specs("ag", sh, axes))

        @jax.custom_vjp
        def op(tok, sh):
            outs = run_fwd(tok, sh)
            return (outs[0],) + tuple(outs[3:])

        def op_fwd(tok, sh):
            outs = run_fwd(tok, sh)
            return (outs[0],) + tuple(outs[3:]), (tok, outs[1], outs[2])

        def op_bwd(res, cts):
            tok, sa, sprev = res
            dg = tuple(cts[1:])
            outs = _wkv_bwd_call(*tok, sa, sprev, cts[0], L, d == 1, dg, _exchange_specs("rs", dg, axes))
            for k, recv in zip(keys, outs[6:]):
                sink[k] = recv
            return tuple(outs[:6]), tuple(jnp.zeros(shards[k].shape, shards[k].dtype) for k in keys)

        op.defvjp(op_fwd, op_bwd)
        return op, keys

    scans = [make_scan(0), make_scan(1)]
    tc_head = _pick(D, 2 * LANES)
    gm = _head_group_matrix(tc_head)
    tr_row = _pick(math.gcd(L, T), 128, 8)
    op_norm = make_rowwise("norm_mod", _f_norm_mod, (F32,), tr_row, D, nb0=L // tr_row)
    op_res = [make_rowwise(f"res_norm_mod{i}", _f_res_norm_mod, (F32, BF16), _pick(T, 128, 8), D) for i in range(3)]
    op_prep = make_rowwise("wkv_prep", _f_prep, (F32,) * 7, _pick(N, 256, 8), tc_head, consts=(gm,))
    op_read = make_rowwise("wkv_readout", _f_readout, (BF16,), _pick(N, 256, 8), tc_head, consts=(gm,))

    def v3(a):
        return a.reshape(a.shape[0], 1, a.shape[-1])

    def fwd(xin, Ps, Wb):
        modx, modc = Ps["modx"], Ps["modc"]
        cat = jnp.concatenate([ctx2d, xin], axis=0)
        seg = lambda a, b: jnp.stack([a, b])[:, None, :]
        (hcat,) = op_norm((cat,), (Ps["n1"][0][None, None, :], seg(modc[0], modx[0, 0]), seg(modc[1], modx[0, 1])))
        xr, xw, xk, xv, xa, xg = shift_mix(hcat, Ps["mix"][:, None, :], L)
        r = linear(xr, Wb["wr"], F32, "wr")
        k = linear(xk, Wb["wk"], F32, "wk")
        v = linear(xv, Wb["wv"], F32, "wv")
        gl = jax.nn.sigmoid(linear(xg, Wb["g1"], F32, "g1"))
        g = linear(gl.astype(BF16), Wb["g2"], F32, "g2")
        tw = jnp.tanh(linear(xw, Wb["w1"], F32, "w1")).astype(BF16)
        ta = linear(xa, Wb["a1"], F32, "a1").astype(BF16)
        lw = [linear(tw[:, LORA_PAD * d:LORA_PAD * (d + 1)], Wb["w2d"][d], F32, f"w2_{d}") for d in range(2)]
        la = [linear(ta[:, LORA_PAD * d:LORA_PAD * (d + 1)], Wb["a2d"][d], F32, f"a2_{d}") for d in range(2)]
        kk, dec0, dec1, kd0, kd1, as0, as1 = op_prep(
            (k, lw[0], lw[1], la[0], la[1]),
            (v3(Ps["kk"]), v3(Ps["ka"]), Ps["w0"][0][None, None, :], Ps["w0"][1][None, None, :],
             Ps["a0"][0][None, None, :], Ps["a0"][1][None, None, :]))
        r2, kk2, v2 = _tile_heads(r), _tile_heads(kk), _tile_heads(v)
        ys = []
        Wb = dict(Wb)
        for d, (dec, kd, sg) in enumerate(((dec0, kd0, as0), (dec1, kd1, as1))):
            op, keys = scans[d]
            outs = op((r2, _tile_heads(dec), _tile_heads(kd), kk2, _tile_heads(sg), v2),
                      tuple(shards[k] for k in keys))
            yx = outs[0]
            ys.append((yx[:, :, :HEAD] + yx[:, :, HEAD:]).reshape(N, D))
            Wb.update(zip(keys, outs[1:]))
        (o,) = op_read((ys[0], ys[1], r, kd0, kd1, v, g), (v3(Ps["rk"]), v3(Ps["lnw"]), v3(Ps["lnb"])))
        att = linear(o[L:], Wb["wo"], F32, "wo")
        x1, h2 = op_res[0]((xin, att), (modx[0, 2][None, None, :], Ps["n2"][0][None, None, :],
                                        modx[0, 3][None, None, :], modx[0, 4][None, None, :]))
        act = swiglu_act(linear(h2, Wb["w13_0"], BF16, "w13_0"), "swiglu0")
        f0 = linear(act, Wb["w2_0"], F32, "w2_0")
        x2, h = op_res[1]((x1, f0), (modx[0, 5][None, None, :], Ps["n1"][1][None, None, :],
                                     modx[1, 0][None, None, :], modx[1, 1][None, None, :]))
        guc = linear(h, Wb["win"], BF16, "win")
        p = gated_conv(guc, Ps["conv"][:, None, :])
        cv = linear(p, Wb["wout"], F32, "wout")
        x3, h2b = op_res[2]((x2, cv), (modx[1, 2][None, None, :], Ps["n2"][1][None, None, :],
                                       modx[1, 3][None, None, :], modx[1, 4][None, None, :]))
        act1 = swiglu_act(linear(h2b, Wb["w13_1"], BF16, "w13_1"), "swiglu1")
        f1 = linear(act1, Wb["w2_1"], F32, "w2_1")
        return x3, f1


    return fwd


def kernel(x, c, ctx, c_ctx, norm1_g, norm2_g, ada_w, ada_b, rw_mix, rw_wr, rw_wk, rw_wv, rw_wo, rw_w0, rw_w1, rw_w2, rw_a0, rw_a1, rw_a2, rw_g1, rw_g2, rw_kk, rw_ka, rw_rk, rw_lnw, rw_lnb, sc_win, sc_conv, sc_wout, ffn_w13, ffn_w2, final_g, loss_target, m_c_ctx, m_norm1_g, m_norm2_g, m_ada_w, m_ada_b, m_rw_mix, m_rw_wr, m_rw_wk, m_rw_wv, m_rw_wo, m_rw_w0, m_rw_w1, m_rw_w2, m_rw_a0, m_rw_a1, m_rw_a2, m_rw_g1, m_rw_g2, m_rw_kk, m_rw_ka, m_rw_rk, m_rw_lnw, m_rw_lnb, m_sc_win, m_sc_conv, m_sc_wout, m_ffn_w13, m_ffn_w2, m_final_g, v_c_ctx, v_norm1_g, v_norm2_g, v_ada_w, v_ada_b, v_rw_mix, v_rw_wr, v_rw_wk, v_rw_wv, v_rw_wo, v_rw_w0, v_rw_w1, v_rw_w2, v_rw_a0, v_rw_a1, v_rw_a2, v_rw_g1, v_rw_g2, v_rw_kk, v_rw_ka, v_rw_rk, v_rw_lnw, v_rw_lnb, v_sc_win, v_sc_conv, v_sc_wout, v_ffn_w13, v_ffn_w2, v_final_g):
    W = dict(c_ctx=c_ctx, norm1_g=norm1_g, norm2_g=norm2_g, ada_w=ada_w, ada_b=ada_b, rw_mix=rw_mix, rw_wr=rw_wr,
             rw_wk=rw_wk, rw_wv=rw_wv, rw_wo=rw_wo, rw_w0=rw_w0, rw_w1=rw_w1, rw_w2=rw_w2, rw_a0=rw_a0, rw_a1=rw_a1,
             rw_a2=rw_a2, rw_g1=rw_g1, rw_g2=rw_g2, rw_kk=rw_kk, rw_ka=rw_ka, rw_rk=rw_rk, rw_lnw=rw_lnw,
             rw_lnb=rw_lnb, sc_win=sc_win, sc_conv=sc_conv, sc_wout=sc_wout, ffn_w13=ffn_w13, ffn_w2=ffn_w2,
             final_g=final_g)
    Mo = dict(c_ctx=m_c_ctx, norm1_g=m_norm1_g, norm2_g=m_norm2_g, ada_w=m_ada_w, ada_b=m_ada_b, rw_mix=m_rw_mix,
              rw_wr=m_rw_wr, rw_wk=m_rw_wk, rw_wv=m_rw_wv, rw_wo=m_rw_wo, rw_w0=m_rw_w0, rw_w1=m_rw_w1,
              rw_w2=m_rw_w2, rw_a0=m_rw_a0, rw_a1=m_rw_a1, rw_a2=m_rw_a2, rw_g1=m_rw_g1, rw_g2=m_rw_g2,
              rw_kk=m_rw_kk, rw_ka=m_rw_ka, rw_rk=m_rw_rk, rw_lnw=m_rw_lnw, rw_lnb=m_rw_lnb, sc_win=m_sc_win,
              sc_conv=m_sc_conv, sc_wout=m_sc_wout, ffn_w13=m_ffn_w13, ffn_w2=m_ffn_w2, final_g=m_final_g)
    Vo = dict(c_ctx=v_c_ctx, norm1_g=v_norm1_g, norm2_g=v_norm2_g, ada_w=v_ada_w, ada_b=v_ada_b, rw_mix=v_rw_mix,
              rw_wr=v_rw_wr, rw_wk=v_rw_wk, rw_wv=v_rw_wv, rw_wo=v_rw_wo, rw_w0=v_rw_w0, rw_w1=v_rw_w1,
              rw_w2=v_rw_w2, rw_a0=v_rw_a0, rw_a1=v_rw_a1, rw_a2=v_rw_a2, rw_g1=v_rw_g1, rw_g2=v_rw_g2,
              rw_kk=v_rw_kk, rw_ka=v_rw_ka, rw_rk=v_rw_rk, rw_lnw=v_rw_lnw, rw_lnb=v_rw_lnb, sc_win=v_sc_win,
              sc_conv=v_sc_conv, sc_wout=v_sc_wout, ffn_w13=v_ffn_w13, ffn_w2=v_ffn_w2, final_g=v_final_g)
    names = list(W)

    x2d = x[0]
    ctx2d = ctx[0]
    tgt = loss_target[0]
    T, D = x2d.shape
    L = ctx2d.shape[0]
    N = L + T
    nh = D // HEAD
    mx, my, mc = _me()
    me = 4 * mx + 2 * my + mc
    dloc = D // N_DEV

    lr = rw_w1.shape[-1]
    pad_r = LORA_PAD - lr
    w1p = jnp.pad(rw_w1[0], ((0, 0), (0, 0), (0, pad_r)))
    a1p = jnp.pad(rw_a1[0], ((0, 0), (0, 0), (0, pad_r)))
    w2p = jnp.pad(rw_w2[0], ((0, 0), (0, pad_r), (0, 0)))
    a2p = jnp.pad(rw_a2[0], ((0, 0), (0, pad_r), (0, 0)))
    small_loc = [rw_mix[0], rw_w0[0], rw_a0[0], sc_conv[0], w1p, a1p, w2p, a2p, rw_g1[0], rw_g2[0]]
    small_dim = [1, 1, 1, 1, 1, 1, 2, 2, 0, 1]
    small_shapes = [a.shape for a in small_loc]
    small_groups = ((slice(0, 4), F32, "vec"), (slice(4, 10), BF16, "mat"))
    small_full = []
    for sl, dt, tag in small_groups:
        sm_all = all_gather_stack(_pack(small_loc[sl], dt), "ag_small_" + tag)
        sm_parts = _unpack(sm_all, small_shapes[sl], lead=(N_DEV,))
        small_full += [_gather_dim(p, dm) for p, dm in zip(sm_parts, small_dim[sl])]
    mix_f, w0_f, a0_f, conv_f, w1_f, a1_f, w2_f, a2_f, g1_f, g2_f = small_full

    c_all = all_gather_stack(jnp.pad(c, ((0, 7), (0, 0))), "ag_c")[:, 0, :]
    cond_pre = jnp.concatenate([c_all, c_ctx[None, :], jnp.zeros((7, D), F32)], axis=0)
    cond_rows = jax.nn.silu(cond_pre)
    ncol = ada_w.shape[-1]
    mod_loc = []
    for i in range(2):
        bi = lax.dynamic_slice(ada_b[i], (me * ncol,), (ncol,))
        mod_loc.append(_mm(cond_rows, ada_w[i], out_dtype=F32, name=f"ada_fwd{i}") + bi[None, :])
    mod_all = all_gather_stack(jnp.concatenate(mod_loc, axis=0), "ag_mod")
    mod_full = _gather_lastdim(mod_all).reshape(2, 16, 6, D)
    mod_x = lax.dynamic_index_in_dim(mod_full, me, axis=1, keepdims=False)
    mod_c = mod_full[0, 8, :2, :]

    def ag_w(wl, axis, name):
        return all_gather(wl.astype(BF16), axis, name)

    shards = dict(wo=rw_wo[0], win=sc_win[0], wout=sc_wout[0], w13_0=ffn_w13[0], w13_1=ffn_w13[1],
                  w2_0=ffn_w2[0], w2_1=ffn_w2[1])
    shards = {k_: a.astype(BF16) for k_, a in shards.items()}
    sink = {}
    Wb = dict(
        wr=ag_w(rw_wr[0], 0, "ag_wr"), wk=ag_w(rw_wk[0], 0, "ag_wk"), wv=ag_w(rw_wv[0], 0, "ag_wv"),
        w1=jnp.concatenate([w1_f[0], w1_f[1]], axis=1).astype(BF16),
        a1=jnp.concatenate([a1_f[0], a1_f[1]], axis=1).astype(BF16),
        w2d=w2_f.astype(BF16), a2d=a2_f.astype(BF16),
        g1=g1_f.astype(BF16), g2=g2_f.astype(BF16),
    )
    Ps = dict(n1=norm1_g, n2=norm2_g, modx=mod_x, modc=mod_c, mix=mix_f, w0=w0_f, a0=a0_f, conv=conv_f,
              kk=rw_kk, ka=rw_ka, rk=rw_rk.reshape(1, D), lnw=rw_lnw, lnb=rw_lnb)

    fwd = _build_forward(ctx2d, T, D, shards, sink)
    (x3, f1), vjp_fn = jax.vjp(fwd, x2d, Ps, Wb)
    loss_acc, dx3, df1, dgate, dfinal = loss_head(x3, f1, tgt, mod_x[1, 5][None, :], final_g[None, :])
    dx, dPs, dWb = vjp_fn((dx3, df1))
    loss = lax.psum(loss_acc[0, 0], ("x", "y", "c"))

    dmodx = dPs["modx"].at[1, 5].add(dgate[0])
    dmodc = jnp.concatenate([dPs["modc"], jnp.zeros((4, D), F32)], axis=0)
    drow = jnp.stack([dmodx.reshape(2, 6 * D), jnp.stack([dmodc.reshape(6 * D), jnp.zeros((6 * D,), F32)])], axis=1)
    drow_all = all_gather_stack(drow.reshape(4, 6 * D), "ag_dmod").reshape(N_DEV, 2, 2, 6 * D)
    dctx_tot = drow_all[0, :, 1, :]
    for s in range(1, N_DEV):
        dctx_tot = dctx_tot + drow_all[s, :, 1, :]
    dmod_rows = jnp.concatenate([jnp.moveaxis(drow_all[:, :, 0, :], 0, 1), dctx_tot[:, None, :],
                                 jnp.zeros((2, 7, 6 * D), F32)], axis=1)
    grad_ada_b = dctx_tot
    for s in range(N_DEV):
        grad_ada_b = grad_ada_b + drow_all[s, :, 0, :]
    dmod_mine = lax.dynamic_slice_in_dim(dmod_rows, me * ncol, ncol, axis=2)
    g_ada_w = [_mm(cond_rows, dmod_mine[i], ta=True, out_dtype=F32, name=f"ada_dw{i}") for i in range(2)]
    dcond_part = _mm(dmod_mine[0], ada_w[0], tb=True, out_dtype=F32, name="ada_dcond")[8]

    rep_names = ["c_ctx", "norm1_g", "norm2_g", "rw_kk", "rw_ka", "rw_rk", "rw_lnw", "rw_lnb", "final_g"]
    rep_part = [dcond_part, dPs["n1"], dPs["n2"], dPs["kk"], dPs["ka"], dPs["rk"].reshape(W["rw_rk"].shape),
                dPs["lnw"], dPs["lnb"], dfinal[0]]
    rep_shapes = [W[n_].shape for n_ in rep_names]
    rep_all = all_gather_stack(_pack(rep_part), "ag_rep_grads")
    sg = jax.nn.sigmoid(c_ctx)
    dsilu = sg * (1.0 + c_ctx * (1.0 - sg))
    rep_scale = _pack([dsilu] + [jnp.ones(s, F32) for s in rep_shapes[1:]])
    rep_all = rep_all * rep_scale[None]
    rep_w = _pack([W[n_] for n_ in rep_names])
    rep_m = _pack([Mo[n_] for n_ in rep_names])
    rep_v = _pack([Vo[n_] for n_ in rep_names])
    rep_out = sum_adam(rep_all, rep_w, rep_m, rep_v, "adam_rep")
    results = {}
    for nm_, vals in zip(rep_names, zip(*[_unpack(o, rep_shapes) for o in rep_out])):
        results[nm_] = vals

    results["ada_b"] = tuple(sum_adam(grad_ada_b.reshape(1, 2 * 6, D), ada_b.reshape(12, D), m_ada_b.reshape(12, D),
                                      v_ada_b.reshape(12, D), "adam_ada_b"))
    results["ada_b"] = tuple(o.reshape(ada_b.shape) for o in results["ada_b"])

    outs = sum_adam(g_ada_w[0][None], ada_w, m_ada_w, v_ada_w, "adam_ada_w0", lead=0)
    results["ada_w"] = tuple(sum_adam(g_ada_w[1][None], ada_w, m_ada_w, v_ada_w, "adam_ada_w1", lead=1, prev=outs))

    dw1 = jnp.stack([dWb["w1"][:, :LORA_PAD], dWb["w1"][:, LORA_PAD:]])
    da1 = jnp.stack([dWb["a1"][:, :LORA_PAD], dWb["a1"][:, LORA_PAD:]])
    small_g = [dPs["mix"], dPs["w0"], dPs["a0"], dPs["conv"], dw1, da1, dWb["w2d"], dWb["a2d"], dWb["g1"], dWb["g2"]]
    small_names = ["rw_mix", "rw_w0", "rw_a0", "sc_conv", "rw_w1", "rw_a1", "rw_w2", "rw_a2", "rw_g1", "rw_g2"]

    def padded_local(nm_, src):
        a = src[nm_][0]
        if nm_ in ("rw_w1", "rw_a1"):
            return jnp.pad(a, ((0, 0), (0, 0), (0, pad_r)))
        if nm_ in ("rw_w2", "rw_a2"):
            return jnp.pad(a, ((0, 0), (0, pad_r), (0, 0)))
        return a

    for sl, dt, tag in small_groups:
        blocks = [_scatter_dim(gf, dm) for gf, dm in zip(small_g[sl], small_dim[sl])]
        sm_recv = reduce_scatter_exchange(_pack(blocks, dt, lead=1), None, "rs_small_" + tag)
        sm_out = sum_adam(sm_recv, _pack([padded_local(n_, W) for n_ in small_names[sl]]),
                          _pack([padded_local(n_, Mo) for n_ in small_names[sl]]),
                          _pack([padded_local(n_, Vo) for n_ in small_names[sl]]), "adam_small_" + tag)
        for nm_, vals in zip(small_names[sl], zip(*[_unpack(o, small_shapes[sl]) for o in sm_out])):
            if nm_ in ("rw_w1", "rw_a1"):
                vals = tuple(a[:, :, :lr] for a in vals)
            if nm_ in ("rw_w2", "rw_a2"):
                vals = tuple(a[:, :lr, :] for a in vals)
            results[nm_] = tuple(a[None] for a in vals)

    def rs_adam(key, nm_, lead, prev=None):
        recv = sink[key] if key in sink else reduce_scatter_exchange(dWb[key], _W_AXIS[key], "rs_" + key)
        return sum_adam(recv, W[nm_], Mo[nm_], Vo[nm_], "adam_" + key, lead=lead, prev=prev)

    for nm_, key in (("rw_wr", "wr"), ("rw_wk", "wk"), ("rw_wv", "wv"), ("rw_wo", "wo"), ("sc_win", "win"),
                     ("sc_wout", "wout")):
        results[nm_] = tuple(rs_adam(key, nm_, 0))
    for nm_, key in (("ffn_w13", "w13"), ("ffn_w2", "w2")):
        results[nm_] = tuple(rs_adam(f"{key}_1", nm_, 1, prev=rs_adam(f"{key}_0", nm_, 0)))

    grads = [results[n_][0] for n_ in names]
    deltas = [results[n_][1] for n_ in names]
    new_m = [results[n_][2] for n_ in names]
    new_v = [results[n_][3] for n_ in names]
    return (loss, dx[None], *grads, *deltas, *new_m, *new_v)
```

```python
import functools
import math

import numpy as np
import jax
import jax.numpy as jnp
from jax import lax
from jax.experimental import pallas as pl
from jax.experimental.pallas import tpu as pltpu

F32 = jnp.float32
BF16 = jnp.bfloat16

N_DEV = 8
HEAD = 64
LANES = 128
GRID_W = 64
LORA_PAD = 128
NORM_EPS = 1e-6
GN_EPS = 64e-5
ADAM_LR, ADAM_B1, ADAM_B2, ADAM_EPS, ADAM_WD, ADAM_STEP = 0.001, 0.9, 0.999, 1e-08, 0.01, 10
VMEM_LIMIT = 52 * 1024 * 1024
SCAN_CHUNK_FWD = 16
SCAN_CHUNK_BWD = 8
SCAN_UNROLL = 4
HI = lax.Precision.HIGHEST


def _cparams(sem):
    return pltpu.CompilerParams(dimension_semantics=sem, vmem_limit_bytes=VMEM_LIMIT)


def _pick(n, cap, quantum=LANES):
    best = None
    for t in range(quantum, min(n, cap) + 1, quantum):
        if n % t == 0:
            best = t
    return n if best is None else best


MM_VMEM_BUDGET = 36 * 1024 * 1024


def _divisors(n, cap, quantum=LANES):
    ds = [t for t in range(quantum, min(n, cap) + 1, quantum) if n % t == 0]
    return sorted(ds, reverse=True) or [n]


def _mm_tiles(M, N, K, sa, sb, so):
    tms, tns, tks = _divisors(M, 1024), _divisors(N, 1024), _divisors(K, 2816)
    im = jn = ik = 0

    def est(tm, tn, tk):
        b = 2 * (tm * tk * sa + tk * tn * sb) + 2 * tm * tn * so + tm * tn * 4
        b += tm * tk * 2 if sa == 4 else 0
        b += tk * tn * 2 if sb == 4 else 0
        return b + (tm * tn * 4 if tk < K else 0)

    while est(tms[im], tns[jn], tks[ik]) > MM_VMEM_BUDGET:
        if tms[im] >= tns[jn] and im + 1 < len(tms):
            im += 1
        elif jn + 1 < len(tns):
            jn += 1
        elif im + 1 < len(tms):
            im += 1
        elif ik + 1 < len(tks):
            ik += 1
        else:
            break
    return tms[im], tns[jn], tks[ik]


def _mm(a, b, *, ta=False, tb=False, out_dtype, name):
    if ta:
        K, M = a.shape
    else:
        M, K = a.shape
    if tb:
        N, Kb = b.shape
    else:
        Kb, N = b.shape
    assert K == Kb, (a.shape, b.shape, ta, tb)
    tm, tn, tk = _mm_tiles(M, N, K, a.dtype.itemsize, b.dtype.itemsize, jnp.dtype(out_dtype).itemsize)
    nk = K // tk
    dims = (((0 if ta else 1,), (1 if tb else 0,)), ((), ()))

    def body(a_ref, b_ref, o_ref, *acc):
        part = lax.dot_general(a_ref[...].astype(BF16), b_ref[...].astype(BF16), dims, preferred_element_type=F32)
        if nk == 1:
            o_ref[...] = part.astype(o_ref.dtype)
            return
        acc_ref, = acc
        k = pl.program_id(2)

        @pl.when(k == 0)
        def _():
            acc_ref[...] = part

        @pl.when(k > 0)
        def _():
            acc_ref[...] += part

        @pl.when(k == nk - 1)
        def _():
            o_ref[...] = acc_ref[...].astype(o_ref.dtype)

    a_spec = pl.BlockSpec((tk, tm), lambda i, j, k: (k, i)) if ta else pl.BlockSpec((tm, tk), lambda i, j, k: (i, k))
    b_spec = pl.BlockSpec((tn, tk), lambda i, j, k: (j, k)) if tb else pl.BlockSpec((tk, tn), lambda i, j, k: (k, j))
    return pl.pallas_call(
        body, name=name, grid=(M // tm, N // tn, nk),
        in_specs=[a_spec, b_spec],
        out_specs=pl.BlockSpec((tm, tn), lambda i, j, k: (i, j)),
        out_shape=jax.ShapeDtypeStruct((M, N), out_dtype),
        scratch_shapes=[pltpu.VMEM((tm, tn), F32)] if nk > 1 else [],
        compiler_params=_cparams(("parallel", "parallel", "arbitrary")),
    )(a, b)


@functools.partial(jax.custom_vjp, nondiff_argnums=(2, 3))
def linear(a, w, out_dtype, name):
    return _mm(a, w, out_dtype=out_dtype, name=name + "_fwd")


def _linear_fwd(a, w, out_dtype, name):
    return _mm(a, w, out_dtype=out_dtype, name=name + "_fwd"), (a, w)


def _linear_bwd(out_dtype, name, res, g):
    a, w = res
    da = _mm(g, w, tb=True, out_dtype=a.dtype, name=name + "_da")
    dw = _mm(a, g, ta=True, out_dtype=w.dtype, name=name + "_dw")
    return da, dw


linear.defvjp(_linear_fwd, _linear_bwd)


def _rw_specs(tiles, col_offs, vecs, consts, tr, tc, nb0):
    tile_specs = [pl.BlockSpec((tr, tc), functools.partial(lambda j, i, off: (i, j + off), off=off))
                  for _, off in zip(tiles, col_offs)]

    def vec_map(S):
        if S == 1:
            return lambda j, i: (0, 0, j)
        return lambda j, i: (jnp.where(i < nb0, 0, 1), 0, j)

    vec_specs = [pl.BlockSpec((None, 1, tc), vec_map(v.shape[0])) for v in vecs]
    const_specs = [pl.BlockSpec(c.shape, lambda j, i: (0, 0)) for c in consts]
    return tile_specs, vec_specs, const_specs


def _rw_forward(name, f, tiles, col_offs, vecs, consts, out_dtypes, tr, tc, nb0, width):
    n = tiles[0].shape[0]
    nt, nv, nc = len(tiles), len(vecs), len(consts)
    tile_specs, vec_specs, const_specs = _rw_specs(tiles, col_offs, vecs, consts, tr, tc, nb0)

    def body(*refs):
        ins = [r[...].astype(F32) for r in refs[:nt]] + [r[...] for r in refs[nt:nt + nv + nc]]
        outs = f(*ins)
        for o_ref, o in zip(refs[nt + nv + nc:], outs):
            o_ref[...] = o.astype(o_ref.dtype)

    return pl.pallas_call(
        body, name=name + "_fwd", grid=(width // tc, n // tr),
        in_specs=tile_specs + vec_specs + const_specs,
        out_specs=[pl.BlockSpec((tr, tc), lambda j, i: (i, j)) for _ in out_dtypes],
        out_shape=[jax.ShapeDtypeStruct((n, width), dt) for dt in out_dtypes],
        compiler_params=_cparams(("parallel", "parallel")),
    )(*tiles, *vecs, *consts)


def _rw_backward(name, f, tiles, col_offs, vecs, consts, douts, tr, tc, nb0, width):
    n = tiles[0].shape[0]
    nt, nv, nc, no = len(tiles), len(vecs), len(consts), len(douts)
    tile_specs, vec_specs, const_specs = _rw_specs(tiles, col_offs, vecs, consts, tr, tc, nb0)

    def body(*refs):
        t_in = [r[...].astype(F32) for r in refs[:nt]]
        v_in = [r[...] for r in refs[nt:nt + nv]]
        c_in = [r[...] for r in refs[nt + nv:nt + nv + nc]]
        d_in = tuple(r[...].astype(F32) for r in refs[nt + nv + nc:nt + nv + nc + no])
        o_refs = refs[nt + nv + nc + no:]
        _, vjp = jax.vjp(lambda *tv: tuple(f(*tv, *c_in)), *t_in, *v_in)
        grads = vjp(d_in)
        for o_ref, g in zip(o_refs[:nt], grads[:nt]):
            o_ref[...] = g.astype(o_ref.dtype)
        i = pl.program_id(1)
        for o_ref, g, v in zip(o_refs[nt:], grads[nt:], vecs):
            first = jnp.logical_or(i == 0, i == nb0) if v.shape[0] == 2 else i == 0

            @pl.when(first)
            def _(o_ref=o_ref, g=g):
                o_ref[...] = g

            @pl.when(jnp.logical_not(first))
            def _(o_ref=o_ref, g=g):
                o_ref[...] += g

    dout_specs = [pl.BlockSpec((tr, tc), lambda j, i: (i, j)) for _ in douts]
    out_specs = [pl.BlockSpec((tr, tc), lambda j, i: (i, j)) for _ in tiles] + list(vec_specs)
    out_shape = ([jax.ShapeDtypeStruct((n, width), t.dtype) for t in tiles]
                 + [jax.ShapeDtypeStruct(v.shape, F32) for v in vecs])
    return pl.pallas_call(
        body, name=name + "_bwd", grid=(width // tc, n // tr),
        in_specs=tile_specs + vec_specs + const_specs + dout_specs,
        out_specs=out_specs, out_shape=out_shape,
        compiler_params=_cparams(("parallel", "arbitrary")),
    )(*tiles, *vecs, *consts, *douts)


def make_rowwise(name, f, out_dtypes, tr, tc, consts=(), nb0=-1):
    consts = tuple(consts)

    @jax.custom_vjp
    def op(tiles, vecs):
        w = tiles[0].shape[1]
        return tuple(_rw_forward(name, f, tiles, (0,) * len(tiles), vecs, consts, out_dtypes, tr, min(tc, w), nb0, w))

    def op_fwd(tiles, vecs):
        return op(tiles, vecs), (tiles, vecs)

    def op_bwd(res, douts):
        tiles, vecs = res
        w = tiles[0].shape[1]
        g = _rw_backward(name, f, tiles, (0,) * len(tiles), vecs, consts, tuple(douts), tr, min(tc, w), nb0, w)
        return tuple(g[:len(tiles)]), tuple(g[len(tiles):])

    op.defvjp(op_fwd, op_bwd)
    return op


def _f_norm_mod(x, g, sh, sc):
    hn = x * lax.rsqrt(jnp.mean(x * x, axis=-1, keepdims=True) + NORM_EPS)
    return ((hn * g) * (1.0 + sc) + sh,)


def _f_res_norm_mod(x, y, gate, g, sh, sc):
    x1 = x + gate * y
    hn = x1 * lax.rsqrt(jnp.mean(x1 * x1, axis=-1, keepdims=True) + NORM_EPS)
    return x1, (hn * g) * (1.0 + sc) + sh


def _head_sum_3pass(t, gmat):
    hi = t.astype(BF16)
    r1 = t - hi.astype(F32)
    mid = r1.astype(BF16)
    lo = (r1 - mid.astype(F32)).astype(BF16)
    g = gmat.astype(BF16)
    dot = lambda u: jnp.dot(u, g, preferred_element_type=F32)
    return dot(hi) + dot(mid) + dot(lo)


@jax.custom_vjp
def _head_sum(t, gmat):
    return _head_sum_3pass(t, gmat)


def _head_sum_fwd(t, gmat):
    return _head_sum_3pass(t, gmat), gmat


def _head_sum_bwd(gmat, ct):
    return _head_sum_3pass(ct, gmat), None


_head_sum.defvjp(_head_sum_fwd, _head_sum_bwd)


def _f_prep(k, lw0, lw1, la0, la1, kkp, kap, w00, w01, a00, a01, gmat):
    t = k * kkp
    kk = t / jnp.maximum(jnp.sqrt(_head_sum(t * t, gmat)), 1e-12)
    outs = [kk]
    decs, kds, sigs = [], [], []
    for lw, la, w0, a0 in ((lw0, la0, w00, a00), (lw1, la1, w01, a01)):
        decs.append(jnp.exp(-jax.nn.sigmoid(w0 + lw) * float(np.exp(-0.5))))
        a = jax.nn.sigmoid(a0 + la)
        sigs.append(a)
        kds.append(k * (1.0 + (a - 1.0) * kap))
    return tuple(outs + decs + kds + sigs)


def _f_readout(y0, y1, r, kd0, kd1, v, g, rk, lnw, lnb, gmat):
    y = y0 + y1
    mu = _head_sum(y, gmat) * (1.0 / HEAD)
    d = y - mu
    var = _head_sum(d * d, gmat) * (1.0 / HEAD)
    o = d * lax.rsqrt(var + GN_EPS) * lnw + lnb
    bonus = _head_sum(r * (kd0 + kd1) * rk, gmat) * v
    return ((o + bonus) * g,)


def _f_swiglu(a, b):
    return (jax.nn.silu(a) * b,)


def swiglu_act(ab, name):
    t, f2 = ab.shape
    fdim = f2 // 2
    tr, tc = _pick(t, 512, 8), _pick(fdim, 512)
    offs = (0, fdim // tc)

    @jax.custom_vjp
    def op(ab_):
        return _rw_forward(name, _f_swiglu, (ab_, ab_), offs, (), (), (BF16,), tr, tc, -1, fdim)[0]

    def op_fwd(ab_):
        return op(ab_), ab_

    def op_bwd(ab_, dact):
        da, db = _rw_backward(name, _f_swiglu, (ab_, ab_), offs, (), (), (dact,), tr, tc, -1, fdim)
        return (jnp.concatenate([da, db], axis=1),)

    op.defvjp(op_fwd, op_bwd)
    return op(ab)


def _row_iota(n, tc):
    return lax.broadcasted_iota(jnp.int32, (n, tc), 0)


def _shift_rows(x, s, keep):
    n = x.shape[0]
    return jnp.where(keep, pltpu.roll(x, s % n, 0), 0.0)


def _unshift_rows(d, s, keep):
    n = d.shape[0]
    return pltpu.roll(jnp.where(keep, d, 0.0), (-s) % n, 0)


def _ctx_shift_spec(L, tc, quarter):
    row = _row_iota(L, tc)
    if quarter < 2:
        return 1, row >= 1
    return -1, row < L - 1


def _grid_shift_spec(T, tc, quarter):
    row = _row_iota(T, tc)
    col = jnp.bitwise_and(row, GRID_W - 1)
    if quarter == 0:
        return 1, col != 0
    if quarter == 1:
        return -1, col != GRID_W - 1
    if quarter == 2:
        return GRID_W, row >= GRID_W
    return -GRID_W, row < T - GRID_W


def _shift_mix_fwd_call(h, mix3, L):
    n, d = h.shape
    T = n - L
    tc = _pick(d // 4, 256)
    nq = (d // 4) // tc

    def body(h_ref, mix_ref, *o_refs):
        q = pl.program_id(0) // nq
        for quarter in range(4):
            @pl.when(q == quarter)
            def _(quarter=quarter):
                for lo, cnt, spec in ((0, L, _ctx_shift_spec), (L, T, _grid_shift_spec)):
                    hh = h_ref[pl.ds(lo, cnt), :]
                    s, keep = spec(cnt, tc, quarter)
                    xx = _shift_rows(hh, s, keep) - hh
                    for m in range(6):
                        o_refs[m][pl.ds(lo, cnt), :] = (hh + xx * mix_ref[m]).astype(BF16)

    return pl.pallas_call(
        body, name="shift_mix_fwd", grid=(d // tc,),
        in_specs=[pl.BlockSpec((n, tc), lambda j: (0, j)), pl.BlockSpec((6, 1, tc), lambda j: (0, 0, j))],
        out_specs=[pl.BlockSpec((n, tc), lambda j: (0, j)) for _ in range(6)],
        out_shape=[jax.ShapeDtypeStruct((n, d), BF16) for _ in range(6)],
        compiler_params=_cparams(("parallel",)),
    )(h, mix3)


def _shift_mix_bwd_call(h, mix3, douts, L):
    n, d = h.shape
    T = n - L
    tc = _pick(d // 4, 256)
    nq = (d // 4) // tc

    def body(h_ref, mix_ref, d0, d1, d2, d3, d4, d5, dh_ref, dmix_ref):
        d_refs = (d0, d1, d2, d3, d4, d5)
        q = pl.program_id(0) // nq
        for quarter in range(4):
            @pl.when(q == quarter)
            def _(quarter=quarter):
                dmix = [jnp.zeros((1, tc), F32) for _ in range(6)]
                for lo, cnt, spec in ((0, L, _ctx_shift_spec), (L, T, _grid_shift_spec)):
                    hh = h_ref[pl.ds(lo, cnt), :]
                    s, keep = spec(cnt, tc, quarter)
                    xx = _shift_rows(hh, s, keep) - hh
                    direct = jnp.zeros((cnt, tc), F32)
                    shifted = jnp.zeros((cnt, tc), F32)
                    for m in range(6):
                        dm = d_refs[m][pl.ds(lo, cnt), :].astype(F32)
                        mx = mix_ref[m]
                        direct = direct + dm * (1.0 - mx)
                        shifted = shifted + dm * mx
                        dmix[m] = dmix[m] + jnp.sum(dm * xx, axis=0, keepdims=True)
                    dh_ref[pl.ds(lo, cnt), :] = direct + _unshift_rows(shifted, s, keep)
                for m in range(6):
                    dmix_ref[m] = dmix[m]

    tile = pl.BlockSpec((n, tc), lambda j: (0, j))
    return pl.pallas_call(
        body, name="shift_mix_bwd", grid=(d // tc,),
        in_specs=[tile, pl.BlockSpec((6, 1, tc), lambda j: (0, 0, j))] + [tile] * 6,
        out_specs=[tile, pl.BlockSpec((6, 1, tc), lambda j: (0, 0, j))],
        out_shape=[jax.ShapeDtypeStruct((n, d), F32), jax.ShapeDtypeStruct((6, 1, d), F32)],
        compiler_params=_cparams(("parallel",)),
    )(h, mix3, *douts)


@functools.partial(jax.custom_vjp, nondiff_argnums=(2,))
def shift_mix(h, mix3, L):
    return tuple(_shift_mix_fwd_call(h, mix3, L))


def _shift_mix_fwd(h, mix3, L):
    return tuple(_shift_mix_fwd_call(h, mix3, L)), (h, mix3)


def _shift_mix_bwd(L, res, douts):
    h, mix3 = res
    dh, dmix = _shift_mix_bwd_call(h, mix3, tuple(douts), L)
    return dh, dmix


shift_mix.defvjp(_shift_mix_fwd, _shift_mix_bwd)


def _conv_specs(T, d, tc):
    nd = d // tc
    ins = [pl.BlockSpec((T, tc), functools.partial(lambda j, off: (0, j + off), off=o * nd)) for o in range(3)]
    return ins, pl.BlockSpec((3, 1, tc), lambda j: (0, 0, j))


def _conv_terms(gc, u, tc):
    T = gc.shape[0]
    row = _row_iota(T, tc)
    z = gc * u
    return z, _shift_rows(z, 1, row >= 1), _shift_rows(z, -1, row < T - 1), row


def _conv_fwd_call(guc, cw3):
    T, d3 = guc.shape
    d = d3 // 3
    tc = _pick(d, 256)
    ins, wspec = _conv_specs(T, d, tc)

    def body(gb_ref, gc_ref, u_ref, w_ref, p_ref):
        z, zp, zn, _ = _conv_terms(gc_ref[...].astype(F32), u_ref[...].astype(F32), tc)
        conv = zp * w_ref[0] + z * w_ref[1] + zn * w_ref[2]
        p_ref[...] = (gb_ref[...].astype(F32) * conv).astype(BF16)

    return pl.pallas_call(
        body, name="conv_fwd", grid=(d // tc,), in_specs=ins + [wspec],
        out_specs=pl.BlockSpec((T, tc), lambda j: (0, j)),
        out_shape=jax.ShapeDtypeStruct((T, d), BF16),
        compiler_params=_cparams(("parallel",)),
    )(guc, guc, guc, cw3)


def _conv_bwd_call(guc, cw3, dp):
    T, d3 = guc.shape
    d = d3 // 3
    tc = _pick(d, 256)
    ins, wspec = _conv_specs(T, d, tc)
    tile = pl.BlockSpec((T, tc), lambda j: (0, j))

    def body(gb_ref, gc_ref, u_ref, w_ref, dp_ref, dgb_ref, dgc_ref, du_ref, dw_ref):
        gc = gc_ref[...].astype(F32)
        u = u_ref[...].astype(F32)
        z, zp, zn, row = _conv_terms(gc, u, tc)
        conv = zp * w_ref[0] + z * w_ref[1] + zn * w_ref[2]
        dpv = dp_ref[...].astype(F32)
        dgb_ref[...] = (dpv * conv).astype(dgb_ref.dtype)
        dconv = dpv * gb_ref[...].astype(F32)
        dz = (_shift_rows(dconv, -1, row < T - 1) * w_ref[0] + dconv * w_ref[1]
              + _shift_rows(dconv, 1, row >= 1) * w_ref[2])
        dgc_ref[...] = (dz * u).astype(dgc_ref.dtype)
        du_ref[...] = (dz * gc).astype(du_ref.dtype)
        dw_ref[0] = jnp.sum(dconv * zp, axis=0, keepdims=True)
        dw_ref[1] = jnp.sum(dconv * z, axis=0, keepdims=True)
        dw_ref[2] = jnp.sum(dconv * zn, axis=0, keepdims=True)

    return pl.pallas_call(
        body, name="conv_bwd", grid=(d // tc,), in_specs=ins + [wspec, tile],
        out_specs=[tile, tile, tile, wspec],
        out_shape=[jax.ShapeDtypeStruct((T, d), guc.dtype)] * 3 + [jax.ShapeDtypeStruct((3, 1, d), F32)],
        compiler_params=_cparams(("parallel",)),
    )(guc, guc, guc, cw3, dp)


@jax.custom_vjp
def gated_conv(guc, cw3):
    return _conv_fwd_call(guc, cw3)


def _gated_conv_fwd(guc, cw3):
    return _conv_fwd_call(guc, cw3), (guc, cw3)


def _gated_conv_bwd(res, dp):
    guc, cw3 = res
    dgb, dgc, du, dw = _conv_bwd_call(guc, cw3, dp)
    return jnp.concatenate([dgb, dgc, du], axis=1), dw


gated_conv.defvjp(_gated_conv_fwd, _gated_conv_bwd)


def _chunk_map(nchunk, nctx_chunk, reverse):
    if not reverse:
        return lambda c: c
    return lambda c: jnp.where(c < nctx_chunk, nctx_chunk - 1 - c, nchunk - 1 - (c - nctx_chunk))


def _spread_bf16(row_ref, dst_scr, ni, C):
    packed = [pltpu.bitcast(row_ref[tt].astype(BF16), jnp.int32) for tt in range(C)]
    lane = lax.broadcasted_iota(jnp.int32, packed[0].shape, 1)
    for i in range(ni):
        idx = jnp.where(lane < HEAD, 2 * i, HEAD + 1 + 2 * i).astype(jnp.int32)
        for tt in range(C):
            got = jnp.take_along_axis(packed[tt], idx, axis=1)
            dst_scr[tt, i] = pltpu.bitcast(got, BF16).astype(F32)


def _half_sums(p, lo_mask):
    lo = jnp.sum(jnp.where(lo_mask, p, 0.0), axis=1, keepdims=True)
    hi = jnp.sum(jnp.where(lo_mask, 0.0, p), axis=1, keepdims=True)
    return jnp.where(lo_mask, lo, hi)


def _half_sums_mxu(ps, gmat, passes):
    p = jnp.concatenate(ps, axis=0)
    hi = p.astype(BF16)
    s = jnp.dot(hi, gmat, preferred_element_type=F32)
    if passes == 2:
        lo = (p - hi.astype(F32)).astype(BF16)
        s = s + jnp.dot(lo, gmat, preferred_element_type=F32)
    nh = ps[0].shape[0]
    return [s[i * nh:(i + 1) * nh] for i in range(len(ps))]


def _split_row(sums, lane, nh):
    acc = jnp.zeros((nh, LANES), F32)
    for i, s in enumerate(sums):
        acc = acc + jnp.where(jnp.logical_or(lane == 2 * i, lane == HEAD + 1 + 2 * i), s, 0.0)
    return acc


def _wkv_fwd_call(r2, w2, kd2, kk2, as2, v2, nctx, reverse, xchg_arrs=(), xchg_specs=()):
    n, nh, _ = r2.shape
    C = math.gcd(SCAN_CHUNK_FWD, nctx)
    ni = HEAD // 2
    nchunk = n // C
    cmap = _chunk_map(nchunk, nctx // C, reverse)
    nx = len(xchg_arrs)

    def body(*refs):
        g_ref, refs = refs[0], refs[1:]
        r_ref, w_ref, kd_ref, kk_ref, as_ref, v_ref = refs[:6]
        x_in = refs[6:6 + nx]
        y_ref, sa_ref, sp_ref = refs[6 + nx:9 + nx]
        x_out = refs[9 + nx:9 + 2 * nx]
        s_scr, vc_scr = refs[9 + 2 * nx:11 + 2 * nx]
        if nx:
            _fused_exchanges(list(zip(x_in, x_out)), xchg_specs, *refs[11 + 2 * nx:], first=pl.program_id(0) == 0)

        @pl.when(pl.program_id(0) == 0)
        def _():
            s_scr[...] = jnp.zeros_like(s_scr)

        lane = lax.broadcasted_iota(jnp.int32, (nh, LANES), 1)
        lo_mask = lane < HEAD
        _spread_bf16(v_ref, vc_scr, ni, C)

        def make_step(with_y):
            def step(j, carry):
                t = (C - 1 - j) if reverse else j
                kk = kk_ref[t]
                a2 = -kk
                b2 = kk * as_ref[t]
                w = w_ref[t]
                k = kd_ref[t]
                r = r_ref[t]
                sas = []
                for i in range(ni):
                    si = s_scr[i]
                    sp_ref[t, i] = si
                    sas.append(_half_sums(si * a2, lo_mask))
                qs = []
                for i in range(ni):
                    sn = s_scr[i] * w + sas[i] * b2 + vc_scr[t, i] * k
                    s_scr[i] = sn
                    if with_y:
                        qs.append(sn * r)
                if with_y:
                    y_ref[t] = _split_row(_half_sums_mxu(qs, g_ref[...], 2), lane, nh)
                else:
                    y_ref[t] = jnp.zeros((nh, LANES), F32)
                sa_ref[t] = _split_row(sas, lane, nh)
                return carry
            return step

        is_ctx = pl.program_id(0) < nctx // C

        @pl.when(is_ctx)
        def _():
            lax.fori_loop(0, C, make_step(False), 0)

        @pl.when(jnp.logical_not(is_ctx))
        def _():
            lax.fori_loop(0, C, make_step(True), 0, unroll=SCAN_UNROLL)

        if nx:
            _fused_exchanges(list(zip(x_in, x_out)), xchg_specs, *refs[11 + 2 * nx:],
                             last=pl.program_id(0) == nchunk - 1)

    tok = pl.BlockSpec((C, nh, LANES), lambda c: (cmap(c), 0, 0))
    return pl.pallas_call(
        body, name="wkv_fwd_rev" if reverse else "wkv_fwd", grid=(nchunk,),
        in_specs=[pl.BlockSpec((LANES, LANES), lambda c: (0, 0))] + [tok] * 6 + [_ANY] * nx,
        out_specs=[tok, tok, pl.BlockSpec((C, ni, nh, LANES), lambda c: (cmap(c), 0, 0, 0))] + [_ANY] * nx,
        out_shape=[jax.ShapeDtypeStruct((n, nh, LANES), F32), jax.ShapeDtypeStruct((n, nh, LANES), F32),
                   jax.ShapeDtypeStruct((n, ni, nh, LANES), F32)] + _exchange_out_shapes(xchg_arrs, xchg_specs),
        scratch_shapes=[pltpu.VMEM((ni, nh, LANES), F32), pltpu.VMEM((C, ni, nh, LANES), F32)]
        + (_exchange_sems(nx) if nx else []),
        compiler_params=_cparams(("arbitrary",)),
    )(_head_group_matrix(LANES).astype(BF16), r2, w2, kd2, kk2, as2, v2, *xchg_arrs)


def _wkv_bwd_call(r2, w2, kd2, kk2, as2, v2, sa, sprev, dy, nctx, reverse, xchg_arrs=(), xchg_specs=()):
    n, nh, _ = r2.shape
    C = math.gcd(SCAN_CHUNK_BWD, nctx)
    ni = HEAD // 2
    nchunk = n // C
    fmap = _chunk_map(nchunk, nctx // C, reverse)
    cmap = lambda c: fmap(nchunk - 1 - c)
    nx = len(xchg_arrs)

    def body(*refs):
        g_ref, refs = refs[0], refs[1:]
        r_ref, w_ref, kd_ref, kk_ref, as_ref, v_ref, sa_ref, sp_ref, dy_ref = refs[:9]
        x_in = refs[9:9 + nx]
        dr_ref, dw_ref, dkd_ref, dkk_ref, das_ref, dv_ref = refs[9 + nx:15 + nx]
        x_out = refs[15 + nx:15 + 2 * nx]
        ds_scr, vc_scr, sac_scr, dyc_scr = refs[15 + 2 * nx:19 + 2 * nx]
        if nx:
            _fused_exchanges(list(zip(x_in, x_out)), xchg_specs, *refs[19 + 2 * nx:], first=pl.program_id(0) == 0)

        @pl.when(pl.program_id(0) == 0)
        def _():
            ds_scr[...] = jnp.zeros_like(ds_scr)

        lane = lax.broadcasted_iota(jnp.int32, (nh, LANES), 1)
        lo_mask = lane < HEAD
        _spread_bf16(v_ref, vc_scr, ni, C)
        _spread_bf16(sa_ref, sac_scr, ni, C)
        is_ctx = pl.program_id(0) >= nchunk - nctx // C

        @pl.when(jnp.logical_not(is_ctx))
        def _():
            _spread_bf16(dy_ref, dyc_scr, ni, C)

        def make_step(with_dy):
            def step(j, carry):
                t = j if reverse else (C - 1 - j)
                kk = kk_ref[t]
                sig = as_ref[t]
                a2 = -kk
                b2 = kk * sig
                w = w_ref[t]
                k = kd_ref[t]
                r = r_ref[t]
                zero = jnp.zeros((nh, LANES), F32)
                acc_dk, acc_db, acc_dw, acc_g, acc_sady, acc_vdy, acc_da = zero, zero, zero, zero, zero, zero, zero
                dvp, dsas = [], []
                for i in range(ni):
                    sp = sp_ref[t, i]
                    vc = vc_scr[t, i]
                    sac = sac_scr[t, i]
                    ds = ds_scr[i]
                    if with_dy:
                        dyc = dyc_scr[t, i]
                        ds = ds + dyc * r
                        ds_scr[i] = ds
                        acc_g = acc_g + sp * dyc
                    dvp.append(ds * k)
                    dsas.append(_half_sums(ds * b2, lo_mask))
                    acc_dk = acc_dk + ds * vc
                    acc_db = acc_db + ds * sac
                    acc_dw = acc_dw + ds * sp
                for i in range(ni):
                    acc_da = acc_da + sp_ref[t, i] * dsas[i]
                    ds_scr[i] = ds_scr[i] * w + dsas[i] * a2
                if with_dy:
                    dyr = jnp.where(jnp.bitwise_and(lane, 1) == (lane >= HEAD).astype(jnp.int32), dy_ref[t], 0.0)
                    acc_sady = _half_sums(sa_ref[t] * dyr, lo_mask)
                    acc_vdy = _half_sums(v_ref[t] * dyr, lo_mask)
                dr_ref[t] = acc_g * w + b2 * acc_sady + k * acc_vdy
                dw_ref[t] = acc_dw
                dkd_ref[t] = acc_dk
                dkk_ref[t] = acc_db * sig - acc_da
                das_ref[t] = acc_db * kk
                dv_ref[t] = _split_row(_half_sums_mxu(dvp, g_ref[...], 1), lane, nh)
                return carry
            return step

        @pl.when(is_ctx)
        def _():
            lax.fori_loop(0, C, make_step(False), 0)

        @pl.when(jnp.logical_not(is_ctx))
        def _():
            lax.fori_loop(0, C, make_step(True), 0, unroll=SCAN_UNROLL)

        if nx:
            _fused_exchanges(list(zip(x_in, x_out)), xchg_specs, *refs[19 + 2 * nx:],
                             last=pl.program_id(0) == nchunk - 1)

    tok = pl.BlockSpec((C, nh, LANES), lambda c: (cmap(c), 0, 0))
    big = pltpu.VMEM((C, ni, nh, LANES), F32)
    return pl.pallas_call(
        body, name="wkv_bwd_rev" if reverse else "wkv_bwd", grid=(nchunk,),
        in_specs=[pl.BlockSpec((LANES, LANES), lambda c: (0, 0))] + [tok] * 7
        + [pl.BlockSpec((C, ni, nh, LANES), lambda c: (cmap(c), 0, 0, 0)), tok] + [_ANY] * nx,
        out_specs=[tok] * 6 + [_ANY] * nx,
        out_shape=[jax.ShapeDtypeStruct((n, nh, LANES), F32)] * 6 + _exchange_out_shapes(xchg_arrs, xchg_specs),
        scratch_shapes=[pltpu.VMEM((ni, nh, LANES), F32), big, big, big] + (_exchange_sems(nx) if nx else []),
        compiler_params=_cparams(("arbitrary",)),
    )(_head_group_matrix(LANES).astype(BF16), r2, w2, kd2, kk2, as2, v2, sa, sprev, dy, *xchg_arrs)


def _tile_heads(t):
    n, d = t.shape
    th = t.reshape(n, d // HEAD, HEAD)
    return jnp.concatenate([th, th], axis=-1)


def loss_head(x3, fo, tgt, gate, g):
    T, d = x3.shape
    tr = _pick(T, 128, 8)

    def body(x_ref, f_ref, t_ref, gate_ref, g_ref, loss_ref, dx_ref, df_ref, dgate_ref, dg_ref):
        tg = t_ref[...]

        def fl(x, fo_, gate_, g_):
            x4 = x + gate_ * fo_
            y = (x4 * lax.rsqrt(jnp.mean(x4 * x4, axis=-1, keepdims=True) + NORM_EPS)) * g_
            return 0.5 * jnp.sum(jnp.mean(jnp.square(y - tg), axis=-1))

        val, vjp = jax.vjp(fl, x_ref[...], f_ref[...], gate_ref[...], g_ref[...])
        dx, dfo, dgate, dg = vjp(jnp.ones((), F32))
        dx_ref[...] = dx
        df_ref[...] = dfo
        i = pl.program_id(0)

        @pl.when(i == 0)
        def _():
            loss_ref[...] = jnp.zeros_like(loss_ref)
            dgate_ref[...] = jnp.zeros_like(dgate_ref)
            dg_ref[...] = jnp.zeros_like(dg_ref)

        loss_ref[...] += jnp.full(loss_ref.shape, val, F32)
        dgate_ref[...] += dgate
        dg_ref[...] += dg

    tile = pl.BlockSpec((tr, d), lambda i: (i, 0))
    vec = pl.BlockSpec((1, d), lambda i: (0, 0))
    return pl.pallas_call(
        body, name="loss_head", grid=(T // tr,),
        in_specs=[tile, tile, tile, vec, vec],
        out_specs=[pl.BlockSpec((8, LANES), lambda i: (0, 0)), tile, tile, vec, vec],
        out_shape=[jax.ShapeDtypeStruct((8, LANES), F32), jax.ShapeDtypeStruct((T, d), F32),
                   jax.ShapeDtypeStruct((T, d), F32), jax.ShapeDtypeStruct((1, d), F32),
                   jax.ShapeDtypeStruct((1, d), F32)],
        compiler_params=_cparams(("arbitrary",)),
    )(x3, fo, tgt, gate, g)


def sum_adam(parts, w, m, v, name, lead=None, prev=None):
    P, R, Cc = parts.shape
    tc = _pick(Cc, 1024)
    tr = _pick(R, max(8, (256 * 1024) // tc), 16 if parts.dtype == BF16 else 8)
    prev = tuple(prev) if prev is not None else ()

    def body(p_ref, w_ref, m_ref, v_ref, *rest):
        g_ref, d_ref, nm_ref, nv_ref = rest[len(prev):]
        g = p_ref[0].astype(F32)
        for s in range(1, P):
            g = g + p_ref[s].astype(F32)
        m_new = ADAM_B1 * m_ref[...] + (1.0 - ADAM_B1) * g
        v_new = ADAM_B2 * v_ref[...] + (1.0 - ADAM_B2) * jnp.square(g)
        m_hat = m_new / (1.0 - ADAM_B1 ** ADAM_STEP)
        v_hat = v_new / (1.0 - ADAM_B2 ** ADAM_STEP)
        g_ref[...] = g
        d_ref[...] = -ADAM_LR * (m_hat / (jnp.sqrt(v_hat) + ADAM_EPS) + ADAM_WD * w_ref[...])
        nm_ref[...] = m_new
        nv_ref[...] = v_new

    if lead is None:
        pspec = pl.BlockSpec((tr, tc), lambda i, j: (i, j))
        oshape = (R, Cc)
    else:
        pspec = pl.BlockSpec((None, tr, tc), lambda i, j: (lead, i, j))
        oshape = w.shape
    return pl.pallas_call(
        body, name=name, grid=(R // tr, Cc // tc),
        in_specs=[pl.BlockSpec((P, tr, tc), lambda i, j: (0, i, j)), pspec, pspec, pspec] + [_ANY] * len(prev),
        out_specs=[pspec] * 4,
        out_shape=[jax.ShapeDtypeStruct(oshape, F32)] * 4,
        input_output_aliases={4 + q: q for q in range(len(prev))},
        compiler_params=_cparams(("parallel", "parallel")),
    )(parts, w, m, v, *prev)


def _me():
    return lax.axis_index("x"), lax.axis_index("y"), lax.axis_index("c")


def _peer(p):
    x, y, c = _me()
    px = 1 - x if p & 4 else x
    py = 1 - y if p & 2 else y
    pc = 1 - c if p & 1 else c
    return (px, py, pc), 4 * px + 2 * py + pc


def _block_view(ref, axis, idx, r, c):
    if axis is None:
        return ref.at[idx]
    if axis == 0:
        return ref.at[pl.ds(idx * r, r), :]
    return ref.at[:, pl.ds(idx * c, c)]


def _exchange_copies(src_of, dst_of, ssem, rsem, lsem, with_recvs):
    x, y, c = _me()
    me = 4 * x + 2 * y + c
    local = pltpu.make_async_copy(src_of(me), dst_of(me), lsem)
    sends, recvs = [], []
    for p in range(1, N_DEV):
        dev, idx = _peer(p)
        sends.append(pltpu.make_async_remote_copy(src_ref=src_of(idx), dst_ref=dst_of(me), send_sem=ssem(p),
                                                  recv_sem=rsem(p), device_id=dev,
                                                  device_id_type=pl.DeviceIdType.MESH))
        if with_recvs:
            recvs.append(pltpu.make_async_remote_copy(src_ref=src_of(idx), dst_ref=dst_of(idx), send_sem=ssem(p),
                                                      recv_sem=rsem(p), device_id=dev,
                                                      device_id_type=pl.DeviceIdType.MESH))
    return local, sends, recvs


def _exchange_start(*args):
    local, sends, _ = _exchange_copies(*args, with_recvs=False)
    local.start()
    for cp in sends:
        cp.start()


def _exchange_wait(*args):
    local, sends, recvs = _exchange_copies(*args, with_recvs=True)
    for cp in recvs:
        cp.wait_recv()
    for cp in sends:
        cp.wait_send()
    local.wait()


def _exchange(src_of, dst_of, send_sems, recv_sems, local_sem):
    args = (src_of, dst_of, lambda p: send_sems.at[p], lambda p: recv_sems.at[p], local_sem)
    _exchange_start(*args)
    _exchange_wait(*args)


def _fused_exchanges(pairs, specs, send_sems, recv_sems, local_sems, first=None, last=None):
    def args(j):
        src, dst = pairs[j]
        kind, axis, r, c = specs[j]
        if kind == "ag":
            src_of = lambda idx: src
            dst_of = lambda idx: _block_view(dst, axis, idx, r, c)
        else:
            src_of = lambda idx: _block_view(src, axis, idx, r, c)
            dst_of = lambda idx: dst.at[idx]
        return (src_of, dst_of, lambda p: send_sems.at[j, p], lambda p: recv_sems.at[j, p], local_sems.at[j])

    if first is not None:
        @pl.when(first)
        def _():
            for j in range(len(pairs)):
                _exchange_start(*args(j))

    if last is not None:
        @pl.when(last)
        def _():
            for j in range(len(pairs)):
                _exchange_wait(*args(j))


def _exchange_out_shapes(arrs, specs):
    out = []
    for a, (kind, axis, r, c) in zip(arrs, specs):
        if kind == "rs":
            out.append(jax.ShapeDtypeStruct((N_DEV, r, c), a.dtype))
        else:
            out.append(jax.ShapeDtypeStruct((N_DEV * r, c) if axis == 0 else (r, N_DEV * c), a.dtype))
    return out


def _exchange_specs(kind, arrs, axes):
    specs = []
    for a, axis in zip(arrs, axes):
        if kind == "ag":
            r, c = a.shape
        elif axis == 0:
            r, c = a.shape[0] // N_DEV, a.shape[1]
        else:
            r, c = a.shape[0], a.shape[1] // N_DEV
        specs.append((kind, axis, r, c))
    return specs


def _exchange_sems(n):
    return [pltpu.SemaphoreType.DMA((n, N_DEV)), pltpu.SemaphoreType.DMA((n, N_DEV)), pltpu.SemaphoreType.DMA((n,))]


_SEMS = [pltpu.SemaphoreType.DMA((N_DEV,)), pltpu.SemaphoreType.DMA((N_DEV,)), pltpu.SemaphoreType.DMA]
_ANY = pl.BlockSpec(memory_space=pl.ANY)


def all_gather(x, axis, name):
    r, c = x.shape
    shape = (N_DEV * r, c) if axis == 0 else (r, N_DEV * c)

    def body(x_ref, o_ref, send_sems, recv_sems, local_sem):
        _exchange(lambda idx: x_ref, lambda idx: _block_view(o_ref, axis, idx, r, c), send_sems, recv_sems, local_sem)

    return pl.pallas_call(
        body, name=name, in_specs=[_ANY], out_specs=_ANY,
        out_shape=jax.ShapeDtypeStruct(shape, x.dtype), scratch_shapes=_SEMS,
    )(x)


def all_gather_stack(x, name):
    r, c = x.shape

    def body(x_ref, o_ref, send_sems, recv_sems, local_sem):
        _exchange(lambda idx: x_ref, lambda idx: o_ref.at[idx], send_sems, recv_sems, local_sem)

    return pl.pallas_call(
        body, name=name, in_specs=[_ANY], out_specs=_ANY,
        out_shape=jax.ShapeDtypeStruct((N_DEV, r, c), x.dtype), scratch_shapes=_SEMS,
    )(x)


def reduce_scatter_exchange(g, axis, name):
    if axis is None:
        _, r, c = g.shape
    elif axis == 0:
        r, c = g.shape[0] // N_DEV, g.shape[1]
    else:
        r, c = g.shape[0], g.shape[1] // N_DEV

    def body(g_ref, o_ref, send_sems, recv_sems, local_sem):
        _exchange(lambda idx: _block_view(g_ref, axis, idx, r, c), lambda idx: o_ref.at[idx],
                  send_sems, recv_sems, local_sem)

    return pl.pallas_call(
        body, name=name, in_specs=[_ANY], out_specs=_ANY,
        out_shape=jax.ShapeDtypeStruct((N_DEV, r, c), g.dtype), scratch_shapes=_SEMS,
    )(g)


PACK_QUANTUM = 16 * LANES


def _pack(arrs, dtype=F32, lead=0):
    keep = arrs[0].shape[:lead]
    flat = jnp.concatenate([a.reshape(keep + (-1,)).astype(dtype) for a in arrs], axis=-1)
    pad = (-flat.shape[-1]) % PACK_QUANTUM
    flat = jnp.pad(flat, ((0, 0),) * lead + ((0, pad),))
    return flat.reshape(keep + (-1, LANES))


def _unpack(flat2d, shapes, lead=()):
    flat = flat2d.reshape(lead + (-1,))
    out, off = [], 0
    for s in shapes:
        n = int(np.prod(s))
        out.append(flat[..., off:off + n].reshape(lead + tuple(s)))
        off += n
    return out


def _gather_lastdim(stk):
    return jnp.moveaxis(stk, 0, -2).reshape(stk.shape[1:-1] + (N_DEV * stk.shape[-1],))


def _gather_dim(stk, dim):
    moved = jnp.moveaxis(stk, 0, dim)
    sh = list(stk.shape[1:])
    sh[dim] = sh[dim] * N_DEV
    return moved.reshape(sh)


def _scatter_dim(full, dim):
    sh = list(full.shape)
    sh[dim:dim + 1] = [N_DEV, sh[dim] // N_DEV]
    return jnp.moveaxis(full.reshape(sh), dim, 0)


def _head_group_matrix(tc):
    return np.kron(np.eye(tc // HEAD, dtype=np.float32), np.ones((HEAD, HEAD), np.float32))


_SCAN_COMM = {0: ("w13_0", "w13_1"), 1: ("win", "w2_0", "w2_1", "wout", "wo")}
_W_AXIS = dict(wr=0, wk=0, wv=0, wo=0, win=1, wout=0, w13_0=1, w13_1=1, w2_0=0, w2_1=0)


def _build_forward(ctx2d, T, D, shards=None, sink=None):
    L = ctx2d.shape[0]
    N = L + T

    def make_scan(d):
        keys = _SCAN_COMM[d] if shards is not None else ()
        axes = tuple(_W_AXIS[k] for k in keys)

        def run_fwd(tok, sh):
            return _wkv_fwd_call(*tok, L, d == 1, tuple(sh), _exchange_specs("ag", sh, axes))

        @jax.custom_vjp
        def op(tok, sh):
            outs = run_fwd(tok, sh)
            return (outs[0],) + tuple(outs[3:])

        def op_fwd(tok, sh):
            outs = run_fwd(tok, sh)
            return (outs[0],) + tuple(outs[3:]), (tok, outs[1], outs[2])

        def op_bwd(res, cts):
            tok, sa, sprev = res
            dg = tuple(cts[1:])
            outs = _wkv_bwd_call(*tok, sa, sprev, cts[0], L, d == 1, dg, _exchange_specs("rs", dg, axes))
            for k, recv in zip(keys, outs[6:]):
                sink[k] = recv
            return tuple(outs[:6]), tuple(jnp.zeros(shards[k].shape, shards[k].dtype) for k in keys)

        op.defvjp(op_fwd, op_bwd)
        return op, keys

    scans = [make_scan(0), make_scan(1)]
    tc_head = _pick(D, 2 * LANES)
    gm = _head_group_matrix(tc_head)
    tr_row = _pick(math.gcd(L, T), 128, 8)
    op_norm = make_rowwise("norm_mod", _f_norm_mod, (F32,), tr_row, D, nb0=L // tr_row)
    op_res = [make_rowwise(f"res_norm_mod{i}", _f_res_norm_mod, (F32, BF16), _pick(T, 128, 8), D) for i in range(3)]
    op_prep = make_rowwise("wkv_prep", _f_prep, (F32,) * 7, _pick(N, 256, 8), tc_head, consts=(gm,))
    op_read = make_rowwise("wkv_readout", _f_readout, (BF16,), _pick(N, 256, 8), tc_head, consts=(gm,))

    def v3(a):
        return a.reshape(a.shape[0], 1, a.shape[-1])

    def fwd(xin, Ps, Wb):
        modx, modc = Ps["modx"], Ps["modc"]
        cat = jnp.concatenate([ctx2d, xin], axis=0)
        seg = lambda a, b: jnp.stack([a, b])[:, None, :]
        (hcat,) = op_norm((cat,), (Ps["n1"][0][None, None, :], seg(modc[0], modx[0, 0]), seg(modc[1], modx[0, 1])))
        xr, xw, xk, xv, xa, xg = shift_mix(hcat, Ps["mix"][:, None, :], L)
        r = linear(xr, Wb["wr"], F32, "wr")
        k = linear(xk, Wb["wk"], F32, "wk")
        v = linear(xv, Wb["wv"], F32, "wv")
        gl = jax.nn.sigmoid(linear(xg, Wb["g1"], F32, "g1"))
        g = linear(gl.astype(BF16), Wb["g2"], F32, "g2")
        tw = jnp.tanh(linear(xw, Wb["w1"], F32, "w1")).astype(BF16)
        ta = linear(xa, Wb["a1"], F32, "a1").astype(BF16)
        lw = [linear(tw[:, LORA_PAD * d:LORA_PAD * (d + 1)], Wb["w2d"][d], F32, f"w2_{d}") for d in range(2)]
        la = [linear(ta[:, LORA_PAD * d:LORA_PAD * (d + 1)], Wb["a2d"][d], F32, f"a2_{d}") for d in range(2)]
        kk, dec0, dec1, kd0, kd1, as0, as1 = op_prep(
            (k, lw[0], lw[1], la[0], la[1]),
            (v3(Ps["kk"]), v3(Ps["ka"]), Ps["w0"][0][None, None, :], Ps["w0"][1][None, None, :],
             Ps["a0"][0][None, None, :], Ps["a0"][1][None, None, :]))
        r2, kk2, v2 = _tile_heads(r), _tile_heads(kk), _tile_heads(v)
        ys = []
        Wb = dict(Wb)
        for d, (dec, kd, sg) in enumerate(((dec0, kd0, as0), (dec1, kd1, as1))):
            op, keys = scans[d]
            outs = op((r2, _tile_heads(dec), _tile_heads(kd), kk2, _tile_heads(sg), v2),
                      tuple(shards[k] for k in keys))
            yx = outs[0]
            ys.append((yx[:, :, :HEAD] + yx[:, :, HEAD:]).reshape(N, D))
            Wb.update(zip(keys, outs[1:]))
        (o,) = op_read((ys[0], ys[1], r, kd0, kd1, v, g), (v3(Ps["rk"]), v3(Ps["lnw"]), v3(Ps["lnb"])))
        att = linear(o[L:], Wb["wo"], F32, "wo")
        x1, h2 = op_res[0]((xin, att), (modx[0, 2][None, None, :], Ps["n2"][0][None, None, :],
                                        modx[0, 3][None, None, :], modx[0, 4][None, None, :]))
        act = swiglu_act(linear(h2, Wb["w13_0"], BF16, "w13_0"), "swiglu0")
        f0 = linear(act, Wb["w2_0"], F32, "w2_0")
        x2, h = op_res[1]((x1, f0), (modx[0, 5][None, None, :], Ps["n1"][1][None, None, :],
                                     modx[1, 0][None, None, :], modx[1, 1][None, None, :]))
        guc = linear(h, Wb["win"], BF16, "win")
        p = gated_conv(guc, Ps["conv"][:, None, :])
        cv = linear(p, Wb["wout"], F32, "wout")
        x3, h2b = op_res[2]((x2, cv), (modx[1, 2][None, None, :], Ps["n2"][1][None, None, :],
                                       modx[1, 3][None, None, :], modx[1, 4][None, None, :]))
        act1 = swiglu_act(linear(h2b, Wb["w13_1"], BF16, "w13_1"), "swiglu1")
        f1 = linear(act1, Wb["w2_1"], F32, "w2_1")
        return x3, f1


    return fwd


def kernel(x, c, ctx, c_ctx, norm1_g, norm2_g, ada_w, ada_b, rw_mix, rw_wr, rw_wk, rw_wv, rw_wo, rw_w0, rw_w1, rw_w2, rw_a0, rw_a1, rw_a2, rw_g1, rw_g2, rw_kk, rw_ka, rw_rk, rw_lnw, rw_lnb, sc_win, sc_conv, sc_wout, ffn_w13, ffn_w2, final_g, loss_target, m_c_ctx, m_norm1_g, m_norm2_g, m_ada_w, m_ada_b, m_rw_mix, m_rw_wr, m_rw_wk, m_rw_wv, m_rw_wo, m_rw_w0, m_rw_w1, m_rw_w2, m_rw_a0, m_rw_a1, m_rw_a2, m_rw_g1, m_rw_g2, m_rw_kk, m_rw_ka, m_rw_rk, m_rw_lnw, m_rw_lnb, m_sc_win, m_sc_conv, m_sc_wout, m_ffn_w13, m_ffn_w2, m_final_g, v_c_ctx, v_norm1_g, v_norm2_g, v_ada_w, v_ada_b, v_rw_mix, v_rw_wr, v_rw_wk, v_rw_wv, v_rw_wo, v_rw_w0, v_rw_w1, v_rw_w2, v_rw_a0, v_rw_a1, v_rw_a2, v_rw_g1, v_rw_g2, v_rw_kk, v_rw_ka, v_rw_rk, v_rw_lnw, v_rw_lnb, v_sc_win, v_sc_conv, v_sc_wout, v_ffn_w13, v_ffn_w2, v_final_g):
    W = dict(c_ctx=c_ctx, norm1_g=norm1_g, norm2_g=norm2_g, ada_w=ada_w, ada_b=ada_b, rw_mix=rw_mix, rw_wr=rw_wr,
             rw_wk=rw_wk, rw_wv=rw_wv, rw_wo=rw_wo, rw_w0=rw_w0, rw_w1=rw_w1, rw_w2=rw_w2, rw_a0=rw_a0, rw_a1=rw_a1,
             rw_a2=rw_a2, rw_g1=rw_g1, rw_g2=rw_g2, rw_kk=rw_kk, rw_ka=rw_ka, rw_rk=rw_rk, rw_lnw=rw_lnw,
             rw_lnb=rw_lnb, sc_win=sc_win, sc_conv=sc_conv, sc_wout=sc_wout, ffn_w13=ffn_w13, ffn_w2=ffn_w2,
             final_g=final_g)
    Mo = dict(c_ctx=m_c_ctx, norm1_g=m_norm1_g, norm2_g=m_norm2_g, ada_w=m_ada_w, ada_b=m_ada_b, rw_mix=m_rw_mix,
              rw_wr=m_rw_wr, rw_wk=m_rw_wk, rw_wv=m_rw_wv, rw_wo=m_rw_wo, rw_w0=m_rw_w0, rw_w1=m_rw_w1,
              rw_w2=m_rw_w2, rw_a0=m_rw_a0, rw_a1=m_rw_a1, rw_a2=m_rw_a2, rw_g1=m_rw_g1, rw_g2=m_rw_g2,
              rw_kk=m_rw_kk, rw_ka=m_rw_ka, rw_rk=m_rw_rk, rw_lnw=m_rw_lnw, rw_lnb=m_rw_lnb, sc_win=m_sc_win,
              sc_conv=m_sc_conv, sc_wout=m_sc_wout, ffn_w13=m_ffn_w13, ffn_w2=m_ffn_w2, final_g=m_final_g)
    Vo = dict(c_ctx=v_c_ctx, norm1_g=v_norm1_g, norm2_g=v_norm2_g, ada_w=v_ada_w, ada_b=v_ada_b, rw_mix=v_rw_mix,
              rw_wr=v_rw_wr, rw_wk=v_rw_wk, rw_wv=v_rw_wv, rw_wo=v_rw_wo, rw_w0=v_rw_w0, rw_w1=v_rw_w1,
              rw_w2=v_rw_w2, rw_a0=v_rw_a0, rw_a1=v_rw_a1, rw_a2=v_rw_a2, rw_g1=v_rw_g1, rw_g2=v_rw_g2,
              rw_kk=v_rw_kk, rw_ka=v_rw_ka, rw_rk=v_rw_rk, rw_lnw=v_rw_lnw, rw_lnb=v_rw_lnb, sc_win=v_sc_win,
              sc_conv=v_sc_conv, sc_wout=v_sc_wout, ffn_w13=v_ffn_w13, ffn_w2=v_ffn_w2, final_g=v_final_g)
    names = list(W)

    x2d = x[0]
    ctx2d = ctx[0]
    tgt = loss_target[0]
    T, D = x2d.shape
    L = ctx2d.shape[0]
    N = L + T
    nh = D // HEAD
    mx, my, mc = _me()
    me = 4 * mx + 2 * my + mc
    dloc = D // N_DEV

    lr = rw_w1.shape[-1]
    pad_r = LORA_PAD - lr
    w1p = jnp.pad(rw_w1[0], ((0, 0), (0, 0), (0, pad_r)))
    a1p = jnp.pad(rw_a1[0], ((0, 0), (0, 0), (0, pad_r)))
    w2p = jnp.pad(rw_w2[0], ((0, 0), (0, pad_r), (0, 0)))
    a2p = jnp.pad(rw_a2[0], ((0, 0), (0, pad_r), (0, 0)))
    small_loc = [rw_mix[0], rw_w0[0], rw_a0[0], sc_conv[0], w1p, a1p, w2p, a2p, rw_g1[0], rw_g2[0]]
    small_dim = [1, 1, 1, 1, 1, 1, 2, 2, 0, 1]
    small_shapes = [a.shape for a in small_loc]
    small_groups = ((slice(0, 4), F32, "vec"), (slice(4, 10), BF16, "mat"))
    small_full = []
    for sl, dt, tag in small_groups:
        sm_all = all_gather_stack(_pack(small_loc[sl], dt), "ag_small_" + tag)
        sm_parts = _unpack(sm_all, small_shapes[sl], lead=(N_DEV,))
        small_full += [_gather_dim(p, dm) for p, dm in zip(sm_parts, small_dim[sl])]
    mix_f, w0_f, a0_f, conv_f, w1_f, a1_f, w2_f, a2_f, g1_f, g2_f = small_full

    c_all = all_gather_stack(jnp.pad(c, ((0, 7), (0, 0))), "ag_c")[:, 0, :]
    cond_pre = jnp.concatenate([c_all, c_ctx[None, :], jnp.zeros((7, D), F32)], axis=0)
    cond_rows = jax.nn.silu(cond_pre)
    ncol = ada_w.shape[-1]
    mod_loc = []
    for i in range(2):
        bi = lax.dynamic_slice(ada_b[i], (me * ncol,), (ncol,))
        mod_loc.append(_mm(cond_rows, ada_w[i], out_dtype=F32, name=f"ada_fwd{i}") + bi[None, :])
    mod_all = all_gather_stack(jnp.concatenate(mod_loc, axis=0), "ag_mod")
    mod_full = _gather_lastdim(mod_all).reshape(2, 16, 6, D)
    mod_x = lax.dynamic_index_in_dim(mod_full, me, axis=1, keepdims=False)
    mod_c = mod_full[0, 8, :2, :]

    def ag_w(wl, axis, name):
        return all_gather(wl.astype(BF16), axis, name)

    shards = dict(wo=rw_wo[0], win=sc_win[0], wout=sc_wout[0], w13_0=ffn_w13[0], w13_1=ffn_w13[1],
                  w2_0=ffn_w2[0], w2_1=ffn_w2[1])
    shards = {k_: a.astype(BF16) for k_, a in shards.items()}
    sink = {}
    Wb = dict(
        wr=ag_w(rw_wr[0], 0, "ag_wr"), wk=ag_w(rw_wk[0], 0, "ag_wk"), wv=ag_w(rw_wv[0], 0, "ag_wv"),
        w1=jnp.concatenate([w1_f[0], w1_f[1]], axis=1).astype(BF16),
        a1=jnp.concatenate([a1_f[0], a1_f[1]], axis=1).astype(BF16),
        w2d=w2_f.astype(BF16), a2d=a2_f.astype(BF16),
        g1=g1_f.astype(BF16), g2=g2_f.astype(BF16),
    )
    Ps = dict(n1=norm1_g, n2=norm2_g, modx=mod_x, modc=mod_c, mix=mix_f, w0=w0_f, a0=a0_f, conv=conv_f,
              kk=rw_kk, ka=rw_ka, rk=rw_rk.reshape(1, D), lnw=rw_lnw, lnb=rw_lnb)

    fwd = _build_forward(ctx2d, T, D, shards, sink)
    (x3, f1), vjp_fn = jax.vjp(fwd, x2d, Ps, Wb)
    loss_acc, dx3, df1, dgate, dfinal = loss_head(x3, f1, tgt, mod_x[1, 5][None, :], final_g[None, :])
    dx, dPs, dWb = vjp_fn((dx3, df1))
    loss = lax.psum(loss_acc[0, 0], ("x", "y", "c"))

    dmodx = dPs["modx"].at[1, 5].add(dgate[0])
    dmodc = jnp.concatenate([dPs["modc"], jnp.zeros((4, D), F32)], axis=0)
    drow = jnp.stack([dmodx.reshape(2, 6 * D), jnp.stack([dmodc.reshape(6 * D), jnp.zeros((6 * D,), F32)])], axis=1)
    drow_all = all_gather_stack(drow.reshape(4, 6 * D), "ag_dmod").reshape(N_DEV, 2, 2, 6 * D)
    dctx_tot = drow_all[0, :, 1, :]
    for s in range(1, N_DEV):
        dctx_tot = dctx_tot + drow_all[s, :, 1, :]
    dmod_rows = jnp.concatenate([jnp.moveaxis(drow_all[:, :, 0, :], 0, 1), dctx_tot[:, None, :],
                                 jnp.zeros((2, 7, 6 * D), F32)], axis=1)
    grad_ada_b = dctx_tot
    for s in range(N_DEV):
        grad_ada_b = grad_ada_b + drow_all[s, :, 0, :]
    dmod_mine = lax.dynamic_slice_in_dim(dmod_rows, me * ncol, ncol, axis=2)
    g_ada_w = [_mm(cond_rows, dmod_mine[i], ta=True, out_dtype=F32, name=f"ada_dw{i}") for i in range(2)]
    dcond_part = _mm(dmod_mine[0], ada_w[0], tb=True, out_dtype=F32, name="ada_dcond")[8]

    rep_names = ["c_ctx", "norm1_g", "norm2_g", "rw_kk", "rw_ka", "rw_rk", "rw_lnw", "rw_lnb", "final_g"]
    rep_part = [dcond_part, dPs["n1"], dPs["n2"], dPs["kk"], dPs["ka"], dPs["rk"].reshape(W["rw_rk"].shape),
                dPs["lnw"], dPs["lnb"], dfinal[0]]
    rep_shapes = [W[n_].shape for n_ in rep_names]
    rep_all = all_gather_stack(_pack(rep_part), "ag_rep_grads")
    sg = jax.nn.sigmoid(c_ctx)
    dsilu = sg * (1.0 + c_ctx * (1.0 - sg))
    rep_scale = _pack([dsilu] + [jnp.ones(s, F32) for s in rep_shapes[1:]])
    rep_all = rep_all * rep_scale[None]
    rep_w = _pack([W[n_] for n_ in rep_names])
    rep_m = _pack([Mo[n_] for n_ in rep_names])
    rep_v = _pack([Vo[n_] for n_ in rep_names])
    rep_out = sum_adam(rep_all, rep_w, rep_m, rep_v, "adam_rep")
    results = {}
    for nm_, vals in zip(rep_names, zip(*[_unpack(o, rep_shapes) for o in rep_out])):
        results[nm_] = vals

    results["ada_b"] = tuple(sum_adam(grad_ada_b.reshape(1, 2 * 6, D), ada_b.reshape(12, D), m_ada_b.reshape(12, D),
                                      v_ada_b.reshape(12, D), "adam_ada_b"))
    results["ada_b"] = tuple(o.reshape(ada_b.shape) for o in results["ada_b"])

    outs = sum_adam(g_ada_w[0][None], ada_w, m_ada_w, v_ada_w, "adam_ada_w0", lead=0)
    results["ada_w"] = tuple(sum_adam(g_ada_w[1][None], ada_w, m_ada_w, v_ada_w, "adam_ada_w1", lead=1, prev=outs))

    dw1 = jnp.stack([dWb["w1"][:, :LORA_PAD], dWb["w1"][:, LORA_PAD:]])
    da1 = jnp.stack([dWb["a1"][:, :LORA_PAD], dWb["a1"][:, LORA_PAD:]])
    small_g = [dPs["mix"], dPs["w0"], dPs["a0"], dPs["conv"], dw1, da1, dWb["w2d"], dWb["a2d"], dWb["g1"], dWb["g2"]]
    small_names = ["rw_mix", "rw_w0", "rw_a0", "sc_conv", "rw_w1", "rw_a1", "rw_w2", "rw_a2", "rw_g1", "rw_g2"]

    def padded_local(nm_, src):
        a = src[nm_][0]
        if nm_ in ("rw_w1", "rw_a1"):
            return jnp.pad(a, ((0, 0), (0, 0), (0, pad_r)))
        if nm_ in ("rw_w2", "rw_a2"):
            return jnp.pad(a, ((0, 0), (0, pad_r), (0, 0)))
        return a

    for sl, dt, tag in small_groups:
        blocks = [_scatter_dim(gf, dm) for gf, dm in zip(small_g[sl], small_dim[sl])]
        sm_recv = reduce_scatter_exchange(_pack(blocks, dt, lead=1), None, "rs_small_" + tag)
        sm_out = sum_adam(sm_recv, _pack([padded_local(n_, W) for n_ in small_names[sl]]),
                          _pack([padded_local(n_, Mo) for n_ in small_names[sl]]),
                          _pack([padded_local(n_, Vo) for n_ in small_names[sl]]), "adam_small_" + tag)
        for nm_, vals in zip(small_names[sl], zip(*[_unpack(o, small_shapes[sl]) for o in sm_out])):
            if nm_ in ("rw_w1", "rw_a1"):
                vals = tuple(a[:, :, :lr] for a in vals)
            if nm_ in ("rw_w2", "rw_a2"):
                vals = tuple(a[:, :lr, :] for a in vals)
            results[nm_] = tuple(a[None] for a in vals)

    def rs_adam(key, nm_, lead, prev=None):
        recv = sink[key] if key in sink else reduce_scatter_exchange(dWb[key], _W_AXIS[key], "rs_" + key)
        return sum_adam(recv, W[nm_], Mo[nm_], Vo[nm_], "adam_" + key, lead=lead, prev=prev)

    for nm_, key in (("rw_wr", "wr"), ("rw_wk", "wk"), ("rw_wv", "wv"), ("rw_wo", "wo"), ("sc_win", "win"),
                     ("sc_wout", "wout")):
        results[nm_] = tuple(rs_adam(key, nm_, 0))
    for nm_, key in (("ffn_w13", "w13"), ("ffn_w2", "w2")):
        results[nm_] = tuple(rs_adam(f"{key}_1", nm_, 1, prev=rs_adam(f"{key}_0", nm_, 0)))

    grads = [results[n_][0] for n_ in names]
    deltas = [results[n_][1] for n_ in names]
    new_m = [results[n_][2] for n_ in names]
    new_v = [results[n_][3] for n_ in names]
    return (loss, dx[None], *grads, *deltas, *new_m, *new_v)
```

```python
import functools
import math

import numpy as np
import jax
import jax.numpy as jnp
from jax import lax
from jax.experimental import pallas as pl
from jax.experimental.pallas import tpu as pltpu

F32 = jnp.float32
BF16 = jnp.bfloat16

N_DEV = 8
HEAD = 64
LANES = 128
GRID_W = 64
LORA_PAD = 128
NORM_EPS = 1e-6
GN_EPS = 64e-5
ADAM_LR, ADAM_B1, ADAM_B2, ADAM_EPS, ADAM_WD, ADAM_STEP = 0.001, 0.9, 0.999, 1e-08, 0.01, 10
VMEM_LIMIT = 52 * 1024 * 1024
SCAN_CHUNK_FWD = 16
SCAN_CHUNK_BWD = 8
SCAN_UNROLL = 4
HI = lax.Precision.HIGHEST


def _cparams(sem):
    return pltpu.CompilerParams(dimension_semantics=sem, vmem_limit_bytes=VMEM_LIMIT)


def _pick(n, cap, quantum=LANES):
    best = None
    for t in range(quantum, min(n, cap) + 1, quantum):
        if n % t == 0:
            best = t
    return n if best is None else best


MM_VMEM_BUDGET = 36 * 1024 * 1024


def _divisors(n, cap, quantum=LANES):
    ds = [t for t in range(quantum, min(n, cap) + 1, quantum) if n % t == 0]
    return sorted(ds, reverse=True) or [n]


def _mm_tiles(M, N, K, sa, sb, so):
    tms, tns, tks = _divisors(M, 1024), _divisors(N, 1024), _divisors(K, 2816)
    im = jn = ik = 0

    def est(tm, tn, tk):
        b = 2 * (tm * tk * sa + tk * tn * sb) + 2 * tm * tn * so + tm * tn * 4
        b += tm * tk * 2 if sa == 4 else 0
        b += tk * tn * 2 if sb == 4 else 0
        return b + (tm * tn * 4 if tk < K else 0)

    while est(tms[im], tns[jn], tks[ik]) > MM_VMEM_BUDGET:
        if tms[im] >= tns[jn] and im + 1 < len(tms):
            im += 1
        elif jn + 1 < len(tns):
            jn += 1
        elif im + 1 < len(tms):
            im += 1
        elif ik + 1 < len(tks):
            ik += 1
        else:
            break
    return tms[im], tns[jn], tks[ik]


def _mm(a, b, *, ta=False, tb=False, out_dtype, name):
    if ta:
        K, M = a.shape
    else:
        M, K = a.shape
    if tb:
        N, Kb = b.shape
    else:
        Kb, N = b.shape
    assert K == Kb, (a.shape, b.shape, ta, tb)
    tm, tn, tk = _mm_tiles(M, N, K, a.dtype.itemsize, b.dtype.itemsize, jnp.dtype(out_dtype).itemsize)
    nk = K // tk
    dims = (((0 if ta else 1,), (1 if tb else 0,)), ((), ()))

    def body(a_ref, b_ref, o_ref, *acc):
        part = lax.dot_general(a_ref[...].astype(BF16), b_ref[...].astype(BF16), dims, preferred_element_type=F32)
        if nk == 1:
            o_ref[...] = part.astype(o_ref.dtype)
            return
        acc_ref, = acc
        k = pl.program_id(2)

        @pl.when(k == 0)
        def _():
            acc_ref[...] = part

        @pl.when(k > 0)
        def _():
            acc_ref[...] += part

        @pl.when(k == nk - 1)
        def _():
            o_ref[...] = acc_ref[...].astype(o_ref.dtype)

    a_spec = pl.BlockSpec((tk, tm), lambda i, j, k: (k, i)) if ta else pl.BlockSpec((tm, tk), lambda i, j, k: (i, k))
    b_spec = pl.BlockSpec((tn, tk), lambda i, j, k: (j, k)) if tb else pl.BlockSpec((tk, tn), lambda i, j, k: (k, j))
    return pl.pallas_call(
        body, name=name, grid=(M // tm, N // tn, nk),
        in_specs=[a_spec, b_spec],
        out_specs=pl.BlockSpec((tm, tn), lambda i, j, k: (i, j)),
        out_shape=jax.ShapeDtypeStruct((M, N), out_dtype),
        scratch_shapes=[pltpu.VMEM((tm, tn), F32)] if nk > 1 else [],
        compiler_params=_cparams(("parallel", "parallel", "arbitrary")),
    )(a, b)


@functools.partial(jax.custom_vjp, nondiff_argnums=(2, 3))
def linear(a, w, out_dtype, name):
    return _mm(a, w, out_dtype=out_dtype, name=name + "_fwd")


def _linear_fwd(a, w, out_dtype, name):
    return _mm(a, w, out_dtype=out_dtype, name=name + "_fwd"), (a, w)


def _linear_bwd(out_dtype, name, res, g):
    a, w = res
    da = _mm(g, w, tb=True, out_dtype=a.dtype, name=name + "_da")
    dw = _mm(a, g, ta=True, out_dtype=w.dtype, name=name + "_dw")
    return da, dw


linear.defvjp(_linear_fwd, _linear_bwd)


def _rw_specs(tiles, col_offs, vecs, consts, tr, tc, nb0):
    tile_specs = [pl.BlockSpec((tr, tc), functools.partial(lambda j, i, off: (i, j + off), off=off))
                  for _, off in zip(tiles, col_offs)]

    def vec_map(S):
        if S == 1:
            return lambda j, i: (0, 0, j)
        return lambda j, i: (jnp.where(i < nb0, 0, 1), 0, j)

    vec_specs = [pl.BlockSpec((None, 1, tc), vec_map(v.shape[0])) for v in vecs]
    const_specs = [pl.BlockSpec(c.shape, lambda j, i: (0, 0)) for c in consts]
    return tile_specs, vec_specs, const_specs


def _rw_forward(name, f, tiles, col_offs, vecs, consts, out_dtypes, tr, tc, nb0, width):
    n = tiles[0].shape[0]
    nt, nv, nc = len(tiles), len(vecs), len(consts)
    tile_specs, vec_specs, const_specs = _rw_specs(tiles, col_offs, vecs, consts, tr, tc, nb0)

    def body(*refs):
        ins = [r[...].astype(F32) for r in refs[:nt]] + [r[...] for r in refs[nt:nt + nv + nc]]
        outs = f(*ins)
        for o_ref, o in zip(refs[nt + nv + nc:], outs):
            o_ref[...] = o.astype(o_ref.dtype)

    return pl.pallas_call(
        body, name=name + "_fwd", grid=(width // tc, n // tr),
        in_specs=tile_specs + vec_specs + const_specs,
        out_specs=[pl.BlockSpec((tr, tc), lambda j, i: (i, j)) for _ in out_dtypes],
        out_shape=[jax.ShapeDtypeStruct((n, width), dt) for dt in out_dtypes],
        compiler_params=_cparams(("parallel", "parallel")),
    )(*tiles, *vecs, *consts)


def _rw_backward(name, f, tiles, col_offs, vecs, consts, douts, tr, tc, nb0, width):
    n = tiles[0].shape[0]
    nt, nv, nc, no = len(tiles), len(vecs), len(consts), len(douts)
    tile_specs, vec_specs, const_specs = _rw_specs(tiles, col_offs, vecs, consts, tr, tc, nb0)

    def body(*refs):
        t_in = [r[...].astype(F32) for r in refs[:nt]]
        v_in = [r[...] for r in refs[nt:nt + nv]]
        c_in = [r[...] for r in refs[nt + nv:nt + nv + nc]]
        d_in = tuple(r[...].astype(F32) for r in refs[nt + nv + nc:nt + nv + nc + no])
        o_refs = refs[nt + nv + nc + no:]
        _, vjp = jax.vjp(lambda *tv: tuple(f(*tv, *c_in)), *t_in, *v_in)
        grads = vjp(d_in)
        for o_ref, g in zip(o_refs[:nt], grads[:nt]):
            o_ref[...] = g.astype(o_ref.dtype)
        i = pl.program_id(1)
        for o_ref, g, v in zip(o_refs[nt:], grads[nt:], vecs):
            first = jnp.logical_or(i == 0, i == nb0) if v.shape[0] == 2 else i == 0

            @pl.when(first)
            def _(o_ref=o_ref, g=g):
                o_ref[...] = g

            @pl.when(jnp.logical_not(first))
            def _(o_ref=o_ref, g=g):
                o_ref[...] += g

    dout_specs = [pl.BlockSpec((tr, tc), lambda j, i: (i, j)) for _ in douts]
    out_specs = [pl.BlockSpec((tr, tc), lambda j, i: (i, j)) for _ in tiles] + list(vec_specs)
    out_shape = ([jax.ShapeDtypeStruct((n, width), t.dtype) for t in tiles]
                 + [jax.ShapeDtypeStruct(v.shape, F32) for v in vecs])
    return pl.pallas_call(
        body, name=name + "_bwd", grid=(width // tc, n // tr),
        in_specs=tile_specs + vec_specs + const_specs + dout_specs,
        out_specs=out_specs, out_shape=out_shape,
        compiler_params=_cparams(("parallel", "arbitrary")),
    )(*tiles, *vecs, *consts, *douts)


def make_rowwise(name, f, out_dtypes, tr, tc, consts=(), nb0=-1):
    consts = tuple(consts)

    @jax.custom_vjp
    def op(tiles, vecs):
        w = tiles[0].shape[1]
        return tuple(_rw_forward(name, f, tiles, (0,) * len(tiles), vecs, consts, out_dtypes, tr, min(tc, w), nb0, w))

    def op_fwd(tiles, vecs):
        return op(tiles, vecs), (tiles, vecs)

    def op_bwd(res, douts):
        tiles, vecs = res
        w = tiles[0].shape[1]
        g = _rw_backward(name, f, tiles, (0,) * len(tiles), vecs, consts, tuple(douts), tr, min(tc, w), nb0, w)
        return tuple(g[:len(tiles)]), tuple(g[len(tiles):])

    op.defvjp(op_fwd, op_bwd)
    return op


def _f_norm_mod(x, g, sh, sc):
    hn = x * lax.rsqrt(jnp.mean(x * x, axis=-1, keepdims=True) + NORM_EPS)
    return ((hn * g) * (1.0 + sc) + sh,)


def _f_res_norm_mod(x, y, gate, g, sh, sc):
    x1 = x + gate * y
    hn = x1 * lax.rsqrt(jnp.mean(x1 * x1, axis=-1, keepdims=True) + NORM_EPS)
    return x1, (hn * g) * (1.0 + sc) + sh


def _head_sum_3pass(t, gmat):
    hi = t.astype(BF16)
    r1 = t - hi.astype(F32)
    mid = r1.astype(BF16)
    lo = (r1 - mid.astype(F32)).astype(BF16)
    g = gmat.astype(BF16)
    dot = lambda u: jnp.dot(u, g, preferred_element_type=F32)
    return dot(hi) + dot(mid) + dot(lo)


@jax.custom_vjp
def _head_sum(t, gmat):
    return _head_sum_3pass(t, gmat)


def _head_sum_fwd(t, gmat):
    return _head_sum_3pass(t, gmat), gmat


def _head_sum_bwd(gmat, ct):
    return _head_sum_3pass(ct, gmat), None


_head_sum.defvjp(_head_sum_fwd, _head_sum_bwd)


def _f_prep(k, lw0, lw1, la0, la1, kkp, kap, w00, w01, a00, a01, gmat):
    t = k * kkp
    kk = t / jnp.maximum(jnp.sqrt(_head_sum(t * t, gmat)), 1e-12)
    outs = [kk]
    decs, kds, sigs = [], [], []
    for lw, la, w0, a0 in ((lw0, la0, w00, a00), (lw1, la1, w01, a01)):
        decs.append(jnp.exp(-jax.nn.sigmoid(w0 + lw) * float(np.exp(-0.5))))
        a = jax.nn.sigmoid(a0 + la)
        sigs.append(a)
        kds.append(k * (1.0 + (a - 1.0) * kap))
    return tuple(outs + decs + kds + sigs)


def _f_readout(y0, y1, r, kd0, kd1, v, g, rk, lnw, lnb, gmat):
    y = y0 + y1
    mu = _head_sum(y, gmat) * (1.0 / HEAD)
    d = y - mu
    var = _head_sum(d * d, gmat) * (1.0 / HEAD)
    o = d * lax.rsqrt(var + GN_EPS) * lnw + lnb
    bonus = _head_sum(r * (kd0 + kd1) * rk, gmat) * v
    return ((o + bonus) * g,)


def _f_swiglu(a, b):
    return (jax.nn.silu(a) * b,)


def swiglu_act(ab, name):
    t, f2 = ab.shape
    fdim = f2 // 2
    tr, tc = _pick(t, 512, 8), _pick(fdim, 512)
    offs = (0, fdim // tc)

    @jax.custom_vjp
    def op(ab_):
        return _rw_forward(name, _f_swiglu, (ab_, ab_), offs, (), (), (BF16,), tr, tc, -1, fdim)[0]

    def op_fwd(ab_):
        return op(ab_), ab_

    def op_bwd(ab_, dact):
        da, db = _rw_backward(name, _f_swiglu, (ab_, ab_), offs, (), (), (dact,), tr, tc, -1, fdim)
        return (jnp.concatenate([da, db], axis=1),)

    op.defvjp(op_fwd, op_bwd)
    return op(ab)


def _row_iota(n, tc):
    return lax.broadcasted_iota(jnp.int32, (n, tc), 0)


def _shift_rows(x, s, keep):
    n = x.shape[0]
    return jnp.where(keep, pltpu.roll(x, s % n, 0), 0.0)


def _unshift_rows(d, s, keep):
    n = d.shape[0]
    return pltpu.roll(jnp.where(keep, d, 0.0), (-s) % n, 0)


def _ctx_shift_spec(L, tc, quarter):
    row = _row_iota(L, tc)
    if quarter < 2:
        return 1, row >= 1
    return -1, row < L - 1


def _grid_shift_spec(T, tc, quarter):
    row = _row_iota(T, tc)
    col = jnp.bitwise_and(row, GRID_W - 1)
    if quarter == 0:
        return 1, col != 0
    if quarter == 1:
        return -1, col != GRID_W - 1
    if quarter == 2:
        return GRID_W, row >= GRID_W
    return -GRID_W, row < T - GRID_W


def _shift_mix_fwd_call(h, mix3, L):
    n, d = h.shape
    T = n - L
    tc = _pick(d // 4, 256)
    nq = (d // 4) // tc

    def body(h_ref, mix_ref, *o_refs):
        q = pl.program_id(0) // nq
        for quarter in range(4):
            @pl.when(q == quarter)
            def _(quarter=quarter):
                for lo, cnt, spec in ((0, L, _ctx_shift_spec), (L, T, _grid_shift_spec)):
                    hh = h_ref[pl.ds(lo, cnt), :]
                    s, keep = spec(cnt, tc, quarter)
                    xx = _shift_rows(hh, s, keep) - hh
                    for m in range(6):
                        o_refs[m][pl.ds(lo, cnt), :] = (hh + xx * mix_ref[m]).astype(BF16)

    return pl.pallas_call(
        body, name="shift_mix_fwd", grid=(d // tc,),
        in_specs=[pl.BlockSpec((n, tc), lambda j: (0, j)), pl.BlockSpec((6, 1, tc), lambda j: (0, 0, j))],
        out_specs=[pl.BlockSpec((n, tc), lambda j: (0, j)) for _ in range(6)],
        out_shape=[jax.ShapeDtypeStruct((n, d), BF16) for _ in range(6)],
        compiler_params=_cparams(("parallel",)),
    )(h, mix3)


def _shift_mix_bwd_call(h, mix3, douts, L):
    n, d = h.shape
    T = n - L
    tc = _pick(d // 4, 256)
    nq = (d // 4) // tc

    def body(h_ref, mix_ref, d0, d1, d2, d3, d4, d5, dh_ref, dmix_ref):
        d_refs = (d0, d1, d2, d3, d4, d5)
        q = pl.program_id(0) // nq
        for quarter in range(4):
            @pl.when(q == quarter)
            def _(quarter=quarter):
                dmix = [jnp.zeros((1, tc), F32) for _ in range(6)]
                for lo, cnt, spec in ((0, L, _ctx_shift_spec), (L, T, _grid_shift_spec)):
                    hh = h_ref[pl.ds(lo, cnt), :]
                    s, keep = spec(cnt, tc, quarter)
                    xx = _shift_rows(hh, s, keep) - hh
                    direct = jnp.zeros((cnt, tc), F32)
                    shifted = jnp.zeros((cnt, tc), F32)
                    for m in range(6):
                        dm = d_refs[m][pl.ds(lo, cnt), :].astype(F32)
                        mx = mix_ref[m]
                        direct = direct + dm * (1.0 - mx)
                        shifted = shifted + dm * mx
                        dmix[m] = dmix[m] + jnp.sum(dm * xx, axis=0, keepdims=True)
                    dh_ref[pl.ds(lo, cnt), :] = direct + _unshift_rows(shifted, s, keep)
                for m in range(6):
                    dmix_ref[m] = dmix[m]

    tile = pl.BlockSpec((n, tc), lambda j: (0, j))
    return pl.pallas_call(
        body, name="shift_mix_bwd", grid=(d // tc,),
        in_specs=[tile, pl.BlockSpec((6, 1, tc), lambda j: (0, 0, j))] + [tile] * 6,
        out_specs=[tile, pl.BlockSpec((6, 1, tc), lambda j: (0, 0, j))],
        out_shape=[jax.ShapeDtypeStruct((n, d), F32), jax.ShapeDtypeStruct((6, 1, d), F32)],
        compiler_params=_cparams(("parallel",)),
    )(h, mix3, *douts)


@functools.partial(jax.custom_vjp, nondiff_argnums=(2,))
def shift_mix(h, mix3, L):
    return tuple(_shift_mix_fwd_call(h, mix3, L))


def _shift_mix_fwd(h, mix3, L):
    return tuple(_shift_mix_fwd_call(h, mix3, L)), (h, mix3)


def _shift_mix_bwd(L, res, douts):
    h, mix3 = res
    dh, dmix = _shift_mix_bwd_call(h, mix3, tuple(douts), L)
    return dh, dmix


shift_mix.defvjp(_shift_mix_fwd, _shift_mix_bwd)


def _conv_specs(T, d, tc):
    nd = d // tc
    ins = [pl.BlockSpec((T, tc), functools.partial(lambda j, off: (0, j + off), off=o * nd)) for o in range(3)]
    return ins, pl.BlockSpec((3, 1, tc), lambda j: (0, 0, j))


def _conv_terms(gc, u, tc):
    T = gc.shape[0]
    row = _row_iota(T, tc)
    z = gc * u
    return z, _shift_rows(z, 1, row >= 1), _shift_rows(z, -1, row < T - 1), row


def _conv_fwd_call(guc, cw3):
    T, d3 = guc.shape
    d = d3 // 3
    tc = _pick(d, 256)
    ins, wspec = _conv_specs(T, d, tc)

    def body(gb_ref, gc_ref, u_ref, w_ref, p_ref):
        z, zp, zn, _ = _conv_terms(gc_ref[...].astype(F32), u_ref[...].astype(F32), tc)
        conv = zp * w_ref[0] + z * w_ref[1] + zn * w_ref[2]
        p_ref[...] = (gb_ref[...].astype(F32) * conv).astype(BF16)

    return pl.pallas_call(
        body, name="conv_fwd", grid=(d // tc,), in_specs=ins + [wspec],
        out_specs=pl.BlockSpec((T, tc), lambda j: (0, j)),
        out_shape=jax.ShapeDtypeStruct((T, d), BF16),
        compiler_params=_cparams(("parallel",)),
    )(guc, guc, guc, cw3)


def _conv_bwd_call(guc, cw3, dp):
    T, d3 = guc.shape
    d = d3 // 3
    tc = _pick(d, 256)
    ins, wspec = _conv_specs(T, d, tc)
    tile = pl.BlockSpec((T, tc), lambda j: (0, j))

    def body(gb_ref, gc_ref, u_ref, w_ref, dp_ref, dgb_ref, dgc_ref, du_ref, dw_ref):
        gc = gc_ref[...].astype(F32)
        u = u_ref[...].astype(F32)
        z, zp, zn, row = _conv_terms(gc, u, tc)
        conv = zp * w_ref[0] + z * w_ref[1] + zn * w_ref[2]
        dpv = dp_ref[...].astype(F32)
        dgb_ref[...] = (dpv * conv).astype(dgb_ref.dtype)
        dconv = dpv * gb_ref[...].astype(F32)
        dz = (_shift_rows(dconv, -1, row < T - 1) * w_ref[0] + dconv * w_ref[1]
              + _shift_rows(dconv, 1, row >= 1) * w_ref[2])
        dgc_ref[...] = (dz * u).astype(dgc_ref.dtype)
        du_ref[...] = (dz * gc).astype(du_ref.dtype)
        dw_ref[0] = jnp.sum(dconv * zp, axis=0, keepdims=True)
        dw_ref[1] = jnp.sum(dconv * z, axis=0, keepdims=True)
        dw_ref[2] = jnp.sum(dconv * zn, axis=0, keepdims=True)

    return pl.pallas_call(
        body, name="conv_bwd", grid=(d // tc,), in_specs=ins + [wspec, tile],
        out_specs=[tile, tile, tile, wspec],
        out_shape=[jax.ShapeDtypeStruct((T, d), guc.dtype)] * 3 + [jax.ShapeDtypeStruct((3, 1, d), F32)],
        compiler_params=_cparams(("parallel",)),
    )(guc, guc, guc, cw3, dp)


@jax.custom_vjp
def gated_conv(guc, cw3):
    return _conv_fwd_call(guc, cw3)


def _gated_conv_fwd(guc, cw3):
    return _conv_fwd_call(guc, cw3), (guc, cw3)


def _gated_conv_bwd(res, dp):
    guc, cw3 = res
    dgb, dgc, du, dw = _conv_bwd_call(guc, cw3, dp)
    return jnp.concatenate([dgb, dgc, du], axis=1), dw


gated_conv.defvjp(_gated_conv_fwd, _gated_conv_bwd)


def _chunk_map(nchunk, nctx_chunk, reverse):
    if not reverse:
        return lambda c: c
    return lambda c: jnp.where(c < nctx_chunk, nctx_chunk - 1 - c, nchunk - 1 - (c - nctx_chunk))


def _spread_bf16(row_ref, dst_scr, ni, C):
    packed = [pltpu.bitcast(row_ref[tt].astype(BF16), jnp.int32) for tt in range(C)]
    lane = lax.broadcasted_iota(jnp.int32, packed[0].shape, 1)
    for i in range(ni):
        idx = jnp.where(lane < HEAD, 2 * i, HEAD + 1 + 2 * i).astype(jnp.int32)
        for tt in range(C):
            got = jnp.take_along_axis(packed[tt], idx, axis=1)
            dst_scr[tt, i] = pltpu.bitcast(got, BF16).astype(F32)


def _half_sums(p, lo_mask):
    lo = jnp.sum(jnp.where(lo_mask, p, 0.0), axis=1, keepdims=True)
    hi = jnp.sum(jnp.where(lo_mask, 0.0, p), axis=1, keepdims=True)
    return jnp.where(lo_mask, lo, hi)


def _half_sums_mxu(ps, gmat, passes):
    p = jnp.concatenate(ps, axis=0)
    hi = p.astype(BF16)
    s = jnp.dot(hi, gmat, preferred_element_type=F32)
    if passes == 2:
        lo = (p - hi.astype(F32)).astype(BF16)
        s = s + jnp.dot(lo, gmat, preferred_element_type=F32)
    nh = ps[0].shape[0]
    return [s[i * nh:(i + 1) * nh] for i in range(len(ps))]


def _split_row(sums, lane, nh):
    acc = jnp.zeros((nh, LANES), F32)
    for i, s in enumerate(sums):
        acc = acc + jnp.where(jnp.logical_or(lane == 2 * i, lane == HEAD + 1 + 2 * i), s, 0.0)
    return acc


def _wkv_fwd_call(r2, w2, kd2, kk2, as2, v2, nctx, reverse, xchg_arrs=(), xchg_specs=()):
    n, nh, _ = r2.shape
    C = math.gcd(SCAN_CHUNK_FWD, nctx)
    ni = HEAD // 2
    nchunk = n // C
    cmap = _chunk_map(nchunk, nctx // C, reverse)
    nx = len(xchg_arrs)

    def body(*refs):
        g_ref, refs = refs[0], refs[1:]
        r_ref, w_ref, kd_ref, kk_ref, as_ref, v_ref = refs[:6]
        x_in = refs[6:6 + nx]
        y_ref, sa_ref, sp_ref = refs[6 + nx:9 + nx]
        x_out = refs[9 + nx:9 + 2 * nx]
        s_scr, vc_scr = refs[9 + 2 * nx:11 + 2 * nx]
        if nx:
            _fused_exchanges(list(zip(x_in, x_out)), xchg_specs, *refs[11 + 2 * nx:], first=pl.program_id(0) == 0)

        @pl.when(pl.program_id(0) == 0)
        def _():
            s_scr[...] = jnp.zeros_like(s_scr)

        lane = lax.broadcasted_iota(jnp.int32, (nh, LANES), 1)
        lo_mask = lane < HEAD
        _spread_bf16(v_ref, vc_scr, ni, C)

        def make_step(with_y):
            def step(j, carry):
                t = (C - 1 - j) if reverse else j
                kk = kk_ref[t]
                a2 = -kk
                b2 = kk * as_ref[t]
                w = w_ref[t]
                k = kd_ref[t]
                r = r_ref[t]
                sas = []
                for i in range(ni):
                    si = s_scr[i]
                    sp_ref[t, i] = si
                    sas.append(_half_sums(si * a2, lo_mask))
                qs = []
                for i in range(ni):
                    sn = s_scr[i] * w + sas[i] * b2 + vc_scr[t, i] * k
                    s_scr[i] = sn
                    if with_y:
                        qs.append(sn * r)
                if with_y:
                    y_ref[t] = _split_row(_half_sums_mxu(qs, g_ref[...], 1), lane, nh)
                else:
                    y_ref[t] = jnp.zeros((nh, LANES), F32)
                sa_ref[t] = _split_row(sas, lane, nh)
                return carry
            return step

        is_ctx = pl.program_id(0) < nctx // C

        @pl.when(is_ctx)
        def _():
            lax.fori_loop(0, C, make_step(False), 0)

        @pl.when(jnp.logical_not(is_ctx))
        def _():
            lax.fori_loop(0, C, make_step(True), 0, unroll=SCAN_UNROLL)

        if nx:
            _fused_exchanges(list(zip(x_in, x_out)), xchg_specs, *refs[11 + 2 * nx:],
                             last=pl.program_id(0) == nchunk - 1)

    tok = pl.BlockSpec((C, nh, LANES), lambda c: (cmap(c), 0, 0))
    return pl.pallas_call(
        body, name="wkv_fwd_rev" if reverse else "wkv_fwd", grid=(nchunk,),
        in_specs=[pl.BlockSpec((LANES, LANES), lambda c: (0, 0))] + [tok] * 6 + [_ANY] * nx,
        out_specs=[tok, tok, pl.BlockSpec((C, ni, nh, LANES), lambda c: (cmap(c), 0, 0, 0))] + [_ANY] * nx,
        out_shape=[jax.ShapeDtypeStruct((n, nh, LANES), F32), jax.ShapeDtypeStruct((n, nh, LANES), F32),
                   jax.ShapeDtypeStruct((n, ni, nh, LANES), F32)] + _exchange_out_shapes(xchg_arrs, xchg_specs),
        scratch_shapes=[pltpu.VMEM((ni, nh, LANES), F32), pltpu.VMEM((C, ni, nh, LANES), F32)]
        + (_exchange_sems(nx) if nx else []),
        compiler_params=_cparams(("arbitrary",)),
    )(_head_group_matrix(LANES).astype(BF16), r2, w2, kd2, kk2, as2, v2, *xchg_arrs)


def _wkv_bwd_call(r2, w2, kd2, kk2, as2, v2, sa, sprev, dy, nctx, reverse, xchg_arrs=(), xchg_specs=()):
    n, nh, _ = r2.shape
    C = math.gcd(SCAN_CHUNK_BWD, nctx)
    ni = HEAD // 2
    nchunk = n // C
    fmap = _chunk_map(nchunk, nctx // C, reverse)
    cmap = lambda c: fmap(nchunk - 1 - c)
    nx = len(xchg_arrs)

    def body(*refs):
        g_ref, refs = refs[0], refs[1:]
        r_ref, w_ref, kd_ref, kk_ref, as_ref, v_ref, sa_ref, sp_ref, dy_ref = refs[:9]
        x_in = refs[9:9 + nx]
        dr_ref, dw_ref, dkd_ref, dkk_ref, das_ref, dv_ref = refs[9 + nx:15 + nx]
        x_out = refs[15 + nx:15 + 2 * nx]
        ds_scr, vc_scr, sac_scr, dyc_scr = refs[15 + 2 * nx:19 + 2 * nx]
        if nx:
            _fused_exchanges(list(zip(x_in, x_out)), xchg_specs, *refs[19 + 2 * nx:], first=pl.program_id(0) == 0)

        @pl.when(pl.program_id(0) == 0)
        def _():
            ds_scr[...] = jnp.zeros_like(ds_scr)

        lane = lax.broadcasted_iota(jnp.int32, (nh, LANES), 1)
        lo_mask = lane < HEAD
        _spread_bf16(v_ref, vc_scr, ni, C)
        _spread_bf16(sa_ref, sac_scr, ni, C)
        is_ctx = pl.program_id(0) >= nchunk - nctx // C

        @pl.when(jnp.logical_not(is_ctx))
        def _():
            _spread_bf16(dy_ref, dyc_scr, ni, C)

        def make_step(with_dy):
            def step(j, carry):
                t = j if reverse else (C - 1 - j)
                kk = kk_ref[t]
                sig = as_ref[t]
                a2 = -kk
                b2 = kk * sig
                w = w_ref[t]
                k = kd_ref[t]
                r = r_ref[t]
                zero = jnp.zeros((nh, LANES), F32)
                acc_dk, acc_db, acc_dw, acc_g, acc_sady, acc_vdy, acc_da = zero, zero, zero, zero, zero, zero, zero
                dvp, dsas = [], []
                for i in range(ni):
                    sp = sp_ref[t, i]
                    vc = vc_scr[t, i]
                    sac = sac_scr[t, i]
                    ds = ds_scr[i]
                    if with_dy:
                        dyc = dyc_scr[t, i]
                        ds = ds + dyc * r
                        ds_scr[i] = ds
                        acc_g = acc_g + sp * dyc
                    dvp.append(ds * k)
                    dsas.append(_half_sums(ds * b2, lo_mask))
                    acc_dk = acc_dk + ds * vc
                    acc_db = acc_db + ds * sac
                    acc_dw = acc_dw + ds * sp
                for i in range(ni):
                    acc_da = acc_da + sp_ref[t, i] * dsas[i]
                    ds_scr[i] = ds_scr[i] * w + dsas[i] * a2
                if with_dy:
                    dyr = jnp.where(jnp.bitwise_and(lane, 1) == (lane >= HEAD).astype(jnp.int32), dy_ref[t], 0.0)
                    acc_sady = _half_sums(sa_ref[t] * dyr, lo_mask)
                    acc_vdy = _half_sums(v_ref[t] * dyr, lo_mask)
                dr_ref[t] = acc_g * w + b2 * acc_sady + k * acc_vdy
                dw_ref[t] = acc_dw
                dkd_ref[t] = acc_dk
                dkk_ref[t] = acc_db * sig - acc_da
                das_ref[t] = acc_db * kk
                dv_ref[t] = _split_row(_half_sums_mxu(dvp, g_ref[...], 1), lane, nh)
                return carry
            return step

        @pl.when(is_ctx)
        def _():
            lax.fori_loop(0, C, make_step(False), 0)

        @pl.when(jnp.logical_not(is_ctx))
        def _():
            lax.fori_loop(0, C, make_step(True), 0, unroll=SCAN_UNROLL)

        if nx:
            _fused_exchanges(list(zip(x_in, x_out)), xchg_specs, *refs[19 + 2 * nx:],
                             last=pl.program_id(0) == nchunk - 1)

    tok = pl.BlockSpec((C, nh, LANES), lambda c: (cmap(c), 0, 0))
    big = pltpu.VMEM((C, ni, nh, LANES), F32)
    return pl.pallas_call(
        body, name="wkv_bwd_rev" if reverse else "wkv_bwd", grid=(nchunk,),
        in_specs=[pl.BlockSpec((LANES, LANES), lambda c: (0, 0))] + [tok] * 7
        + [pl.BlockSpec((C, ni, nh, LANES), lambda c: (cmap(c), 0, 0, 0)), tok] + [_ANY] * nx,
        out_specs=[tok] * 6 + [_ANY] * nx,
        out_shape=[jax.ShapeDtypeStruct((n, nh, LANES), F32)] * 6 + _exchange_out_shapes(xchg_arrs, xchg_specs),
        scratch_shapes=[pltpu.VMEM((ni, nh, LANES), F32), big, big, big] + (_exchange_sems(nx) if nx else []),
        compiler_params=_cparams(("arbitrary",)),
    )(_head_group_matrix(LANES).astype(BF16), r2, w2, kd2, kk2, as2, v2, sa, sprev, dy, *xchg_arrs)


def _tile_heads(t):
    n, d = t.shape
    th = t.reshape(n, d // HEAD, HEAD)
    return jnp.concatenate([th, th], axis=-1)


def loss_head(x3, fo, tgt, gate, g):
    T, d = x3.shape
    tr = _pick(T, 128, 8)

    def body(x_ref, f_ref, t_ref, gate_ref, g_ref, loss_ref, dx_ref, df_ref, dgate_ref, dg_ref):
        tg = t_ref[...]

        def fl(x, fo_, gate_, g_):
            x4 = x + gate_ * fo_
            y = (x4 * lax.rsqrt(jnp.mean(x4 * x4, axis=-1, keepdims=True) + NORM_EPS)) * g_
            return 0.5 * jnp.sum(jnp.mean(jnp.square(y - tg), axis=-1))

        val, vjp = jax.vjp(fl, x_ref[...], f_ref[...], gate_ref[...], g_ref[...])
        dx, dfo, dgate, dg = vjp(jnp.ones((), F32))
        dx_ref[...] = dx
        df_ref[...] = dfo
        i = pl.program_id(0)

        @pl.when(i == 0)
        def _():
            loss_ref[...] = jnp.zeros_like(loss_ref)
            dgate_ref[...] = jnp.zeros_like(dgate_ref)
            dg_ref[...] = jnp.zeros_like(dg_ref)

        loss_ref[...] += jnp.full(loss_ref.shape, val, F32)
        dgate_ref[...] += dgate
        dg_ref[...] += dg

    tile = pl.BlockSpec((tr, d), lambda i: (i, 0))
    vec = pl.BlockSpec((1, d), lambda i: (0, 0))
    return pl.pallas_call(
        body, name="loss_head", grid=(T // tr,),
        in_specs=[tile, tile, tile, vec, vec],
        out_specs=[pl.BlockSpec((8, LANES), lambda i: (0, 0)), tile, tile, vec, vec],
        out_shape=[jax.ShapeDtypeStruct((8, LANES), F32), jax.ShapeDtypeStruct((T, d), F32),
                   jax.ShapeDtypeStruct((T, d), F32), jax.ShapeDtypeStruct((1, d), F32),
                   jax.ShapeDtypeStruct((1, d), F32)],
        compiler_params=_cparams(("arbitrary",)),
    )(x3, fo, tgt, gate, g)


def sum_adam(parts, w, m, v, name, lead=None, prev=None):
    P, R, Cc = parts.shape
    tc = _pick(Cc, 1024)
    tr = _pick(R, max(8, (256 * 1024) // tc), 16 if parts.dtype == BF16 else 8)
    prev = tuple(prev) if prev is not None else ()

    def body(p_ref, w_ref, m_ref, v_ref, *rest):
        g_ref, d_ref, nm_ref, nv_ref = rest[len(prev):]
        g = p_ref[0].astype(F32)
        for s in range(1, P):
            g = g + p_ref[s].astype(F32)
        m_new = ADAM_B1 * m_ref[...] + (1.0 - ADAM_B1) * g
        v_new = ADAM_B2 * v_ref[...] + (1.0 - ADAM_B2) * jnp.square(g)
        m_hat = m_new / (1.0 - ADAM_B1 ** ADAM_STEP)
        v_hat = v_new / (1.0 - ADAM_B2 ** ADAM_STEP)
        g_ref[...] = g
        d_ref[...] = -ADAM_LR * (m_hat / (jnp.sqrt(v_hat) + ADAM_EPS) + ADAM_WD * w_ref[...])
        nm_ref[...] = m_new
        nv_ref[...] = v_new

    if lead is None:
        pspec = pl.BlockSpec((tr, tc), lambda i, j: (i, j))
        oshape = (R, Cc)
    else:
        pspec = pl.BlockSpec((None, tr, tc), lambda i, j: (lead, i, j))
        oshape = w.shape
    return pl.pallas_call(
        body, name=name, grid=(R // tr, Cc // tc),
        in_specs=[pl.BlockSpec((P, tr, tc), lambda i, j: (0, i, j)), pspec, pspec, pspec] + [_ANY] * len(prev),
        out_specs=[pspec] * 4,
        out_shape=[jax.ShapeDtypeStruct(oshape, F32)] * 4,
        input_output_aliases={4 + q: q for q in range(len(prev))},
        compiler_params=_cparams(("parallel", "parallel")),
    )(parts, w, m, v, *prev)


def _me():
    return lax.axis_index("x"), lax.axis_index("y"), lax.axis_index("c")


def _peer(p):
    x, y, c = _me()
    px = 1 - x if p & 4 else x
    py = 1 - y if p & 2 else y
    pc = 1 - c if p & 1 else c
    return (px, py, pc), 4 * px + 2 * py + pc


def _block_view(ref, axis, idx, r, c):
    if axis is None:
        return ref.at[idx]
    if axis == 0:
        return ref.at[pl.ds(idx * r, r), :]
    return ref.at[:, pl.ds(idx * c, c)]


def _exchange_copies(src_of, dst_of, ssem, rsem, lsem, with_recvs):
    x, y, c = _me()
    me = 4 * x + 2 * y + c
    local = pltpu.make_async_copy(src_of(me), dst_of(me), lsem)
    sends, recvs = [], []
    for p in range(1, N_DEV):
        dev, idx = _peer(p)
        sends.append(pltpu.make_async_remote_copy(src_ref=src_of(idx), dst_ref=dst_of(me), send_sem=ssem(p),
                                                  recv_sem=rsem(p), device_id=dev,
                                                  device_id_type=pl.DeviceIdType.MESH))
        if with_recvs:
            recvs.append(pltpu.make_async_remote_copy(src_ref=src_of(idx), dst_ref=dst_of(idx), send_sem=ssem(p),
                                                      recv_sem=rsem(p), device_id=dev,
                                                      device_id_type=pl.DeviceIdType.MESH))
    return local, sends, recvs


def _exchange_start(*args):
    local, sends, _ = _exchange_copies(*args, with_recvs=False)
    local.start()
    for cp in sends:
        cp.start()


def _exchange_wait(*args):
    local, sends, recvs = _exchange_copies(*args, with_recvs=True)
    for cp in recvs:
        cp.wait_recv()
    for cp in sends:
        cp.wait_send()
    local.wait()


def _exchange(src_of, dst_of, send_sems, recv_sems, local_sem):
    args = (src_of, dst_of, lambda p: send_sems.at[p], lambda p: recv_sems.at[p], local_sem)
    _exchange_start(*args)
    _exchange_wait(*args)


def _fused_exchanges(pairs, specs, send_sems, recv_sems, local_sems, first=None, last=None):
    def args(j):
        src, dst = pairs[j]
        kind, axis, r, c = specs[j]
        if kind == "ag":
            src_of = lambda idx: src
            dst_of = lambda idx: _block_view(dst, axis, idx, r, c)
        else:
            src_of = lambda idx: _block_view(src, axis, idx, r, c)
            dst_of = lambda idx: dst.at[idx]
        return (src_of, dst_of, lambda p: send_sems.at[j, p], lambda p: recv_sems.at[j, p], local_sems.at[j])

    if first is not None:
        @pl.when(first)
        def _():
            for j in range(len(pairs)):
                _exchange_start(*args(j))

    if last is not None:
        @pl.when(last)
        def _():
            for j in range(len(pairs)):
                _exchange_wait(*args(j))


def _exchange_out_shapes(arrs, specs):
    out = []
    for a, (kind, axis, r, c) in zip(arrs, specs):
        if kind == "rs":
            out.append(jax.ShapeDtypeStruct((N_DEV, r, c), a.dtype))
        else:
            out.append(jax.ShapeDtypeStruct((N_DEV * r, c) if axis == 0 else (r, N_DEV * c), a.dtype))
    return out


def _exchange_specs(kind, arrs, axes):
    specs = []
    for a, axis in zip(arrs, axes):
        if kind == "ag":
            r, c = a.shape
        elif axis == 0:
            r, c = a.shape[0] // N_DEV, a.shape[1]
        else:
            r, c = a.shape[0], a.shape[1] // N_DEV
        specs.append((kind, axis, r, c))
    return specs


def _exchange_sems(n):
    return [pltpu.SemaphoreType.DMA((n, N_DEV)), pltpu.SemaphoreType.DMA((n, N_DEV)), pltpu.SemaphoreType.DMA((n,))]


_SEMS = [pltpu.SemaphoreType.DMA((N_DEV,)), pltpu.SemaphoreType.DMA((N_DEV,)), pltpu.SemaphoreType.DMA]
_ANY = pl.BlockSpec(memory_space=pl.ANY)


def all_gather(x, axis, name):
    r, c = x.shape
    shape = (N_DEV * r, c) if axis == 0 else (r, N_DEV * c)

    def body(x_ref, o_ref, send_sems, recv_sems, local_sem):
        _exchange(lambda idx: x_ref, lambda idx: _block_view(o_ref, axis, idx, r, c), send_sems, recv_sems, local_sem)

    return pl.pallas_call(
        body, name=name, in_specs=[_ANY], out_specs=_ANY,
        out_shape=jax.ShapeDtypeStruct(shape, x.dtype), scratch_shapes=_SEMS,
    )(x)


def all_gather_stack(x, name):
    r, c = x.shape

    def body(x_ref, o_ref, send_sems, recv_sems, local_sem):
        _exchange(lambda idx: x_ref, lambda idx: o_ref.at[idx], send_sems, recv_sems, local_sem)

    return pl.pallas_call(
        body, name=name, in_specs=[_ANY], out_specs=_ANY,
        out_shape=jax.ShapeDtypeStruct((N_DEV, r, c), x.dtype), scratch_shapes=_SEMS,
    )(x)


def reduce_scatter_exchange(g, axis, name):
    if axis is None:
        _, r, c = g.shape
    elif axis == 0:
        r, c = g.shape[0] // N_DEV, g.shape[1]
    else:
        r, c = g.shape[0], g.shape[1] // N_DEV

    def body(g_ref, o_ref, send_sems, recv_sems, local_sem):
        _exchange(lambda idx: _block_view(g_ref, axis, idx, r, c), lambda idx: o_ref.at[idx],
                  send_sems, recv_sems, local_sem)

    return pl.pallas_call(
        body, name=name, in_specs=[_ANY], out_specs=_ANY,
        out_shape=jax.ShapeDtypeStruct((N_DEV, r, c), g.dtype), scratch_shapes=_SEMS,
    )(g)


PACK_QUANTUM = 16 * LANES


def _pack(arrs, dtype=F32, lead=0):
    keep = arrs[0].shape[:lead]
    flat = jnp.concatenate([a.reshape(keep + (-1,)).astype(dtype) for a in arrs], axis=-1)
    pad = (-flat.shape[-1]) % PACK_QUANTUM
    flat = jnp.pad(flat, ((0, 0),) * lead + ((0, pad),))
    return flat.reshape(keep + (-1, LANES))


def _unpack(flat2d, shapes, lead=()):
    flat = flat2d.reshape(lead + (-1,))
    out, off = [], 0
    for s in shapes:
        n = int(np.prod(s))
        out.append(flat[..., off:off + n].reshape(lead + tuple(s)))
        off += n
    return out


def _gather_lastdim(stk):
    return jnp.moveaxis(stk, 0, -2).reshape(stk.shape[1:-1] + (N_DEV * stk.shape[-1],))


def _gather_dim(stk, dim):
    moved = jnp.moveaxis(stk, 0, dim)
    sh = list(stk.shape[1:])
    sh[dim] = sh[dim] * N_DEV
    return moved.reshape(sh)


def _scatter_dim(full, dim):
    sh = list(full.shape)
    sh[dim:dim + 1] = [N_DEV, sh[dim] // N_DEV]
    return jnp.moveaxis(full.reshape(sh), dim, 0)


def _head_group_matrix(tc):
    return np.kron(np.eye(tc // HEAD, dtype=np.float32), np.ones((HEAD, HEAD), np.float32))


_SCAN_COMM = {0: ("w13_0", "w13_1"), 1: ("win", "w2_0", "w2_1", "wout", "wo")}
_W_AXIS = dict(wr=0, wk=0, wv=0, wo=0, win=1, wout=0, w13_0=1, w13_1=1, w2_0=0, w2_1=0)


def _build_forward(ctx2d, T, D, shards=None, sink=None):
    L = ctx2d.shape[0]
    N = L + T

    def make_scan(d):
        keys = _SCAN_COMM[d] if shards is not None else ()
        axes = tuple(_W_AXIS[k] for k in keys)

        def run_fwd(tok, sh):
            return _wkv_fwd_call(*tok, L, d == 1, tuple(sh), _exchange_specs("ag", sh, axes))

        @jax.custom_vjp
        def op(tok, sh):
            outs = run_fwd(tok, sh)
            return (outs[0],) + tuple(outs[3:])

        def op_fwd(tok, sh):
            outs = run_fwd(tok, sh)
            return (outs[0],) + tuple(outs[3:]), (tok, outs[1], outs[2])

        def op_bwd(res, cts):
            tok, sa, sprev = res
            dg = tuple(cts[1:])
            outs = _wkv_bwd_call(*tok, sa, sprev, cts[0], L, d == 1, dg, _exchange_specs("rs", dg, axes))
            for k, recv in zip(keys, outs[6:]):
                sink[k] = recv
            return tuple(outs[:6]), tuple(jnp.zeros(shards[k].shape, shards[k].dtype) for k in keys)

        op.defvjp(op_fwd, op_bwd)
        return op, keys

    scans = [make_scan(0), make_scan(1)]
    tc_head = _pick(D, 2 * LANES)
    gm = _head_group_matrix(tc_head)
    tr_row = _pick(math.gcd(L, T), 128, 8)
    op_norm = make_rowwise("norm_mod", _f_norm_mod, (F32,), tr_row, D, nb0=L // tr_row)
    op_res = [make_rowwise(f"res_norm_mod{i}", _f_res_norm_mod, (F32, BF16), _pick(T, 128, 8), D) for i in range(3)]
    op_prep = make_rowwise("wkv_prep", _f_prep, (F32,) * 7, _pick(N, 256, 8), tc_head, consts=(gm,))
    op_read = make_rowwise("wkv_readout", _f_readout, (BF16,), _pick(N, 256, 8), tc_head, consts=(gm,))

    def v3(a):
        return a.reshape(a.shape[0], 1, a.shape[-1])

    def fwd(xin, Ps, Wb):
        modx, modc = Ps["modx"], Ps["modc"]
        cat = jnp.concatenate([ctx2d, xin], axis=0)
        seg = lambda a, b: jnp.stack([a, b])[:, None, :]
        (hcat,) = op_norm((cat,), (Ps["n1"][0][None, None, :], seg(modc[0], modx[0, 0]), seg(modc[1], modx[0, 1])))
        xr, xw, xk, xv, xa, xg = shift_mix(hcat, Ps["mix"][:, None, :], L)
        r = linear(xr, Wb["wr"], F32, "wr")
        k = linear(xk, Wb["wk"], F32, "wk")
        v = linear(xv, Wb["wv"], F32, "wv")
        gl = jax.nn.sigmoid(linear(xg, Wb["g1"], F32, "g1"))
        g = linear(gl.astype(BF16), Wb["g2"], F32, "g2")
        tw = jnp.tanh(linear(xw, Wb["w1"], F32, "w1")).astype(BF16)
        ta = linear(xa, Wb["a1"], F32, "a1").astype(BF16)
        lw = [linear(tw[:, LORA_PAD * d:LORA_PAD * (d + 1)], Wb["w2d"][d], F32, f"w2_{d}") for d in range(2)]
        la = [linear(ta[:, LORA_PAD * d:LORA_PAD * (d + 1)], Wb["a2d"][d], F32, f"a2_{d}") for d in range(2)]
        kk, dec0, dec1, kd0, kd1, as0, as1 = op_prep(
            (k, lw[0], lw[1], la[0], la[1]),
            (v3(Ps["kk"]), v3(Ps["ka"]), Ps["w0"][0][None, None, :], Ps["w0"][1][None, None, :],
             Ps["a0"][0][None, None, :], Ps["a0"][1][None, None, :]))
        r2, kk2, v2 = _tile_heads(r), _tile_heads(kk), _tile_heads(v)
        ys = []
        Wb = dict(Wb)
        for d, (dec, kd, sg) in enumerate(((dec0, kd0, as0), (dec1, kd1, as1))):
            op, keys = scans[d]
            outs = op((r2, _tile_heads(dec), _tile_heads(kd), kk2, _tile_heads(sg), v2),
                      tuple(shards[k] for k in keys))
            yx = outs[0]
            ys.append((yx[:, :, :HEAD] + yx[:, :, HEAD:]).reshape(N, D))
            Wb.update(zip(keys, outs[1:]))
        (o,) = op_read((ys[0], ys[1], r, kd0, kd1, v, g), (v3(Ps["rk"]), v3(Ps["lnw"]), v3(Ps["lnb"])))
        att = linear(o[L:], Wb["wo"], F32, "wo")
        x1, h2 = op_res[0]((xin, att), (modx[0, 2][None, None, :], Ps["n2"][0][None, None, :],
                                        modx[0, 3][None, None, :], modx[0, 4][None, None, :]))
        act = swiglu_act(linear(h2, Wb["w13_0"], BF16, "w13_0"), "swiglu0")
        f0 = linear(act, Wb["w2_0"], F32, "w2_0")
        x2, h = op_res[1]((x1, f0), (modx[0, 5][None, None, :], Ps["n1"][1][None, None, :],
                                     modx[1, 0][None, None, :], modx[1, 1][None, None, :]))
        guc = linear(h, Wb["win"], BF16, "win")
        p = gated_conv(guc, Ps["conv"][:, None, :])
        cv = linear(p, Wb["wout"], F32, "wout")
        x3, h2b = op_res[2]((x2, cv), (modx[1, 2][None, None, :], Ps["n2"][1][None, None, :],
                                       modx[1, 3][None, None, :], modx[1, 4][None, None, :]))
        act1 = swiglu_act(linear(h2b, Wb["w13_1"], BF16, "w13_1"), "swiglu1")
        f1 = linear(act1, Wb["w2_1"], F32, "w2_1")
        return x3, f1


    return fwd


def kernel(x, c, ctx, c_ctx, norm1_g, norm2_g, ada_w, ada_b, rw_mix, rw_wr, rw_wk, rw_wv, rw_wo, rw_w0, rw_w1, rw_w2, rw_a0, rw_a1, rw_a2, rw_g1, rw_g2, rw_kk, rw_ka, rw_rk, rw_lnw, rw_lnb, sc_win, sc_conv, sc_wout, ffn_w13, ffn_w2, final_g, loss_target, m_c_ctx, m_norm1_g, m_norm2_g, m_ada_w, m_ada_b, m_rw_mix, m_rw_wr, m_rw_wk, m_rw_wv, m_rw_wo, m_rw_w0, m_rw_w1, m_rw_w2, m_rw_a0, m_rw_a1, m_rw_a2, m_rw_g1, m_rw_g2, m_rw_kk, m_rw_ka, m_rw_rk, m_rw_lnw, m_rw_lnb, m_sc_win, m_sc_conv, m_sc_wout, m_ffn_w13, m_ffn_w2, m_final_g, v_c_ctx, v_norm1_g, v_norm2_g, v_ada_w, v_ada_b, v_rw_mix, v_rw_wr, v_rw_wk, v_rw_wv, v_rw_wo, v_rw_w0, v_rw_w1, v_rw_w2, v_rw_a0, v_rw_a1, v_rw_a2, v_rw_g1, v_rw_g2, v_rw_kk, v_rw_ka, v_rw_rk, v_rw_lnw, v_rw_lnb, v_sc_win, v_sc_conv, v_sc_wout, v_ffn_w13, v_ffn_w2, v_final_g):
    W = dict(c_ctx=c_ctx, norm1_g=norm1_g, norm2_g=norm2_g, ada_w=ada_w, ada_b=ada_b, rw_mix=rw_mix, rw_wr=rw_wr,
             rw_wk=rw_wk, rw_wv=rw_wv, rw_wo=rw_wo, rw_w0=rw_w0, rw_w1=rw_w1, rw_w2=rw_w2, rw_a0=rw_a0, rw_a1=rw_a1,
             rw_a2=rw_a2, rw_g1=rw_g1, rw_g2=rw_g2, rw_kk=rw_kk, rw_ka=rw_ka, rw_rk=rw_rk, rw_lnw=rw_lnw,
             rw_lnb=rw_lnb, sc_win=sc_win, sc_conv=sc_conv, sc_wout=sc_wout, ffn_w13=ffn_w13, ffn_w2=ffn_w2,
             final_g=final_g)
    Mo = dict(c_ctx=m_c_ctx, norm1_g=m_norm1_g, norm2_g=m_norm2_g, ada_w=m_ada_w, ada_b=m_ada_b, rw_mix=m_rw_mix,
              rw_wr=m_rw_wr, rw_wk=m_rw_wk, rw_wv=m_rw_wv, rw_wo=m_rw_wo, rw_w0=m_rw_w0, rw_w1=m_rw_w1,
              rw_w2=m_rw_w2, rw_a0=m_rw_a0, rw_a1=m_rw_a1, rw_a2=m_rw_a2, rw_g1=m_rw_g1, rw_g2=m_rw_g2,
              rw_kk=m_rw_kk, rw_ka=m_rw_ka, rw_rk=m_rw_rk, rw_lnw=m_rw_lnw, rw_lnb=m_rw_lnb, sc_win=m_sc_win,
              sc_conv=m_sc_conv, sc_wout=m_sc_wout, ffn_w13=m_ffn_w13, ffn_w2=m_ffn_w2, final_g=m_final_g)
    Vo = dict(c_ctx=v_c_ctx, norm1_g=v_norm1_g, norm2_g=v_norm2_g, ada_w=v_ada_w, ada_b=v_ada_b, rw_mix=v_rw_mix,
              rw_wr=v_rw_wr, rw_wk=v_rw_wk, rw_wv=v_rw_wv, rw_wo=v_rw_wo, rw_w0=v_rw_w0, rw_w1=v_rw_w1,
              rw_w2=v_rw_w2, rw_a0=v_rw_a0, rw_a1=v_rw_a1, rw_a2=v_rw_a2, rw_g1=v_rw_g1, rw_g2=v_rw_g2,
              rw_kk=v_rw_kk, rw_ka=v_rw_ka, rw_rk=v_rw_rk, rw_lnw=v_rw_lnw, rw_lnb=v_rw_lnb, sc_win=v_sc_win,
              sc_conv=v_sc_conv, sc_wout=v_sc_wout, ffn_w13=v_ffn_w13, ffn_w2=v_ffn_w2, final_g=v_final_g)
    names = list(W)

    x2d = x[0]
    ctx2d = ctx[0]
    tgt = loss_target[0]
    T, D = x2d.shape
    L = ctx2d.shape[0]
    N = L + T
    nh = D // HEAD
    mx, my, mc = _me()
    me = 4 * mx + 2 * my + mc
    dloc = D // N_DEV

    lr = rw_w1.shape[-1]
    pad_r = LORA_PAD - lr
    w1p = jnp.pad(rw_w1[0], ((0, 0), (0, 0), (0, pad_r)))
    a1p = jnp.pad(rw_a1[0], ((0, 0), (0, 0), (0, pad_r)))
    w2p = jnp.pad(rw_w2[0], ((0, 0), (0, pad_r), (0, 0)))
    a2p = jnp.pad(rw_a2[0], ((0, 0), (0, pad_r), (0, 0)))
    small_loc = [rw_mix[0], rw_w0[0], rw_a0[0], sc_conv[0], w1p, a1p, w2p, a2p, rw_g1[0], rw_g2[0]]
    small_dim = [1, 1, 1, 1, 1, 1, 2, 2, 0, 1]
    small_shapes = [a.shape for a in small_loc]
    small_groups = ((slice(0, 4), F32, "vec"), (slice(4, 10), BF16, "mat"))
    small_full = []
    for sl, dt, tag in small_groups:
        sm_all = all_gather_stack(_pack(small_loc[sl], dt), "ag_small_" + tag)
        sm_parts = _unpack(sm_all, small_shapes[sl], lead=(N_DEV,))
        small_full += [_gather_dim(p, dm) for p, dm in zip(sm_parts, small_dim[sl])]
    mix_f, w0_f, a0_f, conv_f, w1_f, a1_f, w2_f, a2_f, g1_f, g2_f = small_full

    c_all = all_gather_stack(jnp.pad(c, ((0, 7), (0, 0))), "ag_c")[:, 0, :]
    cond_pre = jnp.concatenate([c_all, c_ctx[None, :], jnp.zeros((7, D), F32)], axis=0)
    cond_rows = jax.nn.silu(cond_pre)
    ncol = ada_w.shape[-1]
    mod_loc = []
    for i in range(2):
        bi = lax.dynamic_slice(ada_b[i], (me * ncol,), (ncol,))
        mod_loc.append(_mm(cond_rows, ada_w[i], out_dtype=F32, name=f"ada_fwd{i}") + bi[None, :])
    mod_all = all_gather_stack(jnp.concatenate(mod_loc, axis=0), "ag_mod")
    mod_full = _gather_lastdim(mod_all).reshape(2, 16, 6, D)
    mod_x = lax.dynamic_index_in_dim(mod_full, me, axis=1, keepdims=False)
    mod_c = mod_full[0, 8, :2, :]

    def ag_w(wl, axis, name):
        return all_gather(wl.astype(BF16), axis, name)

    shards = dict(wo=rw_wo[0], win=sc_win[0], wout=sc_wout[0], w13_0=ffn_w13[0], w13_1=ffn_w13[1],
                  w2_0=ffn_w2[0], w2_1=ffn_w2[1])
    shards = {k_: a.astype(BF16) for k_, a in shards.items()}
    sink = {}
    Wb = dict(
        wr=ag_w(rw_wr[0], 0, "ag_wr"), wk=ag_w(rw_wk[0], 0, "ag_wk"), wv=ag_w(rw_wv[0], 0, "ag_wv"),
        w1=jnp.concatenate([w1_f[0], w1_f[1]], axis=1).astype(BF16),
        a1=jnp.concatenate([a1_f[0], a1_f[1]], axis=1).astype(BF16),
        w2d=w2_f.astype(BF16), a2d=a2_f.astype(BF16),
        g1=g1_f.astype(BF16), g2=g2_f.astype(BF16),
    )
    Ps = dict(n1=norm1_g, n2=norm2_g, modx=mod_x, modc=mod_c, mix=mix_f, w0=w0_f, a0=a0_f, conv=conv_f,
              kk=rw_kk, ka=rw_ka, rk=rw_rk.reshape(1, D), lnw=rw_lnw, lnb=rw_lnb)

    fwd = _build_forward(ctx2d, T, D, shards, sink)
    (x3, f1), vjp_fn = jax.vjp(fwd, x2d, Ps, Wb)
    loss_acc, dx3, df1, dgate, dfinal = loss_head(x3, f1, tgt, mod_x[1, 5][None, :], final_g[None, :])
    dx, dPs, dWb = vjp_fn((dx3, df1))
    loss = lax.psum(loss_acc[0, 0], ("x", "y", "c"))

    dmodx = dPs["modx"].at[1, 5].add(dgate[0])
    dmodc = jnp.concatenate([dPs["modc"], jnp.zeros((4, D), F32)], axis=0)
    drow = jnp.stack([dmodx.reshape(2, 6 * D), jnp.stack([dmodc.reshape(6 * D), jnp.zeros((6 * D,), F32)])], axis=1)
    drow_all = all_gather_stack(drow.reshape(4, 6 * D), "ag_dmod").reshape(N_DEV, 2, 2, 6 * D)
    dctx_tot = drow_all[0, :, 1, :]
    for s in range(1, N_DEV):
        dctx_tot = dctx_tot + drow_all[s, :, 1, :]
    dmod_rows = jnp.concatenate([jnp.moveaxis(drow_all[:, :, 0, :], 0, 1), dctx_tot[:, None, :],
                                 jnp.zeros((2, 7, 6 * D), F32)], axis=1)
    grad_ada_b = dctx_tot
    for s in range(N_DEV):
        grad_ada_b = grad_ada_b + drow_all[s, :, 0, :]
    dmod_mine = lax.dynamic_slice_in_dim(dmod_rows, me * ncol, ncol, axis=2)
    g_ada_w = [_mm(cond_rows, dmod_mine[i], ta=True, out_dtype=F32, name=f"ada_dw{i}") for i in range(2)]
    dcond_part = _mm(dmod_mine[0], ada_w[0], tb=True, out_dtype=F32, name="ada_dcond")[8]

    rep_names = ["c_ctx", "norm1_g", "norm2_g", "rw_kk", "rw_ka", "rw_rk", "rw_lnw", "rw_lnb", "final_g"]
    rep_part = [dcond_part, dPs["n1"], dPs["n2"], dPs["kk"], dPs["ka"], dPs["rk"].reshape(W["rw_rk"].shape),
                dPs["lnw"], dPs["lnb"], dfinal[0]]
    rep_shapes = [W[n_].shape for n_ in rep_names]
    rep_all = all_gather_stack(_pack(rep_part), "ag_rep_grads")
    sg = jax.nn.sigmoid(c_ctx)
    dsilu = sg * (1.0 + c_ctx * (1.0 - sg))
    rep_scale = _pack([dsilu] + [jnp.ones(s, F32) for s in rep_shapes[1:]])
    rep_all = rep_all * rep_scale[None]
    rep_w = _pack([W[n_] for n_ in rep_names])
    rep_m = _pack([Mo[n_] for n_ in rep_names])
    rep_v = _pack([Vo[n_] for n_ in rep_names])
    rep_out = sum_adam(rep_all, rep_w, rep_m, rep_v, "adam_rep")
    results = {}
    for nm_, vals in zip(rep_names, zip(*[_unpack(o, rep_shapes) for o in rep_out])):
        results[nm_] = vals

    results["ada_b"] = tuple(sum_adam(grad_ada_b.reshape(1, 2 * 6, D), ada_b.reshape(12, D), m_ada_b.reshape(12, D),
                                      v_ada_b.reshape(12, D), "adam_ada_b"))
    results["ada_b"] = tuple(o.reshape(ada_b.shape) for o in results["ada_b"])

    outs = sum_adam(g_ada_w[0][None], ada_w, m_ada_w, v_ada_w, "adam_ada_w0", lead=0)
    results["ada_w"] = tuple(sum_adam(g_ada_w[1][None], ada_w, m_ada_w, v_ada_w, "adam_ada_w1", lead=1, prev=outs))

    dw1 = jnp.stack([dWb["w1"][:, :LORA_PAD], dWb["w1"][:, LORA_PAD:]])
    da1 = jnp.stack([dWb["a1"][:, :LORA_PAD], dWb["a1"][:, LORA_PAD:]])
    small_g = [dPs["mix"], dPs["w0"], dPs["a0"], dPs["conv"], dw1, da1, dWb["w2d"], dWb["a2d"], dWb["g1"], dWb["g2"]]
    small_names = ["rw_mix", "rw_w0", "rw_a0", "sc_conv", "rw_w1", "rw_a1", "rw_w2", "rw_a2", "rw_g1", "rw_g2"]

    def padded_local(nm_, src):
        a = src[nm_][0]
        if nm_ in ("rw_w1", "rw_a1"):
            return jnp.pad(a, ((0, 0), (0, 0), (0, pad_r)))
        if nm_ in ("rw_w2", "rw_a2"):
            return jnp.pad(a, ((0, 0), (0, pad_r), (0, 0)))
        return a

    for sl, dt, tag in small_groups:
        blocks = [_scatter_dim(gf, dm) for gf, dm in zip(small_g[sl], small_dim[sl])]
        sm_recv = reduce_scatter_exchange(_pack(blocks, dt, lead=1), None, "rs_small_" + tag)
        sm_out = sum_adam(sm_recv, _pack([padded_local(n_, W) for n_ in small_names[sl]]),
                          _pack([padded_local(n_, Mo) for n_ in small_names[sl]]),
                          _pack([padded_local(n_, Vo) for n_ in small_names[sl]]), "adam_small_" + tag)
        for nm_, vals in zip(small_names[sl], zip(*[_unpack(o, small_shapes[sl]) for o in sm_out])):
            if nm_ in ("rw_w1", "rw_a1"):
                vals = tuple(a[:, :, :lr] for a in vals)
            if nm_ in ("rw_w2", "rw_a2"):
                vals = tuple(a[:, :lr, :] for a in vals)
            results[nm_] = tuple(a[None] for a in vals)

    def rs_adam(key, nm_, lead, prev=None):
        recv = sink[key] if key in sink else reduce_scatter_exchange(dWb[key], _W_AXIS[key], "rs_" + key)
        return sum_adam(recv, W[nm_], Mo[nm_], Vo[nm_], "adam_" + key, lead=lead, prev=prev)

    for nm_, key in (("rw_wr", "wr"), ("rw_wk", "wk"), ("rw_wv", "wv"), ("rw_wo", "wo"), ("sc_win", "win"),
                     ("sc_wout", "wout")):
        results[nm_] = tuple(rs_adam(key, nm_, 0))
    for nm_, key in (("ffn_w13", "w13"), ("ffn_w2", "w2")):
        results[nm_] = tuple(rs_adam(f"{key}_1", nm_, 1, prev=rs_adam(f"{key}_0", nm_, 0)))

    grads = [results[n_][0] for n_ in names]
    deltas = [results[n_][1] for n_ in names]
    new_m = [results[n_][2] for n_ in names]
    new_v = [results[n_][3] for n_ in names]
    return (loss, dx[None], *grads, *deltas, *new_m, *new_v)
```
